```python
import jax, jax.numpy as jnp
from jax import lax
import numpy as np


D_MODEL = 2048
BATCH = 2
SEQ = 4096
DEPTH = 4

CHUNK = 64
Q_BLOCK = 128
EPS = 1e-6
ROPE_BASE = 10000.0

D_MIX = D_MODEL
MLA_HEADS = 8
MLA_NOPE = 128
MLA_ROPE = 64
MLA_QK = MLA_NOPE + MLA_ROPE
MLA_V = 128
Q_LORA = 512
KV_LORA = 256
MLA_WIDTH = MLA_HEADS * MLA_V
SB_HEADS = 8
SB_DIM = 128
SB_WIDTH = SB_HEADS * SB_DIM
IN_COLS = Q_LORA + KV_LORA + MLA_ROPE + 3 * SB_WIDTH

N_EXPERTS = 16
N_GROUPS = 4
EXPERTS_PER_GROUP = N_EXPERTS // N_GROUPS
TOP_K = 2
D_EXPERT = 512
MOE_BLOCK = 128

kernel_name = 'hybrid_mla_stickbreak_grouped_moe_adaln'


def rms_norm(x, g):
    xf = x.astype(jnp.float32)
    y = xf * lax.rsqrt(jnp.mean(xf * xf, axis=-1, keepdims=True) + EPS)
    return (y * g.astype(jnp.float32)).astype(x.dtype)


def modulate(h, shift, scale):
    return h * (1 + scale[:, None, :]) + shift[:, None, :]


def rope_tables(seq, dim):
    inv = 1.0 / (ROPE_BASE ** (jnp.arange(0, dim, 2, dtype=jnp.float32) / dim))
    ang = jnp.arange(seq, dtype=jnp.float32)[:, None] * inv[None, :]
    return jnp.cos(ang), jnp.sin(ang)


def apply_rope(x, cos, sin):
    x1, x2 = jnp.split(x, 2, axis=-1)
    c = cos[:, None, :].astype(x.dtype)
    s = sin[:, None, :].astype(x.dtype)
    return jnp.concatenate([x1 * c - x2 * s, x1 * s + x2 * c], axis=-1)


def to_query_blocks(q):
    B, S, H, d = q.shape
    return q.reshape(B, S // Q_BLOCK, Q_BLOCK, H, d).transpose(1, 0, 3, 2, 4)


def from_query_blocks(o):
    nb, B, Qb, H, d = o.shape
    return o.transpose(1, 0, 2, 3, 4).reshape(B, nb * Qb, H * d)


def mla_attention(cq, ckv, krope, q_lora_g, kv_lora_g, w_uq, w_ukv, q_head_g, k_head_g, cos, sin):
    B, S, _ = cq.shape
    q = (rms_norm(cq, q_lora_g) @ w_uq).reshape(B, S, MLA_HEADS, MLA_QK)
    kv = (rms_norm(ckv, kv_lora_g) @ w_ukv).reshape(B, S, MLA_HEADS, MLA_NOPE + MLA_V)
    k_nope, v = kv[..., :MLA_NOPE], kv[..., MLA_NOPE:]
    k_rope = jnp.broadcast_to(krope[:, :, None, :], (B, S, MLA_HEADS, MLA_ROPE))
    k = jnp.concatenate([k_nope, k_rope], axis=-1)
    q = rms_norm(q, q_head_g)
    k = rms_norm(k, k_head_g)
    q = jnp.concatenate([q[..., :MLA_NOPE], apply_rope(q[..., MLA_NOPE:], cos, sin)], axis=-1)
    k = jnp.concatenate([k[..., :MLA_NOPE], apply_rope(k[..., MLA_NOPE:], cos, sin)], axis=-1)
    scale = MLA_QK ** -0.5
    key_chunk = jnp.arange(S) // CHUNK

    def block(args):
        qi, b_idx = args
        s = jnp.einsum('bhqd,bshd->bhqs', qi, k).astype(jnp.float32) * scale
        q_chunk = (b_idx * Q_BLOCK + jnp.arange(Q_BLOCK)) // CHUNK
        mask = key_chunk[None, :] <= q_chunk[:, None]
        p = jax.nn.softmax(jnp.where(mask, s, -jnp.inf), axis=-1).astype(v.dtype)
        return jnp.einsum('bhqs,bshd->bqhd', p, v)

    qb = to_query_blocks(q)
    o = lax.map(block, (qb, jnp.arange(qb.shape[0])))
    return from_query_blocks(o)


def stick_breaking_attention(q, k, v):
    B, S, H, d = q.shape
    scale = d ** -0.5
    key_pos = jnp.arange(S)

    def block(args):
        qi, b_idx = args
        z = jnp.einsum('bhqd,bshd->bhqs', qi, k).astype(jnp.float32) * scale
        t = b_idx * Q_BLOCK + jnp.arange(Q_BLOCK)
        mask = key_pos[None, :] < t[:, None]
        log_stay = jnp.where(mask, jax.nn.log_sigmoid(-z), 0.0)
        after = lax.cumsum(log_stay, axis=3, reverse=True) - log_stay
        a = jnp.where(mask, jnp.exp(jax.nn.log_sigmoid(z) + after), 0.0)
        return jnp.einsum('bhqs,bshd->bqhd', a.astype(v.dtype), v)

    qb = to_query_blocks(q)
    o = lax.map(block, (qb, jnp.arange(qb.shape[0])))
    return from_query_blocks(o)


def grouped_route(h, router_w, router_b):
    T = h.shape[0]
    scores = jax.nn.sigmoid((h @ router_w).astype(jnp.float32))
    sel = (scores + router_b.astype(jnp.float32)).reshape(T, N_GROUPS, EXPERTS_PER_GROUP)
    group_score = lax.top_k(sel, TOP_K)[0].sum(-1)
    g = jnp.argmax(group_score, axis=-1)
    in_group = jnp.take_along_axis(sel, g[:, None, None], axis=1)[:, 0]
    _, local = lax.top_k(in_group, TOP_K)
    idx = g[:, None] * EXPERTS_PER_GROUP + local
    w = jnp.take_along_axis(scores, idx, axis=1)
    w = w / jnp.sum(w, axis=-1, keepdims=True)
    return idx, w


def moe_ffn(h, idx, w, w_gate, w_up, w_down):
    T, D = h.shape
    A = T * TOP_K
    flat_e = idx.reshape(A)
    order = jnp.argsort(flat_e)
    e_sorted = flat_e[order]
    tok_sorted = order // TOP_K
    w_sorted = w.reshape(A)[order]
    counts = jnp.bincount(flat_e, length=N_EXPERTS)
    padded = (counts + MOE_BLOCK - 1) // MOE_BLOCK * MOE_BLOCK
    pad_end = jnp.cumsum(padded)
    pad_start = pad_end - padded
    start = jnp.cumsum(counts) - counts
    dest = pad_start[e_sorted] + jnp.arange(A) - start[e_sorted]
    n_blocks = -(-A // MOE_BLOCK) + N_EXPERTS
    rows = n_blocks * MOE_BLOCK
    row_tok = jnp.full((rows,), T, dtype=jnp.int32).at[dest].set(tok_sorted.astype(jnp.int32))
    block_e = jnp.minimum(jnp.searchsorted(pad_end, jnp.arange(n_blocks) * MOE_BLOCK, side='right'), N_EXPERTS - 1)
    h_pad = jnp.concatenate([h, jnp.zeros((1, D), h.dtype)], axis=0)

    def block(args):
        toks, e = args
        xb = h_pad[toks]
        a = jax.nn.silu(xb @ w_gate[e]) * (xb @ w_up[e])
        return a @ w_down[e]

    y = lax.map(block, (row_tok.reshape(n_blocks, MOE_BLOCK), block_e)).reshape(rows, D)
    contrib = y[dest] * w_sorted[:, None].astype(h.dtype)
    return jnp.zeros((T, D), h.dtype).at[tok_sorted].add(contrib)


def setup_inputs(seed: int = 0) -> dict:
    key = jax.random.key(seed)
    ks = jax.random.split(key, 24)
    nrm = lambda k, shape, s: jax.random.normal(k, shape, jnp.float32) * s
    gain = lambda k, shape: 1.0 + 0.02 * jax.random.normal(k, shape, jnp.float32)
    return {
        'x': nrm(ks[0], (BATCH, SEQ, D_MODEL), 1.0),
        'c': nrm(ks[1], (BATCH, D_MODEL), 1.0),
        'norm1_g': gain(ks[2], (DEPTH, D_MODEL)),
        'norm2_g': gain(ks[3], (DEPTH, D_MODEL)),
        'w_ada': nrm(ks[4], (DEPTH, D_MODEL, 6 * D_MODEL), 0.5 * D_MODEL ** -0.5),
        'b_ada': nrm(ks[5], (DEPTH, 6 * D_MODEL), 0.02),
        'w_in': nrm(ks[6], (DEPTH, D_MODEL, IN_COLS), D_MODEL ** -0.5),
        'q_lora_g': gain(ks[7], (DEPTH, Q_LORA)),
        'kv_lora_g': gain(ks[8], (DEPTH, KV_LORA)),
        'w_uq': nrm(ks[9], (DEPTH, Q_LORA, MLA_HEADS * MLA_QK), Q_LORA ** -0.5),
        'w_ukv': nrm(ks[10], (DEPTH, KV_LORA, MLA_HEADS * (MLA_NOPE + MLA_V)), KV_LORA ** -0.5),
        'q_head_g': gain(ks[11], (DEPTH, MLA_QK)),
        'k_head_g': gain(ks[12], (DEPTH, MLA_QK)),
        'mla_out_g': gain(ks[13], (DEPTH, MLA_WIDTH)),
        'sb_out_g': gain(ks[14], (DEPTH, SB_WIDTH)),
        'w_o': nrm(ks[15], (DEPTH, D_MIX, D_MODEL), D_MIX ** -0.5),
        'router_w': nrm(ks[16], (D_MODEL, N_EXPERTS), D_MODEL ** -0.5),
        'router_b': nrm(ks[17], (N_EXPERTS,), 0.01),
        'w_gate': nrm(ks[18], (DEPTH, N_EXPERTS, D_MODEL, D_EXPERT), D_MODEL ** -0.5),
        'w_up': nrm(ks[19], (DEPTH, N_EXPERTS, D_MODEL, D_EXPERT), D_MODEL ** -0.5),
        'w_down': nrm(ks[20], (DEPTH, N_EXPERTS, D_EXPERT, D_MODEL), D_EXPERT ** -0.5),
    }


def reference(x, c, norm1_g, norm2_g, w_ada, b_ada, w_in, q_lora_g, kv_lora_g, w_uq, w_ukv,
              q_head_g, k_head_g, mla_out_g, sb_out_g, w_o, router_w, router_b,
              w_gate, w_up, w_down):
    B, S, D = x.shape
    T = B * S
    cos, sin = rope_tables(S, MLA_ROPE)
    c_act = jax.nn.silu(c)
    splits = [Q_LORA, Q_LORA + KV_LORA, Q_LORA + KV_LORA + MLA_ROPE,
              Q_LORA + KV_LORA + MLA_ROPE + SB_WIDTH, Q_LORA + KV_LORA + MLA_ROPE + 2 * SB_WIDTH]
    for l in range(DEPTH):
        mod = c_act @ w_ada[l] + b_ada[l]
        shift_a, scale_a, gate_a, shift_f, scale_f, gate_f = jnp.split(mod, 6, axis=-1)

        h = modulate(rms_norm(x, norm1_g[l]), shift_a, scale_a)
        proj = h @ w_in[l]
        cq, ckv, krope, sq, sk, sv = jnp.split(proj, splits, axis=-1)
        o_mla = mla_attention(cq, ckv, krope, q_lora_g[l], kv_lora_g[l], w_uq[l], w_ukv[l],
                              q_head_g[l], k_head_g[l], cos, sin)
        o_sb = stick_breaking_attention(sq.reshape(B, S, SB_HEADS, SB_DIM),
                                        sk.reshape(B, S, SB_HEADS, SB_DIM),
                                        sv.reshape(B, S, SB_HEADS, SB_DIM))
        o = jnp.concatenate([rms_norm(o_mla, mla_out_g[l]), rms_norm(o_sb, sb_out_g[l])], axis=-1)
        x = x + gate_a[:, None, :] * (o @ w_o[l])

        h2 = modulate(rms_norm(x, norm2_g[l]), shift_f, scale_f).reshape(T, D)
        idx, w = grouped_route(h2, router_w, router_b)
        y = moe_ffn(h2, idx, w, w_gate[l], w_up[l], w_down[l]).reshape(B, S, D)
        x = x + gate_f[:, None, :] * y
    return x
```

```python
import functools

import jax
import jax.numpy as jnp
from jax import lax
from jax.experimental import pallas as pl
from jax.experimental.pallas import tpu as pltpu

F32 = jnp.float32
BF16 = jnp.bfloat16

EPS = 1e-6
ROPE_BASE = 10000.0
CHUNK = 64

MLA_HEADS = 8
MLA_NOPE = 128
MLA_ROPE = 64
MLA_QK = MLA_NOPE + MLA_ROPE
MLA_V = 128
Q_LORA = 512
KV_LORA = 256
SB_HEADS = 8
SB_DIM = 128
SB_WIDTH = SB_HEADS * SB_DIM
LANE = 128
QK_PAD = 2 * LANE
COL_CQ = 0
COL_CKV = Q_LORA
COL_KR = Q_LORA + KV_LORA
COL_SQ = COL_KR + LANE
COL_SK = COL_SQ + SB_WIDTH
COL_SV = COL_SK + SB_WIDTH
IN_COLS_PAD = COL_SV + SB_WIDTH

N_EXPERTS = 16
N_GROUPS = 4
EXPERTS_PER_GROUP = N_EXPERTS // N_GROUPS
TOP_K = 2
D_EXPERT = 512
MOE_ROWS = 256

VMEM_LIMIT = 56 * 1024 * 1024


def _params(*dims):
    return pltpu.CompilerParams(dimension_semantics=dims, vmem_limit_bytes=VMEM_LIMIT)


def _rms(x):
    return x * lax.rsqrt(jnp.mean(x * x, axis=-1, keepdims=True) + EPS)


def _ada_kernel(c_ref, w_ref, b_ref, o_ref):
    w = w_ref[...]
    for b in range(c_ref.shape[0]):
        cb = c_ref[b]
        cb = cb * jax.nn.sigmoid(cb)
        o_ref[pl.ds(b, 1), :] = jnp.sum(cb * w, axis=0, keepdims=True) + b_ref[...]


def _ada_modulation(c, w_ada, b_ada, tn=1024):
    depth, d, n = w_ada.shape
    bsz = c.shape[0]
    return pl.pallas_call(
        _ada_kernel,
        grid=(depth, n // tn),
        in_specs=[pl.BlockSpec((bsz, d, 1), lambda l, j: (0, 0, 0)),
                  pl.BlockSpec((None, d, tn), lambda l, j: (l, 0, j)),
                  pl.BlockSpec((None, 1, tn), lambda l, j: (l, 0, j))],
        out_specs=pl.BlockSpec((None, bsz, tn), lambda l, j: (l, 0, j)),
        out_shape=jax.ShapeDtypeStruct((depth, bsz, n), F32),
        compiler_params=_params("arbitrary", "arbitrary"),
        name="ada_modulation",
    )(c[:, :, None], w_ada, b_ada[:, None, :])


def _in_proj_kernel(x_ref, g_ref, sh_ref, sc_ref, w_ref, o_ref):
    h = _rms(x_ref[...]) * g_ref[...]
    h = h * (1.0 + sc_ref[...]) + sh_ref[...]
    o_ref[...] = jnp.dot(h.astype(BF16), w_ref[...],
                         preferred_element_type=F32).astype(o_ref.dtype)


def _in_proj(x, g, mod, w_in, layer, seq, tm=256):
    t, d = x.shape
    n = w_in.shape[-1]
    per_b = seq // tm
    return pl.pallas_call(
        _in_proj_kernel,
        grid=(t // tm,),
        in_specs=[pl.BlockSpec((tm, d), lambda i: (i, 0)),
                  pl.BlockSpec((1, d), lambda i: (0, 0)),
                  pl.BlockSpec((None, 1, d), lambda i: ((i // per_b) * 6 + 0, 0, 0)),
                  pl.BlockSpec((None, 1, d), lambda i: ((i // per_b) * 6 + 1, 0, 0)),
                  pl.BlockSpec((None, d, n), lambda i: (layer, 0, 0))],
        out_specs=pl.BlockSpec((tm, n), lambda i: (i, 0)),
        out_shape=jax.ShapeDtypeStruct((t, n), BF16),
        compiler_params=_params("arbitrary"),
        name="in_proj",
    )(x, g, mod, mod, w_in)


def _mla_prep_kernel(cq_ref, ckv_ref, kr_ref, cos_ref, s1_ref, s2_ref, gql_ref, gkvl_ref,
                     wuq_ref, wukv_ref, gqn_ref, gqr_ref, gkn_ref, gkr_ref,
                     q_ref, k_ref, v_ref):
    nope_w = MLA_HEADS * MLA_NOPE
    cqn = _rms(cq_ref[...].astype(F32)) * gql_ref[...]
    q = jnp.dot(cqn.astype(BF16), wuq_ref[...], preferred_element_type=F32)
    ckvn = _rms(ckv_ref[...].astype(F32)) * gkvl_ref[...]
    kv = jnp.dot(ckvn.astype(BF16), wukv_ref[...], preferred_element_type=F32)
    cosv, s1, s2 = cos_ref[...], s1_ref[...], s2_ref[...]

    def rope(x):
        return (x * cosv + pltpu.roll(x, MLA_ROPE // 2, 1) * s1
                + pltpu.roll(x, LANE - MLA_ROPE // 2, 1) * s2)

    kr = kr_ref[...].astype(F32)
    kr_ss = jnp.sum(kr * kr, axis=-1, keepdims=True)
    kr_roped = rope(kr * gkr_ref[...])
    gqn, gqr, gkn = gqn_ref[...], gqr_ref[...], gkn_ref[...]
    for h in range(MLA_HEADS):
        qn = q[:, h * LANE:(h + 1) * LANE]
        qr = q[:, nope_w + h * LANE:nope_w + (h + 1) * LANE]
        ss = jnp.sum(qn * qn, axis=-1, keepdims=True) + jnp.sum(qr * qr, axis=-1, keepdims=True)
        r = lax.rsqrt(ss * (1.0 / MLA_QK) + EPS)
        q_ref[:, h * QK_PAD:h * QK_PAD + LANE] = (qn * r * gqn).astype(BF16)
        q_ref[:, h * QK_PAD + LANE:(h + 1) * QK_PAD] = (rope(qr * gqr) * r).astype(BF16)
        kn = kv[:, h * LANE:(h + 1) * LANE]
        rk = lax.rsqrt((jnp.sum(kn * kn, axis=-1, keepdims=True) + kr_ss) * (1.0 / MLA_QK) + EPS)
        k_ref[:, h * QK_PAD:h * QK_PAD + LANE] = (kn * rk * gkn).astype(BF16)
        k_ref[:, h * QK_PAD + LANE:(h + 1) * QK_PAD] = (kr_roped * rk).astype(BF16)
    v_ref[...] = kv[:, nope_w:].astype(BF16)


def _mla_prep(proj, rope_tabs, gql, gkvl, wuq, wukv, gqn, gqr, gkn, gkr, layer, seq, tm=256):
    t = proj.shape[0]
    per_b = seq // tm
    cos_t, s1_t, s2_t = rope_tabs
    row = lambda i: (i, 0)
    full2 = lambda i: (0, 0)
    lay = lambda i: (layer, 0, 0)
    tab = lambda i: (i % per_b, 0)
    return pl.pallas_call(
        _mla_prep_kernel,
        grid=(t // tm,),
        in_specs=[pl.BlockSpec((tm, Q_LORA), lambda i: (i, COL_CQ // Q_LORA)),
                  pl.BlockSpec((tm, KV_LORA), lambda i: (i, COL_CKV // KV_LORA)),
                  pl.BlockSpec((tm, LANE), lambda i: (i, COL_KR // LANE)),
                  pl.BlockSpec((tm, LANE), tab),
                  pl.BlockSpec((tm, LANE), tab),
                  pl.BlockSpec((tm, LANE), tab),
                  pl.BlockSpec((None, 1, Q_LORA), lay),
                  pl.BlockSpec((None, 1, KV_LORA), lay),
                  pl.BlockSpec((None,) + wuq.shape[1:], lay),
                  pl.BlockSpec((None,) + wukv.shape[1:], lay),
                  pl.BlockSpec((None, 1, LANE), lay),
                  pl.BlockSpec((None, 1, LANE), lay),
                  pl.BlockSpec((None, 1, LANE), lay),
                  pl.BlockSpec((None, 1, LANE), lay)],
        out_specs=[pl.BlockSpec((tm, MLA_HEADS * QK_PAD), row),
                   pl.BlockSpec((tm, MLA_HEADS * QK_PAD), row),
                   pl.BlockSpec((tm, MLA_HEADS * MLA_V), row)],
        out_shape=[jax.ShapeDtypeStruct((t, MLA_HEADS * QK_PAD), BF16),
                   jax.ShapeDtypeStruct((t, MLA_HEADS * QK_PAD), BF16),
                   jax.ShapeDtypeStruct((t, MLA_HEADS * MLA_V), BF16)],
        compiler_params=_params("arbitrary"),
        name="mla_prep",
    )(proj, proj, proj, cos_t, s1_t, s2_t, gql, gkvl, wuq, wukv, gqn, gqr, gkn, gkr)


def _mla_attn_kernel(q_ref, k_ref, v_ref, o_ref, m_ref, l_ref, acc_ref, *, tq):
    i = pl.program_id(2)
    q = q_ref[...]

    def scores(kb):
        k = k_ref[pl.ds(pl.multiple_of(kb * tq, tq), tq), :]
        return lax.dot_general(q, k, (((1,), (1,)), ((), ())), preferred_element_type=F32)

    s = scores(i)
    q_chunk = lax.broadcasted_iota(jnp.int32, (tq, 1), 0) // CHUNK
    k_chunk = lax.broadcasted_iota(jnp.int32, (1, tq), 1) // CHUNK
    s = jnp.where(k_chunk <= q_chunk, s, -jnp.inf)
    m = jnp.max(s, axis=-1, keepdims=True)
    p = jnp.exp(s - m)
    m_ref[...] = m
    l_ref[...] = jnp.sum(p, axis=-1, keepdims=True)
    acc_ref[...] = jnp.dot(p.astype(BF16), v_ref[pl.ds(pl.multiple_of(i * tq, tq), tq), :],
                           preferred_element_type=F32)

    def body(kb, carry):
        s = scores(kb)
        m_old = m_ref[...]
        m_new = jnp.maximum(m_old, jnp.max(s, axis=-1, keepdims=True))
        alpha = jnp.exp(m_old - m_new)
        p = jnp.exp(s - m_new)
        l_ref[...] = alpha * l_ref[...] + jnp.sum(p, axis=-1, keepdims=True)
        acc_ref[...] = alpha * acc_ref[...] + jnp.dot(
            p.astype(BF16), v_ref[pl.ds(pl.multiple_of(kb * tq, tq), tq), :],
            preferred_element_type=F32)
        m_ref[...] = m_new
        return carry

    lax.fori_loop(0, i, body, 0)
    o_ref[...] = acc_ref[...] / l_ref[...]


def _mla_attention(q2d, k2d, v2d, bsz, seq, tq=256):
    t = q2d.shape[0]
    nq = seq // tq
    return pl.pallas_call(
        functools.partial(_mla_attn_kernel, tq=tq),
        grid=(bsz, MLA_HEADS, nq),
        in_specs=[pl.BlockSpec((tq, QK_PAD), lambda b, h, i: (b * nq + i, h)),
                  pl.BlockSpec((seq, QK_PAD), lambda b, h, i: (b, h)),
                  pl.BlockSpec((seq, MLA_V), lambda b, h, i: (b, h))],
        out_specs=pl.BlockSpec((tq, MLA_V), lambda b, h, i: (b * nq + i, h)),
        out_shape=jax.ShapeDtypeStruct((t, MLA_HEADS * MLA_V), F32),
        scratch_shapes=[pltpu.VMEM((tq, 1), F32), pltpu.VMEM((tq, 1), F32),
                        pltpu.VMEM((tq, MLA_V), F32)],
        compiler_params=_params("arbitrary", "arbitrary", "arbitrary"),
        name="mla_attention",
    )(q2d, k2d, v2d)


def _sb_attn_kernel(q_ref, k_ref, v_ref, tri_ref, o_ref, run_ref, acc_ref, *, tq):
    i = pl.program_id(2)
    q = q_ref[...]
    tri = tri_ref[...]
    scale = SB_DIM ** -0.5

    def block(kb, mask):
        rows = pl.ds(pl.multiple_of(kb * tq, tq), tq)
        z = lax.dot_general(q, k_ref[rows, :], (((1,), (1,)), ((), ())),
                            preferred_element_type=F32) * scale
        log_stay = -(jnp.maximum(z, 0.0) + jnp.log(1.0 + jnp.exp(-jnp.abs(z))))
        if mask is not None:
            log_stay = jnp.where(mask, log_stay, 0.0)
        hi = log_stay.astype(BF16)
        lo = (log_stay - hi.astype(F32)).astype(BF16)
        after = (jnp.dot(hi, tri, preferred_element_type=F32)
                 + jnp.dot(lo, tri, preferred_element_type=F32)) + run_ref[...]
        a = jnp.exp(z + log_stay + after)
        if mask is not None:
            a = jnp.where(mask, a, 0.0)
        acc_ref[...] += jnp.dot(a.astype(BF16), v_ref[rows, :], preferred_element_type=F32)
        run_ref[...] += jnp.sum(log_stay, axis=-1, keepdims=True)

    run_ref[...] = jnp.zeros_like(run_ref)
    acc_ref[...] = jnp.zeros_like(acc_ref)
    qi = lax.broadcasted_iota(jnp.int32, (tq, tq), 0)
    ki = lax.broadcasted_iota(jnp.int32, (tq, tq), 1)
    block(i, ki < qi)

    def body(step, carry):
        block(i - 1 - step, None)
        return carry

    lax.fori_loop(0, i, body, 0)
    o_ref[...] = acc_ref[...]


def _sb_attention(proj, bsz, seq, tq=256):
    t = proj.shape[0]
    nq = seq // tq
    tri = (lax.broadcasted_iota(jnp.int32, (tq, tq), 0)
           > lax.broadcasted_iota(jnp.int32, (tq, tq), 1)).astype(BF16)
    cq, ck, cv = COL_SQ // SB_DIM, COL_SK // SB_DIM, COL_SV // SB_DIM
    return pl.pallas_call(
        functools.partial(_sb_attn_kernel, tq=tq),
        grid=(bsz, SB_HEADS, nq),
        in_specs=[pl.BlockSpec((tq, SB_DIM), lambda b, h, i: (b * nq + i, cq + h)),
                  pl.BlockSpec((seq, SB_DIM), lambda b, h, i: (b, ck + h)),
                  pl.BlockSpec((seq, SB_DIM), lambda b, h, i: (b, cv + h)),
                  pl.BlockSpec((tq, tq), lambda b, h, i: (0, 0))],
        out_specs=pl.BlockSpec((tq, SB_DIM), lambda b, h, i: (b * nq + i, h)),
        out_shape=jax.ShapeDtypeStruct((t, SB_WIDTH), F32),
        scratch_shapes=[pltpu.VMEM((tq, 1), F32), pltpu.VMEM((tq, SB_DIM), F32)],
        compiler_params=_params("arbitrary", "arbitrary", "arbitrary"),
        name="sb_attention",
    )(proj, proj, proj, tri)


def _out_proj_kernel(om_ref, os_ref, gm_ref, gs_ref, wo_ref, x_ref, gate_ref, g2_ref,
                     sh_ref, sc_ref, rw_ref, xo_ref, h2_ref, lg_ref):
    wm = om_ref.shape[1]
    nm = (_rms(om_ref[...]) * gm_ref[...]).astype(BF16)
    ns = (_rms(os_ref[...]) * gs_ref[...]).astype(BF16)
    y = (jnp.dot(nm, wo_ref[:wm, :], preferred_element_type=F32)
         + jnp.dot(ns, wo_ref[wm:, :], preferred_element_type=F32))
    xn = x_ref[...] + gate_ref[...] * y
    xo_ref[...] = xn
    h2 = _rms(xn) * g2_ref[...]
    h2 = h2 * (1.0 + sc_ref[...]) + sh_ref[...]
    h2_ref[...] = h2
    lg_ref[...] = lax.dot_general(rw_ref[...], h2, (((1,), (1,)), ((), ())),
                                  precision=lax.Precision.HIGHEST,
                                  preferred_element_type=F32)


def _out_proj(o_mla, o_sb, gm, gs, w_o, x, mod, g2, rw_t, layer, seq, tm=256):
    t, d = x.shape
    per_b = seq // tm
    row = lambda i: (i, 0)
    full2 = lambda i: (0, 0)
    modspec = lambda j: pl.BlockSpec((None, 1, d), lambda i: ((i // per_b) * 6 + j, 0, 0))
    return pl.pallas_call(
        _out_proj_kernel,
        grid=(t // tm,),
        in_specs=[pl.BlockSpec((tm, o_mla.shape[1]), row),
                  pl.BlockSpec((tm, o_sb.shape[1]), row),
                  pl.BlockSpec((1, o_mla.shape[1]), full2),
                  pl.BlockSpec((1, o_sb.shape[1]), full2),
                  pl.BlockSpec((None,) + w_o.shape[1:], lambda i: (layer, 0, 0)),
                  pl.BlockSpec((tm, d), row),
                  modspec(2),
                  pl.BlockSpec((1, d), full2),
                  modspec(3),
                  modspec(4),
                  pl.BlockSpec(rw_t.shape, full2)],
        out_specs=[pl.BlockSpec((tm, d), row),
                   pl.BlockSpec((tm, d), row),
                   pl.BlockSpec((N_EXPERTS, tm), lambda i: (0, i))],
        out_shape=[jax.ShapeDtypeStruct((t, d), F32),
                   jax.ShapeDtypeStruct((t, d), F32),
                   jax.ShapeDtypeStruct((N_EXPERTS, t), F32)],
        compiler_params=_params("arbitrary"),
        name="out_proj",
    )(o_mla, o_sb, gm, gs, w_o, x, mod, g2, mod, mod, rw_t)


def _route_kernel(lg_ref, rb_ref, tri_ref, e_ref, w_ref, rank_ref, cnt_ref, run_ref):
    i = pl.program_id(0)

    @pl.when(i == 0)
    def _():
        run_ref[...] = jnp.zeros_like(run_ref)

    scores = jax.nn.sigmoid(lg_ref[...])
    sel = scores + rb_ref[...]
    tm = sel.shape[1]
    epg = EXPERTS_PER_GROUP
    srow = [sel[e:e + 1, :] for e in range(N_EXPERTS)]
    prow = [scores[e:e + 1, :] for e in range(N_EXPERTS)]

    gscore = []
    for g in range(N_GROUPS):
        a, b, c, d = srow[g * epg:(g + 1) * epg]
        hi1, lo1 = jnp.maximum(a, b), jnp.minimum(a, b)
        hi2, lo2 = jnp.maximum(c, d), jnp.minimum(c, d)
        gscore.append(jnp.maximum(hi1, hi2)
                      + jnp.maximum(jnp.minimum(hi1, hi2), jnp.maximum(lo1, lo2)))
    best = gscore[0]
    gidx = jnp.zeros((1, tm), jnp.int32)
    for g in range(1, N_GROUPS):
        better = gscore[g] > best
        gidx = jnp.where(better, g, gidx)
        best = jnp.where(better, gscore[g], best)

    def pick(rows, j):
        out = rows[j]
        for g in range(1, N_GROUPS):
            out = jnp.where(gidx == g, rows[g * epg + j], out)
        return out

    v = [pick(srow, j) for j in range(epg)]
    p = [pick(prow, j) for j in range(epg)]
    pos = []
    for j in range(epg):
        r = jnp.zeros((1, tm), jnp.int32)
        for o in range(epg):
            if o == j:
                continue
            ahead = (v[o] > v[j]) | ((v[o] == v[j]) & (o < j))
            r = r + ahead.astype(jnp.int32)
        pos.append(r)
    local, wsel = [], []
    for k in range(TOP_K):
        lk = jnp.zeros((1, tm), jnp.int32)
        wk = jnp.zeros((1, tm), F32)
        for j in range(epg):
            hit = pos[j] == k
            lk = jnp.where(hit, j, lk)
            wk = jnp.where(hit, p[j], wk)
        local.append(lk)
        wsel.append(wk)
    wsum = wsel[0] + wsel[1]
    idx = [gidx * epg + local[k] for k in range(TOP_K)]

    eiota = lax.broadcasted_iota(jnp.int32, (N_EXPERTS, tm), 0)
    hit = [eiota == idx[k] for k in range(TOP_K)]
    onehot = (hit[0] | hit[1]).astype(F32)
    before = jnp.dot(onehot.astype(BF16), tri_ref[...], preferred_element_type=F32) + run_ref[...]
    for k in range(TOP_K):
        e_ref[pl.ds(k, 1), :] = idx[k]
        w_ref[pl.ds(k, 1), :] = wsel[k] / wsum
        rank_ref[pl.ds(k, 1), :] = jnp.sum(jnp.where(hit[k], before, 0.0), axis=0,
                                           keepdims=True).astype(jnp.int32)
    run_ref[...] += jnp.sum(onehot, axis=1, keepdims=True)
    cnt_ref[...] = jnp.broadcast_to(run_ref[...], cnt_ref.shape)


def _route(logits_t, router_b, tm=512):
    t = logits_t.shape[1]
    tm = min(tm, t)
    tri = (lax.broadcasted_iota(jnp.int32, (tm, tm), 0)
           < lax.broadcasted_iota(jnp.int32, (tm, tm), 1)).astype(BF16)
    blk = lambda rows: pl.BlockSpec((rows, tm), lambda i: (0, i))
    return pl.pallas_call(
        _route_kernel,
        grid=(t // tm,),
        in_specs=[blk(N_EXPERTS),
                  pl.BlockSpec((N_EXPERTS, 1), lambda i: (0, 0)),
                  pl.BlockSpec((tm, tm), lambda i: (0, 0))],
        out_specs=[blk(TOP_K), blk(TOP_K), blk(TOP_K),
                   pl.BlockSpec((N_EXPERTS, LANE), lambda i: (0, 0))],
        out_shape=[jax.ShapeDtypeStruct((TOP_K, t), jnp.int32),
                   jax.ShapeDtypeStruct((TOP_K, t), F32),
                   jax.ShapeDtypeStruct((TOP_K, t), jnp.int32),
                   jax.ShapeDtypeStruct((N_EXPERTS, LANE), F32)],
        scratch_shapes=[pltpu.VMEM((N_EXPERTS, 1), F32)],
        compiler_params=_params("arbitrary"),
        name="route",
    )(logits_t, router_b[:, None], tri)


def _dispatch_kernel(dest_ref, h_ref, init_ref, o_ref, sem, *, tm, t):
    del init_ref
    base = pl.program_id(0) * tm

    def issue(r, carry):
        for k in range(TOP_K):
            d = dest_ref[k * t + base + r]
            pltpu.make_async_copy(h_ref.at[pl.ds(r, 1), :], o_ref.at[pl.ds(d, 1), :], sem).start()
        return carry

    lax.fori_loop(0, tm, issue, 0)
    for k in range(TOP_K):
        pltpu.make_async_copy(h_ref, o_ref.at[pl.ds(0, tm), :], sem).wait()


def _dispatch(h2, dest_flat, rows, tm=256):
    t, d = h2.shape
    init = jnp.zeros((rows, d), h2.dtype)
    return pl.pallas_call(
        functools.partial(_dispatch_kernel, tm=tm, t=t),
        grid_spec=pltpu.PrefetchScalarGridSpec(
            num_scalar_prefetch=1,
            grid=(t // tm,),
            in_specs=[pl.BlockSpec((tm, d), lambda i, dest: (i, 0)),
                      pl.BlockSpec(memory_space=pl.ANY)],
            out_specs=pl.BlockSpec(memory_space=pl.ANY),
            scratch_shapes=[pltpu.SemaphoreType.DMA(())]),
        out_shape=jax.ShapeDtypeStruct((rows, d), h2.dtype),
        input_output_aliases={2: 0},
        compiler_params=_params("arbitrary"),
        name="moe_dispatch",
    )(dest_flat, h2, init)


def _expert_kernel(be_ref, nu_ref, x_ref, wg_ref, wu_ref, wd_ref, y_ref, wgb, wub, wdb):
    r = pl.program_id(0)
    live = r < nu_ref[0]
    new_expert = (r == 0) | (be_ref[jnp.maximum(r - 1, 0)] != be_ref[r])

    @pl.when(live & new_expert)
    def _():
        wgb[...] = wg_ref[...].astype(BF16)
        wub[...] = wu_ref[...].astype(BF16)
        wdb[...] = wd_ref[...].astype(BF16)

    @pl.when(live)
    def _():
        xb = x_ref[...].astype(BF16)
        g = jnp.dot(xb, wgb[...], preferred_element_type=F32)
        u = jnp.dot(xb, wub[...], preferred_element_type=F32)
        a = (g * jax.nn.sigmoid(g)) * u
        y_ref[...] = jnp.dot(a.astype(BF16), wdb[...], preferred_element_type=F32)

    @pl.when(jnp.logical_not(live))
    def _():
        y_ref[...] = jnp.zeros_like(y_ref)


def _experts(x_sorted, block_e, n_used, w_gate, w_up, w_down, layer):
    rows, d = x_sorted.shape
    de = w_gate.shape[-1]
    bm = MOE_ROWS
    return pl.pallas_call(
        _expert_kernel,
        grid_spec=pltpu.PrefetchScalarGridSpec(
            num_scalar_prefetch=2,
            grid=(rows // bm,),
            in_specs=[pl.BlockSpec((bm, d), lambda r, be, nu: (r, 0)),
                      pl.BlockSpec((None, None, d, de), lambda r, be, nu: (layer, be[r], 0, 0)),
                      pl.BlockSpec((None, None, d, de), lambda r, be, nu: (layer, be[r], 0, 0)),
                      pl.BlockSpec((None, None, de, d), lambda r, be, nu: (layer, be[r], 0, 0))],
            out_specs=pl.BlockSpec((bm, d), lambda r, be, nu: (r, 0)),
            scratch_shapes=[pltpu.VMEM((d, de), BF16), pltpu.VMEM((d, de), BF16),
                            pltpu.VMEM((de, d), BF16)]),
        out_shape=jax.ShapeDtypeStruct((rows, d), F32),
        compiler_params=_params("arbitrary"),
        name="moe_experts",
    )(block_e, n_used, x_sorted, w_gate, w_up, w_down)


def _combine_kernel(dest_ref, y_ref, x_ref, gate_ref, w_ref, o_ref, buf, sem, *, tm, t):
    base = pl.program_id(0) * tm

    def issue(r, carry):
        for k in range(TOP_K):
            d = dest_ref[k * t + base + r]
            pltpu.make_async_copy(y_ref.at[pl.ds(d, 1), :], buf.at[k, pl.ds(r, 1), :], sem).start()
        return carry

    lax.fori_loop(0, tm, issue, 0)
    for k in range(TOP_K):
        pltpu.make_async_copy(y_ref.at[pl.ds(0, tm), :], buf.at[k], sem).wait()
    w = w_ref[...]
    y = w[:, 0:1] * buf[0] + w[:, 1:2] * buf[1]
    o_ref[...] = x_ref[...] + gate_ref[...] * y


def _combine(y_sorted, dest_flat, x, mod, w_tok, seq, tm=256):
    t, d = x.shape
    per_b = seq // tm
    return pl.pallas_call(
        functools.partial(_combine_kernel, tm=tm, t=t),
        grid_spec=pltpu.PrefetchScalarGridSpec(
            num_scalar_prefetch=1,
            grid=(t // tm,),
            in_specs=[pl.BlockSpec(memory_space=pl.ANY),
                      pl.BlockSpec((tm, d), lambda i, dest: (i, 0)),
                      pl.BlockSpec((None, 1, d), lambda i, dest: ((i // per_b) * 6 + 5, 0, 0)),
                      pl.BlockSpec((tm, TOP_K), lambda i, dest: (i, 0))],
            out_specs=pl.BlockSpec((tm, d), lambda i, dest: (i, 0)),
            scratch_shapes=[pltpu.VMEM((TOP_K, tm, d), F32), pltpu.SemaphoreType.DMA(())]),
        out_shape=jax.ShapeDtypeStruct((t, d), F32),
        compiler_params=_params("arbitrary"),
        name="moe_combine",
    )(dest_flat, y_sorted, x, mod, w_tok)


def _rope_tables(seq):
    half = MLA_ROPE // 2
    inv = 1.0 / (ROPE_BASE ** (jnp.arange(0, MLA_ROPE, 2, dtype=F32) / MLA_ROPE))
    ang = jnp.arange(seq, dtype=F32)[:, None] * inv[None, :]
    cos, sin = jnp.cos(ang), jnp.sin(ang)
    z = jnp.zeros((seq, half), F32)
    zz = jnp.zeros((seq, LANE - MLA_ROPE), F32)
    cos_t = jnp.concatenate([cos, cos, zz], axis=1)
    s1_t = jnp.concatenate([z, sin, zz], axis=1)
    s2_t = jnp.concatenate([-sin, z, zz], axis=1)
    return cos_t, s1_t, s2_t


def _pad_last(a, width):
    return jnp.pad(a, [(0, 0)] * (a.ndim - 1) + [(0, width - a.shape[-1])])


def kernel(x, c, norm1_g, norm2_g, w_ada, b_ada, w_in, q_lora_g, kv_lora_g, w_uq, w_ukv,
           q_head_g, k_head_g, mla_out_g, sb_out_g, w_o, router_w, router_b,
           w_gate, w_up, w_down):
    bsz, seq, d = x.shape
    t = bsz * seq
    depth = w_ada.shape[0]

    w_in_p = jnp.concatenate(
        [w_in[..., :COL_KR], _pad_last(w_in[..., COL_KR:COL_KR + MLA_ROPE], LANE),
         w_in[..., COL_KR + MLA_ROPE:]], axis=-1).astype(BF16)
    wq = w_uq.reshape(depth, Q_LORA, MLA_HEADS, MLA_QK)
    wuq_p = jnp.concatenate(
        [wq[..., :MLA_NOPE].reshape(depth, Q_LORA, -1),
         _pad_last(wq[..., MLA_NOPE:], LANE).reshape(depth, Q_LORA, -1)], axis=-1).astype(BF16)
    wkv = w_ukv.reshape(depth, KV_LORA, MLA_HEADS, MLA_NOPE + MLA_V)
    wukv_p = jnp.concatenate(
        [wkv[..., :MLA_NOPE].reshape(depth, KV_LORA, -1),
         wkv[..., MLA_NOPE:].reshape(depth, KV_LORA, -1)], axis=-1).astype(BF16)
    w_o_b = w_o.astype(BF16)
    q_scale = MLA_QK ** -0.5
    gqn = (q_head_g[:, None, :MLA_NOPE] * q_scale)
    gqr = _pad_last(q_head_g[:, None, MLA_NOPE:] * q_scale, LANE)
    gkn = k_head_g[:, None, :MLA_NOPE]
    gkr = _pad_last(k_head_g[:, None, MLA_NOPE:], LANE)
    rope_tabs = _rope_tables(seq)
    rw_t = router_w.T

    mod_all = _ada_modulation(c, w_ada, b_ada)
    n_blocks = -(-(t * TOP_K) // MOE_ROWS) + N_EXPERTS
    rows = n_blocks * MOE_ROWS

    xt = x.reshape(t, d)
    for l in range(depth):
        mod = mod_all[l].reshape(bsz * 6, 1, d)
        proj = _in_proj(xt, norm1_g[l][None, :], mod, w_in_p, l, seq)
        q2d, k2d, v2d = _mla_prep(proj, rope_tabs, q_lora_g[:, None, :], kv_lora_g[:, None, :],
                                  wuq_p, wukv_p, gqn, gqr, gkn, gkr, l, seq)
        o_mla = _mla_attention(q2d, k2d, v2d, bsz, seq)
        o_sb = _sb_attention(proj, bsz, seq)
        xt, h2, logits_t = _out_proj(o_mla, o_sb, mla_out_g[l][None, :], sb_out_g[l][None, :],
                                     w_o_b, xt, mod, norm2_g[l][None, :], rw_t, l, seq)
        e_idx, w_tok, rank, cnt = _route(logits_t, router_b)
        counts = cnt[:, 0].astype(jnp.int32)
        padded = (counts + MOE_ROWS - 1) // MOE_ROWS * MOE_ROWS
        pad_end = jnp.cumsum(padded)
        pad_start = pad_end - padded
        dest = (pad_start[e_idx] + rank).reshape(TOP_K * t).astype(jnp.int32)
        block_e = jnp.minimum(
            jnp.searchsorted(pad_end, jnp.arange(n_blocks) * MOE_ROWS, side='right'),
            N_EXPERTS - 1).astype(jnp.int32)
        n_used = (pad_end[-1:] // MOE_ROWS).astype(jnp.int32)
        x_sorted = _dispatch(h2, dest, rows)
        y_sorted = _experts(x_sorted, block_e, n_used, w_gate, w_up, w_down, l)
        xt = _combine(y_sorted, dest, xt, mod, w_tok.T, seq)
    return xt.reshape(bsz, seq, d)
```

```python
import functools

import jax
import jax.numpy as jnp
from jax import lax
from jax.experimental import pallas as pl
from jax.experimental.pallas import tpu as pltpu

F32 = jnp.float32
BF16 = jnp.bfloat16

EPS = 1e-6
ROPE_BASE = 10000.0
CHUNK = 64

MLA_HEADS = 8
MLA_NOPE = 128
MLA_ROPE = 64
MLA_QK = MLA_NOPE + MLA_ROPE
MLA_V = 128
Q_LORA = 512
KV_LORA = 256
SB_HEADS = 8
SB_DIM = 128
SB_WIDTH = SB_HEADS * SB_DIM
LANE = 128
QK_PAD = 2 * LANE
COL_CQ = 0
COL_CKV = Q_LORA
COL_KR = Q_LORA + KV_LORA
COL_SQ = COL_KR + LANE
COL_SK = COL_SQ + SB_WIDTH
COL_SV = COL_SK + SB_WIDTH
IN_COLS_PAD = COL_SV + SB_WIDTH

N_EXPERTS = 16
N_GROUPS = 4
EXPERTS_PER_GROUP = N_EXPERTS // N_GROUPS
TOP_K = 2
D_EXPERT = 512
MOE_ROWS = 256
LOG2_E = 1.4426950408889634
SB_UNDERFLOW_LOG2 = -150.0

VMEM_LIMIT = 56 * 1024 * 1024


def _params(*dims):
    return pltpu.CompilerParams(dimension_semantics=dims, vmem_limit_bytes=VMEM_LIMIT)


def _rms(x):
    return x * lax.rsqrt(jnp.mean(x * x, axis=-1, keepdims=True) + EPS)


def _ada_kernel(c_ref, w_ref, b_ref, o_ref):
    w = w_ref[...]
    for b in range(c_ref.shape[0]):
        cb = c_ref[b]
        cb = cb * jax.nn.sigmoid(cb)
        o_ref[pl.ds(b, 1), :] = jnp.sum(cb * w, axis=0, keepdims=True) + b_ref[...]


def _ada_modulation(c, w_ada, b_ada, tn=1024):
    depth, d, n = w_ada.shape
    bsz = c.shape[0]
    return pl.pallas_call(
        _ada_kernel,
        grid=(depth, n // tn),
        in_specs=[pl.BlockSpec((bsz, d, 1), lambda l, j: (0, 0, 0)),
                  pl.BlockSpec((None, d, tn), lambda l, j: (l, 0, j)),
                  pl.BlockSpec((None, 1, tn), lambda l, j: (l, 0, j))],
        out_specs=pl.BlockSpec((None, bsz, tn), lambda l, j: (l, 0, j)),
        out_shape=jax.ShapeDtypeStruct((depth, bsz, n), F32),
        compiler_params=_params("arbitrary", "arbitrary"),
        name="ada_modulation",
    )(c[:, :, None], w_ada, b_ada[:, None, :])


def _in_proj_kernel(x_ref, g_ref, sh_ref, sc_ref, w_ref, o_ref):
    h = _rms(x_ref[...]) * g_ref[...]
    h = h * (1.0 + sc_ref[...]) + sh_ref[...]
    o_ref[...] = jnp.dot(h.astype(BF16), w_ref[...],
                         preferred_element_type=F32).astype(o_ref.dtype)


def _in_proj(x, g, mod, w_in, layer, seq, tm=256):
    t, d = x.shape
    n = w_in.shape[-1]
    per_b = seq // tm
    return pl.pallas_call(
        _in_proj_kernel,
        grid=(t // tm,),
        in_specs=[pl.BlockSpec((tm, d), lambda i: (i, 0)),
                  pl.BlockSpec((1, d), lambda i: (0, 0)),
                  pl.BlockSpec((None, 1, d), lambda i: ((i // per_b) * 6 + 0, 0, 0)),
                  pl.BlockSpec((None, 1, d), lambda i: ((i // per_b) * 6 + 1, 0, 0)),
                  pl.BlockSpec((None, d, n), lambda i: (layer, 0, 0))],
        out_specs=pl.BlockSpec((tm, n), lambda i: (i, 0)),
        out_shape=jax.ShapeDtypeStruct((t, n), BF16),
        compiler_params=_params("arbitrary"),
        name="in_proj",
    )(x, g, mod, mod, w_in)


def _mla_prep_kernel(cq_ref, ckv_ref, kr_ref, cos_ref, s1_ref, s2_ref, gql_ref, gkvl_ref,
                     wuq_ref, wukv_ref, gqn_ref, gqr_ref, gkn_ref, gkr_ref,
                     q_ref, k_ref, v_ref):
    nope_w = MLA_HEADS * MLA_NOPE
    cqn = _rms(cq_ref[...].astype(F32)) * gql_ref[...]
    q = jnp.dot(cqn.astype(BF16), wuq_ref[...], preferred_element_type=F32)
    ckvn = _rms(ckv_ref[...].astype(F32)) * gkvl_ref[...]
    kv = jnp.dot(ckvn.astype(BF16), wukv_ref[...], preferred_element_type=F32)
    cosv, s1, s2 = cos_ref[...], s1_ref[...], s2_ref[...]

    def rope(x):
        return (x * cosv + pltpu.roll(x, MLA_ROPE // 2, 1) * s1
                + pltpu.roll(x, LANE - MLA_ROPE // 2, 1) * s2)

    kr = kr_ref[...].astype(F32)
    kr_ss = jnp.sum(kr * kr, axis=-1, keepdims=True)
    kr_roped = rope(kr * gkr_ref[...])
    gqn, gqr, gkn = gqn_ref[...], gqr_ref[...], gkn_ref[...]
    for h in range(MLA_HEADS):
        qn = q[:, h * LANE:(h + 1) * LANE]
        qr = q[:, nope_w + h * LANE:nope_w + (h + 1) * LANE]
        ss = jnp.sum(qn * qn, axis=-1, keepdims=True) + jnp.sum(qr * qr, axis=-1, keepdims=True)
        r = lax.rsqrt(ss * (1.0 / MLA_QK) + EPS)
        q_ref[:, h * QK_PAD:h * QK_PAD + LANE] = (qn * r * gqn).astype(BF16)
        q_ref[:, h * QK_PAD + LANE:(h + 1) * QK_PAD] = (rope(qr * gqr) * r).astype(BF16)
        kn = kv[:, h * LANE:(h + 1) * LANE]
        rk = lax.rsqrt((jnp.sum(kn * kn, axis=-1, keepdims=True) + kr_ss) * (1.0 / MLA_QK) + EPS)
        k_ref[:, h * QK_PAD:h * QK_PAD + LANE] = (kn * rk * gkn).astype(BF16)
        k_ref[:, h * QK_PAD + LANE:(h + 1) * QK_PAD] = (kr_roped * rk).astype(BF16)
        v_ref[:, 2 * h * MLA_V:(2 * h + 1) * MLA_V] = (
            kv[:, nope_w + h * MLA_V:nope_w + (h + 1) * MLA_V].astype(BF16))
        v_ref[:, (2 * h + 1) * MLA_V:(2 * h + 2) * MLA_V] = jnp.ones((kn.shape[0], MLA_V), BF16)


def _mla_prep(proj, rope_tabs, gql, gkvl, wuq, wukv, gqn, gqr, gkn, gkr, layer, seq, tm=256):
    t = proj.shape[0]
    per_b = seq // tm
    cos_t, s1_t, s2_t = rope_tabs
    row = lambda i: (i, 0)
    full2 = lambda i: (0, 0)
    lay = lambda i: (layer, 0, 0)
    tab = lambda i: (i % per_b, 0)
    return pl.pallas_call(
        _mla_prep_kernel,
        grid=(t // tm,),
        in_specs=[pl.BlockSpec((tm, Q_LORA), lambda i: (i, COL_CQ // Q_LORA)),
                  pl.BlockSpec((tm, KV_LORA), lambda i: (i, COL_CKV // KV_LORA)),
                  pl.BlockSpec((tm, LANE), lambda i: (i, COL_KR // LANE)),
                  pl.BlockSpec((tm, LANE), tab),
                  pl.BlockSpec((tm, LANE), tab),
                  pl.BlockSpec((tm, LANE), tab),
                  pl.BlockSpec((None, 1, Q_LORA), lay),
                  pl.BlockSpec((None, 1, KV_LORA), lay),
                  pl.BlockSpec((None,) + wuq.shape[1:], lay),
                  pl.BlockSpec((None,) + wukv.shape[1:], lay),
                  pl.BlockSpec((None, 1, LANE), lay),
                  pl.BlockSpec((None, 1, LANE), lay),
                  pl.BlockSpec((None, 1, LANE), lay),
                  pl.BlockSpec((None, 1, LANE), lay)],
        out_specs=[pl.BlockSpec((tm, MLA_HEADS * QK_PAD), row),
                   pl.BlockSpec((tm, MLA_HEADS * QK_PAD), row),
                   pl.BlockSpec((tm, MLA_HEADS * 2 * MLA_V), row)],
        out_shape=[jax.ShapeDtypeStruct((t, MLA_HEADS * QK_PAD), BF16),
                   jax.ShapeDtypeStruct((t, MLA_HEADS * QK_PAD), BF16),
                   jax.ShapeDtypeStruct((t, MLA_HEADS * 2 * MLA_V), BF16)],
        compiler_params=_params("arbitrary"),
        name="mla_prep",
    )(proj, proj, proj, cos_t, s1_t, s2_t, gql, gkvl, wuq, wukv, gqn, gqr, gkn, gkr)


def _mla_attn_kernel(q_ref, k_ref, v_ref, o_ref, m_ref, l_ref, acc_ref, *, tq):
    i = pl.program_id(2)
    q = q_ref[...]

    def scores(kb):
        k = k_ref[pl.ds(pl.multiple_of(kb * tq, tq), tq), :]
        return lax.dot_general(q, k, (((1,), (1,)), ((), ())), preferred_element_type=F32)

    def probs(s, m):
        return jnp.concatenate(
            [jnp.exp2(s[:, c * LANE:(c + 1) * LANE] - m) for c in range(tq // LANE)],
            axis=1).astype(BF16)

    def pv(p, kb):
        return jnp.dot(p, v_ref[pl.ds(pl.multiple_of(kb * tq, tq), tq), :],
                       preferred_element_type=F32)

    s = scores(i)
    q_chunk = lax.broadcasted_iota(jnp.int32, (tq, 1), 0) // CHUNK
    k_chunk = lax.broadcasted_iota(jnp.int32, (1, tq), 1) // CHUNK
    s = jnp.where(k_chunk <= q_chunk, s, -jnp.inf)
    m = jnp.broadcast_to(jnp.max(s, axis=-1, keepdims=True), (tq, LANE))
    r = pv(probs(s, m), i)
    m_ref[...] = m
    acc_ref[...] = r[:, :MLA_V]
    l_ref[...] = r[:, MLA_V:]

    def body(kb, carry):
        s = scores(kb)
        m_old = m_ref[...]
        m_new = jnp.maximum(m_old, jnp.max(s, axis=-1, keepdims=True))
        alpha = jnp.exp2(m_old - m_new)
        r = pv(probs(s, m_new), kb)
        acc_ref[...] = alpha * acc_ref[...] + r[:, :MLA_V]
        l_ref[...] = alpha * l_ref[...] + r[:, MLA_V:]
        m_ref[...] = m_new
        return carry

    lax.fori_loop(0, i, body, 0)
    o_ref[...] = acc_ref[...] / l_ref[...]


def _mla_attention(q2d, k2d, v2d, bsz, seq, tq=512):
    t = q2d.shape[0]
    tq = min(tq, seq)
    nq = seq // tq
    vw = 2 * MLA_V
    return pl.pallas_call(
        functools.partial(_mla_attn_kernel, tq=tq),
        grid=(bsz, MLA_HEADS, nq),
        in_specs=[pl.BlockSpec((tq, QK_PAD), lambda b, h, i: (b * nq + i, h)),
                  pl.BlockSpec((seq, QK_PAD), lambda b, h, i: (b, h)),
                  pl.BlockSpec((seq, vw), lambda b, h, i: (b, h))],
        out_specs=pl.BlockSpec((tq, MLA_V), lambda b, h, i: (b * nq + i, h)),
        out_shape=jax.ShapeDtypeStruct((t, MLA_HEADS * MLA_V), F32),
        scratch_shapes=[pltpu.VMEM((tq, LANE), F32), pltpu.VMEM((tq, MLA_V), F32),
                        pltpu.VMEM((tq, MLA_V), F32)],
        compiler_params=_params("arbitrary", "arbitrary", "arbitrary"),
        name="mla_attention",
    )(q2d, k2d, v2d)


def _sb_attn_kernel(q_ref, k_ref, v_ref, tri_ref, o_ref, run_ref, acc_ref, *, tq, tk):
    i = pl.program_id(2)
    q = q_ref[...]
    tri = tri_ref[...]
    sub = tq // tk

    def block(kb, mask):
        rows = pl.ds(pl.multiple_of(kb * tk, tk), tk)
        z = lax.dot_general(q, k_ref[rows, :], (((1,), (1,)), ((), ())),
                            preferred_element_type=F32)
        log_stay = -(jnp.maximum(z, 0.0) + jnp.log2(1.0 + jnp.exp2(-jnp.abs(z))))
        if mask is not None:
            log_stay = jnp.where(mask, log_stay, 0.0)
        hi = log_stay.astype(BF16)
        lo = (log_stay - hi.astype(F32)).astype(BF16)
        after = (jnp.dot(hi, tri, preferred_element_type=F32)
                 + jnp.dot(lo, tri, preferred_element_type=F32)) + run_ref[...]
        a = jnp.exp2(z + log_stay + after)
        if mask is not None:
            a = jnp.where(mask, a, 0.0)
        acc_ref[...] += jnp.dot(a.astype(BF16), v_ref[rows, :], preferred_element_type=F32)
        run_ref[...] += jnp.sum(log_stay, axis=-1, keepdims=True)

    def more():
        return jnp.max(run_ref[...]) > SB_UNDERFLOW_LOG2

    run_ref[...] = jnp.zeros_like(run_ref)
    acc_ref[...] = jnp.zeros_like(acc_ref)
    qi = lax.broadcasted_iota(jnp.int32, (tq, tk), 0)
    ki = lax.broadcasted_iota(jnp.int32, (tq, tk), 1)
    for j in reversed(range(sub)):
        block(i * sub + j, ki + j * tk < qi)

    def cond(c):
        return (c[0] < i * sub) & c[1]

    def body(c):
        block(i * sub - 1 - c[0], None)
        return c[0] + 1, more()

    lax.while_loop(cond, body, (jnp.int32(0), more()))
    o_ref[...] = acc_ref[...]


def _sb_attention(proj, bsz, seq, tq=512, tk=256):
    t = proj.shape[0]
    tq = min(tq, seq)
    nq = seq // tq
    tri = (lax.broadcasted_iota(jnp.int32, (tk, tk), 0)
           > lax.broadcasted_iota(jnp.int32, (tk, tk), 1)).astype(BF16)
    cq, ck, cv = COL_SQ // SB_DIM, COL_SK // SB_DIM, COL_SV // SB_DIM
    return pl.pallas_call(
        functools.partial(_sb_attn_kernel, tq=tq, tk=tk),
        grid=(bsz, SB_HEADS, nq),
        in_specs=[pl.BlockSpec((tq, SB_DIM), lambda b, h, i: (b * nq + i, cq + h)),
                  pl.BlockSpec((seq, SB_DIM), lambda b, h, i: (b, ck + h)),
                  pl.BlockSpec((seq, SB_DIM), lambda b, h, i: (b, cv + h)),
                  pl.BlockSpec((tk, tk), lambda b, h, i: (0, 0))],
        out_specs=pl.BlockSpec((tq, SB_DIM), lambda b, h, i: (b * nq + i, h)),
        out_shape=jax.ShapeDtypeStruct((t, SB_WIDTH), F32),
        scratch_shapes=[pltpu.VMEM((tq, 1), F32), pltpu.VMEM((tq, SB_DIM), F32)],
        compiler_params=_params("arbitrary", "arbitrary", "arbitrary"),
        name="sb_attention",
    )(proj, proj, proj, tri)


def _out_proj_kernel(om_ref, os_ref, gm_ref, gs_ref, wo_ref, x_ref, gate_ref, g2_ref,
                     sh_ref, sc_ref, rw_ref, xo_ref, h2_ref, lg_ref):
    wm = om_ref.shape[1]
    nm = (_rms(om_ref[...]) * gm_ref[...]).astype(BF16)
    ns = (_rms(os_ref[...]) * gs_ref[...]).astype(BF16)
    y = (jnp.dot(nm, wo_ref[:wm, :], preferred_element_type=F32)
         + jnp.dot(ns, wo_ref[wm:, :], preferred_element_type=F32))
    xn = x_ref[...] + gate_ref[...] * y
    xo_ref[...] = xn
    h2 = _rms(xn) * g2_ref[...]
    h2 = h2 * (1.0 + sc_ref[...]) + sh_ref[...]
    h2_ref[...] = h2
    lg_ref[...] = lax.dot_general(rw_ref[...], h2, (((1,), (1,)), ((), ())),
                                  precision=lax.Precision.HIGHEST,
                                  preferred_element_type=F32)


def _out_proj(o_mla, o_sb, gm, gs, w_o, x, mod, g2, rw_t, layer, seq, tm=256):
    t, d = x.shape
    per_b = seq // tm
    row = lambda i: (i, 0)
    full2 = lambda i: (0, 0)
    modspec = lambda j: pl.BlockSpec((None, 1, d), lambda i: ((i // per_b) * 6 + j, 0, 0))
    return pl.pallas_call(
        _out_proj_kernel,
        grid=(t // tm,),
        in_specs=[pl.BlockSpec((tm, o_mla.shape[1]), row),
                  pl.BlockSpec((tm, o_sb.shape[1]), row),
                  pl.BlockSpec((1, o_mla.shape[1]), full2),
                  pl.BlockSpec((1, o_sb.shape[1]), full2),
                  pl.BlockSpec((None,) + w_o.shape[1:], lambda i: (layer, 0, 0)),
                  pl.BlockSpec((tm, d), row),
                  modspec(2),
                  pl.BlockSpec((1, d), full2),
                  modspec(3),
                  modspec(4),
                  pl.BlockSpec(rw_t.shape, full2)],
        out_specs=[pl.BlockSpec((tm, d), row),
                   pl.BlockSpec((tm, d), row),
                   pl.BlockSpec((N_EXPERTS, tm), lambda i: (0, i))],
        out_shape=[jax.ShapeDtypeStruct((t, d), F32),
                   jax.ShapeDtypeStruct((t, d), F32),
                   jax.ShapeDtypeStruct((N_EXPERTS, t), F32)],
        compiler_params=_params("arbitrary"),
        name="out_proj",
    )(o_mla, o_sb, gm, gs, w_o, x, mod, g2, mod, mod, rw_t)


def _route_kernel(lg_ref, rb_ref, tri_ref, e_ref, w_ref, rank_ref, cnt_ref, run_ref):
    i = pl.program_id(0)

    @pl.when(i == 0)
    def _():
        run_ref[...] = jnp.zeros_like(run_ref)

    scores = jax.nn.sigmoid(lg_ref[...])
    sel = scores + rb_ref[...]
    tm = sel.shape[1]
    epg = EXPERTS_PER_GROUP
    srow = [sel[e:e + 1, :] for e in range(N_EXPERTS)]
    prow = [scores[e:e + 1, :] for e in range(N_EXPERTS)]

    gscore = []
    for g in range(N_GROUPS):
        a, b, c, d = srow[g * epg:(g + 1) * epg]
        hi1, lo1 = jnp.maximum(a, b), jnp.minimum(a, b)
        hi2, lo2 = jnp.maximum(c, d), jnp.minimum(c, d)
        gscore.append(jnp.maximum(hi1, hi2)
                      + jnp.maximum(jnp.minimum(hi1, hi2), jnp.maximum(lo1, lo2)))
    best = gscore[0]
    gidx = jnp.zeros((1, tm), jnp.int32)
    for g in range(1, N_GROUPS):
        better = gscore[g] > best
        gidx = jnp.where(better, g, gidx)
        best = jnp.where(better, gscore[g], best)

    def pick(rows, j):
        out = rows[j]
        for g in range(1, N_GROUPS):
            out = jnp.where(gidx == g, rows[g * epg + j], out)
        return out

    v = [pick(srow, j) for j in range(epg)]
    p = [pick(prow, j) for j in range(epg)]
    pos = []
    for j in range(epg):
        r = jnp.zeros((1, tm), jnp.int32)
        for o in range(epg):
            if o == j:
                continue
            ahead = (v[o] > v[j]) | ((v[o] == v[j]) & (o < j))
            r = r + ahead.astype(jnp.int32)
        pos.append(r)
    local, wsel = [], []
    for k in range(TOP_K):
        lk = jnp.zeros((1, tm), jnp.int32)
        wk = jnp.zeros((1, tm), F32)
        for j in range(epg):
            hit = pos[j] == k
            lk = jnp.where(hit, j, lk)
            wk = jnp.where(hit, p[j], wk)
        local.append(lk)
        wsel.append(wk)
    wsum = wsel[0] + wsel[1]
    idx = [gidx * epg + local[k] for k in range(TOP_K)]

    eiota = lax.broadcasted_iota(jnp.int32, (N_EXPERTS, tm), 0)
    hit = [eiota == idx[k] for k in range(TOP_K)]
    onehot = (hit[0] | hit[1]).astype(F32)
    before = jnp.dot(onehot.astype(BF16), tri_ref[...], preferred_element_type=F32) + run_ref[...]
    for k in range(TOP_K):
        e_ref[pl.ds(k, 1), :] = idx[k]
        w_ref[pl.ds(k, 1), :] = wsel[k] / wsum
        rank_ref[pl.ds(k, 1), :] = jnp.sum(jnp.where(hit[k], before, 0.0), axis=0,
                                           keepdims=True).astype(jnp.int32)
    run_ref[...] += jnp.sum(onehot, axis=1, keepdims=True)
    cnt_ref[...] = jnp.broadcast_to(run_ref[...], cnt_ref.shape)


def _route(logits_t, router_b, tm=512):
    t = logits_t.shape[1]
    tm = min(tm, t)
    tri = (lax.broadcasted_iota(jnp.int32, (tm, tm), 0)
           < lax.broadcasted_iota(jnp.int32, (tm, tm), 1)).astype(BF16)
    blk = lambda rows: pl.BlockSpec((rows, tm), lambda i: (0, i))
    return pl.pallas_call(
        _route_kernel,
        grid=(t // tm,),
        in_specs=[blk(N_EXPERTS),
                  pl.BlockSpec((N_EXPERTS, 1), lambda i: (0, 0)),
                  pl.BlockSpec((tm, tm), lambda i: (0, 0))],
        out_specs=[blk(TOP_K), blk(TOP_K), blk(TOP_K),
                   pl.BlockSpec((N_EXPERTS, LANE), lambda i: (0, 0))],
        out_shape=[jax.ShapeDtypeStruct((TOP_K, t), jnp.int32),
                   jax.ShapeDtypeStruct((TOP_K, t), F32),
                   jax.ShapeDtypeStruct((TOP_K, t), jnp.int32),
                   jax.ShapeDtypeStruct((N_EXPERTS, LANE), F32)],
        scratch_shapes=[pltpu.VMEM((N_EXPERTS, 1), F32)],
        compiler_params=_params("arbitrary"),
        name="route",
    )(logits_t, router_b[:, None], tri)


def _dispatch_kernel(dest_ref, pend_ref, h_ref, o_ref, zero_ref, sem, zsem, *, tm, t):
    base = pl.program_id(0) * tm

    @pl.when(pl.program_id(0) == 0)
    def _():
        zero_ref[...] = jnp.zeros_like(zero_ref)

        def fill(e):
            end = pend_ref[e]
            start = pend_ref[e - 1] if e > 0 else 0
            dst = o_ref.at[pl.ds(pl.multiple_of(jnp.maximum(end - MOE_ROWS, 0), MOE_ROWS),
                                 MOE_ROWS), :]
            return end > start, pltpu.make_async_copy(zero_ref, dst, zsem)

        for e in range(N_EXPERTS):
            nonempty, cp = fill(e)
            pl.when(nonempty)(cp.start)
        for e in range(N_EXPERTS):
            nonempty, cp = fill(e)
            pl.when(nonempty)(cp.wait)

        def tail(r):
            dst = o_ref.at[pl.ds(pl.multiple_of(r * MOE_ROWS, MOE_ROWS), MOE_ROWS), :]
            return pltpu.make_async_copy(zero_ref, dst, zsem)

        first_free = pend_ref[N_EXPERTS - 1] // MOE_ROWS
        n_blocks = o_ref.shape[0] // MOE_ROWS
        lax.fori_loop(first_free, n_blocks, lambda r, c: (tail(r).start(), c)[1], 0)
        lax.fori_loop(first_free, n_blocks, lambda r, c: (tail(r).wait(), c)[1], 0)

    def issue(r, carry):
        for k in range(TOP_K):
            d = dest_ref[k * t + base + r]
            pltpu.make_async_copy(h_ref.at[pl.ds(r, 1), :], o_ref.at[pl.ds(d, 1), :], sem).start()
        return carry

    lax.fori_loop(0, tm, issue, 0)
    for k in range(TOP_K):
        pltpu.make_async_copy(h_ref, o_ref.at[pl.ds(0, tm), :], sem).wait()


def _dispatch(h2, dest_flat, pad_end, rows, tm=256):
    t, d = h2.shape
    return pl.pallas_call(
        functools.partial(_dispatch_kernel, tm=tm, t=t),
        grid_spec=pltpu.PrefetchScalarGridSpec(
            num_scalar_prefetch=2,
            grid=(t // tm,),
            in_specs=[pl.BlockSpec((tm, d), lambda i, dest, pend: (i, 0))],
            out_specs=pl.BlockSpec(memory_space=pl.ANY),
            scratch_shapes=[pltpu.VMEM((MOE_ROWS, d), h2.dtype),
                            pltpu.SemaphoreType.DMA(()), pltpu.SemaphoreType.DMA(())]),
        out_shape=jax.ShapeDtypeStruct((rows, d), h2.dtype),
        compiler_params=_params("arbitrary"),
        name="moe_dispatch",
    )(dest_flat, pad_end, h2)


def _expert_kernel(be_ref, nu_ref, x_ref, wg_ref, wu_ref, wd_ref, y_ref, wgb, wub, wdb):
    r = pl.program_id(0)
    live = r < nu_ref[0]
    new_expert = (r == 0) | (be_ref[jnp.maximum(r - 1, 0)] != be_ref[r])

    @pl.when(live & new_expert)
    def _():
        wgb[...] = wg_ref[...].astype(BF16)
        wub[...] = wu_ref[...].astype(BF16)
        wdb[...] = wd_ref[...].astype(BF16)

    @pl.when(live)
    def _():
        xb = x_ref[...].astype(BF16)
        g = jnp.dot(xb, wgb[...], preferred_element_type=F32)
        u = jnp.dot(xb, wub[...], preferred_element_type=F32)
        a = (g * jax.nn.sigmoid(g)) * u
        y_ref[...] = jnp.dot(a.astype(BF16), wdb[...], preferred_element_type=F32)

    @pl.when(jnp.logical_not(live))
    def _():
        y_ref[...] = jnp.zeros_like(y_ref)


def _experts(x_sorted, block_e, n_used, w_gate, w_up, w_down, layer):
    rows, d = x_sorted.shape
    de = w_gate.shape[-1]
    bm = MOE_ROWS
    blk = lambda r, be, nu: (jnp.minimum(r, nu[0] - 1), 0)
    return pl.pallas_call(
        _expert_kernel,
        grid_spec=pltpu.PrefetchScalarGridSpec(
            num_scalar_prefetch=2,
            grid=(rows // bm,),
            in_specs=[pl.BlockSpec((bm, d), blk),
                      pl.BlockSpec((None, None, d, de), lambda r, be, nu: (layer, be[r], 0, 0)),
                      pl.BlockSpec((None, None, d, de), lambda r, be, nu: (layer, be[r], 0, 0)),
                      pl.BlockSpec((None, None, de, d), lambda r, be, nu: (layer, be[r], 0, 0))],
            out_specs=pl.BlockSpec((bm, d), lambda r, be, nu: (r, 0)),
            scratch_shapes=[pltpu.VMEM((d, de), BF16), pltpu.VMEM((d, de), BF16),
                            pltpu.VMEM((de, d), BF16)]),
        out_shape=jax.ShapeDtypeStruct((rows, d), F32),
        compiler_params=_params("arbitrary"),
        name="moe_experts",
    )(block_e, n_used, x_sorted, w_gate, w_up, w_down)


def _combine_kernel(dest_ref, y_ref, x_ref, gate_ref, w_ref, o_ref, buf, sem, *, tm, t):
    base = pl.program_id(0) * tm

    def issue(r, carry):
        for k in range(TOP_K):
            d = dest_ref[k * t + base + r]
            pltpu.make_async_copy(y_ref.at[pl.ds(d, 1), :], buf.at[k, pl.ds(r, 1), :], sem).start()
        return carry

    lax.fori_loop(0, tm, issue, 0)
    for k in range(TOP_K):
        pltpu.make_async_copy(y_ref.at[pl.ds(0, tm), :], buf.at[k], sem).wait()
    w = w_ref[...]
    y = w[:, 0:1] * buf[0] + w[:, 1:2] * buf[1]
    o_ref[...] = x_ref[...] + gate_ref[...] * y


def _combine(y_sorted, dest_flat, x, mod, w_tok, seq, tm=256):
    t, d = x.shape
    per_b = seq // tm
    return pl.pallas_call(
        functools.partial(_combine_kernel, tm=tm, t=t),
        grid_spec=pltpu.PrefetchScalarGridSpec(
            num_scalar_prefetch=1,
            grid=(t // tm,),
            in_specs=[pl.BlockSpec(memory_space=pl.ANY),
                      pl.BlockSpec((tm, d), lambda i, dest: (i, 0)),
                      pl.BlockSpec((None, 1, d), lambda i, dest: ((i // per_b) * 6 + 5, 0, 0)),
                      pl.BlockSpec((tm, TOP_K), lambda i, dest: (i, 0))],
            out_specs=pl.BlockSpec((tm, d), lambda i, dest: (i, 0)),
            scratch_shapes=[pltpu.VMEM((TOP_K, tm, d), F32), pltpu.SemaphoreType.DMA(())]),
        out_shape=jax.ShapeDtypeStruct((t, d), F32),
        compiler_params=_params("arbitrary"),
        name="moe_combine",
    )(dest_flat, y_sorted, x, mod, w_tok)


def _rope_tables(seq):
    half = MLA_ROPE // 2
    inv = 1.0 / (ROPE_BASE ** (jnp.arange(0, MLA_ROPE, 2, dtype=F32) / MLA_ROPE))
    ang = jnp.arange(seq, dtype=F32)[:, None] * inv[None, :]
    cos, sin = jnp.cos(ang), jnp.sin(ang)
    z = jnp.zeros((seq, half), F32)
    zz = jnp.zeros((seq, LANE - MLA_ROPE), F32)
    cos_t = jnp.concatenate([cos, cos, zz], axis=1)
    s1_t = jnp.concatenate([z, sin, zz], axis=1)
    s2_t = jnp.concatenate([-sin, z, zz], axis=1)
    return cos_t, s1_t, s2_t


def _pad_last(a, width):
    return jnp.pad(a, [(0, 0)] * (a.ndim - 1) + [(0, width - a.shape[-1])])


def kernel(x, c, norm1_g, norm2_g, w_ada, b_ada, w_in, q_lora_g, kv_lora_g, w_uq, w_ukv,
           q_head_g, k_head_g, mla_out_g, sb_out_g, w_o, router_w, router_b,
           w_gate, w_up, w_down):
    bsz, seq, d = x.shape
    t = bsz * seq
    depth = w_ada.shape[0]

    sq_end = COL_KR + MLA_ROPE + SB_WIDTH
    w_in_p = jnp.concatenate(
        [w_in[..., :COL_KR], _pad_last(w_in[..., COL_KR:COL_KR + MLA_ROPE], LANE),
         w_in[..., COL_KR + MLA_ROPE:sq_end] * (SB_DIM ** -0.5 * LOG2_E),
         w_in[..., sq_end:]], axis=-1).astype(BF16)
    wq = w_uq.reshape(depth, Q_LORA, MLA_HEADS, MLA_QK)
    wuq_p = jnp.concatenate(
        [wq[..., :MLA_NOPE].reshape(depth, Q_LORA, -1),
         _pad_last(wq[..., MLA_NOPE:], LANE).reshape(depth, Q_LORA, -1)], axis=-1).astype(BF16)
    wkv = w_ukv.reshape(depth, KV_LORA, MLA_HEADS, MLA_NOPE + MLA_V)
    wukv_p = jnp.concatenate(
        [wkv[..., :MLA_NOPE].reshape(depth, KV_LORA, -1),
         wkv[..., MLA_NOPE:].reshape(depth, KV_LORA, -1)], axis=-1).astype(BF16)
    w_o_b = w_o.astype(BF16)
    q_scale = MLA_QK ** -0.5 * LOG2_E
    gqn = (q_head_g[:, None, :MLA_NOPE] * q_scale)
    gqr = _pad_last(q_head_g[:, None, MLA_NOPE:] * q_scale, LANE)
    gkn = k_head_g[:, None, :MLA_NOPE]
    gkr = _pad_last(k_head_g[:, None, MLA_NOPE:], LANE)
    rope_tabs = _rope_tables(seq)
    rw_t = router_w.T

    mod_all = _ada_modulation(c, w_ada, b_ada)
    n_blocks = -(-(t * TOP_K) // MOE_ROWS) + N_EXPERTS
    rows = n_blocks * MOE_ROWS

    xt = x.reshape(t, d)
    for l in range(depth):
        mod = mod_all[l].reshape(bsz * 6, 1, d)
        proj = _in_proj(xt, norm1_g[l][None, :], mod, w_in_p, l, seq)
        q2d, k2d, v2d = _mla_prep(proj, rope_tabs, q_lora_g[:, None, :], kv_lora_g[:, None, :],
                                  wuq_p, wukv_p, gqn, gqr, gkn, gkr, l, seq)
        o_mla = _mla_attention(q2d, k2d, v2d, bsz, seq)
        o_sb = _sb_attention(proj, bsz, seq)
        xt, h2, logits_t = _out_proj(o_mla, o_sb, mla_out_g[l][None, :], sb_out_g[l][None, :],
                                     w_o_b, xt, mod, norm2_g[l][None, :], rw_t, l, seq)
        e_idx, w_tok, rank, cnt = _route(logits_t, router_b)
        counts = cnt[:, 0].astype(jnp.int32)
        padded = (counts + MOE_ROWS - 1) // MOE_ROWS * MOE_ROWS
        pad_end = jnp.cumsum(padded)
        pad_start = pad_end - padded
        eids = jnp.arange(N_EXPERTS, dtype=jnp.int32)
        seg_start = jnp.sum(jnp.where(e_idx[None] == eids[:, None, None],
                                      pad_start[:, None, None], 0), axis=0)
        dest = (seg_start + rank).reshape(TOP_K * t).astype(jnp.int32)
        n_used = (pad_end[-1:] // MOE_ROWS).astype(jnp.int32)
        first_row = jnp.minimum(jnp.arange(n_blocks, dtype=jnp.int32), n_used - 1) * MOE_ROWS
        block_e = jnp.sum(pad_end[None, :] <= first_row[:, None], axis=1).astype(jnp.int32)
        x_sorted = _dispatch(h2, dest, pad_end.astype(jnp.int32), rows)
        y_sorted = _experts(x_sorted, block_e, n_used, w_gate, w_up, w_down, l)
        xt = _combine(y_sorted, dest, xt, mod, w_tok.T, seq)
    return xt.reshape(bsz, seq, d)
```

```python
import functools

import jax
import jax.numpy as jnp
from jax import lax
from jax.experimental import pallas as pl
from jax.experimental.pallas import tpu as pltpu

F32 = jnp.float32
BF16 = jnp.bfloat16

EPS = 1e-6
ROPE_BASE = 10000.0
CHUNK = 64

MLA_HEADS = 8
MLA_NOPE = 128
MLA_ROPE = 64
MLA_QK = MLA_NOPE + MLA_ROPE
MLA_V = 128
Q_LORA = 512
KV_LORA = 256
SB_HEADS = 8
SB_DIM = 128
SB_WIDTH = SB_HEADS * SB_DIM
LANE = 128
QK_PAD = 2 * LANE
COL_CQ = 0
COL_CKV = Q_LORA
COL_KR = Q_LORA + KV_LORA
MLA_IN_COLS = COL_KR + LANE
COL_SQ = 0
COL_SK = SB_WIDTH
COL_SV = 2 * SB_WIDTH
SB_IN_COLS = 3 * SB_WIDTH

N_EXPERTS = 16
N_GROUPS = 4
EXPERTS_PER_GROUP = N_EXPERTS // N_GROUPS
TOP_K = 2
D_EXPERT = 512
MOE_ROWS = 256
DMA_ISSUE_UNROLL = 8
LOG2_E = 1.4426950408889634
SB_UNDERFLOW_LOG2 = -150.0

VMEM_LIMIT = 56 * 1024 * 1024


def _params(*dims):
    return pltpu.CompilerParams(dimension_semantics=dims, vmem_limit_bytes=VMEM_LIMIT)


def _rms(x):
    return x * lax.rsqrt(jnp.mean(x * x, axis=-1, keepdims=True) + EPS)


def _ada_kernel(c_ref, w_ref, b_ref, o_ref):
    w = w_ref[...]
    for b in range(c_ref.shape[0]):
        cb = c_ref[b]
        cb = cb * jax.nn.sigmoid(cb)
        o_ref[pl.ds(b, 1), :] = jnp.sum(cb * w, axis=0, keepdims=True) + b_ref[...]


def _ada_modulation(c, w_ada, b_ada, tn=1024):
    depth, d, n = w_ada.shape
    bsz = c.shape[0]
    return pl.pallas_call(
        _ada_kernel,
        grid=(depth, n // tn),
        in_specs=[pl.BlockSpec((bsz, d, 1), lambda l, j: (0, 0, 0)),
                  pl.BlockSpec((None, d, tn), lambda l, j: (l, 0, j)),
                  pl.BlockSpec((None, 1, tn), lambda l, j: (l, 0, j))],
        out_specs=pl.BlockSpec((None, bsz, tn), lambda l, j: (l, 0, j)),
        out_shape=jax.ShapeDtypeStruct((depth, bsz, n), F32),
        compiler_params=_params("arbitrary", "arbitrary"),
        name="ada_modulation",
    )(c[:, :, None], w_ada, b_ada[:, None, :])


def _in_proj_kernel(x_ref, g_ref, sh_ref, sc_ref, wa_ref, wb_ref, oa_ref, ob_ref):
    h = _rms(x_ref[...]) * g_ref[...]
    h = (h * (1.0 + sc_ref[...]) + sh_ref[...]).astype(BF16)
    oa_ref[...] = jnp.dot(h, wa_ref[...], preferred_element_type=F32).astype(oa_ref.dtype)
    ob_ref[...] = jnp.dot(h, wb_ref[...], preferred_element_type=F32).astype(ob_ref.dtype)


def _in_proj(x, g, mod, w_mla, w_sb, layer, seq, tm=256):
    t, d = x.shape
    na, nb = w_mla.shape[-1], w_sb.shape[-1]
    per_b = seq // tm
    return pl.pallas_call(
        _in_proj_kernel,
        grid=(t // tm,),
        in_specs=[pl.BlockSpec((tm, d), lambda i: (i, 0)),
                  pl.BlockSpec((1, d), lambda i: (0, 0)),
                  pl.BlockSpec((None, 1, d), lambda i: ((i // per_b) * 6 + 0, 0, 0)),
                  pl.BlockSpec((None, 1, d), lambda i: ((i // per_b) * 6 + 1, 0, 0)),
                  pl.BlockSpec((None, d, na), lambda i: (layer, 0, 0)),
                  pl.BlockSpec((None, d, nb), lambda i: (layer, 0, 0))],
        out_specs=[pl.BlockSpec((tm, na), lambda i: (i, 0)),
                   pl.BlockSpec((tm, nb), lambda i: (i, 0))],
        out_shape=[jax.ShapeDtypeStruct((t, na), BF16),
                   jax.ShapeDtypeStruct((t, nb), BF16)],
        compiler_params=_params("arbitrary"),
        name="in_proj",
    )(x, g, mod, mod, w_mla, w_sb)


def _mla_prep_kernel(cq_ref, ckv_ref, kr_ref, cos_ref, s1_ref, s2_ref, gql_ref, gkvl_ref,
                     wuq_ref, wukv_ref, gqn_ref, gqr_ref, gkn_ref, gkr_ref,
                     q_ref, k_ref, v_ref):
    nope_w = MLA_HEADS * MLA_NOPE
    cqn = _rms(cq_ref[...].astype(F32)) * gql_ref[...]
    q = jnp.dot(cqn.astype(BF16), wuq_ref[...], preferred_element_type=F32)
    ckvn = _rms(ckv_ref[...].astype(F32)) * gkvl_ref[...]
    kv = jnp.dot(ckvn.astype(BF16), wukv_ref[...], preferred_element_type=F32)
    cosv, s1, s2 = cos_ref[...], s1_ref[...], s2_ref[...]

    def rope(x):
        return (x * cosv + pltpu.roll(x, MLA_ROPE // 2, 1) * s1
                + pltpu.roll(x, LANE - MLA_ROPE // 2, 1) * s2)

    kr = kr_ref[...].astype(F32)
    kr_ss = jnp.sum(kr * kr, axis=-1, keepdims=True)
    kr_roped = rope(kr * gkr_ref[...])
    gqn, gqr, gkn = gqn_ref[...], gqr_ref[...], gkn_ref[...]
    for h in range(MLA_HEADS):
        qn = q[:, h * LANE:(h + 1) * LANE]
        qr = q[:, nope_w + h * LANE:nope_w + (h + 1) * LANE]
        ss = jnp.sum(qn * qn, axis=-1, keepdims=True) + jnp.sum(qr * qr, axis=-1, keepdims=True)
        r = lax.rsqrt(ss * (1.0 / MLA_QK) + EPS)
        q_ref[:, h * QK_PAD:h * QK_PAD + LANE] = (qn * r * gqn).astype(BF16)
        q_ref[:, h * QK_PAD + LANE:(h + 1) * QK_PAD] = (rope(qr * gqr) * r).astype(BF16)
        kn = kv[:, h * LANE:(h + 1) * LANE]
        rk = lax.rsqrt((jnp.sum(kn * kn, axis=-1, keepdims=True) + kr_ss) * (1.0 / MLA_QK) + EPS)
        k_ref[:, h * QK_PAD:h * QK_PAD + LANE] = (kn * rk * gkn).astype(BF16)
        k_ref[:, h * QK_PAD + LANE:(h + 1) * QK_PAD] = (kr_roped * rk).astype(BF16)
        v_ref[:, 2 * h * MLA_V:(2 * h + 1) * MLA_V] = (
            kv[:, nope_w + h * MLA_V:nope_w + (h + 1) * MLA_V].astype(BF16))
        v_ref[:, (2 * h + 1) * MLA_V:(2 * h + 2) * MLA_V] = jnp.ones((kn.shape[0], MLA_V), BF16)


def _mla_prep(proj, rope_tabs, gql, gkvl, wuq, wukv, gqn, gqr, gkn, gkr, layer, seq, tm=256):
    t = proj.shape[0]
    per_b = seq // tm
    cos_t, s1_t, s2_t = rope_tabs
    row = lambda i: (i, 0)
    full2 = lambda i: (0, 0)
    lay = lambda i: (layer, 0, 0)
    tab = lambda i: (i % per_b, 0)
    return pl.pallas_call(
        _mla_prep_kernel,
        grid=(t // tm,),
        in_specs=[pl.BlockSpec((tm, Q_LORA), lambda i: (i, COL_CQ // Q_LORA)),
                  pl.BlockSpec((tm, KV_LORA), lambda i: (i, COL_CKV // KV_LORA)),
                  pl.BlockSpec((tm, LANE), lambda i: (i, COL_KR // LANE)),
                  pl.BlockSpec((tm, LANE), tab),
                  pl.BlockSpec((tm, LANE), tab),
                  pl.BlockSpec((tm, LANE), tab),
                  pl.BlockSpec((None, 1, Q_LORA), lay),
                  pl.BlockSpec((None, 1, KV_LORA), lay),
                  pl.BlockSpec((None,) + wuq.shape[1:], lay),
                  pl.BlockSpec((None,) + wukv.shape[1:], lay),
                  pl.BlockSpec((None, 1, LANE), lay),
                  pl.BlockSpec((None, 1, LANE), lay),
                  pl.BlockSpec((None, 1, LANE), lay),
                  pl.BlockSpec((None, 1, LANE), lay)],
        out_specs=[pl.BlockSpec((tm, MLA_HEADS * QK_PAD), row),
                   pl.BlockSpec((tm, MLA_HEADS * QK_PAD), row),
                   pl.BlockSpec((tm, MLA_HEADS * 2 * MLA_V), row)],
        out_shape=[jax.ShapeDtypeStruct((t, MLA_HEADS * QK_PAD), BF16),
                   jax.ShapeDtypeStruct((t, MLA_HEADS * QK_PAD), BF16),
                   jax.ShapeDtypeStruct((t, MLA_HEADS * 2 * MLA_V), BF16)],
        compiler_params=_params("arbitrary"),
        name="mla_prep",
    )(proj, proj, proj, cos_t, s1_t, s2_t, gql, gkvl, wuq, wukv, gqn, gqr, gkn, gkr)


def _mla_attn_kernel(q_ref, k_ref, v_ref, o_ref, m_ref, l_ref, acc_ref, *, tq):
    i = pl.program_id(2)
    q = q_ref[...]

    def scores(kb):
        k = k_ref[pl.ds(pl.multiple_of(kb * tq, tq), tq), :]
        return lax.dot_general(q, k, (((1,), (1,)), ((), ())), preferred_element_type=F32)

    def probs(s, m):
        return jnp.concatenate(
            [jnp.exp2(s[:, c * LANE:(c + 1) * LANE] - m) for c in range(tq // LANE)],
            axis=1).astype(BF16)

    def pv(p, kb):
        return jnp.dot(p, v_ref[pl.ds(pl.multiple_of(kb * tq, tq), tq), :],
                       preferred_element_type=F32)

    s = scores(i)
    q_chunk = lax.broadcasted_iota(jnp.int32, (tq, 1), 0) // CHUNK
    k_chunk = lax.broadcasted_iota(jnp.int32, (1, tq), 1) // CHUNK
    s = jnp.where(k_chunk <= q_chunk, s, -jnp.inf)
    m = jnp.broadcast_to(jnp.max(s, axis=-1, keepdims=True), (tq, LANE))
    r = pv(probs(s, m), i)
    m_ref[...] = m
    acc_ref[...] = r[:, :MLA_V]
    l_ref[...] = r[:, MLA_V:]

    def body(kb, carry):
        s = scores(kb)
        m_old = m_ref[...]
        m_new = jnp.maximum(m_old, jnp.max(s, axis=-1, keepdims=True))
        alpha = jnp.exp2(m_old - m_new)
        r = pv(probs(s, m_new), kb)
        acc_ref[...] = alpha * acc_ref[...] + r[:, :MLA_V]
        l_ref[...] = alpha * l_ref[...] + r[:, MLA_V:]
        m_ref[...] = m_new
        return carry

    lax.fori_loop(0, i, body, 0)
    o_ref[...] = acc_ref[...] / l_ref[...]


def _mla_attention(q2d, k2d, v2d, bsz, seq, tq=512):
    t = q2d.shape[0]
    tq = min(tq, seq)
    nq = seq // tq
    vw = 2 * MLA_V
    return pl.pallas_call(
        functools.partial(_mla_attn_kernel, tq=tq),
        grid=(bsz, MLA_HEADS, nq),
        in_specs=[pl.BlockSpec((tq, QK_PAD), lambda b, h, i: (b * nq + i, h)),
                  pl.BlockSpec((seq, QK_PAD), lambda b, h, i: (b, h)),
                  pl.BlockSpec((seq, vw), lambda b, h, i: (b, h))],
        out_specs=pl.BlockSpec((tq, MLA_V), lambda b, h, i: (b * nq + i, h)),
        out_shape=jax.ShapeDtypeStruct((t, MLA_HEADS * MLA_V), F32),
        scratch_shapes=[pltpu.VMEM((tq, LANE), F32), pltpu.VMEM((tq, MLA_V), F32),
                        pltpu.VMEM((tq, MLA_V), F32)],
        compiler_params=_params("arbitrary", "arbitrary", "arbitrary"),
        name="mla_attention",
    )(q2d, k2d, v2d)


def _sb_attn_kernel(q_ref, k_ref, v_ref, tri_ref, o_ref, run_ref, acc_ref, *, tq, tk):
    i = pl.program_id(2)
    q = q_ref[...]
    tri = tri_ref[...]
    sub = tq // tk

    def block(kb, mask):
        rows = pl.ds(pl.multiple_of(kb * tk, tk), tk)
        z = lax.dot_general(q, k_ref[rows, :], (((1,), (1,)), ((), ())),
                            preferred_element_type=F32)
        log_stay = -(jnp.maximum(z, 0.0) + jnp.log2(1.0 + jnp.exp2(-jnp.abs(z))))
        if mask is not None:
            log_stay = jnp.where(mask, log_stay, 0.0)
        after = jnp.dot(log_stay.astype(BF16), tri, preferred_element_type=F32) + run_ref[...]
        a = jnp.exp2(z + log_stay + after)
        if mask is not None:
            a = jnp.where(mask, a, 0.0)
        acc_ref[...] += jnp.dot(a.astype(BF16), v_ref[rows, :], preferred_element_type=F32)
        run_ref[...] += jnp.sum(log_stay, axis=-1, keepdims=True)

    def more():
        return jnp.max(run_ref[...]) > SB_UNDERFLOW_LOG2

    run_ref[...] = jnp.zeros_like(run_ref)
    acc_ref[...] = jnp.zeros_like(acc_ref)
    qi = lax.broadcasted_iota(jnp.int32, (tq, tk), 0)
    ki = lax.broadcasted_iota(jnp.int32, (tq, tk), 1)
    for j in reversed(range(sub)):
        block(i * sub + j, ki + j * tk < qi)

    def cond(c):
        return (c[0] < i * sub) & c[1]

    def body(c):
        block(i * sub - 1 - c[0], None)
        return c[0] + 1, more()

    lax.while_loop(cond, body, (jnp.int32(0), more()))
    o_ref[...] = acc_ref[...]


def _sb_attention(proj, bsz, seq, tq=512, tk=256):
    t = proj.shape[0]
    tq = min(tq, seq)
    nq = seq // tq
    tri = (lax.broadcasted_iota(jnp.int32, (tk, tk), 0)
           > lax.broadcasted_iota(jnp.int32, (tk, tk), 1)).astype(BF16)
    cq, ck, cv = COL_SQ // SB_DIM, COL_SK // SB_DIM, COL_SV // SB_DIM
    return pl.pallas_call(
        functools.partial(_sb_attn_kernel, tq=tq, tk=tk),
        grid=(bsz, SB_HEADS, nq),
        in_specs=[pl.BlockSpec((tq, SB_DIM), lambda b, h, i: (b * nq + i, cq + h)),
                  pl.BlockSpec((seq, SB_DIM), lambda b, h, i: (b, ck + h)),
                  pl.BlockSpec((seq, SB_DIM), lambda b, h, i: (b, cv + h)),
                  pl.BlockSpec((tk, tk), lambda b, h, i: (0, 0))],
        out_specs=pl.BlockSpec((tq, SB_DIM), lambda b, h, i: (b * nq + i, h)),
        out_shape=jax.ShapeDtypeStruct((t, SB_WIDTH), F32),
        scratch_shapes=[pltpu.VMEM((tq, 1), F32), pltpu.VMEM((tq, SB_DIM), F32)],
        compiler_params=_params("arbitrary", "arbitrary", "arbitrary"),
        name="sb_attention",
    )(proj, proj, proj, tri)


def _out_proj_kernel(om_ref, os_ref, gm_ref, gs_ref, wo_ref, x_ref, gate_ref, g2_ref,
                     sh_ref, sc_ref, rw_ref, xo_ref, h2_ref, lg_ref):
    wm = om_ref.shape[1]
    nm = (_rms(om_ref[...]) * gm_ref[...]).astype(BF16)
    ns = (_rms(os_ref[...]) * gs_ref[...]).astype(BF16)
    y = (jnp.dot(nm, wo_ref[:wm, :], preferred_element_type=F32)
         + jnp.dot(ns, wo_ref[wm:, :], preferred_element_type=F32))
    xn = x_ref[...] + gate_ref[...] * y
    xo_ref[...] = xn
    h2 = _rms(xn) * g2_ref[...]
    h2 = h2 * (1.0 + sc_ref[...]) + sh_ref[...]
    h2_ref[...] = h2
    lg_ref[...] = lax.dot_general(rw_ref[...], h2.astype(BF16), (((1,), (1,)), ((), ())),
                                  preferred_element_type=F32)


def _out_proj(o_mla, o_sb, gm, gs, w_o, x, mod, g2, rw_t, layer, seq, tm=256):
    t, d = x.shape
    per_b = seq // tm
    row = lambda i: (i, 0)
    full2 = lambda i: (0, 0)
    modspec = lambda j: pl.BlockSpec((None, 1, d), lambda i: ((i // per_b) * 6 + j, 0, 0))
    return pl.pallas_call(
        _out_proj_kernel,
        grid=(t // tm,),
        in_specs=[pl.BlockSpec((tm, o_mla.shape[1]), row),
                  pl.BlockSpec((tm, o_sb.shape[1]), row),
                  pl.BlockSpec((1, o_mla.shape[1]), full2),
                  pl.BlockSpec((1, o_sb.shape[1]), full2),
                  pl.BlockSpec((None,) + w_o.shape[1:], lambda i: (layer, 0, 0)),
                  pl.BlockSpec((tm, d), row),
                  modspec(2),
                  pl.BlockSpec((1, d), full2),
                  modspec(3),
                  modspec(4),
                  pl.BlockSpec(rw_t.shape, full2)],
        out_specs=[pl.BlockSpec((tm, d), row),
                   pl.BlockSpec((tm, d), row),
                   pl.BlockSpec((N_EXPERTS, tm), lambda i: (0, i))],
        out_shape=[jax.ShapeDtypeStruct((t, d), F32),
                   jax.ShapeDtypeStruct((t, d), F32),
                   jax.ShapeDtypeStruct((N_EXPERTS, t), F32)],
        compiler_params=_params("arbitrary"),
        name="out_proj",
    )(o_mla, o_sb, gm, gs, w_o, x, mod, g2, mod, mod, rw_t)


def _route_kernel(lg_ref, rb_ref, tri_ref, e_ref, w_ref, rank_ref, cnt_ref, run_ref):
    i = pl.program_id(0)

    @pl.when(i == 0)
    def _():
        run_ref[...] = jnp.zeros_like(run_ref)

    scores = jax.nn.sigmoid(lg_ref[...])
    sel = scores + rb_ref[...]
    tm = sel.shape[1]
    epg = EXPERTS_PER_GROUP
    srow = [sel[e:e + 1, :] for e in range(N_EXPERTS)]
    prow = [scores[e:e + 1, :] for e in range(N_EXPERTS)]

    gscore = []
    for g in range(N_GROUPS):
        a, b, c, d = srow[g * epg:(g + 1) * epg]
        hi1, lo1 = jnp.maximum(a, b), jnp.minimum(a, b)
        hi2, lo2 = jnp.maximum(c, d), jnp.minimum(c, d)
        gscore.append(jnp.maximum(hi1, hi2)
                      + jnp.maximum(jnp.minimum(hi1, hi2), jnp.maximum(lo1, lo2)))
    best = gscore[0]
    gidx = jnp.zeros((1, tm), jnp.int32)
    for g in range(1, N_GROUPS):
        better = gscore[g] > best
        gidx = jnp.where(better, g, gidx)
        best = jnp.where(better, gscore[g], best)

    def pick(rows, j):
        out = rows[j]
        for g in range(1, N_GROUPS):
            out = jnp.where(gidx == g, rows[g * epg + j], out)
        return out

    v = [pick(srow, j) for j in range(epg)]
    p = [pick(prow, j) for j in range(epg)]
    pos = []
    for j in range(epg):
        r = jnp.zeros((1, tm), jnp.int32)
        for o in range(epg):
            if o == j:
                continue
            ahead = (v[o] > v[j]) | ((v[o] == v[j]) & (o < j))
            r = r + ahead.astype(jnp.int32)
        pos.append(r)
    local, wsel = [], []
    for k in range(TOP_K):
        lk = jnp.zeros((1, tm), jnp.int32)
        wk = jnp.zeros((1, tm), F32)
        for j in range(epg):
            hit = pos[j] == k
            lk = jnp.where(hit, j, lk)
            wk = jnp.where(hit, p[j], wk)
        local.append(lk)
        wsel.append(wk)
    wsum = wsel[0] + wsel[1]
    idx = [gidx * epg + local[k] for k in range(TOP_K)]

    eiota = lax.broadcasted_iota(jnp.int32, (N_EXPERTS, tm), 0)
    hit = [eiota == idx[k] for k in range(TOP_K)]
    onehot = (hit[0] | hit[1]).astype(F32)
    before = jnp.dot(onehot.astype(BF16), tri_ref[...], preferred_element_type=F32) + run_ref[...]
    for k in range(TOP_K):
        e_ref[pl.ds(k, 1), :] = idx[k]
        w_ref[pl.ds(k, 1), :] = wsel[k] / wsum
        rank_ref[pl.ds(k, 1), :] = jnp.sum(jnp.where(hit[k], before, 0.0), axis=0,
                                           keepdims=True).astype(jnp.int32)
    run_ref[...] += jnp.sum(onehot, axis=1, keepdims=True)
    cnt_ref[...] = jnp.broadcast_to(run_ref[...], cnt_ref.shape)


def _route(logits_t, router_b, tm=512):
    t = logits_t.shape[1]
    tm = min(tm, t)
    tri = (lax.broadcasted_iota(jnp.int32, (tm, tm), 0)
           < lax.broadcasted_iota(jnp.int32, (tm, tm), 1)).astype(BF16)
    blk = lambda rows: pl.BlockSpec((rows, tm), lambda i: (0, i))
    return pl.pallas_call(
        _route_kernel,
        grid=(t // tm,),
        in_specs=[blk(N_EXPERTS),
                  pl.BlockSpec((N_EXPERTS, 1), lambda i: (0, 0)),
                  pl.BlockSpec((tm, tm), lambda i: (0, 0))],
        out_specs=[blk(TOP_K), blk(TOP_K), blk(TOP_K),
                   pl.BlockSpec((N_EXPERTS, LANE), lambda i: (0, 0))],
        out_shape=[jax.ShapeDtypeStruct((TOP_K, t), jnp.int32),
                   jax.ShapeDtypeStruct((TOP_K, t), F32),
                   jax.ShapeDtypeStruct((TOP_K, t), jnp.int32),
                   jax.ShapeDtypeStruct((N_EXPERTS, LANE), F32)],
        scratch_shapes=[pltpu.VMEM((N_EXPERTS, 1), F32)],
        compiler_params=_params("arbitrary"),
        name="route",
    )(logits_t, router_b[:, None], tri)


def _dispatch_kernel(dest_ref, pend_ref, h_ref, o_ref, zero_ref, sem, zsem, *, tm, t):
    base = pl.program_id(0) * tm

    @pl.when(pl.program_id(0) == 0)
    def _():
        zero_ref[...] = jnp.zeros_like(zero_ref)

        def fill(e):
            end = pend_ref[e]
            start = pend_ref[e - 1] if e > 0 else 0
            dst = o_ref.at[pl.ds(pl.multiple_of(jnp.maximum(end - MOE_ROWS, 0), MOE_ROWS),
                                 MOE_ROWS), :]
            return end > start, pltpu.make_async_copy(zero_ref, dst, zsem)

        for e in range(N_EXPERTS):
            nonempty, cp = fill(e)
            pl.when(nonempty)(cp.start)
        for e in range(N_EXPERTS):
            nonempty, cp = fill(e)
            pl.when(nonempty)(cp.wait)

        def tail(r):
            dst = o_ref.at[pl.ds(pl.multiple_of(r * MOE_ROWS, MOE_ROWS), MOE_ROWS), :]
            return pltpu.make_async_copy(zero_ref, dst, zsem)

        first_free = pend_ref[N_EXPERTS - 1] // MOE_ROWS
        n_blocks = o_ref.shape[0] // MOE_ROWS
        lax.fori_loop(first_free, n_blocks, lambda r, c: (tail(r).start(), c)[1], 0)
        lax.fori_loop(first_free, n_blocks, lambda r, c: (tail(r).wait(), c)[1], 0)

    def issue(r, carry):
        for k in range(TOP_K):
            d = dest_ref[k * t + base + r]
            pltpu.make_async_copy(h_ref.at[pl.ds(r, 1), :], o_ref.at[pl.ds(d, 1), :], sem).start()
        return carry

    lax.fori_loop(0, tm, issue, 0, unroll=DMA_ISSUE_UNROLL)
    for k in range(TOP_K):
        pltpu.make_async_copy(h_ref, o_ref.at[pl.ds(0, tm), :], sem).wait()


def _dispatch(h2, dest_flat, pad_end, rows, tm=256):
    t, d = h2.shape
    return pl.pallas_call(
        functools.partial(_dispatch_kernel, tm=tm, t=t),
        grid_spec=pltpu.PrefetchScalarGridSpec(
            num_scalar_prefetch=2,
            grid=(t // tm,),
            in_specs=[pl.BlockSpec((tm, d), lambda i, dest, pend: (i, 0))],
            out_specs=pl.BlockSpec(memory_space=pl.ANY),
            scratch_shapes=[pltpu.VMEM((MOE_ROWS, d), h2.dtype),
                            pltpu.SemaphoreType.DMA(()), pltpu.SemaphoreType.DMA(())]),
        out_shape=jax.ShapeDtypeStruct((rows, d), h2.dtype),
        compiler_params=_params("arbitrary"),
        name="moe_dispatch",
    )(dest_flat, pad_end, h2)


def _expert_kernel(be_ref, nu_ref, x_ref, wg_ref, wu_ref, wd_ref, y_ref, wgb, wub, wdb):
    r = pl.program_id(0)
    live = r < nu_ref[0]
    new_expert = (r == 0) | (be_ref[jnp.maximum(r - 1, 0)] != be_ref[r])

    @pl.when(live & new_expert)
    def _():
        wgb[...] = wg_ref[...].astype(BF16)
        wub[...] = wu_ref[...].astype(BF16)
        wdb[...] = wd_ref[...].astype(BF16)

    @pl.when(live)
    def _():
        xb = x_ref[...].astype(BF16)
        g = jnp.dot(xb, wgb[...], preferred_element_type=F32)
        u = jnp.dot(xb, wub[...], preferred_element_type=F32)
        a = (g * jax.nn.sigmoid(g)) * u
        y_ref[...] = jnp.dot(a.astype(BF16), wdb[...], preferred_element_type=F32)

    @pl.when(jnp.logical_not(live))
    def _():
        y_ref[...] = jnp.zeros_like(y_ref)


def _experts(x_sorted, block_e, n_used, w_gate, w_up, w_down, layer):
    rows, d = x_sorted.shape
    de = w_gate.shape[-1]
    bm = MOE_ROWS
    blk = lambda r, be, nu: (jnp.minimum(r, nu[0] - 1), 0)
    return pl.pallas_call(
        _expert_kernel,
        grid_spec=pltpu.PrefetchScalarGridSpec(
            num_scalar_prefetch=2,
            grid=(rows // bm,),
            in_specs=[pl.BlockSpec((bm, d), blk),
                      pl.BlockSpec((None, None, d, de), lambda r, be, nu: (layer, be[r], 0, 0)),
                      pl.BlockSpec((None, None, d, de), lambda r, be, nu: (layer, be[r], 0, 0)),
                      pl.BlockSpec((None, None, de, d), lambda r, be, nu: (layer, be[r], 0, 0))],
            out_specs=pl.BlockSpec((bm, d), lambda r, be, nu: (r, 0)),
            scratch_shapes=[pltpu.VMEM((d, de), BF16), pltpu.VMEM((d, de), BF16),
                            pltpu.VMEM((de, d), BF16)]),
        out_shape=jax.ShapeDtypeStruct((rows, d), F32),
        compiler_params=_params("arbitrary"),
        name="moe_experts",
    )(block_e, n_used, x_sorted, w_gate, w_up, w_down)


def _combine_kernel(dest_ref, y_ref, x_ref, gate_ref, w_ref, o_ref, buf, sem, *, tm, t):
    base = pl.program_id(0) * tm

    def issue(r, carry):
        for k in range(TOP_K):
            d = dest_ref[k * t + base + r]
            pltpu.make_async_copy(y_ref.at[pl.ds(d, 1), :], buf.at[k, pl.ds(r, 1), :], sem).start()
        return carry

    lax.fori_loop(0, tm, issue, 0, unroll=DMA_ISSUE_UNROLL)
    for k in range(TOP_K):
        pltpu.make_async_copy(y_ref.at[pl.ds(0, tm), :], buf.at[k], sem).wait()
    w = w_ref[...]
    y = w[:, 0:1] * buf[0] + w[:, 1:2] * buf[1]
    o_ref[...] = x_ref[...] + gate_ref[...] * y


def _combine(y_sorted, dest_flat, x, mod, w_tok, seq, tm=256):
    t, d = x.shape
    per_b = seq // tm
    return pl.pallas_call(
        functools.partial(_combine_kernel, tm=tm, t=t),
        grid_spec=pltpu.PrefetchScalarGridSpec(
            num_scalar_prefetch=1,
            grid=(t // tm,),
            in_specs=[pl.BlockSpec(memory_space=pl.ANY),
                      pl.BlockSpec((tm, d), lambda i, dest: (i, 0)),
                      pl.BlockSpec((None, 1, d), lambda i, dest: ((i // per_b) * 6 + 5, 0, 0)),
                      pl.BlockSpec((tm, TOP_K), lambda i, dest: (i, 0))],
            out_specs=pl.BlockSpec((tm, d), lambda i, dest: (i, 0)),
            scratch_shapes=[pltpu.VMEM((TOP_K, tm, d), F32), pltpu.SemaphoreType.DMA(())]),
        out_shape=jax.ShapeDtypeStruct((t, d), F32),
        compiler_params=_params("arbitrary"),
        name="moe_combine",
    )(dest_flat, y_sorted, x, mod, w_tok)


def _rope_tables(seq):
    half = MLA_ROPE // 2
    inv = 1.0 / (ROPE_BASE ** (jnp.arange(0, MLA_ROPE, 2, dtype=F32) / MLA_ROPE))
    ang = jnp.arange(seq, dtype=F32)[:, None] * inv[None, :]
    cos, sin = jnp.cos(ang), jnp.sin(ang)
    z = jnp.zeros((seq, half), F32)
    zz = jnp.zeros((seq, LANE - MLA_ROPE), F32)
    cos_t = jnp.concatenate([cos, cos, zz], axis=1)
    s1_t = jnp.concatenate([z, sin, zz], axis=1)
    s2_t = jnp.concatenate([-sin, z, zz], axis=1)
    return cos_t, s1_t, s2_t


def _pad_last(a, width):
    return jnp.pad(a, [(0, 0)] * (a.ndim - 1) + [(0, width - a.shape[-1])])


def kernel(x, c, norm1_g, norm2_g, w_ada, b_ada, w_in, q_lora_g, kv_lora_g, w_uq, w_ukv,
           q_head_g, k_head_g, mla_out_g, sb_out_g, w_o, router_w, router_b,
           w_gate, w_up, w_down):
    bsz, seq, d = x.shape
    t = bsz * seq
    depth = w_ada.shape[0]

    sb_col0 = COL_KR + MLA_ROPE
    w_in_mla = _pad_last(w_in[..., :sb_col0], MLA_IN_COLS).astype(BF16)
    sb_col_scale = jnp.where(jnp.arange(SB_IN_COLS) < SB_WIDTH, SB_DIM ** -0.5 * LOG2_E, 1.0)
    w_in_sb = (w_in[..., sb_col0:] * sb_col_scale.astype(F32)).astype(BF16)
    wq = w_uq.reshape(depth, Q_LORA, MLA_HEADS, MLA_QK)
    wuq_p = jnp.concatenate(
        [wq[..., :MLA_NOPE].reshape(depth, Q_LORA, -1),
         _pad_last(wq[..., MLA_NOPE:], LANE).reshape(depth, Q_LORA, -1)], axis=-1).astype(BF16)
    wkv = w_ukv.reshape(depth, KV_LORA, MLA_HEADS, MLA_NOPE + MLA_V)
    wukv_p = jnp.concatenate(
        [wkv[..., :MLA_NOPE].reshape(depth, KV_LORA, -1),
         wkv[..., MLA_NOPE:].reshape(depth, KV_LORA, -1)], axis=-1).astype(BF16)
    w_o_b = w_o.astype(BF16)
    q_scale = MLA_QK ** -0.5 * LOG2_E
    gqn = (q_head_g[:, None, :MLA_NOPE] * q_scale)
    gqr = _pad_last(q_head_g[:, None, MLA_NOPE:] * q_scale, LANE)
    gkn = k_head_g[:, None, :MLA_NOPE]
    gkr = _pad_last(k_head_g[:, None, MLA_NOPE:], LANE)
    rope_tabs = _rope_tables(seq)
    rw_t = router_w.T.astype(BF16)

    mod_all = _ada_modulation(c, w_ada, b_ada)
    n_blocks = -(-(t * TOP_K) // MOE_ROWS) + N_EXPERTS
    rows = n_blocks * MOE_ROWS

    xt = x.reshape(t, d)
    for l in range(depth):
        mod = mod_all[l].reshape(bsz * 6, 1, d)
        proj_mla, proj_sb = _in_proj(xt, norm1_g[l][None, :], mod, w_in_mla, w_in_sb, l, seq)
        q2d, k2d, v2d = _mla_prep(proj_mla, rope_tabs, q_lora_g[:, None, :],
                                  kv_lora_g[:, None, :], wuq_p, wukv_p, gqn, gqr, gkn, gkr, l, seq)
        o_mla = _mla_attention(q2d, k2d, v2d, bsz, seq)
        o_sb = _sb_attention(proj_sb, bsz, seq)
        xt, h2, logits_t = _out_proj(o_mla, o_sb, mla_out_g[l][None, :], sb_out_g[l][None, :],
                                     w_o_b, xt, mod, norm2_g[l][None, :], rw_t, l, seq)
        e_idx, w_tok, rank, cnt = _route(logits_t, router_b)
        counts = cnt[:, 0].astype(jnp.int32)
        padded = (counts + MOE_ROWS - 1) // MOE_ROWS * MOE_ROWS
        pad_end = jnp.cumsum(padded)
        pad_start = pad_end - padded
        eids = jnp.arange(N_EXPERTS, dtype=jnp.int32)
        seg_start = jnp.sum(jnp.where(e_idx[None] == eids[:, None, None],
                                      pad_start[:, None, None], 0), axis=0)
        dest = (seg_start + rank).reshape(TOP_K * t).astype(jnp.int32)
        n_used = (pad_end[-1:] // MOE_ROWS).astype(jnp.int32)
        first_row = jnp.minimum(jnp.arange(n_blocks, dtype=jnp.int32), n_used - 1) * MOE_ROWS
        block_e = jnp.sum(pad_end[None, :] <= first_row[:, None], axis=1).astype(jnp.int32)
        x_sorted = _dispatch(h2, dest, pad_end.astype(jnp.int32), rows)
        y_sorted = _experts(x_sorted, block_e, n_used, w_gate, w_up, w_down, l)
        xt = _combine(y_sorted, dest, xt, mod, w_tok.T, seq)
    return xt.reshape(bsz, seq, d)
```

```python
import functools

import jax
import jax.numpy as jnp
from jax import lax
from jax.experimental import pallas as pl
from jax.experimental.pallas import tpu as pltpu

F32 = jnp.float32
BF16 = jnp.bfloat16

EPS = 1e-6
ROPE_BASE = 10000.0
CHUNK = 64

MLA_HEADS = 8
MLA_NOPE = 128
MLA_ROPE = 64
MLA_QK = MLA_NOPE + MLA_ROPE
MLA_V = 128
Q_LORA = 512
KV_LORA = 256
SB_HEADS = 8
SB_DIM = 128
SB_WIDTH = SB_HEADS * SB_DIM
LANE = 128
QK_PAD = 2 * LANE
COL_CQ = 0
COL_CKV = Q_LORA
COL_KR = Q_LORA + KV_LORA
MLA_IN_COLS = COL_KR + LANE
COL_SQ = 0
COL_SK = SB_WIDTH
COL_SV = 2 * SB_WIDTH
SB_IN_COLS = 3 * SB_WIDTH

N_EXPERTS = 16
N_GROUPS = 4
EXPERTS_PER_GROUP = N_EXPERTS // N_GROUPS
TOP_K = 2
D_EXPERT = 512
MOE_ROWS = 256
DMA_ISSUE_UNROLL = 8
LOG2_E = 1.4426950408889634
SB_UNDERFLOW_LOG2 = -150.0

VMEM_LIMIT = 56 * 1024 * 1024


def _params(*dims):
    return pltpu.CompilerParams(dimension_semantics=dims, vmem_limit_bytes=VMEM_LIMIT)


def _rms(x):
    return x * lax.rsqrt(jnp.mean(x * x, axis=-1, keepdims=True) + EPS)


def _ada_kernel(c_ref, w_ref, b_ref, o_ref):
    w = w_ref[...]
    for b in range(c_ref.shape[0]):
        cb = c_ref[b]
        cb = cb * jax.nn.sigmoid(cb)
        o_ref[pl.ds(b, 1), :] = jnp.sum(cb * w, axis=0, keepdims=True) + b_ref[...]


def _ada_modulation(c, w_ada, b_ada, tn=1024):
    depth, d, n = w_ada.shape
    bsz = c.shape[0]
    return pl.pallas_call(
        _ada_kernel,
        grid=(depth, n // tn),
        in_specs=[pl.BlockSpec((bsz, d, 1), lambda l, j: (0, 0, 0)),
                  pl.BlockSpec((None, d, tn), lambda l, j: (l, 0, j)),
                  pl.BlockSpec((None, 1, tn), lambda l, j: (l, 0, j))],
        out_specs=pl.BlockSpec((None, bsz, tn), lambda l, j: (l, 0, j)),
        out_shape=jax.ShapeDtypeStruct((depth, bsz, n), F32),
        compiler_params=_params("arbitrary", "arbitrary"),
        name="ada_modulation",
    )(c[:, :, None], w_ada, b_ada[:, None, :])


def _in_proj_kernel(x_ref, g_ref, sh_ref, sc_ref, wa_ref, wb_ref, oa_ref, ob_ref):
    h = _rms(x_ref[...]) * g_ref[...]
    h = (h * (1.0 + sc_ref[...]) + sh_ref[...]).astype(BF16)
    oa_ref[...] = jnp.dot(h, wa_ref[...], preferred_element_type=F32).astype(oa_ref.dtype)
    ob_ref[...] = jnp.dot(h, wb_ref[...], preferred_element_type=F32).astype(ob_ref.dtype)


def _in_proj(x, g, mod, w_mla, w_sb, layer, seq, tm=256):
    t, d = x.shape
    na, nb = w_mla.shape[-1], w_sb.shape[-1]
    per_b = seq // tm
    return pl.pallas_call(
        _in_proj_kernel,
        grid=(t // tm,),
        in_specs=[pl.BlockSpec((tm, d), lambda i: (i, 0)),
                  pl.BlockSpec((1, d), lambda i: (0, 0)),
                  pl.BlockSpec((None, 1, d), lambda i: ((i // per_b) * 6 + 0, 0, 0)),
                  pl.BlockSpec((None, 1, d), lambda i: ((i // per_b) * 6 + 1, 0, 0)),
                  pl.BlockSpec((None, d, na), lambda i: (layer, 0, 0)),
                  pl.BlockSpec((None, d, nb), lambda i: (layer, 0, 0))],
        out_specs=[pl.BlockSpec((tm, na), lambda i: (i, 0)),
                   pl.BlockSpec((tm, nb), lambda i: (i, 0))],
        out_shape=[jax.ShapeDtypeStruct((t, na), BF16),
                   jax.ShapeDtypeStruct((t, nb), BF16)],
        compiler_params=_params("arbitrary"),
        name="in_proj",
    )(x, g, mod, mod, w_mla, w_sb)


def _mla_prep_kernel(cq_ref, ckv_ref, kr_ref, cos_ref, s1_ref, s2_ref, gql_ref, gkvl_ref,
                     wuq_ref, wukv_ref, gqn_ref, gqr_ref, gkn_ref, gkr_ref,
                     q_ref, k_ref, v_ref):
    nope_w = MLA_HEADS * MLA_NOPE
    cqn = _rms(cq_ref[...].astype(F32)) * gql_ref[...]
    q = jnp.dot(cqn.astype(BF16), wuq_ref[...], preferred_element_type=F32)
    ckvn = _rms(ckv_ref[...].astype(F32)) * gkvl_ref[...]
    kv = jnp.dot(ckvn.astype(BF16), wukv_ref[...], preferred_element_type=F32)
    cosv, s1, s2 = cos_ref[...], s1_ref[...], s2_ref[...]

    def rope(x):
        return (x * cosv + pltpu.roll(x, MLA_ROPE // 2, 1) * s1
                + pltpu.roll(x, LANE - MLA_ROPE // 2, 1) * s2)

    kr = kr_ref[...].astype(F32)
    kr_ss = jnp.sum(kr * kr, axis=-1, keepdims=True)
    kr_roped = rope(kr * gkr_ref[...])
    gqn, gqr, gkn = gqn_ref[...], gqr_ref[...], gkn_ref[...]
    for h in range(MLA_HEADS):
        qn = q[:, h * LANE:(h + 1) * LANE]
        qr = q[:, nope_w + h * LANE:nope_w + (h + 1) * LANE]
        ss = jnp.sum(qn * qn, axis=-1, keepdims=True) + jnp.sum(qr * qr, axis=-1, keepdims=True)
        r = lax.rsqrt(ss * (1.0 / MLA_QK) + EPS)
        q_ref[:, h * QK_PAD:h * QK_PAD + LANE] = (qn * r * gqn).astype(BF16)
        q_ref[:, h * QK_PAD + LANE:(h + 1) * QK_PAD] = (rope(qr * gqr) * r).astype(BF16)
        kn = kv[:, h * LANE:(h + 1) * LANE]
        rk = lax.rsqrt((jnp.sum(kn * kn, axis=-1, keepdims=True) + kr_ss) * (1.0 / MLA_QK) + EPS)
        k_ref[:, h * QK_PAD:h * QK_PAD + LANE] = (kn * rk * gkn).astype(BF16)
        k_ref[:, h * QK_PAD + LANE:(h + 1) * QK_PAD] = (kr_roped * rk).astype(BF16)
        v_ref[:, 2 * h * MLA_V:(2 * h + 1) * MLA_V] = (
            kv[:, nope_w + h * MLA_V:nope_w + (h + 1) * MLA_V].astype(BF16))
        v_ref[:, (2 * h + 1) * MLA_V:(2 * h + 2) * MLA_V] = jnp.ones((kn.shape[0], MLA_V), BF16)


def _mla_prep(proj, rope_tabs, gql, gkvl, wuq, wukv, gqn, gqr, gkn, gkr, layer, seq, tm=256):
    t = proj.shape[0]
    per_b = seq // tm
    cos_t, s1_t, s2_t = rope_tabs
    row = lambda i: (i, 0)
    full2 = lambda i: (0, 0)
    lay = lambda i: (layer, 0, 0)
    tab = lambda i: (i % per_b, 0)
    return pl.pallas_call(
        _mla_prep_kernel,
        grid=(t // tm,),
        in_specs=[pl.BlockSpec((tm, Q_LORA), lambda i: (i, COL_CQ // Q_LORA)),
                  pl.BlockSpec((tm, KV_LORA), lambda i: (i, COL_CKV // KV_LORA)),
                  pl.BlockSpec((tm, LANE), lambda i: (i, COL_KR // LANE)),
                  pl.BlockSpec((tm, LANE), tab),
                  pl.BlockSpec((tm, LANE), tab),
                  pl.BlockSpec((tm, LANE), tab),
                  pl.BlockSpec((None, 1, Q_LORA), lay),
                  pl.BlockSpec((None, 1, KV_LORA), lay),
                  pl.BlockSpec((None,) + wuq.shape[1:], lay),
                  pl.BlockSpec((None,) + wukv.shape[1:], lay),
                  pl.BlockSpec((None, 1, LANE), lay),
                  pl.BlockSpec((None, 1, LANE), lay),
                  pl.BlockSpec((None, 1, LANE), lay),
                  pl.BlockSpec((None, 1, LANE), lay)],
        out_specs=[pl.BlockSpec((tm, MLA_HEADS * QK_PAD), row),
                   pl.BlockSpec((tm, MLA_HEADS * QK_PAD), row),
                   pl.BlockSpec((tm, MLA_HEADS * 2 * MLA_V), row)],
        out_shape=[jax.ShapeDtypeStruct((t, MLA_HEADS * QK_PAD), BF16),
                   jax.ShapeDtypeStruct((t, MLA_HEADS * QK_PAD), BF16),
                   jax.ShapeDtypeStruct((t, MLA_HEADS * 2 * MLA_V), BF16)],
        compiler_params=_params("arbitrary"),
        name="mla_prep",
    )(proj, proj, proj, cos_t, s1_t, s2_t, gql, gkvl, wuq, wukv, gqn, gqr, gkn, gkr)


def _mla_attn_kernel(q_ref, k_ref, v_ref, o_ref, m_ref, l_ref, acc_ref, sa_ref, sb_ref, *, tq):
    i = pl.program_id(2)
    q = q_ref[...]

    def scores(kb):
        k = k_ref[pl.ds(pl.multiple_of(kb * tq, tq), tq), :]
        return lax.dot_general(q, k, (((1,), (1,)), ((), ())), preferred_element_type=F32)

    def probs(s, m):
        return jnp.concatenate(
            [jnp.exp2(s[:, c * LANE:(c + 1) * LANE] - m) for c in range(tq // LANE)],
            axis=1).astype(BF16)

    def pv(p, kb):
        return jnp.dot(p, v_ref[pl.ds(pl.multiple_of(kb * tq, tq), tq), :],
                       preferred_element_type=F32)

    s = scores(i)
    q_chunk = lax.broadcasted_iota(jnp.int32, (tq, 1), 0) // CHUNK
    k_chunk = lax.broadcasted_iota(jnp.int32, (1, tq), 1) // CHUNK
    s = jnp.where(k_chunk <= q_chunk, s, -jnp.inf)
    m = jnp.broadcast_to(jnp.max(s, axis=-1, keepdims=True), (tq, LANE))
    r = pv(probs(s, m), i)
    m_ref[...] = m
    acc_ref[...] = r[:, :MLA_V]
    l_ref[...] = r[:, MLA_V:]

    def update(s, kb):
        m_old = m_ref[...]
        m_new = jnp.maximum(m_old, jnp.max(s, axis=-1, keepdims=True))
        alpha = jnp.exp2(m_old - m_new)
        r = pv(probs(s, m_new), kb)
        acc_ref[...] = alpha * acc_ref[...] + r[:, :MLA_V]
        l_ref[...] = alpha * l_ref[...] + r[:, MLA_V:]
        m_ref[...] = m_new

    last = jnp.maximum(i - 1, 0)
    sa_ref[...] = scores(0)

    def pair(t, carry):
        j = 2 * t
        sb_ref[...] = scores(jnp.minimum(j + 1, last))
        update(sa_ref[...], j)
        sa_ref[...] = scores(jnp.minimum(j + 2, last))
        update(sb_ref[...], j + 1)
        return carry

    lax.fori_loop(0, i // 2, pair, 0)

    @pl.when(i % 2 == 1)
    def _():
        update(sa_ref[...], i - 1)

    o_ref[...] = acc_ref[...] / l_ref[...]


def _mla_attention(q2d, k2d, v2d, bsz, seq, tq=512):
    t = q2d.shape[0]
    tq = min(tq, seq)
    nq = seq // tq
    vw = 2 * MLA_V
    return pl.pallas_call(
        functools.partial(_mla_attn_kernel, tq=tq),
        grid=(bsz, MLA_HEADS, nq),
        in_specs=[pl.BlockSpec((tq, QK_PAD), lambda b, h, i: (b * nq + i, h)),
                  pl.BlockSpec((seq, QK_PAD), lambda b, h, i: (b, h)),
                  pl.BlockSpec((seq, vw), lambda b, h, i: (b, h))],
        out_specs=pl.BlockSpec((tq, MLA_V), lambda b, h, i: (b * nq + i, h)),
        out_shape=jax.ShapeDtypeStruct((t, MLA_HEADS * MLA_V), F32),
        scratch_shapes=[pltpu.VMEM((tq, LANE), F32), pltpu.VMEM((tq, MLA_V), F32),
                        pltpu.VMEM((tq, MLA_V), F32),
                        pltpu.VMEM((tq, tq), F32), pltpu.VMEM((tq, tq), F32)],
        compiler_params=_params("arbitrary", "arbitrary", "arbitrary"),
        name="mla_attention",
    )(q2d, k2d, v2d)


def _sb_attn_kernel(q_ref, k_ref, v_ref, tri_ref, o_ref, run_ref, acc_ref, *, tq, tk):
    i = pl.program_id(2)
    q = q_ref[...]
    tri = tri_ref[...]
    sub = tq // tk

    def block(kb, mask):
        rows = pl.ds(pl.multiple_of(kb * tk, tk), tk)
        z = lax.dot_general(q, k_ref[rows, :], (((1,), (1,)), ((), ())),
                            preferred_element_type=F32)
        log_stay = -(jnp.maximum(z, 0.0) + jnp.log2(1.0 + jnp.exp2(-jnp.abs(z))))
        if mask is not None:
            log_stay = jnp.where(mask, log_stay, 0.0)
        after = jnp.dot(log_stay.astype(BF16), tri, preferred_element_type=F32) + run_ref[...]
        a = jnp.exp2(z + log_stay + after)
        if mask is not None:
            a = jnp.where(mask, a, 0.0)
        acc_ref[...] += jnp.dot(a.astype(BF16), v_ref[rows, :], preferred_element_type=F32)
        run_ref[...] += jnp.sum(log_stay, axis=-1, keepdims=True)

    def more():
        return jnp.max(run_ref[...]) > SB_UNDERFLOW_LOG2

    run_ref[...] = jnp.zeros_like(run_ref)
    acc_ref[...] = jnp.zeros_like(acc_ref)
    qi = lax.broadcasted_iota(jnp.int32, (tq, tk), 0)
    ki = lax.broadcasted_iota(jnp.int32, (tq, tk), 1)
    for j in reversed(range(sub)):
        block(i * sub + j, ki + j * tk < qi)

    def cond(c):
        return (c[0] < i * sub) & c[1]

    def body(c):
        block(i * sub - 1 - c[0], None)
        return c[0] + 1, more()

    lax.while_loop(cond, body, (jnp.int32(0), more()))
    o_ref[...] = acc_ref[...]


def _sb_attention(proj, bsz, seq, tq=512, tk=256):
    t = proj.shape[0]
    tq = min(tq, seq)
    nq = seq // tq
    tri = (lax.broadcasted_iota(jnp.int32, (tk, tk), 0)
           > lax.broadcasted_iota(jnp.int32, (tk, tk), 1)).astype(BF16)
    cq, ck, cv = COL_SQ // SB_DIM, COL_SK // SB_DIM, COL_SV // SB_DIM
    return pl.pallas_call(
        functools.partial(_sb_attn_kernel, tq=tq, tk=tk),
        grid=(bsz, SB_HEADS, nq),
        in_specs=[pl.BlockSpec((tq, SB_DIM), lambda b, h, i: (b * nq + i, cq + h)),
                  pl.BlockSpec((seq, SB_DIM), lambda b, h, i: (b, ck + h)),
                  pl.BlockSpec((seq, SB_DIM), lambda b, h, i: (b, cv + h)),
                  pl.BlockSpec((tk, tk), lambda b, h, i: (0, 0))],
        out_specs=pl.BlockSpec((tq, SB_DIM), lambda b, h, i: (b * nq + i, h)),
        out_shape=jax.ShapeDtypeStruct((t, SB_WIDTH), F32),
        scratch_shapes=[pltpu.VMEM((tq, 1), F32), pltpu.VMEM((tq, SB_DIM), F32)],
        compiler_params=_params("arbitrary", "arbitrary", "arbitrary"),
        name="sb_attention",
    )(proj, proj, proj, tri)


def _out_proj_kernel(om_ref, os_ref, gm_ref, gs_ref, wo_ref, x_ref, gate_ref, g2_ref,
                     sh_ref, sc_ref, rw_ref, xo_ref, h2_ref, lg_ref):
    wm = om_ref.shape[1]
    nm = (_rms(om_ref[...]) * gm_ref[...]).astype(BF16)
    ns = (_rms(os_ref[...]) * gs_ref[...]).astype(BF16)
    y = (jnp.dot(nm, wo_ref[:wm, :], preferred_element_type=F32)
         + jnp.dot(ns, wo_ref[wm:, :], preferred_element_type=F32))
    xn = x_ref[...] + gate_ref[...] * y
    xo_ref[...] = xn
    h2 = _rms(xn) * g2_ref[...]
    h2 = h2 * (1.0 + sc_ref[...]) + sh_ref[...]
    h2_ref[...] = h2
    lg_ref[...] = lax.dot_general(rw_ref[...], h2.astype(BF16), (((1,), (1,)), ((), ())),
                                  preferred_element_type=F32)


def _out_proj(o_mla, o_sb, gm, gs, w_o, x, mod, g2, rw_t, layer, seq, tm=256):
    t, d = x.shape
    per_b = seq // tm
    row = lambda i: (i, 0)
    full2 = lambda i: (0, 0)
    modspec = lambda j: pl.BlockSpec((None, 1, d), lambda i: ((i // per_b) * 6 + j, 0, 0))
    return pl.pallas_call(
        _out_proj_kernel,
        grid=(t // tm,),
        in_specs=[pl.BlockSpec((tm, o_mla.shape[1]), row),
                  pl.BlockSpec((tm, o_sb.shape[1]), row),
                  pl.BlockSpec((1, o_mla.shape[1]), full2),
                  pl.BlockSpec((1, o_sb.shape[1]), full2),
                  pl.BlockSpec((None,) + w_o.shape[1:], lambda i: (layer, 0, 0)),
                  pl.BlockSpec((tm, d), row),
                  modspec(2),
                  pl.BlockSpec((1, d), full2),
                  modspec(3),
                  modspec(4),
                  pl.BlockSpec(rw_t.shape, full2)],
        out_specs=[pl.BlockSpec((tm, d), row),
                   pl.BlockSpec((tm, d), row),
                   pl.BlockSpec((N_EXPERTS, tm), lambda i: (0, i))],
        out_shape=[jax.ShapeDtypeStruct((t, d), F32),
                   jax.ShapeDtypeStruct((t, d), F32),
                   jax.ShapeDtypeStruct((N_EXPERTS, t), F32)],
        compiler_params=_params("arbitrary"),
        name="out_proj",
    )(o_mla, o_sb, gm, gs, w_o, x, mod, g2, mod, mod, rw_t)


def _route_kernel(lg_ref, rb_ref, tri_ref, e_ref, w_ref, rank_ref, cnt_ref, run_ref):
    i = pl.program_id(0)

    @pl.when(i == 0)
    def _():
        run_ref[...] = jnp.zeros_like(run_ref)

    scores = jax.nn.sigmoid(lg_ref[...])
    sel = scores + rb_ref[...]
    tm = sel.shape[1]
    epg = EXPERTS_PER_GROUP
    srow = [sel[e:e + 1, :] for e in range(N_EXPERTS)]
    prow = [scores[e:e + 1, :] for e in range(N_EXPERTS)]

    gscore = []
    for g in range(N_GROUPS):
        a, b, c, d = srow[g * epg:(g + 1) * epg]
        hi1, lo1 = jnp.maximum(a, b), jnp.minimum(a, b)
        hi2, lo2 = jnp.maximum(c, d), jnp.minimum(c, d)
        gscore.append(jnp.maximum(hi1, hi2)
                      + jnp.maximum(jnp.minimum(hi1, hi2), jnp.maximum(lo1, lo2)))
    best = gscore[0]
    gidx = jnp.zeros((1, tm), jnp.int32)
    for g in range(1, N_GROUPS):
        better = gscore[g] > best
        gidx = jnp.where(better, g, gidx)
        best = jnp.where(better, gscore[g], best)

    def pick(rows, j):
        out = rows[j]
        for g in range(1, N_GROUPS):
            out = jnp.where(gidx == g, rows[g * epg + j], out)
        return out

    v = [pick(srow, j) for j in range(epg)]
    p = [pick(prow, j) for j in range(epg)]
    pos = []
    for j in range(epg):
        r = jnp.zeros((1, tm), jnp.int32)
        for o in range(epg):
            if o == j:
                continue
            ahead = (v[o] > v[j]) | ((v[o] == v[j]) & (o < j))
            r = r + ahead.astype(jnp.int32)
        pos.append(r)
    local, wsel = [], []
    for k in range(TOP_K):
        lk = jnp.zeros((1, tm), jnp.int32)
        wk = jnp.zeros((1, tm), F32)
        for j in range(epg):
            hit = pos[j] == k
            lk = jnp.where(hit, j, lk)
            wk = jnp.where(hit, p[j], wk)
        local.append(lk)
        wsel.append(wk)
    wsum = wsel[0] + wsel[1]
    idx = [gidx * epg + local[k] for k in range(TOP_K)]

    eiota = lax.broadcasted_iota(jnp.int32, (N_EXPERTS, tm), 0)
    hit = [eiota == idx[k] for k in range(TOP_K)]
    onehot = (hit[0] | hit[1]).astype(F32)
    before = jnp.dot(onehot.astype(BF16), tri_ref[...], preferred_element_type=F32) + run_ref[...]
    for k in range(TOP_K):
        e_ref[pl.ds(k, 1), :] = idx[k]
        w_ref[pl.ds(k, 1), :] = wsel[k] / wsum
        rank_ref[pl.ds(k, 1), :] = jnp.sum(jnp.where(hit[k], before, 0.0), axis=0,
                                           keepdims=True).astype(jnp.int32)
    run_ref[...] += jnp.sum(onehot, axis=1, keepdims=True)
    cnt_ref[...] = jnp.broadcast_to(run_ref[...], cnt_ref.shape)


def _route(logits_t, router_b, tm=512):
    t = logits_t.shape[1]
    tm = min(tm, t)
    tri = (lax.broadcasted_iota(jnp.int32, (tm, tm), 0)
           < lax.broadcasted_iota(jnp.int32, (tm, tm), 1)).astype(BF16)
    blk = lambda rows: pl.BlockSpec((rows, tm), lambda i: (0, i))
    return pl.pallas_call(
        _route_kernel,
        grid=(t // tm,),
        in_specs=[blk(N_EXPERTS),
                  pl.BlockSpec((N_EXPERTS, 1), lambda i: (0, 0)),
                  pl.BlockSpec((tm, tm), lambda i: (0, 0))],
        out_specs=[blk(TOP_K), blk(TOP_K), blk(TOP_K),
                   pl.BlockSpec((N_EXPERTS, LANE), lambda i: (0, 0))],
        out_shape=[jax.ShapeDtypeStruct((TOP_K, t), jnp.int32),
                   jax.ShapeDtypeStruct((TOP_K, t), F32),
                   jax.ShapeDtypeStruct((TOP_K, t), jnp.int32),
                   jax.ShapeDtypeStruct((N_EXPERTS, LANE), F32)],
        scratch_shapes=[pltpu.VMEM((N_EXPERTS, 1), F32)],
        compiler_params=_params("arbitrary"),
        name="route",
    )(logits_t, router_b[:, None], tri)


def _dispatch_kernel(dest_ref, pend_ref, h_ref, o_ref, zero_ref, sem, zsem, *, tm, t):
    base = pl.program_id(0) * tm

    @pl.when(pl.program_id(0) == 0)
    def _():
        zero_ref[...] = jnp.zeros_like(zero_ref)

        def fill(e):
            end = pend_ref[e]
            start = pend_ref[e - 1] if e > 0 else 0
            dst = o_ref.at[pl.ds(pl.multiple_of(jnp.maximum(end - MOE_ROWS, 0), MOE_ROWS),
                                 MOE_ROWS), :]
            return end > start, pltpu.make_async_copy(zero_ref, dst, zsem)

        for e in range(N_EXPERTS):
            nonempty, cp = fill(e)
            pl.when(nonempty)(cp.start)
        for e in range(N_EXPERTS):
            nonempty, cp = fill(e)
            pl.when(nonempty)(cp.wait)

        def tail(r):
            dst = o_ref.at[pl.ds(pl.multiple_of(r * MOE_ROWS, MOE_ROWS), MOE_ROWS), :]
            return pltpu.make_async_copy(zero_ref, dst, zsem)

        first_free = pend_ref[N_EXPERTS - 1] // MOE_ROWS
        n_blocks = o_ref.shape[0] // MOE_ROWS
        lax.fori_loop(first_free, n_blocks, lambda r, c: (tail(r).start(), c)[1], 0)
        lax.fori_loop(first_free, n_blocks, lambda r, c: (tail(r).wait(), c)[1], 0)

    def issue(r, carry):
        for k in range(TOP_K):
            d = dest_ref[k * t + base + r]
            pltpu.make_async_copy(h_ref.at[pl.ds(r, 1), :], o_ref.at[pl.ds(d, 1), :], sem).start()
        return carry

    lax.fori_loop(0, tm, issue, 0, unroll=DMA_ISSUE_UNROLL)
    for k in range(TOP_K):
        pltpu.make_async_copy(h_ref, o_ref.at[pl.ds(0, tm), :], sem).wait()


def _dispatch(h2, dest_flat, pad_end, rows, tm=256):
    t, d = h2.shape
    return pl.pallas_call(
        functools.partial(_dispatch_kernel, tm=tm, t=t),
        grid_spec=pltpu.PrefetchScalarGridSpec(
            num_scalar_prefetch=2,
            grid=(t // tm,),
            in_specs=[pl.BlockSpec((tm, d), lambda i, dest, pend: (i, 0))],
            out_specs=pl.BlockSpec(memory_space=pl.ANY),
            scratch_shapes=[pltpu.VMEM((MOE_ROWS, d), h2.dtype),
                            pltpu.SemaphoreType.DMA(()), pltpu.SemaphoreType.DMA(())]),
        out_shape=jax.ShapeDtypeStruct((rows, d), h2.dtype),
        compiler_params=_params("arbitrary"),
        name="moe_dispatch",
    )(dest_flat, pad_end, h2)


def _expert_kernel(be_ref, nu_ref, x_ref, wg_ref, wu_ref, wd_ref, y_ref, wgb, wub, wdb):
    r = pl.program_id(0)
    live = r < nu_ref[0]
    new_expert = (r == 0) | (be_ref[jnp.maximum(r - 1, 0)] != be_ref[r])

    @pl.when(live & new_expert)
    def _():
        wgb[...] = wg_ref[...].astype(BF16)
        wub[...] = wu_ref[...].astype(BF16)
        wdb[...] = wd_ref[...].astype(BF16)

    @pl.when(live)
    def _():
        xb = x_ref[...].astype(BF16)
        g = jnp.dot(xb, wgb[...], preferred_element_type=F32)
        u = jnp.dot(xb, wub[...], preferred_element_type=F32)
        a = (g * jax.nn.sigmoid(g)) * u
        y_ref[...] = jnp.dot(a.astype(BF16), wdb[...], preferred_element_type=F32)

    @pl.when(jnp.logical_not(live))
    def _():
        y_ref[...] = jnp.zeros_like(y_ref)


def _experts(x_sorted, block_e, n_used, w_gate, w_up, w_down, layer):
    rows, d = x_sorted.shape
    de = w_gate.shape[-1]
    bm = MOE_ROWS
    blk = lambda r, be, nu: (jnp.minimum(r, nu[0] - 1), 0)
    return pl.pallas_call(
        _expert_kernel,
        grid_spec=pltpu.PrefetchScalarGridSpec(
            num_scalar_prefetch=2,
            grid=(rows // bm,),
            in_specs=[pl.BlockSpec((bm, d), blk),
                      pl.BlockSpec((None, None, d, de), lambda r, be, nu: (layer, be[r], 0, 0)),
                      pl.BlockSpec((None, None, d, de), lambda r, be, nu: (layer, be[r], 0, 0)),
                      pl.BlockSpec((None, None, de, d), lambda r, be, nu: (layer, be[r], 0, 0))],
            out_specs=pl.BlockSpec((bm, d), lambda r, be, nu: (r, 0)),
            scratch_shapes=[pltpu.VMEM((d, de), BF16), pltpu.VMEM((d, de), BF16),
                            pltpu.VMEM((de, d), BF16)]),
        out_shape=jax.ShapeDtypeStruct((rows, d), F32),
        compiler_params=_params("arbitrary"),
        name="moe_experts",
    )(block_e, n_used, x_sorted, w_gate, w_up, w_down)


def _combine_kernel(dest_ref, y_ref, x_ref, gate_ref, w_ref, o_ref, buf, sem, *, tm, t):
    base = pl.program_id(0) * tm

    def issue(r, carry):
        for k in range(TOP_K):
            d = dest_ref[k * t + base + r]
            pltpu.make_async_copy(y_ref.at[pl.ds(d, 1), :], buf.at[k, pl.ds(r, 1), :], sem).start()
        return carry

    lax.fori_loop(0, tm, issue, 0, unroll=DMA_ISSUE_UNROLL)
    for k in range(TOP_K):
        pltpu.make_async_copy(y_ref.at[pl.ds(0, tm), :], buf.at[k], sem).wait()
    w = w_ref[...]
    y = w[:, 0:1] * buf[0] + w[:, 1:2] * buf[1]
    o_ref[...] = x_ref[...] + gate_ref[...] * y


def _combine(y_sorted, dest_flat, x, mod, w_tok, seq, tm=256):
    t, d = x.shape
    per_b = seq // tm
    return pl.pallas_call(
        functools.partial(_combine_kernel, tm=tm, t=t),
        grid_spec=pltpu.PrefetchScalarGridSpec(
            num_scalar_prefetch=1,
            grid=(t // tm,),
            in_specs=[pl.BlockSpec(memory_space=pl.ANY),
                      pl.BlockSpec((tm, d), lambda i, dest: (i, 0)),
                      pl.BlockSpec((None, 1, d), lambda i, dest: ((i // per_b) * 6 + 5, 0, 0)),
                      pl.BlockSpec((tm, TOP_K), lambda i, dest: (i, 0))],
            out_specs=pl.BlockSpec((tm, d), lambda i, dest: (i, 0)),
            scratch_shapes=[pltpu.VMEM((TOP_K, tm, d), F32), pltpu.SemaphoreType.DMA(())]),
        out_shape=jax.ShapeDtypeStruct((t, d), F32),
        compiler_params=_params("arbitrary"),
        name="moe_combine",
    )(dest_flat, y_sorted, x, mod, w_tok)


def _rope_tables(seq):
    half = MLA_ROPE // 2
    inv = 1.0 / (ROPE_BASE ** (jnp.arange(0, MLA_ROPE, 2, dtype=F32) / MLA_ROPE))
    ang = jnp.arange(seq, dtype=F32)[:, None] * inv[None, :]
    cos, sin = jnp.cos(ang), jnp.sin(ang)
    z = jnp.zeros((seq, half), F32)
    zz = jnp.zeros((seq, LANE - MLA_ROPE), F32)
    cos_t = jnp.concatenate([cos, cos, zz], axis=1)
    s1_t = jnp.concatenate([z, sin, zz], axis=1)
    s2_t = jnp.concatenate([-sin, z, zz], axis=1)
    return cos_t, s1_t, s2_t


def _pad_last(a, width):
    return jnp.pad(a, [(0, 0)] * (a.ndim - 1) + [(0, width - a.shape[-1])])


def kernel(x, c, norm1_g, norm2_g, w_ada, b_ada, w_in, q_lora_g, kv_lora_g, w_uq, w_ukv,
           q_head_g, k_head_g, mla_out_g, sb_out_g, w_o, router_w, router_b,
           w_gate, w_up, w_down):
    bsz, seq, d = x.shape
    t = bsz * seq
    depth = w_ada.shape[0]

    sb_col0 = COL_KR + MLA_ROPE
    w_in_mla = _pad_last(w_in[..., :sb_col0], MLA_IN_COLS).astype(BF16)
    sb_col_scale = jnp.where(jnp.arange(SB_IN_COLS) < SB_WIDTH, SB_DIM ** -0.5 * LOG2_E, 1.0)
    w_in_sb = (w_in[..., sb_col0:] * sb_col_scale.astype(F32)).astype(BF16)
    wq = w_uq.reshape(depth, Q_LORA, MLA_HEADS, MLA_QK)
    wuq_p = jnp.concatenate(
        [wq[..., :MLA_NOPE].reshape(depth, Q_LORA, -1),
         _pad_last(wq[..., MLA_NOPE:], LANE).reshape(depth, Q_LORA, -1)], axis=-1).astype(BF16)
    wkv = w_ukv.reshape(depth, KV_LORA, MLA_HEADS, MLA_NOPE + MLA_V)
    wukv_p = jnp.concatenate(
        [wkv[..., :MLA_NOPE].reshape(depth, KV_LORA, -1),
         wkv[..., MLA_NOPE:].reshape(depth, KV_LORA, -1)], axis=-1).astype(BF16)
    w_o_b = w_o.astype(BF16)
    q_scale = MLA_QK ** -0.5 * LOG2_E
    gqn = (q_head_g[:, None, :MLA_NOPE] * q_scale)
    gqr = _pad_last(q_head_g[:, None, MLA_NOPE:] * q_scale, LANE)
    gkn = k_head_g[:, None, :MLA_NOPE]
    gkr = _pad_last(k_head_g[:, None, MLA_NOPE:], LANE)
    rope_tabs = _rope_tables(seq)
    rw_t = router_w.T.astype(BF16)

    mod_all = _ada_modulation(c, w_ada, b_ada)
    n_blocks = -(-(t * TOP_K) // MOE_ROWS) + N_EXPERTS
    rows = n_blocks * MOE_ROWS

    xt = x.reshape(t, d)
    for l in range(depth):
        mod = mod_all[l].reshape(bsz * 6, 1, d)
        proj_mla, proj_sb = _in_proj(xt, norm1_g[l][None, :], mod, w_in_mla, w_in_sb, l, seq)
        q2d, k2d, v2d = _mla_prep(proj_mla, rope_tabs, q_lora_g[:, None, :],
                                  kv_lora_g[:, None, :], wuq_p, wukv_p, gqn, gqr, gkn, gkr, l, seq)
        o_mla = _mla_attention(q2d, k2d, v2d, bsz, seq)
        o_sb = _sb_attention(proj_sb, bsz, seq)
        xt, h2, logits_t = _out_proj(o_mla, o_sb, mla_out_g[l][None, :], sb_out_g[l][None, :],
                                     w_o_b, xt, mod, norm2_g[l][None, :], rw_t, l, seq)
        e_idx, w_tok, rank, cnt = _route(logits_t, router_b)
        counts = cnt[:, 0].astype(jnp.int32)
        padded = (counts + MOE_ROWS - 1) // MOE_ROWS * MOE_ROWS
        pad_end = jnp.cumsum(padded)
        pad_start = pad_end - padded
        eids = jnp.arange(N_EXPERTS, dtype=jnp.int32)
        seg_start = jnp.sum(jnp.where(e_idx[None] == eids[:, None, None],
                                      pad_start[:, None, None], 0), axis=0)
        dest = (seg_start + rank).reshape(TOP_K * t).astype(jnp.int32)
        n_used = (pad_end[-1:] // MOE_ROWS).astype(jnp.int32)
        first_row = jnp.minimum(jnp.arange(n_blocks, dtype=jnp.int32), n_used - 1) * MOE_ROWS
        block_e = jnp.sum(pad_end[None, :] <= first_row[:, None], axis=1).astype(jnp.int32)
        x_sorted = _dispatch(h2, dest, pad_end.astype(jnp.int32), rows)
        y_sorted = _experts(x_sorted, block_e, n_used, w_gate, w_up, w_down, l)
        xt = _combine(y_sorted, dest, xt, mod, w_tok.T, seq)
    return xt.reshape(bsz, seq, d)
```

```python
import functools

import jax
import jax.numpy as jnp
from jax import lax
from jax.experimental import pallas as pl
from jax.experimental.pallas import tpu as pltpu

F32 = jnp.float32
BF16 = jnp.bfloat16

EPS = 1e-6
ROPE_BASE = 10000.0
CHUNK = 64

MLA_HEADS = 8
MLA_NOPE = 128
MLA_ROPE = 64
MLA_QK = MLA_NOPE + MLA_ROPE
MLA_V = 128
Q_LORA = 512
KV_LORA = 256
SB_HEADS = 8
SB_DIM = 128
SB_WIDTH = SB_HEADS * SB_DIM
LANE = 128
QK_PAD = 2 * LANE
COL_CQ = 0
COL_CKV = Q_LORA
COL_KR = Q_LORA + KV_LORA
MLA_IN_COLS = COL_KR + LANE
COL_SQ = 0
COL_SK = SB_WIDTH
COL_SV = 2 * SB_WIDTH
SB_IN_COLS = 3 * SB_WIDTH

N_EXPERTS = 16
N_GROUPS = 4
EXPERTS_PER_GROUP = N_EXPERTS // N_GROUPS
TOP_K = 2
D_EXPERT = 512
MOE_ROWS = 256
DMA_ISSUE_UNROLL = 8
LOG2_E = 1.4426950408889634
SB_UNDERFLOW_LOG2 = 150.0
SB_FINISHED_LOG2 = 1e30

VMEM_LIMIT = 56 * 1024 * 1024


def _params(*dims):
    return pltpu.CompilerParams(dimension_semantics=dims, vmem_limit_bytes=VMEM_LIMIT)


def _rms(x):
    return x * lax.rsqrt(jnp.mean(x * x, axis=-1, keepdims=True) + EPS)


def _ada_kernel(c_ref, w_ref, b_ref, o_ref):
    w = w_ref[...]
    for b in range(c_ref.shape[0]):
        cb = c_ref[b]
        cb = cb * jax.nn.sigmoid(cb)
        o_ref[pl.ds(b, 1), :] = jnp.sum(cb * w, axis=0, keepdims=True) + b_ref[...]


def _ada_modulation(c, w_ada, b_ada, tn=1024):
    depth, d, n = w_ada.shape
    bsz = c.shape[0]
    return pl.pallas_call(
        _ada_kernel,
        grid=(depth, n // tn),
        in_specs=[pl.BlockSpec((bsz, d, 1), lambda l, j: (0, 0, 0)),
                  pl.BlockSpec((None, d, tn), lambda l, j: (l, 0, j)),
                  pl.BlockSpec((None, 1, tn), lambda l, j: (l, 0, j))],
        out_specs=pl.BlockSpec((None, bsz, tn), lambda l, j: (l, 0, j)),
        out_shape=jax.ShapeDtypeStruct((depth, bsz, n), F32),
        compiler_params=_params("arbitrary", "arbitrary"),
        name="ada_modulation",
    )(c[:, :, None], w_ada, b_ada[:, None, :])


def _in_proj_kernel(x_ref, g_ref, sh_ref, sc_ref, wa_ref, wb_ref, oa_ref, ob_ref):
    h = _rms(x_ref[...]) * g_ref[...]
    h = (h * (1.0 + sc_ref[...]) + sh_ref[...]).astype(BF16)
    oa_ref[...] = jnp.dot(h, wa_ref[...], preferred_element_type=F32).astype(oa_ref.dtype)
    ob_ref[...] = jnp.dot(h, wb_ref[...], preferred_element_type=F32).astype(ob_ref.dtype)


def _in_proj(x, g, mod, w_mla, w_sb, layer, seq, tm=256):
    t, d = x.shape
    na, nb = w_mla.shape[-1], w_sb.shape[-1]
    per_b = seq // tm
    return pl.pallas_call(
        _in_proj_kernel,
        grid=(t // tm,),
        in_specs=[pl.BlockSpec((tm, d), lambda i: (i, 0)),
                  pl.BlockSpec((1, d), lambda i: (0, 0)),
                  pl.BlockSpec((None, 1, d), lambda i: ((i // per_b) * 6 + 0, 0, 0)),
                  pl.BlockSpec((None, 1, d), lambda i: ((i // per_b) * 6 + 1, 0, 0)),
                  pl.BlockSpec((None, d, na), lambda i: (layer, 0, 0)),
                  pl.BlockSpec((None, d, nb), lambda i: (layer, 0, 0))],
        out_specs=[pl.BlockSpec((tm, na), lambda i: (i, 0)),
                   pl.BlockSpec((tm, nb), lambda i: (i, 0))],
        out_shape=[jax.ShapeDtypeStruct((t, na), BF16),
                   jax.ShapeDtypeStruct((t, nb), BF16)],
        compiler_params=_params("arbitrary"),
        name="in_proj",
    )(x, g, mod, mod, w_mla, w_sb)


def _mla_prep_kernel(cq_ref, ckv_ref, kr_ref, cos_ref, s1_ref, s2_ref, gql_ref, gkvl_ref,
                     wuq_ref, wukv_ref, gqn_ref, gqr_ref, gkn_ref, gkr_ref,
                     q_ref, k_ref, v_ref):
    nope_w = MLA_HEADS * MLA_NOPE
    cqn = _rms(cq_ref[...].astype(F32)) * gql_ref[...]
    q = jnp.dot(cqn.astype(BF16), wuq_ref[...], preferred_element_type=F32)
    ckvn = _rms(ckv_ref[...].astype(F32)) * gkvl_ref[...]
    kv = jnp.dot(ckvn.astype(BF16), wukv_ref[...], preferred_element_type=F32)
    cosv, s1, s2 = cos_ref[...], s1_ref[...], s2_ref[...]

    def rope(x):
        return (x * cosv + pltpu.roll(x, MLA_ROPE // 2, 1) * s1
                + pltpu.roll(x, LANE - MLA_ROPE // 2, 1) * s2)

    kr = kr_ref[...].astype(F32)
    kr_ss = jnp.sum(kr * kr, axis=-1, keepdims=True)
    kr_roped = rope(kr * gkr_ref[...])
    gqn, gqr, gkn = gqn_ref[...], gqr_ref[...], gkn_ref[...]
    for h in range(MLA_HEADS):
        qn = q[:, h * LANE:(h + 1) * LANE]
        qr = q[:, nope_w + h * LANE:nope_w + (h + 1) * LANE]
        ss = jnp.sum(qn * qn, axis=-1, keepdims=True) + jnp.sum(qr * qr, axis=-1, keepdims=True)
        r = lax.rsqrt(ss * (1.0 / MLA_QK) + EPS)
        q_ref[:, h * QK_PAD:h * QK_PAD + LANE] = (qn * r * gqn).astype(BF16)
        q_ref[:, h * QK_PAD + LANE:(h + 1) * QK_PAD] = (rope(qr * gqr) * r).astype(BF16)
        kn = kv[:, h * LANE:(h + 1) * LANE]
        rk = lax.rsqrt((jnp.sum(kn * kn, axis=-1, keepdims=True) + kr_ss) * (1.0 / MLA_QK) + EPS)
        k_ref[:, h * QK_PAD:h * QK_PAD + LANE] = (kn * rk * gkn).astype(BF16)
        k_ref[:, h * QK_PAD + LANE:(h + 1) * QK_PAD] = (kr_roped * rk).astype(BF16)
        v_ref[:, 2 * h * MLA_V:(2 * h + 1) * MLA_V] = (
            kv[:, nope_w + h * MLA_V:nope_w + (h + 1) * MLA_V].astype(BF16))
        v_ref[:, (2 * h + 1) * MLA_V:(2 * h + 2) * MLA_V] = jnp.ones((kn.shape[0], MLA_V), BF16)


def _mla_prep(proj, rope_tabs, gql, gkvl, wuq, wukv, gqn, gqr, gkn, gkr, layer, seq, tm=256):
    t = proj.shape[0]
    per_b = seq // tm
    cos_t, s1_t, s2_t = rope_tabs
    row = lambda i: (i, 0)
    full2 = lambda i: (0, 0)
    lay = lambda i: (layer, 0, 0)
    tab = lambda i: (i % per_b, 0)
    return pl.pallas_call(
        _mla_prep_kernel,
        grid=(t // tm,),
        in_specs=[pl.BlockSpec((tm, Q_LORA), lambda i: (i, COL_CQ // Q_LORA)),
                  pl.BlockSpec((tm, KV_LORA), lambda i: (i, COL_CKV // KV_LORA)),
                  pl.BlockSpec((tm, LANE), lambda i: (i, COL_KR // LANE)),
                  pl.BlockSpec((tm, LANE), tab),
                  pl.BlockSpec((tm, LANE), tab),
                  pl.BlockSpec((tm, LANE), tab),
                  pl.BlockSpec((None, 1, Q_LORA), lay),
                  pl.BlockSpec((None, 1, KV_LORA), lay),
                  pl.BlockSpec((None,) + wuq.shape[1:], lay),
                  pl.BlockSpec((None,) + wukv.shape[1:], lay),
                  pl.BlockSpec((None, 1, LANE), lay),
                  pl.BlockSpec((None, 1, LANE), lay),
                  pl.BlockSpec((None, 1, LANE), lay),
                  pl.BlockSpec((None, 1, LANE), lay)],
        out_specs=[pl.BlockSpec((tm, MLA_HEADS * QK_PAD), row),
                   pl.BlockSpec((tm, MLA_HEADS * QK_PAD), row),
                   pl.BlockSpec((tm, MLA_HEADS * 2 * MLA_V), row)],
        out_shape=[jax.ShapeDtypeStruct((t, MLA_HEADS * QK_PAD), BF16),
                   jax.ShapeDtypeStruct((t, MLA_HEADS * QK_PAD), BF16),
                   jax.ShapeDtypeStruct((t, MLA_HEADS * 2 * MLA_V), BF16)],
        compiler_params=_params("arbitrary"),
        name="mla_prep",
    )(proj, proj, proj, cos_t, s1_t, s2_t, gql, gkvl, wuq, wukv, gqn, gqr, gkn, gkr)


def _mla_attn_kernel(q_ref, k_ref, v_ref, o_ref, m_ref, l_ref, acc_ref, sa_ref, sb_ref, *, tq):
    i = pl.program_id(2)
    q = q_ref[...]

    def scores(kb):
        k = k_ref[pl.ds(pl.multiple_of(kb * tq, tq), tq), :]
        return lax.dot_general(q, k, (((1,), (1,)), ((), ())), preferred_element_type=F32)

    def probs(s, m):
        return jnp.concatenate(
            [jnp.exp2(s[:, c * LANE:(c + 1) * LANE] - m) for c in range(tq // LANE)],
            axis=1).astype(BF16)

    def pv(p, kb):
        return jnp.dot(p, v_ref[pl.ds(pl.multiple_of(kb * tq, tq), tq), :],
                       preferred_element_type=F32)

    s = scores(i)
    q_chunk = lax.broadcasted_iota(jnp.int32, (tq, 1), 0) // CHUNK
    k_chunk = lax.broadcasted_iota(jnp.int32, (1, tq), 1) // CHUNK
    s = jnp.where(k_chunk <= q_chunk, s, -jnp.inf)
    m = jnp.broadcast_to(jnp.max(s, axis=-1, keepdims=True), (tq, LANE))
    r = pv(probs(s, m), i)
    m_ref[...] = m
    acc_ref[...] = r[:, :MLA_V]
    l_ref[...] = r[:, MLA_V:]

    def update(s, kb):
        m_old = m_ref[...]
        m_new = jnp.maximum(m_old, jnp.max(s, axis=-1, keepdims=True))
        alpha = jnp.exp2(m_old - m_new)
        r = pv(probs(s, m_new), kb)
        acc_ref[...] = alpha * acc_ref[...] + r[:, :MLA_V]
        l_ref[...] = alpha * l_ref[...] + r[:, MLA_V:]
        m_ref[...] = m_new

    last = jnp.maximum(i - 1, 0)
    sa_ref[...] = scores(0)

    def pair(t, carry):
        j = 2 * t
        sb_ref[...] = scores(jnp.minimum(j + 1, last))
        update(sa_ref[...], j)
        sa_ref[...] = scores(jnp.minimum(j + 2, last))
        update(sb_ref[...], j + 1)
        return carry

    lax.fori_loop(0, i // 2, pair, 0)

    @pl.when(i % 2 == 1)
    def _():
        update(sa_ref[...], i - 1)

    o_ref[...] = acc_ref[...] / l_ref[...]


def _mla_attention(q2d, k2d, v2d, bsz, seq, tq=512):
    t = q2d.shape[0]
    tq = min(tq, seq)
    nq = seq // tq
    vw = 2 * MLA_V
    return pl.pallas_call(
        functools.partial(_mla_attn_kernel, tq=tq),
        grid=(bsz, MLA_HEADS, nq),
        in_specs=[pl.BlockSpec((tq, QK_PAD), lambda b, h, i: (b * nq + i, h)),
                  pl.BlockSpec((seq, QK_PAD), lambda b, h, i: (b, h)),
                  pl.BlockSpec((seq, vw), lambda b, h, i: (b, h))],
        out_specs=pl.BlockSpec((tq, MLA_V), lambda b, h, i: (b * nq + i, h)),
        out_shape=jax.ShapeDtypeStruct((t, MLA_HEADS * MLA_V), F32),
        scratch_shapes=[pltpu.VMEM((tq, LANE), F32), pltpu.VMEM((tq, MLA_V), F32),
                        pltpu.VMEM((tq, MLA_V), F32),
                        pltpu.VMEM((tq, tq), F32), pltpu.VMEM((tq, tq), F32)],
        compiler_params=_params("arbitrary", "arbitrary", "arbitrary"),
        name="mla_attention",
    )(q2d, k2d, v2d)


def _sb_attn_kernel(q_ref, k_ref, v_ref, tri_ref, o_ref, run_ref, acc_ref, *, tq, tk, heads):
    i = pl.program_id(2)
    tri = tri_ref[...]
    sub = tq // tk
    streams = [(g, r) for g in range(heads) for r in range(sub)]

    def block(s, kb, mask):
        g, r = streams[s]
        cols = slice(g * SB_DIM, (g + 1) * SB_DIM)
        keys = pl.ds(pl.multiple_of(kb * tk, tk), tk)
        q = q_ref[r * tk:(r + 1) * tk, cols]
        z = lax.dot_general(q, k_ref[keys, cols], (((1,), (1,)), ((), ())),
                            preferred_element_type=F32)
        cost = jnp.maximum(z, 0.0) + jnp.log2(1.0 + jnp.exp2(-jnp.abs(z)))
        if mask is not None:
            cost = jnp.where(mask, cost, 0.0)
        after = jnp.dot(cost.astype(BF16), tri, preferred_element_type=F32) + run_ref[s]
        a = jnp.exp2(z - cost - after)
        if mask is not None:
            a = jnp.where(mask, a, 0.0)
        acc_ref[s] += jnp.dot(a.astype(BF16), v_ref[keys, cols], preferred_element_type=F32)
        run_ref[s] += jnp.sum(cost, axis=-1, keepdims=True)

    def more():
        return jnp.min(run_ref[...]) < SB_UNDERFLOW_LOG2

    run_ref[...] = jnp.zeros_like(run_ref)
    acc_ref[...] = jnp.zeros_like(acc_ref)
    strictly_before = (lax.broadcasted_iota(jnp.int32, (tk, tk), 1)
                       < lax.broadcasted_iota(jnp.int32, (tk, tk), 0))
    for s, (g, r) in enumerate(streams):
        block(s, i * sub + r, strictly_before)

    def cond(c):
        return (c[0] < i * sub + sub - 1) & c[1]

    def body(c):
        for s, (g, r) in enumerate(streams):
            kb = i * sub + r - 1 - c[0]
            run_ref[s] = jnp.where(kb >= 0, run_ref[s], SB_FINISHED_LOG2)
            block(s, jnp.maximum(kb, 0), None)
        return c[0] + 1, more()

    lax.while_loop(cond, body, (jnp.int32(0), more()))
    for s, (g, r) in enumerate(streams):
        o_ref[r * tk:(r + 1) * tk, g * SB_DIM:(g + 1) * SB_DIM] = acc_ref[s]


def _sb_attention(proj, bsz, seq, tq=512, tk=256, heads=2):
    t = proj.shape[0]
    tq = min(tq, seq)
    nq = seq // tq
    tri = (lax.broadcasted_iota(jnp.int32, (tk, tk), 0)
           > lax.broadcasted_iota(jnp.int32, (tk, tk), 1)).astype(BF16)
    w = heads * SB_DIM
    cq, ck, cv = COL_SQ // w, COL_SK // w, COL_SV // w
    n_streams = heads * (tq // tk)
    return pl.pallas_call(
        functools.partial(_sb_attn_kernel, tq=tq, tk=tk, heads=heads),
        grid=(bsz, SB_HEADS // heads, nq),
        in_specs=[pl.BlockSpec((tq, w), lambda b, h, i: (b * nq + i, cq + h)),
                  pl.BlockSpec((seq, w), lambda b, h, i: (b, ck + h)),
                  pl.BlockSpec((seq, w), lambda b, h, i: (b, cv + h)),
                  pl.BlockSpec((tk, tk), lambda b, h, i: (0, 0))],
        out_specs=pl.BlockSpec((tq, w), lambda b, h, i: (b * nq + i, h)),
        out_shape=jax.ShapeDtypeStruct((t, SB_WIDTH), F32),
        scratch_shapes=[pltpu.VMEM((n_streams, tk, 1), F32),
                        pltpu.VMEM((n_streams, tk, SB_DIM), F32)],
        compiler_params=_params("arbitrary", "arbitrary", "arbitrary"),
        name="sb_attention",
    )(proj, proj, proj, tri)


def _out_proj_kernel(om_ref, os_ref, gm_ref, gs_ref, wo_ref, x_ref, gate_ref, g2_ref,
                     sh_ref, sc_ref, rw_ref, xo_ref, h2_ref, lg_ref):
    wm = om_ref.shape[1]
    nm = (_rms(om_ref[...]) * gm_ref[...]).astype(BF16)
    ns = (_rms(os_ref[...]) * gs_ref[...]).astype(BF16)
    y = (jnp.dot(nm, wo_ref[:wm, :], preferred_element_type=F32)
         + jnp.dot(ns, wo_ref[wm:, :], preferred_element_type=F32))
    xn = x_ref[...] + gate_ref[...] * y
    xo_ref[...] = xn
    h2 = _rms(xn) * g2_ref[...]
    h2 = h2 * (1.0 + sc_ref[...]) + sh_ref[...]
    h2_ref[...] = h2
    lg_ref[...] = lax.dot_general(rw_ref[...], h2.astype(BF16), (((1,), (1,)), ((), ())),
                                  preferred_element_type=F32)


def _out_proj(o_mla, o_sb, gm, gs, w_o, x, mod, g2, rw_t, layer, seq, tm=256):
    t, d = x.shape
    per_b = seq // tm
    row = lambda i: (i, 0)
    full2 = lambda i: (0, 0)
    modspec = lambda j: pl.BlockSpec((None, 1, d), lambda i: ((i // per_b) * 6 + j, 0, 0))
    return pl.pallas_call(
        _out_proj_kernel,
        grid=(t // tm,),
        in_specs=[pl.BlockSpec((tm, o_mla.shape[1]), row),
                  pl.BlockSpec((tm, o_sb.shape[1]), row),
                  pl.BlockSpec((1, o_mla.shape[1]), full2),
                  pl.BlockSpec((1, o_sb.shape[1]), full2),
                  pl.BlockSpec((None,) + w_o.shape[1:], lambda i: (layer, 0, 0)),
                  pl.BlockSpec((tm, d), row),
                  modspec(2),
                  pl.BlockSpec((1, d), full2),
                  modspec(3),
                  modspec(4),
                  pl.BlockSpec(rw_t.shape, full2)],
        out_specs=[pl.BlockSpec((tm, d), row),
                   pl.BlockSpec((tm, d), row),
                   pl.BlockSpec((N_EXPERTS, tm), lambda i: (0, i))],
        out_shape=[jax.ShapeDtypeStruct((t, d), F32),
                   jax.ShapeDtypeStruct((t, d), F32),
                   jax.ShapeDtypeStruct((N_EXPERTS, t), F32)],
        compiler_params=_params("arbitrary"),
        name="out_proj",
    )(o_mla, o_sb, gm, gs, w_o, x, mod, g2, mod, mod, rw_t)


def _route_kernel(lg_ref, rb_ref, tri_ref, e_ref, w_ref, rank_ref, cnt_ref, run_ref):
    i = pl.program_id(0)

    @pl.when(i == 0)
    def _():
        run_ref[...] = jnp.zeros_like(run_ref)

    scores = jax.nn.sigmoid(lg_ref[...])
    sel = scores + rb_ref[...]
    tm = sel.shape[1]
    epg = EXPERTS_PER_GROUP
    srow = [sel[e:e + 1, :] for e in range(N_EXPERTS)]
    prow = [scores[e:e + 1, :] for e in range(N_EXPERTS)]

    gscore = []
    for g in range(N_GROUPS):
        a, b, c, d = srow[g * epg:(g + 1) * epg]
        hi1, lo1 = jnp.maximum(a, b), jnp.minimum(a, b)
        hi2, lo2 = jnp.maximum(c, d), jnp.minimum(c, d)
        gscore.append(jnp.maximum(hi1, hi2)
                      + jnp.maximum(jnp.minimum(hi1, hi2), jnp.maximum(lo1, lo2)))
    best = gscore[0]
    gidx = jnp.zeros((1, tm), jnp.int32)
    for g in range(1, N_GROUPS):
        better = gscore[g] > best
        gidx = jnp.where(better, g, gidx)
        best = jnp.where(better, gscore[g], best)

    def pick(rows, j):
        out = rows[j]
        for g in range(1, N_GROUPS):
            out = jnp.where(gidx == g, rows[g * epg + j], out)
        return out

    v = [pick(srow, j) for j in range(epg)]
    p = [pick(prow, j) for j in range(epg)]
    pos = []
    for j in range(epg):
        r = jnp.zeros((1, tm), jnp.int32)
        for o in range(epg):
            if o == j:
                continue
            ahead = (v[o] > v[j]) | ((v[o] == v[j]) & (o < j))
            r = r + ahead.astype(jnp.int32)
        pos.append(r)
    local, wsel = [], []
    for k in range(TOP_K):
        lk = jnp.zeros((1, tm), jnp.int32)
        wk = jnp.zeros((1, tm), F32)
        for j in range(epg):
            hit = pos[j] == k
            lk = jnp.where(hit, j, lk)
            wk = jnp.where(hit, p[j], wk)
        local.append(lk)
        wsel.append(wk)
    wsum = wsel[0] + wsel[1]
    idx = [gidx * epg + local[k] for k in range(TOP_K)]

    eiota = lax.broadcasted_iota(jnp.int32, (N_EXPERTS, tm), 0)
    hit = [eiota == idx[k] for k in range(TOP_K)]
    onehot = (hit[0] | hit[1]).astype(F32)
    before = jnp.dot(onehot.astype(BF16), tri_ref[...], preferred_element_type=F32) + run_ref[...]
    for k in range(TOP_K):
        e_ref[pl.ds(k, 1), :] = idx[k]
        w_ref[pl.ds(k, 1), :] = wsel[k] / wsum
        rank_ref[pl.ds(k, 1), :] = jnp.sum(jnp.where(hit[k], before, 0.0), axis=0,
                                           keepdims=True).astype(jnp.int32)
    run_ref[...] += jnp.sum(onehot, axis=1, keepdims=True)
    cnt_ref[...] = jnp.broadcast_to(run_ref[...], cnt_ref.shape)


def _route(logits_t, router_b, tm=512):
    t = logits_t.shape[1]
    tm = min(tm, t)
    tri = (lax.broadcasted_iota(jnp.int32, (tm, tm), 0)
           < lax.broadcasted_iota(jnp.int32, (tm, tm), 1)).astype(BF16)
    blk = lambda rows: pl.BlockSpec((rows, tm), lambda i: (0, i))
    return pl.pallas_call(
        _route_kernel,
        grid=(t // tm,),
        in_specs=[blk(N_EXPERTS),
                  pl.BlockSpec((N_EXPERTS, 1), lambda i: (0, 0)),
                  pl.BlockSpec((tm, tm), lambda i: (0, 0))],
        out_specs=[blk(TOP_K), blk(TOP_K), blk(TOP_K),
                   pl.BlockSpec((N_EXPERTS, LANE), lambda i: (0, 0))],
        out_shape=[jax.ShapeDtypeStruct((TOP_K, t), jnp.int32),
                   jax.ShapeDtypeStruct((TOP_K, t), F32),
                   jax.ShapeDtypeStruct((TOP_K, t), jnp.int32),
                   jax.ShapeDtypeStruct((N_EXPERTS, LANE), F32)],
        scratch_shapes=[pltpu.VMEM((N_EXPERTS, 1), F32)],
        compiler_params=_params("arbitrary"),
        name="route",
    )(logits_t, router_b[:, None], tri)


def _dispatch_kernel(dest_ref, pend_ref, h_ref, o_ref, zero_ref, sem, zsem, *, tm, t):
    base = pl.program_id(0) * tm

    @pl.when(pl.program_id(0) == 0)
    def _():
        zero_ref[...] = jnp.zeros_like(zero_ref)

        def fill(e):
            end = pend_ref[e]
            start = pend_ref[e - 1] if e > 0 else 0
            dst = o_ref.at[pl.ds(pl.multiple_of(jnp.maximum(end - MOE_ROWS, 0), MOE_ROWS),
                                 MOE_ROWS), :]
            return end > start, pltpu.make_async_copy(zero_ref, dst, zsem)

        for e in range(N_EXPERTS):
            nonempty, cp = fill(e)
            pl.when(nonempty)(cp.start)
        for e in range(N_EXPERTS):
            nonempty, cp = fill(e)
            pl.when(nonempty)(cp.wait)

        def tail(r):
            dst = o_ref.at[pl.ds(pl.multiple_of(r * MOE_ROWS, MOE_ROWS), MOE_ROWS), :]
            return pltpu.make_async_copy(zero_ref, dst, zsem)

        first_free = pend_ref[N_EXPERTS - 1] // MOE_ROWS
        n_blocks = o_ref.shape[0] // MOE_ROWS
        lax.fori_loop(first_free, n_blocks, lambda r, c: (tail(r).start(), c)[1], 0)
        lax.fori_loop(first_free, n_blocks, lambda r, c: (tail(r).wait(), c)[1], 0)

    def issue(r, carry):
        for k in range(TOP_K):
            d = dest_ref[k * t + base + r]
            pltpu.make_async_copy(h_ref.at[pl.ds(r, 1), :], o_ref.at[pl.ds(d, 1), :], sem).start()
        return carry

    lax.fori_loop(0, tm, issue, 0, unroll=DMA_ISSUE_UNROLL)
    for k in range(TOP_K):
        pltpu.make_async_copy(h_ref, o_ref.at[pl.ds(0, tm), :], sem).wait()


def _dispatch(h2, dest_flat, pad_end, rows, tm=256):
    t, d = h2.shape
    return pl.pallas_call(
        functools.partial(_dispatch_kernel, tm=tm, t=t),
        grid_spec=pltpu.PrefetchScalarGridSpec(
            num_scalar_prefetch=2,
            grid=(t // tm,),
            in_specs=[pl.BlockSpec((tm, d), lambda i, dest, pend: (i, 0))],
            out_specs=pl.BlockSpec(memory_space=pl.ANY),
            scratch_shapes=[pltpu.VMEM((MOE_ROWS, d), h2.dtype),
                            pltpu.SemaphoreType.DMA(()), pltpu.SemaphoreType.DMA(())]),
        out_shape=jax.ShapeDtypeStruct((rows, d), h2.dtype),
        compiler_params=_params("arbitrary"),
        name="moe_dispatch",
    )(dest_flat, pad_end, h2)


def _expert_kernel(be_ref, nu_ref, x_ref, wg_ref, wu_ref, wd_ref, y_ref, wgb, wub, wdb):
    r = pl.program_id(0)
    live = r < nu_ref[0]
    new_expert = (r == 0) | (be_ref[jnp.maximum(r - 1, 0)] != be_ref[r])

    @pl.when(live & new_expert)
    def _():
        wgb[...] = wg_ref[...].astype(BF16)
        wub[...] = wu_ref[...].astype(BF16)
        wdb[...] = wd_ref[...].astype(BF16)

    @pl.when(live)
    def _():
        xb = x_ref[...].astype(BF16)
        g = jnp.dot(xb, wgb[...], preferred_element_type=F32)
        u = jnp.dot(xb, wub[...], preferred_element_type=F32)
        a = (g * jax.nn.sigmoid(g)) * u
        y_ref[...] = jnp.dot(a.astype(BF16), wdb[...], preferred_element_type=F32)

    @pl.when(jnp.logical_not(live))
    def _():
        y_ref[...] = jnp.zeros_like(y_ref)


def _experts(x_sorted, block_e, n_used, w_gate, w_up, w_down, layer):
    rows, d = x_sorted.shape
    de = w_gate.shape[-1]
    bm = MOE_ROWS
    blk = lambda r, be, nu: (jnp.minimum(r, nu[0] - 1), 0)
    return pl.pallas_call(
        _expert_kernel,
        grid_spec=pltpu.PrefetchScalarGridSpec(
            num_scalar_prefetch=2,
            grid=(rows // bm,),
            in_specs=[pl.BlockSpec((bm, d), blk),
                      pl.BlockSpec((None, None, d, de), lambda r, be, nu: (layer, be[r], 0, 0)),
                      pl.BlockSpec((None, None, d, de), lambda r, be, nu: (layer, be[r], 0, 0)),
                      pl.BlockSpec((None, None, de, d), lambda r, be, nu: (layer, be[r], 0, 0))],
            out_specs=pl.BlockSpec((bm, d), lambda r, be, nu: (r, 0)),
            scratch_shapes=[pltpu.VMEM((d, de), BF16), pltpu.VMEM((d, de), BF16),
                            pltpu.VMEM((de, d), BF16)]),
        out_shape=jax.ShapeDtypeStruct((rows, d), F32),
        compiler_params=_params("arbitrary"),
        name="moe_experts",
    )(block_e, n_used, x_sorted, w_gate, w_up, w_down)


def _combine_kernel(dest_ref, y_ref, x_ref, gate_ref, w_ref, o_ref, buf, sem, *, tm, t):
    base = pl.program_id(0) * tm

    def issue(r, carry):
        for k in range(TOP_K):
            d = dest_ref[k * t + base + r]
            pltpu.make_async_copy(y_ref.at[pl.ds(d, 1), :], buf.at[k, pl.ds(r, 1), :], sem).start()
        return carry

    lax.fori_loop(0, tm, issue, 0, unroll=DMA_ISSUE_UNROLL)
    for k in range(TOP_K):
        pltpu.make_async_copy(y_ref.at[pl.ds(0, tm), :], buf.at[k], sem).wait()
    w = w_ref[...]
    y = w[:, 0:1] * buf[0] + w[:, 1:2] * buf[1]
    o_ref[...] = x_ref[...] + gate_ref[...] * y


def _combine(y_sorted, dest_flat, x, mod, w_tok, seq, tm=256):
    t, d = x.shape
    per_b = seq // tm
    return pl.pallas_call(
        functools.partial(_combine_kernel, tm=tm, t=t),
        grid_spec=pltpu.PrefetchScalarGridSpec(
            num_scalar_prefetch=1,
            grid=(t // tm,),
            in_specs=[pl.BlockSpec(memory_space=pl.ANY),
                      pl.BlockSpec((tm, d), lambda i, dest: (i, 0)),
                      pl.BlockSpec((None, 1, d), lambda i, dest: ((i // per_b) * 6 + 5, 0, 0)),
                      pl.BlockSpec((tm, TOP_K), lambda i, dest: (i, 0))],
            out_specs=pl.BlockSpec((tm, d), lambda i, dest: (i, 0)),
            scratch_shapes=[pltpu.VMEM((TOP_K, tm, d), F32), pltpu.SemaphoreType.DMA(())]),
        out_shape=jax.ShapeDtypeStruct((t, d), F32),
        compiler_params=_params("arbitrary"),
        name="moe_combine",
    )(dest_flat, y_sorted, x, mod, w_tok)


def _rope_tables(seq):
    half = MLA_ROPE // 2
    inv = 1.0 / (ROPE_BASE ** (jnp.arange(0, MLA_ROPE, 2, dtype=F32) / MLA_ROPE))
    ang = jnp.arange(seq, dtype=F32)[:, None] * inv[None, :]
    cos, sin = jnp.cos(ang), jnp.sin(ang)
    z = jnp.zeros((seq, half), F32)
    zz = jnp.zeros((seq, LANE - MLA_ROPE), F32)
    cos_t = jnp.concatenate([cos, cos, zz], axis=1)
    s1_t = jnp.concatenate([z, sin, zz], axis=1)
    s2_t = jnp.concatenate([-sin, z, zz], axis=1)
    return cos_t, s1_t, s2_t


def _pad_last(a, width):
    return jnp.pad(a, [(0, 0)] * (a.ndim - 1) + [(0, width - a.shape[-1])])


def kernel(x, c, norm1_g, norm2_g, w_ada, b_ada, w_in, q_lora_g, kv_lora_g, w_uq, w_ukv,
           q_head_g, k_head_g, mla_out_g, sb_out_g, w_o, router_w, router_b,
           w_gate, w_up, w_down):
    bsz, seq, d = x.shape
    t = bsz * seq
    depth = w_ada.shape[0]

    sb_col0 = COL_KR + MLA_ROPE
    w_in_mla = _pad_last(w_in[..., :sb_col0], MLA_IN_COLS).astype(BF16)
    sb_col_scale = jnp.where(jnp.arange(SB_IN_COLS) < SB_WIDTH, SB_DIM ** -0.5 * LOG2_E, 1.0)
    w_in_sb = (w_in[..., sb_col0:] * sb_col_scale.astype(F32)).astype(BF16)
    wq = w_uq.reshape(depth, Q_LORA, MLA_HEADS, MLA_QK)
    wuq_p = jnp.concatenate(
        [wq[..., :MLA_NOPE].reshape(depth, Q_LORA, -1),
         _pad_last(wq[..., MLA_NOPE:], LANE).reshape(depth, Q_LORA, -1)], axis=-1).astype(BF16)
    wkv = w_ukv.reshape(depth, KV_LORA, MLA_HEADS, MLA_NOPE + MLA_V)
    wukv_p = jnp.concatenate(
        [wkv[..., :MLA_NOPE].reshape(depth, KV_LORA, -1),
         wkv[..., MLA_NOPE:].reshape(depth, KV_LORA, -1)], axis=-1).astype(BF16)
    w_o_b = w_o.astype(BF16)
    q_scale = MLA_QK ** -0.5 * LOG2_E
    gqn = (q_head_g[:, None, :MLA_NOPE] * q_scale)
    gqr = _pad_last(q_head_g[:, None, MLA_NOPE:] * q_scale, LANE)
    gkn = k_head_g[:, None, :MLA_NOPE]
    gkr = _pad_last(k_head_g[:, None, MLA_NOPE:], LANE)
    rope_tabs = _rope_tables(seq)
    rw_t = router_w.T.astype(BF16)

    mod_all = _ada_modulation(c, w_ada, b_ada)
    n_blocks = -(-(t * TOP_K) // MOE_ROWS) + N_EXPERTS
    rows = n_blocks * MOE_ROWS

    xt = x.reshape(t, d)
    for l in range(depth):
        mod = mod_all[l].reshape(bsz * 6, 1, d)
        proj_mla, proj_sb = _in_proj(xt, norm1_g[l][None, :], mod, w_in_mla, w_in_sb, l, seq)
        q2d, k2d, v2d = _mla_prep(proj_mla, rope_tabs, q_lora_g[:, None, :],
                                  kv_lora_g[:, None, :], wuq_p, wukv_p, gqn, gqr, gkn, gkr, l, seq)
        o_mla = _mla_attention(q2d, k2d, v2d, bsz, seq)
        o_sb = _sb_attention(proj_sb, bsz, seq)
        xt, h2, logits_t = _out_proj(o_mla, o_sb, mla_out_g[l][None, :], sb_out_g[l][None, :],
                                     w_o_b, xt, mod, norm2_g[l][None, :], rw_t, l, seq)
        e_idx, w_tok, rank, cnt = _route(logits_t, router_b)
        counts = cnt[:, 0].astype(jnp.int32)
        padded = (counts + MOE_ROWS - 1) // MOE_ROWS * MOE_ROWS
        pad_end = jnp.cumsum(padded)
        pad_start = pad_end - padded
        eids = jnp.arange(N_EXPERTS, dtype=jnp.int32)
        seg_start = jnp.sum(jnp.where(e_idx[None] == eids[:, None, None],
                                      pad_start[:, None, None], 0), axis=0)
        dest = (seg_start + rank).reshape(TOP_K * t).astype(jnp.int32)
        n_used = (pad_end[-1:] // MOE_ROWS).astype(jnp.int32)
        first_row = jnp.minimum(jnp.arange(n_blocks, dtype=jnp.int32), n_used - 1) * MOE_ROWS
        block_e = jnp.sum(pad_end[None, :] <= first_row[:, None], axis=1).astype(jnp.int32)
        x_sorted = _dispatch(h2, dest, pad_end.astype(jnp.int32), rows)
        y_sorted = _experts(x_sorted, block_e, n_used, w_gate, w_up, w_down, l)
        xt = _combine(y_sorted, dest, xt, mod, w_tok.T, seq)
    return xt.reshape(bsz, seq, d)
```

```python
import functools

import jax
import jax.numpy as jnp
from jax import lax
from jax.experimental import pallas as pl
from jax.experimental.pallas import tpu as pltpu

F32 = jnp.float32
BF16 = jnp.bfloat16

EPS = 1e-6
ROPE_BASE = 10000.0
CHUNK = 64

MLA_HEADS = 8
MLA_NOPE = 128
MLA_ROPE = 64
MLA_QK = MLA_NOPE + MLA_ROPE
MLA_V = 128
Q_LORA = 512
KV_LORA = 256
SB_HEADS = 8
SB_DIM = 128
SB_WIDTH = SB_HEADS * SB_DIM
LANE = 128
QK_PAD = 2 * LANE
COL_CQ = 0
COL_CKV = Q_LORA
COL_KR = Q_LORA + KV_LORA
MLA_IN_COLS = COL_KR + LANE
COL_SQ = 0
COL_SK = SB_WIDTH
COL_SV = 2 * SB_WIDTH
SB_IN_COLS = 3 * SB_WIDTH

N_EXPERTS = 16
N_GROUPS = 4
EXPERTS_PER_GROUP = N_EXPERTS // N_GROUPS
TOP_K = 2
D_EXPERT = 512
MOE_ROWS = 256
DMA_ISSUE_UNROLL = 8
LOG2_E = 1.4426950408889634
SB_UNDERFLOW_LOG2 = 150.0
SB_FINISHED_LOG2 = 1e30

VMEM_LIMIT = 56 * 1024 * 1024


def _params(*dims):
    return pltpu.CompilerParams(dimension_semantics=dims, vmem_limit_bytes=VMEM_LIMIT)


def _rms(x):
    return x * lax.rsqrt(jnp.mean(x * x, axis=-1, keepdims=True) + EPS)


def _pack_rows(x):
    n = x.shape[1] // 2
    lo = lax.bitcast_convert_type(x[:, :n].astype(BF16).astype(F32), jnp.uint32)
    hi = lax.bitcast_convert_type(x[:, n:].astype(BF16).astype(F32), jnp.uint32)
    return (lo >> 16) | (hi & jnp.uint32(0xFFFF0000))


def _unpack_rows(w):
    lo = lax.bitcast_convert_type(w << 16, F32)
    hi = lax.bitcast_convert_type(w & jnp.uint32(0xFFFF0000), F32)
    return lo, hi


def _ada_kernel(c_ref, w_ref, b_ref, o_ref):
    w = w_ref[...]
    for b in range(c_ref.shape[0]):
        cb = c_ref[b]
        cb = cb * jax.nn.sigmoid(cb)
        o_ref[pl.ds(b, 1), :] = jnp.sum(cb * w, axis=0, keepdims=True) + b_ref[...]


def _ada_modulation(c, w_ada, b_ada, tn=1024):
    depth, d, n = w_ada.shape
    bsz = c.shape[0]
    return pl.pallas_call(
        _ada_kernel,
        grid=(depth, n // tn),
        in_specs=[pl.BlockSpec((bsz, d, 1), lambda l, j: (0, 0, 0)),
                  pl.BlockSpec((None, d, tn), lambda l, j: (l, 0, j)),
                  pl.BlockSpec((None, 1, tn), lambda l, j: (l, 0, j))],
        out_specs=pl.BlockSpec((None, bsz, tn), lambda l, j: (l, 0, j)),
        out_shape=jax.ShapeDtypeStruct((depth, bsz, n), F32),
        compiler_params=_params("arbitrary", "arbitrary"),
        name="ada_modulation",
    )(c[:, :, None], w_ada, b_ada[:, None, :])


def _in_proj_kernel(x_ref, g_ref, sh_ref, sc_ref, wa_ref, wb_ref, oa_ref, ob_ref):
    h = _rms(x_ref[...]) * g_ref[...]
    h = (h * (1.0 + sc_ref[...]) + sh_ref[...]).astype(BF16)
    oa_ref[...] = jnp.dot(h, wa_ref[...], preferred_element_type=F32).astype(oa_ref.dtype)
    ob_ref[...] = jnp.dot(h, wb_ref[...], preferred_element_type=F32).astype(ob_ref.dtype)


def _in_proj(x, g, mod, w_mla, w_sb, layer, seq, tm=256):
    t, d = x.shape
    na, nb = w_mla.shape[-1], w_sb.shape[-1]
    per_b = seq // tm
    return pl.pallas_call(
        _in_proj_kernel,
        grid=(t // tm,),
        in_specs=[pl.BlockSpec((tm, d), lambda i: (i, 0)),
                  pl.BlockSpec((1, d), lambda i: (0, 0)),
                  pl.BlockSpec((None, 1, d), lambda i: ((i // per_b) * 6 + 0, 0, 0)),
                  pl.BlockSpec((None, 1, d), lambda i: ((i // per_b) * 6 + 1, 0, 0)),
                  pl.BlockSpec((None, d, na), lambda i: (layer, 0, 0)),
                  pl.BlockSpec((None, d, nb), lambda i: (layer, 0, 0))],
        out_specs=[pl.BlockSpec((tm, na), lambda i: (i, 0)),
                   pl.BlockSpec((tm, nb), lambda i: (i, 0))],
        out_shape=[jax.ShapeDtypeStruct((t, na), BF16),
                   jax.ShapeDtypeStruct((t, nb), BF16)],
        compiler_params=_params("arbitrary"),
        name="in_proj",
    )(x, g, mod, mod, w_mla, w_sb)


def _mla_prep_kernel(cq_ref, ckv_ref, kr_ref, cos_ref, s1_ref, s2_ref, gql_ref, gkvl_ref,
                     wuq_ref, wukv_ref, gqn_ref, gqr_ref, gkn_ref, gkr_ref,
                     q_ref, k_ref, v_ref):
    nope_w = MLA_HEADS * MLA_NOPE
    cqn = _rms(cq_ref[...].astype(F32)) * gql_ref[...]
    q = jnp.dot(cqn.astype(BF16), wuq_ref[...], preferred_element_type=F32)
    ckvn = _rms(ckv_ref[...].astype(F32)) * gkvl_ref[...]
    kv = jnp.dot(ckvn.astype(BF16), wukv_ref[...], preferred_element_type=F32)
    cosv, s1, s2 = cos_ref[...], s1_ref[...], s2_ref[...]

    def rope(x):
        return (x * cosv + pltpu.roll(x, MLA_ROPE // 2, 1) * s1
                + pltpu.roll(x, LANE - MLA_ROPE // 2, 1) * s2)

    kr = kr_ref[...].astype(F32)
    kr_ss = jnp.sum(kr * kr, axis=-1, keepdims=True)
    kr_roped = rope(kr * gkr_ref[...])
    gqn, gqr, gkn = gqn_ref[...], gqr_ref[...], gkn_ref[...]
    for h in range(MLA_HEADS):
        qn = q[:, h * LANE:(h + 1) * LANE]
        qr = q[:, nope_w + h * LANE:nope_w + (h + 1) * LANE]
        ss = jnp.sum(qn * qn, axis=-1, keepdims=True) + jnp.sum(qr * qr, axis=-1, keepdims=True)
        r = lax.rsqrt(ss * (1.0 / MLA_QK) + EPS)
        q_ref[:, h * QK_PAD:h * QK_PAD + LANE] = (qn * r * gqn).astype(BF16)
        q_ref[:, h * QK_PAD + LANE:(h + 1) * QK_PAD] = (rope(qr * gqr) * r).astype(BF16)
        kn = kv[:, h * LANE:(h + 1) * LANE]
        rk = lax.rsqrt((jnp.sum(kn * kn, axis=-1, keepdims=True) + kr_ss) * (1.0 / MLA_QK) + EPS)
        k_ref[:, h * QK_PAD:h * QK_PAD + LANE] = (kn * rk * gkn).astype(BF16)
        k_ref[:, h * QK_PAD + LANE:(h + 1) * QK_PAD] = (kr_roped * rk).astype(BF16)
        v_ref[:, 2 * h * MLA_V:(2 * h + 1) * MLA_V] = (
            kv[:, nope_w + h * MLA_V:nope_w + (h + 1) * MLA_V].astype(BF16))
        v_ref[:, (2 * h + 1) * MLA_V:(2 * h + 2) * MLA_V] = jnp.ones((kn.shape[0], MLA_V), BF16)


def _mla_prep(proj, rope_tabs, gql, gkvl, wuq, wukv, gqn, gqr, gkn, gkr, layer, seq, tm=256):
    t = proj.shape[0]
    per_b = seq // tm
    cos_t, s1_t, s2_t = rope_tabs
    row = lambda i: (i, 0)
    full2 = lambda i: (0, 0)
    lay = lambda i: (layer, 0, 0)
    tab = lambda i: (i % per_b, 0)
    return pl.pallas_call(
        _mla_prep_kernel,
        grid=(t // tm,),
        in_specs=[pl.BlockSpec((tm, Q_LORA), lambda i: (i, COL_CQ // Q_LORA)),
                  pl.BlockSpec((tm, KV_LORA), lambda i: (i, COL_CKV // KV_LORA)),
                  pl.BlockSpec((tm, LANE), lambda i: (i, COL_KR // LANE)),
                  pl.BlockSpec((tm, LANE), tab),
                  pl.BlockSpec((tm, LANE), tab),
                  pl.BlockSpec((tm, LANE), tab),
                  pl.BlockSpec((None, 1, Q_LORA), lay),
                  pl.BlockSpec((None, 1, KV_LORA), lay),
                  pl.BlockSpec((None,) + wuq.shape[1:], lay),
                  pl.BlockSpec((None,) + wukv.shape[1:], lay),
                  pl.BlockSpec((None, 1, LANE), lay),
                  pl.BlockSpec((None, 1, LANE), lay),
                  pl.BlockSpec((None, 1, LANE), lay),
                  pl.BlockSpec((None, 1, LANE), lay)],
        out_specs=[pl.BlockSpec((tm, MLA_HEADS * QK_PAD), row),
                   pl.BlockSpec((tm, MLA_HEADS * QK_PAD), row),
                   pl.BlockSpec((tm, MLA_HEADS * 2 * MLA_V), row)],
        out_shape=[jax.ShapeDtypeStruct((t, MLA_HEADS * QK_PAD), BF16),
                   jax.ShapeDtypeStruct((t, MLA_HEADS * QK_PAD), BF16),
                   jax.ShapeDtypeStruct((t, MLA_HEADS * 2 * MLA_V), BF16)],
        compiler_params=_params("arbitrary"),
        name="mla_prep",
    )(proj, proj, proj, cos_t, s1_t, s2_t, gql, gkvl, wuq, wukv, gqn, gqr, gkn, gkr)


def _mla_attn_kernel(q_ref, k_ref, v_ref, o_ref, m_ref, l_ref, acc_ref, sa_ref, sb_ref, *, tq):
    i = pl.program_id(2)
    q = q_ref[...]

    def scores(kb):
        k = k_ref[pl.ds(pl.multiple_of(kb * tq, tq), tq), :]
        return lax.dot_general(q, k, (((1,), (1,)), ((), ())), preferred_element_type=F32)

    def probs(s, m):
        return jnp.concatenate(
            [jnp.exp2(s[:, c * LANE:(c + 1) * LANE] - m) for c in range(tq // LANE)],
            axis=1).astype(BF16)

    def pv(p, kb):
        return jnp.dot(p, v_ref[pl.ds(pl.multiple_of(kb * tq, tq), tq), :],
                       preferred_element_type=F32)

    s = scores(i)
    q_chunk = lax.broadcasted_iota(jnp.int32, (tq, 1), 0) // CHUNK
    k_chunk = lax.broadcasted_iota(jnp.int32, (1, tq), 1) // CHUNK
    s = jnp.where(k_chunk <= q_chunk, s, -jnp.inf)
    m = jnp.broadcast_to(jnp.max(s, axis=-1, keepdims=True), (tq, LANE))
    r = pv(probs(s, m), i)
    m_ref[...] = m
    acc_ref[...] = r[:, :MLA_V]
    l_ref[...] = r[:, MLA_V:]

    def update(s, kb):
        m_old = m_ref[...]
        m_new = jnp.maximum(m_old, jnp.max(s, axis=-1, keepdims=True))
        alpha = jnp.exp2(m_old - m_new)
        r = pv(probs(s, m_new), kb)
        acc_ref[...] = alpha * acc_ref[...] + r[:, :MLA_V]
        l_ref[...] = alpha * l_ref[...] + r[:, MLA_V:]
        m_ref[...] = m_new

    last = jnp.maximum(i - 1, 0)
    sa_ref[...] = scores(0)

    def pair(t, carry):
        j = 2 * t
        sb_ref[...] = scores(jnp.minimum(j + 1, last))
        update(sa_ref[...], j)
        sa_ref[...] = scores(jnp.minimum(j + 2, last))
        update(sb_ref[...], j + 1)
        return carry

    lax.fori_loop(0, i // 2, pair, 0)

    @pl.when(i % 2 == 1)
    def _():
        update(sa_ref[...], i - 1)

    o_ref[...] = acc_ref[...] / l_ref[...]


def _mla_attention(q2d, k2d, v2d, bsz, seq, tq=512):
    t = q2d.shape[0]
    tq = min(tq, seq)
    nq = seq // tq
    vw = 2 * MLA_V
    return pl.pallas_call(
        functools.partial(_mla_attn_kernel, tq=tq),
        grid=(bsz, MLA_HEADS, nq),
        in_specs=[pl.BlockSpec((tq, QK_PAD), lambda b, h, i: (b * nq + i, h)),
                  pl.BlockSpec((seq, QK_PAD), lambda b, h, i: (b, h)),
                  pl.BlockSpec((seq, vw), lambda b, h, i: (b, h))],
        out_specs=pl.BlockSpec((tq, MLA_V), lambda b, h, i: (b * nq + i, h)),
        out_shape=jax.ShapeDtypeStruct((t, MLA_HEADS * MLA_V), F32),
        scratch_shapes=[pltpu.VMEM((tq, LANE), F32), pltpu.VMEM((tq, MLA_V), F32),
                        pltpu.VMEM((tq, MLA_V), F32),
                        pltpu.VMEM((tq, tq), F32), pltpu.VMEM((tq, tq), F32)],
        compiler_params=_params("arbitrary", "arbitrary", "arbitrary"),
        name="mla_attention",
    )(q2d, k2d, v2d)


def _sb_attn_kernel(q_ref, k_ref, v_ref, tri_ref, o_ref, run_ref, acc_ref, *, tq, tk, heads):
    i = pl.program_id(2)
    tri = tri_ref[...]
    sub = tq // tk
    streams = [(g, r) for g in range(heads) for r in range(sub)]

    def block(s, kb, mask):
        g, r = streams[s]
        cols = slice(g * SB_DIM, (g + 1) * SB_DIM)
        keys = pl.ds(pl.multiple_of(kb * tk, tk), tk)
        q = q_ref[r * tk:(r + 1) * tk, cols]
        z = lax.dot_general(q, k_ref[keys, cols], (((1,), (1,)), ((), ())),
                            preferred_element_type=F32)
        cost = jnp.maximum(z, 0.0) + jnp.log2(1.0 + jnp.exp2(-jnp.abs(z)))
        if mask is not None:
            cost = jnp.where(mask, cost, 0.0)
        after = jnp.dot(cost.astype(BF16), tri, preferred_element_type=F32) + run_ref[s]
        a = jnp.exp2(z - cost - after)
        if mask is not None:
            a = jnp.where(mask, a, 0.0)
        acc_ref[s] += jnp.dot(a.astype(BF16), v_ref[keys, cols], preferred_element_type=F32)
        run_ref[s] += jnp.sum(cost, axis=-1, keepdims=True)

    def more():
        return jnp.min(run_ref[...]) < SB_UNDERFLOW_LOG2

    run_ref[...] = jnp.zeros_like(run_ref)
    acc_ref[...] = jnp.zeros_like(acc_ref)
    strictly_before = (lax.broadcasted_iota(jnp.int32, (tk, tk), 1)
                       < lax.broadcasted_iota(jnp.int32, (tk, tk), 0))
    for s, (g, r) in enumerate(streams):
        block(s, i * sub + r, strictly_before)

    def cond(c):
        return (c[0] < i * sub + sub - 1) & c[1]

    def body(c):
        for s, (g, r) in enumerate(streams):
            kb = i * sub + r - 1 - c[0]
            run_ref[s] = jnp.where(kb >= 0, run_ref[s], SB_FINISHED_LOG2)
            block(s, jnp.maximum(kb, 0), None)
        return c[0] + 1, more()

    lax.while_loop(cond, body, (jnp.int32(0), more()))
    for s, (g, r) in enumerate(streams):
        o_ref[r * tk:(r + 1) * tk, g * SB_DIM:(g + 1) * SB_DIM] = acc_ref[s]


def _sb_attention(proj, bsz, seq, tq=512, tk=256, heads=2):
    t = proj.shape[0]
    tq = min(tq, seq)
    nq = seq // tq
    tri = (lax.broadcasted_iota(jnp.int32, (tk, tk), 0)
           > lax.broadcasted_iota(jnp.int32, (tk, tk), 1)).astype(BF16)
    w = heads * SB_DIM
    cq, ck, cv = COL_SQ // w, COL_SK // w, COL_SV // w
    n_streams = heads * (tq // tk)
    return pl.pallas_call(
        functools.partial(_sb_attn_kernel, tq=tq, tk=tk, heads=heads),
        grid=(bsz, SB_HEADS // heads, nq),
        in_specs=[pl.BlockSpec((tq, w), lambda b, h, i: (b * nq + i, cq + h)),
                  pl.BlockSpec((seq, w), lambda b, h, i: (b, ck + h)),
                  pl.BlockSpec((seq, w), lambda b, h, i: (b, cv + h)),
                  pl.BlockSpec((tk, tk), lambda b, h, i: (0, 0))],
        out_specs=pl.BlockSpec((tq, w), lambda b, h, i: (b * nq + i, h)),
        out_shape=jax.ShapeDtypeStruct((t, SB_WIDTH), F32),
        scratch_shapes=[pltpu.VMEM((n_streams, tk, 1), F32),
                        pltpu.VMEM((n_streams, tk, SB_DIM), F32)],
        compiler_params=_params("arbitrary", "arbitrary", "arbitrary"),
        name="sb_attention",
    )(proj, proj, proj, tri)


def _out_proj_kernel(om_ref, os_ref, gm_ref, gs_ref, wo_ref, x_ref, gate_ref, g2_ref,
                     sh_ref, sc_ref, rw_ref, xo_ref, h2_ref, lg_ref):
    wm = om_ref.shape[1]
    nm = (_rms(om_ref[...]) * gm_ref[...]).astype(BF16)
    ns = (_rms(os_ref[...]) * gs_ref[...]).astype(BF16)
    y = (jnp.dot(nm, wo_ref[:wm, :], preferred_element_type=F32)
         + jnp.dot(ns, wo_ref[wm:, :], preferred_element_type=F32))
    xn = x_ref[...] + gate_ref[...] * y
    xo_ref[...] = xn
    h2 = _rms(xn) * g2_ref[...]
    h2 = h2 * (1.0 + sc_ref[...]) + sh_ref[...]
    h2_ref[...] = _pack_rows(h2)
    lg_ref[...] = lax.dot_general(rw_ref[...], h2.astype(BF16), (((1,), (1,)), ((), ())),
                                  preferred_element_type=F32)


def _out_proj(o_mla, o_sb, gm, gs, w_o, x, mod, g2, rw_t, layer, seq, tm=256):
    t, d = x.shape
    per_b = seq // tm
    row = lambda i: (i, 0)
    full2 = lambda i: (0, 0)
    modspec = lambda j: pl.BlockSpec((None, 1, d), lambda i: ((i // per_b) * 6 + j, 0, 0))
    return pl.pallas_call(
        _out_proj_kernel,
        grid=(t // tm,),
        in_specs=[pl.BlockSpec((tm, o_mla.shape[1]), row),
                  pl.BlockSpec((tm, o_sb.shape[1]), row),
                  pl.BlockSpec((1, o_mla.shape[1]), full2),
                  pl.BlockSpec((1, o_sb.shape[1]), full2),
                  pl.BlockSpec((None,) + w_o.shape[1:], lambda i: (layer, 0, 0)),
                  pl.BlockSpec((tm, d), row),
                  modspec(2),
                  pl.BlockSpec((1, d), full2),
                  modspec(3),
                  modspec(4),
                  pl.BlockSpec(rw_t.shape, full2)],
        out_specs=[pl.BlockSpec((tm, d), row),
                   pl.BlockSpec((tm, d // 2), row),
                   pl.BlockSpec((N_EXPERTS, tm), lambda i: (0, i))],
        out_shape=[jax.ShapeDtypeStruct((t, d), F32),
                   jax.ShapeDtypeStruct((t, d // 2), jnp.uint32),
                   jax.ShapeDtypeStruct((N_EXPERTS, t), F32)],
        compiler_params=_params("arbitrary"),
        name="out_proj",
    )(o_mla, o_sb, gm, gs, w_o, x, mod, g2, mod, mod, rw_t)


def _route_kernel(lg_ref, rb_ref, tri_ref, e_ref, w_ref, rank_ref, cnt_ref, run_ref):
    i = pl.program_id(0)

    @pl.when(i == 0)
    def _():
        run_ref[...] = jnp.zeros_like(run_ref)

    scores = jax.nn.sigmoid(lg_ref[...])
    sel = scores + rb_ref[...]
    tm = sel.shape[1]
    epg = EXPERTS_PER_GROUP
    srow = [sel[e:e + 1, :] for e in range(N_EXPERTS)]
    prow = [scores[e:e + 1, :] for e in range(N_EXPERTS)]

    gscore = []
    for g in range(N_GROUPS):
        a, b, c, d = srow[g * epg:(g + 1) * epg]
        hi1, lo1 = jnp.maximum(a, b), jnp.minimum(a, b)
        hi2, lo2 = jnp.maximum(c, d), jnp.minimum(c, d)
        gscore.append(jnp.maximum(hi1, hi2)
                      + jnp.maximum(jnp.minimum(hi1, hi2), jnp.maximum(lo1, lo2)))
    best = gscore[0]
    gidx = jnp.zeros((1, tm), jnp.int32)
    for g in range(1, N_GROUPS):
        better = gscore[g] > best
        gidx = jnp.where(better, g, gidx)
        best = jnp.where(better, gscore[g], best)

    def pick(rows, j):
        out = rows[j]
        for g in range(1, N_GROUPS):
            out = jnp.where(gidx == g, rows[g * epg + j], out)
        return out

    v = [pick(srow, j) for j in range(epg)]
    p = [pick(prow, j) for j in range(epg)]
    pos = []
    for j in range(epg):
        r = jnp.zeros((1, tm), jnp.int32)
        for o in range(epg):
            if o == j:
                continue
            ahead = (v[o] > v[j]) | ((v[o] == v[j]) & (o < j))
            r = r + ahead.astype(jnp.int32)
        pos.append(r)
    local, wsel = [], []
    for k in range(TOP_K):
        lk = jnp.zeros((1, tm), jnp.int32)
        wk = jnp.zeros((1, tm), F32)
        for j in range(epg):
            hit = pos[j] == k
            lk = jnp.where(hit, j, lk)
            wk = jnp.where(hit, p[j], wk)
        local.append(lk)
        wsel.append(wk)
    wsum = wsel[0] + wsel[1]
    idx = [gidx * epg + local[k] for k in range(TOP_K)]

    eiota = lax.broadcasted_iota(jnp.int32, (N_EXPERTS, tm), 0)
    hit = [eiota == idx[k] for k in range(TOP_K)]
    onehot = (hit[0] | hit[1]).astype(F32)
    before = jnp.dot(onehot.astype(BF16), tri_ref[...], preferred_element_type=F32) + run_ref[...]
    for k in range(TOP_K):
        e_ref[pl.ds(k, 1), :] = idx[k]
        w_ref[pl.ds(k, 1), :] = wsel[k] / wsum
        rank_ref[pl.ds(k, 1), :] = jnp.sum(jnp.where(hit[k], before, 0.0), axis=0,
                                           keepdims=True).astype(jnp.int32)
    run_ref[...] += jnp.sum(onehot, axis=1, keepdims=True)
    cnt_ref[...] = jnp.broadcast_to(run_ref[...], cnt_ref.shape)


def _route(logits_t, router_b, tm=512):
    t = logits_t.shape[1]
    tm = min(tm, t)
    tri = (lax.broadcasted_iota(jnp.int32, (tm, tm), 0)
           < lax.broadcasted_iota(jnp.int32, (tm, tm), 1)).astype(BF16)
    blk = lambda rows: pl.BlockSpec((rows, tm), lambda i: (0, i))
    return pl.pallas_call(
        _route_kernel,
        grid=(t // tm,),
        in_specs=[blk(N_EXPERTS),
                  pl.BlockSpec((N_EXPERTS, 1), lambda i: (0, 0)),
                  pl.BlockSpec((tm, tm), lambda i: (0, 0))],
        out_specs=[blk(TOP_K), blk(TOP_K), blk(TOP_K),
                   pl.BlockSpec((N_EXPERTS, LANE), lambda i: (0, 0))],
        out_shape=[jax.ShapeDtypeStruct((TOP_K, t), jnp.int32),
                   jax.ShapeDtypeStruct((TOP_K, t), F32),
                   jax.ShapeDtypeStruct((TOP_K, t), jnp.int32),
                   jax.ShapeDtypeStruct((N_EXPERTS, LANE), F32)],
        scratch_shapes=[pltpu.VMEM((N_EXPERTS, 1), F32)],
        compiler_params=_params("arbitrary"),
        name="route",
    )(logits_t, router_b[:, None], tri)


def _dispatch_kernel(dest_ref, pend_ref, h_ref, o_ref, zero_ref, sem, zsem, *, tm, t):
    base = pl.program_id(0) * tm

    @pl.when(pl.program_id(0) == 0)
    def _():
        zero_ref[...] = jnp.zeros_like(zero_ref)

        def fill(e):
            end = pend_ref[e]
            start = pend_ref[e - 1] if e > 0 else 0
            dst = o_ref.at[pl.ds(pl.multiple_of(jnp.maximum(end - MOE_ROWS, 0), MOE_ROWS),
                                 MOE_ROWS), :]
            return end > start, pltpu.make_async_copy(zero_ref, dst, zsem)

        for e in range(N_EXPERTS):
            nonempty, cp = fill(e)
            pl.when(nonempty)(cp.start)
        for e in range(N_EXPERTS):
            nonempty, cp = fill(e)
            pl.when(nonempty)(cp.wait)

        def tail(r):
            dst = o_ref.at[pl.ds(pl.multiple_of(r * MOE_ROWS, MOE_ROWS), MOE_ROWS), :]
            return pltpu.make_async_copy(zero_ref, dst, zsem)

        first_free = pend_ref[N_EXPERTS - 1] // MOE_ROWS
        n_blocks = o_ref.shape[0] // MOE_ROWS
        lax.fori_loop(first_free, n_blocks, lambda r, c: (tail(r).start(), c)[1], 0)
        lax.fori_loop(first_free, n_blocks, lambda r, c: (tail(r).wait(), c)[1], 0)

    def issue(r, carry):
        for k in range(TOP_K):
            d = dest_ref[k * t + base + r]
            pltpu.make_async_copy(h_ref.at[pl.ds(r, 1), :], o_ref.at[pl.ds(d, 1), :], sem).start()
        return carry

    lax.fori_loop(0, tm, issue, 0, unroll=DMA_ISSUE_UNROLL)
    for k in range(TOP_K):
        pltpu.make_async_copy(h_ref, o_ref.at[pl.ds(0, tm), :], sem).wait()


def _dispatch(h2, dest_flat, pad_end, rows, tm=256):
    t, d = h2.shape
    return pl.pallas_call(
        functools.partial(_dispatch_kernel, tm=tm, t=t),
        grid_spec=pltpu.PrefetchScalarGridSpec(
            num_scalar_prefetch=2,
            grid=(t // tm,),
            in_specs=[pl.BlockSpec((tm, d), lambda i, dest, pend: (i, 0))],
            out_specs=pl.BlockSpec(memory_space=pl.ANY),
            scratch_shapes=[pltpu.VMEM((MOE_ROWS, d), h2.dtype),
                            pltpu.SemaphoreType.DMA(()), pltpu.SemaphoreType.DMA(())]),
        out_shape=jax.ShapeDtypeStruct((rows, d), h2.dtype),
        compiler_params=_params("arbitrary"),
        name="moe_dispatch",
    )(dest_flat, pad_end, h2)


def _expert_kernel(be_ref, nu_ref, x_ref, wg_ref, wu_ref, wd_ref, y_ref, wgb, wub, wdb):
    r = pl.program_id(0)
    live = r < nu_ref[0]
    new_expert = (r == 0) | (be_ref[jnp.maximum(r - 1, 0)] != be_ref[r])

    @pl.when(live & new_expert)
    def _():
        wgb[...] = wg_ref[...].astype(BF16)
        wub[...] = wu_ref[...].astype(BF16)
        wdb[...] = wd_ref[...].astype(BF16)

    @pl.when(live)
    def _():
        lo, hi = _unpack_rows(x_ref[...])
        xb = jnp.concatenate([lo.astype(BF16), hi.astype(BF16)], axis=1)
        g = jnp.dot(xb, wgb[...], preferred_element_type=F32)
        u = jnp.dot(xb, wub[...], preferred_element_type=F32)
        a = (g * jax.nn.sigmoid(g)) * u
        y_ref[...] = _pack_rows(jnp.dot(a.astype(BF16), wdb[...], preferred_element_type=F32))

    @pl.when(jnp.logical_not(live))
    def _():
        y_ref[...] = jnp.zeros_like(y_ref)


def _experts(x_sorted, block_e, n_used, w_gate, w_up, w_down, layer):
    rows, d = x_sorted.shape
    de = w_gate.shape[-1]
    dm = w_gate.shape[-2]
    bm = MOE_ROWS
    blk = lambda r, be, nu: (jnp.minimum(r, nu[0] - 1), 0)
    return pl.pallas_call(
        _expert_kernel,
        grid_spec=pltpu.PrefetchScalarGridSpec(
            num_scalar_prefetch=2,
            grid=(rows // bm,),
            in_specs=[pl.BlockSpec((bm, d), blk),
                      pl.BlockSpec((None, None, dm, de), lambda r, be, nu: (layer, be[r], 0, 0)),
                      pl.BlockSpec((None, None, dm, de), lambda r, be, nu: (layer, be[r], 0, 0)),
                      pl.BlockSpec((None, None, de, dm), lambda r, be, nu: (layer, be[r], 0, 0))],
            out_specs=pl.BlockSpec((bm, d), lambda r, be, nu: (r, 0)),
            scratch_shapes=[pltpu.VMEM((dm, de), BF16), pltpu.VMEM((dm, de), BF16),
                            pltpu.VMEM((de, dm), BF16)]),
        out_shape=jax.ShapeDtypeStruct((rows, d), jnp.uint32),
        compiler_params=_params("arbitrary"),
        name="moe_experts",
    )(block_e, n_used, x_sorted, w_gate, w_up, w_down)


def _combine_kernel(dest_ref, y_ref, x_ref, gate_ref, w_ref, o_ref, buf, sem, *, tm, t):
    base = pl.program_id(0) * tm

    def issue(r, carry):
        for k in range(TOP_K):
            d = dest_ref[k * t + base + r]
            pltpu.make_async_copy(y_ref.at[pl.ds(d, 1), :], buf.at[k, pl.ds(r, 1), :], sem).start()
        return carry

    lax.fori_loop(0, tm, issue, 0, unroll=DMA_ISSUE_UNROLL)
    for k in range(TOP_K):
        pltpu.make_async_copy(y_ref.at[pl.ds(0, tm), :], buf.at[k], sem).wait()
    w = w_ref[...]
    half = buf.shape[2]
    lo0, hi0 = _unpack_rows(buf[0])
    lo1, hi1 = _unpack_rows(buf[1])
    o_ref[:, :half] = x_ref[:, :half] + gate_ref[:, :half] * (w[:, 0:1] * lo0 + w[:, 1:2] * lo1)
    o_ref[:, half:] = x_ref[:, half:] + gate_ref[:, half:] * (w[:, 0:1] * hi0 + w[:, 1:2] * hi1)


def _combine(y_sorted, dest_flat, x, mod, w_tok, seq, tm=256):
    t, d = x.shape
    per_b = seq // tm
    return pl.pallas_call(
        functools.partial(_combine_kernel, tm=tm, t=t),
        grid_spec=pltpu.PrefetchScalarGridSpec(
            num_scalar_prefetch=1,
            grid=(t // tm,),
            in_specs=[pl.BlockSpec(memory_space=pl.ANY),
                      pl.BlockSpec((tm, d), lambda i, dest: (i, 0)),
                      pl.BlockSpec((None, 1, d), lambda i, dest: ((i // per_b) * 6 + 5, 0, 0)),
                      pl.BlockSpec((tm, TOP_K), lambda i, dest: (i, 0))],
            out_specs=pl.BlockSpec((tm, d), lambda i, dest: (i, 0)),
            scratch_shapes=[pltpu.VMEM((TOP_K, tm, y_sorted.shape[1]), y_sorted.dtype),
                            pltpu.SemaphoreType.DMA(())]),
        out_shape=jax.ShapeDtypeStruct((t, d), F32),
        compiler_params=_params("arbitrary"),
        name="moe_combine",
    )(dest_flat, y_sorted, x, mod, w_tok)


def _rope_tables(seq):
    half = MLA_ROPE // 2
    inv = 1.0 / (ROPE_BASE ** (jnp.arange(0, MLA_ROPE, 2, dtype=F32) / MLA_ROPE))
    ang = jnp.arange(seq, dtype=F32)[:, None] * inv[None, :]
    cos, sin = jnp.cos(ang), jnp.sin(ang)
    z = jnp.zeros((seq, half), F32)
    zz = jnp.zeros((seq, LANE - MLA_ROPE), F32)
    cos_t = jnp.concatenate([cos, cos, zz], axis=1)
    s1_t = jnp.concatenate([z, sin, zz], axis=1)
    s2_t = jnp.concatenate([-sin, z, zz], axis=1)
    return cos_t, s1_t, s2_t


def _pad_last(a, width):
    return jnp.pad(a, [(0, 0)] * (a.ndim - 1) + [(0, width - a.shape[-1])])


def kernel(x, c, norm1_g, norm2_g, w_ada, b_ada, w_in, q_lora_g, kv_lora_g, w_uq, w_ukv,
           q_head_g, k_head_g, mla_out_g, sb_out_g, w_o, router_w, router_b,
           w_gate, w_up, w_down):
    bsz, seq, d = x.shape
    t = bsz * seq
    depth = w_ada.shape[0]

    sb_col0 = COL_KR + MLA_ROPE
    w_in_mla = _pad_last(w_in[..., :sb_col0], MLA_IN_COLS).astype(BF16)
    sb_col_scale = jnp.where(jnp.arange(SB_IN_COLS) < SB_WIDTH, SB_DIM ** -0.5 * LOG2_E, 1.0)
    w_in_sb = (w_in[..., sb_col0:] * sb_col_scale.astype(F32)).astype(BF16)
    wq = w_uq.reshape(depth, Q_LORA, MLA_HEADS, MLA_QK)
    wuq_p = jnp.concatenate(
        [wq[..., :MLA_NOPE].reshape(depth, Q_LORA, -1),
         _pad_last(wq[..., MLA_NOPE:], LANE).reshape(depth, Q_LORA, -1)], axis=-1).astype(BF16)
    wkv = w_ukv.reshape(depth, KV_LORA, MLA_HEADS, MLA_NOPE + MLA_V)
    wukv_p = jnp.concatenate(
        [wkv[..., :MLA_NOPE].reshape(depth, KV_LORA, -1),
         wkv[..., MLA_NOPE:].reshape(depth, KV_LORA, -1)], axis=-1).astype(BF16)
    w_o_b = w_o.astype(BF16)
    q_scale = MLA_QK ** -0.5 * LOG2_E
    gqn = (q_head_g[:, None, :MLA_NOPE] * q_scale)
    gqr = _pad_last(q_head_g[:, None, MLA_NOPE:] * q_scale, LANE)
    gkn = k_head_g[:, None, :MLA_NOPE]
    gkr = _pad_last(k_head_g[:, None, MLA_NOPE:], LANE)
    rope_tabs = _rope_tables(seq)
    rw_t = router_w.T.astype(BF16)

    mod_all = _ada_modulation(c, w_ada, b_ada)
    n_blocks = -(-(t * TOP_K) // MOE_ROWS) + N_EXPERTS
    rows = n_blocks * MOE_ROWS

    xt = x.reshape(t, d)
    for l in range(depth):
        mod = mod_all[l].reshape(bsz * 6, 1, d)
        proj_mla, proj_sb = _in_proj(xt, norm1_g[l][None, :], mod, w_in_mla, w_in_sb, l, seq)
        q2d, k2d, v2d = _mla_prep(proj_mla, rope_tabs, q_lora_g[:, None, :],
                                  kv_lora_g[:, None, :], wuq_p, wukv_p, gqn, gqr, gkn, gkr, l, seq)
        o_mla = _mla_attention(q2d, k2d, v2d, bsz, seq)
        o_sb = _sb_attention(proj_sb, bsz, seq)
        xt, h2, logits_t = _out_proj(o_mla, o_sb, mla_out_g[l][None, :], sb_out_g[l][None, :],
                                     w_o_b, xt, mod, norm2_g[l][None, :], rw_t, l, seq)
        e_idx, w_tok, rank, cnt = _route(logits_t, router_b)
        counts = cnt[:, 0].astype(jnp.int32)
        padded = (counts + MOE_ROWS - 1) // MOE_ROWS * MOE_ROWS
        pad_end = jnp.cumsum(padded)
        pad_start = pad_end - padded
        eids = jnp.arange(N_EXPERTS, dtype=jnp.int32)
        seg_start = jnp.sum(jnp.where(e_idx[None] == eids[:, None, None],
                                      pad_start[:, None, None], 0), axis=0)
        dest = (seg_start + rank).reshape(TOP_K * t).astype(jnp.int32)
        n_used = (pad_end[-1:] // MOE_ROWS).astype(jnp.int32)
        first_row = jnp.minimum(jnp.arange(n_blocks, dtype=jnp.int32), n_used - 1) * MOE_ROWS
        block_e = jnp.sum(pad_end[None, :] <= first_row[:, None], axis=1).astype(jnp.int32)
        x_sorted = _dispatch(h2, dest, pad_end.astype(jnp.int32), rows)
        y_sorted = _experts(x_sorted, block_e, n_used, w_gate, w_up, w_down, l)
        xt = _combine(y_sorted, dest, xt, mod, w_tok.T, seq)
    return xt.reshape(bsz, seq, d)
```

```python
import functools

import jax
import jax.numpy as jnp
from jax import lax
from jax.experimental import pallas as pl
from jax.experimental.pallas import tpu as pltpu

F32 = jnp.float32
BF16 = jnp.bfloat16

EPS = 1e-6
ROPE_BASE = 10000.0
CHUNK = 64

MLA_HEADS = 8
MLA_NOPE = 128
MLA_ROPE = 64
MLA_QK = MLA_NOPE + MLA_ROPE
MLA_V = 128
Q_LORA = 512
KV_LORA = 256
SB_HEADS = 8
SB_DIM = 128
SB_WIDTH = SB_HEADS * SB_DIM
LANE = 128
QK_PAD = 2 * LANE
COL_CQ = 0
COL_CKV = Q_LORA
COL_KR = Q_LORA + KV_LORA
MLA_IN_COLS = COL_KR + LANE
COL_SQ = 0
COL_SK = SB_WIDTH
COL_SV = 2 * SB_WIDTH
SB_IN_COLS = 3 * SB_WIDTH

N_EXPERTS = 16
N_GROUPS = 4
EXPERTS_PER_GROUP = N_EXPERTS // N_GROUPS
TOP_K = 2
D_EXPERT = 512
MOE_ROWS = 256
DMA_ISSUE_UNROLL = 8
LOG2_E = 1.4426950408889634
SB_UNDERFLOW_LOG2 = 150.0
SB_FINISHED_LOG2 = 1e30

VMEM_LIMIT = 56 * 1024 * 1024


def _params(*dims):
    return pltpu.CompilerParams(dimension_semantics=dims, vmem_limit_bytes=VMEM_LIMIT)


def _rms(x):
    return x * lax.rsqrt(jnp.mean(x * x, axis=-1, keepdims=True) + EPS)


def _pack_rows(x):
    n = x.shape[1] // 2
    lo = lax.bitcast_convert_type(x[:, :n].astype(BF16).astype(F32), jnp.uint32)
    hi = lax.bitcast_convert_type(x[:, n:].astype(BF16).astype(F32), jnp.uint32)
    return (lo >> 16) | (hi & jnp.uint32(0xFFFF0000))


def _unpack_rows(w):
    lo = lax.bitcast_convert_type(w << 16, F32)
    hi = lax.bitcast_convert_type(w & jnp.uint32(0xFFFF0000), F32)
    return lo, hi


def _ada_kernel(c_ref, w_ref, b_ref, o_ref):
    w = w_ref[...]
    for b in range(c_ref.shape[0]):
        cb = c_ref[b]
        cb = cb * jax.nn.sigmoid(cb)
        o_ref[pl.ds(b, 1), :] = jnp.sum(cb * w, axis=0, keepdims=True) + b_ref[...]


def _ada_modulation(c, w_ada, b_ada, tn=1024):
    depth, d, n = w_ada.shape
    bsz = c.shape[0]
    return pl.pallas_call(
        _ada_kernel,
        grid=(depth, n // tn),
        in_specs=[pl.BlockSpec((bsz, d, 1), lambda l, j: (0, 0, 0)),
                  pl.BlockSpec((None, d, tn), lambda l, j: (l, 0, j)),
                  pl.BlockSpec((None, 1, tn), lambda l, j: (l, 0, j))],
        out_specs=pl.BlockSpec((None, bsz, tn), lambda l, j: (l, 0, j)),
        out_shape=jax.ShapeDtypeStruct((depth, bsz, n), F32),
        compiler_params=_params("arbitrary", "arbitrary"),
        name="ada_modulation",
    )(c[:, :, None], w_ada, b_ada[:, None, :])


def _in_proj_kernel(x_ref, g_ref, sh_ref, sc_ref, wa_ref, wb_ref, oa_ref, ob_ref):
    h = _rms(x_ref[...]) * g_ref[...]
    h = (h * (1.0 + sc_ref[...]) + sh_ref[...]).astype(BF16)
    oa_ref[...] = jnp.dot(h, wa_ref[...], preferred_element_type=F32).astype(oa_ref.dtype)
    ob_ref[...] = jnp.dot(h, wb_ref[...], preferred_element_type=F32).astype(ob_ref.dtype)


def _in_proj(x, g, mod, w_mla, w_sb, layer, seq, tm=256):
    t, d = x.shape
    na, nb = w_mla.shape[-1], w_sb.shape[-1]
    per_b = seq // tm
    return pl.pallas_call(
        _in_proj_kernel,
        grid=(t // tm,),
        in_specs=[pl.BlockSpec((tm, d), lambda i: (i, 0)),
                  pl.BlockSpec((1, d), lambda i: (0, 0)),
                  pl.BlockSpec((None, 1, d), lambda i: ((i // per_b) * 6 + 0, 0, 0)),
                  pl.BlockSpec((None, 1, d), lambda i: ((i // per_b) * 6 + 1, 0, 0)),
                  pl.BlockSpec((None, d, na), lambda i: (layer, 0, 0)),
                  pl.BlockSpec((None, d, nb), lambda i: (layer, 0, 0))],
        out_specs=[pl.BlockSpec((tm, na), lambda i: (i, 0)),
                   pl.BlockSpec((tm, nb), lambda i: (i, 0))],
        out_shape=[jax.ShapeDtypeStruct((t, na), BF16),
                   jax.ShapeDtypeStruct((t, nb), BF16)],
        compiler_params=_params("arbitrary"),
        name="in_proj",
    )(x, g, mod, mod, w_mla, w_sb)


def _mla_prep_kernel(cq_ref, ckv_ref, kr_ref, cos_ref, s1_ref, s2_ref, gql_ref, gkvl_ref,
                     wuq_ref, wukv_ref, gqn_ref, gqr_ref, gkn_ref, gkr_ref,
                     q_ref, k_ref, v_ref):
    nope_w = MLA_HEADS * MLA_NOPE
    cqn = _rms(cq_ref[...].astype(F32)) * gql_ref[...]
    q = jnp.dot(cqn.astype(BF16), wuq_ref[...], preferred_element_type=F32)
    ckvn = _rms(ckv_ref[...].astype(F32)) * gkvl_ref[...]
    kv = jnp.dot(ckvn.astype(BF16), wukv_ref[...], preferred_element_type=F32)
    cosv, s1, s2 = cos_ref[...], s1_ref[...], s2_ref[...]

    def rope(x):
        return (x * cosv + pltpu.roll(x, MLA_ROPE // 2, 1) * s1
                + pltpu.roll(x, LANE - MLA_ROPE // 2, 1) * s2)

    kr = kr_ref[...].astype(F32)
    kr_ss = jnp.sum(kr * kr, axis=-1, keepdims=True)
    kr_roped = rope(kr * gkr_ref[...])
    gqn, gqr, gkn = gqn_ref[...], gqr_ref[...], gkn_ref[...]
    for h in range(MLA_HEADS):
        qn = q[:, h * LANE:(h + 1) * LANE]
        qr = q[:, nope_w + h * LANE:nope_w + (h + 1) * LANE]
        ss = jnp.sum(qn * qn, axis=-1, keepdims=True) + jnp.sum(qr * qr, axis=-1, keepdims=True)
        r = lax.rsqrt(ss * (1.0 / MLA_QK) + EPS)
        q_ref[:, h * QK_PAD:h * QK_PAD + LANE] = (qn * r * gqn).astype(BF16)
        q_ref[:, h * QK_PAD + LANE:(h + 1) * QK_PAD] = (rope(qr * gqr) * r).astype(BF16)
        kn = kv[:, h * LANE:(h + 1) * LANE]
        rk = lax.rsqrt((jnp.sum(kn * kn, axis=-1, keepdims=True) + kr_ss) * (1.0 / MLA_QK) + EPS)
        k_ref[:, h * QK_PAD:h * QK_PAD + LANE] = (kn * rk * gkn).astype(BF16)
        k_ref[:, h * QK_PAD + LANE:(h + 1) * QK_PAD] = (kr_roped * rk).astype(BF16)
        v_ref[:, 2 * h * MLA_V:(2 * h + 1) * MLA_V] = (
            kv[:, nope_w + h * MLA_V:nope_w + (h + 1) * MLA_V].astype(BF16))
        v_ref[:, (2 * h + 1) * MLA_V:(2 * h + 2) * MLA_V] = jnp.ones((kn.shape[0], MLA_V), BF16)


def _mla_prep(proj, rope_tabs, gql, gkvl, wuq, wukv, gqn, gqr, gkn, gkr, layer, seq, tm=256):
    t = proj.shape[0]
    per_b = seq // tm
    cos_t, s1_t, s2_t = rope_tabs
    row = lambda i: (i, 0)
    full2 = lambda i: (0, 0)
    lay = lambda i: (layer, 0, 0)
    tab = lambda i: (i % per_b, 0)
    return pl.pallas_call(
        _mla_prep_kernel,
        grid=(t // tm,),
        in_specs=[pl.BlockSpec((tm, Q_LORA), lambda i: (i, COL_CQ // Q_LORA)),
                  pl.BlockSpec((tm, KV_LORA), lambda i: (i, COL_CKV // KV_LORA)),
                  pl.BlockSpec((tm, LANE), lambda i: (i, COL_KR // LANE)),
                  pl.BlockSpec((tm, LANE), tab),
                  pl.BlockSpec((tm, LANE), tab),
                  pl.BlockSpec((tm, LANE), tab),
                  pl.BlockSpec((None, 1, Q_LORA), lay),
                  pl.BlockSpec((None, 1, KV_LORA), lay),
                  pl.BlockSpec((None,) + wuq.shape[1:], lay),
                  pl.BlockSpec((None,) + wukv.shape[1:], lay),
                  pl.BlockSpec((None, 1, LANE), lay),
                  pl.BlockSpec((None, 1, LANE), lay),
                  pl.BlockSpec((None, 1, LANE), lay),
                  pl.BlockSpec((None, 1, LANE), lay)],
        out_specs=[pl.BlockSpec((tm, MLA_HEADS * QK_PAD), row),
                   pl.BlockSpec((tm, MLA_HEADS * QK_PAD), row),
                   pl.BlockSpec((tm, MLA_HEADS * 2 * MLA_V), row)],
        out_shape=[jax.ShapeDtypeStruct((t, MLA_HEADS * QK_PAD), BF16),
                   jax.ShapeDtypeStruct((t, MLA_HEADS * QK_PAD), BF16),
                   jax.ShapeDtypeStruct((t, MLA_HEADS * 2 * MLA_V), BF16)],
        compiler_params=_params("arbitrary"),
        name="mla_prep",
    )(proj, proj, proj, cos_t, s1_t, s2_t, gql, gkvl, wuq, wukv, gqn, gqr, gkn, gkr)


def _mla_attn_kernel(q_ref, k_ref, v_ref, o_ref, m_ref, l_ref, acc_ref, sa_ref, sb_ref, *,
                     tq, heads):
    i = pl.program_id(2)
    vw = 2 * MLA_V

    def scores(g, kb):
        q = q_ref[:, g * QK_PAD:(g + 1) * QK_PAD]
        k = k_ref[pl.ds(pl.multiple_of(kb * tq, tq), tq), g * QK_PAD:(g + 1) * QK_PAD]
        return lax.dot_general(q, k, (((1,), (1,)), ((), ())), preferred_element_type=F32)

    def probs(s, m):
        return jnp.concatenate(
            [jnp.exp2(s[:, c * LANE:(c + 1) * LANE] - m) for c in range(tq // LANE)],
            axis=1).astype(BF16)

    def pv(g, p, kb):
        v = v_ref[pl.ds(pl.multiple_of(kb * tq, tq), tq), g * vw:(g + 1) * vw]
        return jnp.dot(p, v, preferred_element_type=F32)

    def update(g, s, kb):
        m_old = m_ref[g]
        m_new = jnp.maximum(m_old, jnp.max(s, axis=-1, keepdims=True))
        alpha = jnp.exp2(m_old - m_new)
        r = pv(g, probs(s, m_new), kb)
        acc_ref[g] = alpha * acc_ref[g] + r[:, :MLA_V]
        l_ref[g] = alpha * l_ref[g] + r[:, MLA_V:]
        m_ref[g] = m_new

    q_chunk = lax.broadcasted_iota(jnp.int32, (tq, 1), 0) // CHUNK
    k_chunk = lax.broadcasted_iota(jnp.int32, (1, tq), 1) // CHUNK
    visible = k_chunk <= q_chunk
    for g in range(heads):
        s = jnp.where(visible, scores(g, i), -jnp.inf)
        m = jnp.broadcast_to(jnp.max(s, axis=-1, keepdims=True), (tq, LANE))
        r = pv(g, probs(s, m), i)
        m_ref[g] = m
        acc_ref[g] = r[:, :MLA_V]
        l_ref[g] = r[:, MLA_V:]
        sa_ref[g] = scores(g, 0)

    last = jnp.maximum(i - 1, 0)

    def pair(t, carry):
        j = 2 * t
        for g in range(heads):
            sb_ref[g] = scores(g, jnp.minimum(j + 1, last))
            update(g, sa_ref[g], j)
        for g in range(heads):
            sa_ref[g] = scores(g, jnp.minimum(j + 2, last))
            update(g, sb_ref[g], j + 1)
        return carry

    lax.fori_loop(0, i // 2, pair, 0)

    @pl.when(i % 2 == 1)
    def _():
        for g in range(heads):
            update(g, sa_ref[g], i - 1)

    for g in range(heads):
        o_ref[:, g * MLA_V:(g + 1) * MLA_V] = acc_ref[g] / l_ref[g]


def _mla_attention(q2d, k2d, v2d, bsz, seq, tq=512, heads=2):
    t = q2d.shape[0]
    tq = min(tq, seq)
    nq = seq // tq
    vw = 2 * MLA_V
    return pl.pallas_call(
        functools.partial(_mla_attn_kernel, tq=tq, heads=heads),
        grid=(bsz, MLA_HEADS // heads, nq),
        in_specs=[pl.BlockSpec((tq, heads * QK_PAD), lambda b, h, i: (b * nq + i, h)),
                  pl.BlockSpec((seq, heads * QK_PAD), lambda b, h, i: (b, h)),
                  pl.BlockSpec((seq, heads * vw), lambda b, h, i: (b, h))],
        out_specs=pl.BlockSpec((tq, heads * MLA_V), lambda b, h, i: (b * nq + i, h)),
        out_shape=jax.ShapeDtypeStruct((t, MLA_HEADS * MLA_V), F32),
        scratch_shapes=[pltpu.VMEM((heads, tq, LANE), F32), pltpu.VMEM((heads, tq, MLA_V), F32),
                        pltpu.VMEM((heads, tq, MLA_V), F32),
                        pltpu.VMEM((heads, tq, tq), F32), pltpu.VMEM((heads, tq, tq), F32)],
        compiler_params=_params("arbitrary", "arbitrary", "arbitrary"),
        name="mla_attention",
    )(q2d, k2d, v2d)


def _sb_attn_kernel(q_ref, k_ref, v_ref, tri_ref, o_ref, run_ref, acc_ref, *, tq, tk, heads):
    i = pl.program_id(2)
    tri = tri_ref[...]
    sub = tq // tk
    streams = [(g, r) for g in range(heads) for r in range(sub)]

    def block(s, kb, mask):
        g, r = streams[s]
        cols = slice(g * SB_DIM, (g + 1) * SB_DIM)
        keys = pl.ds(pl.multiple_of(kb * tk, tk), tk)
        q = q_ref[r * tk:(r + 1) * tk, cols]
        z = lax.dot_general(q, k_ref[keys, cols], (((1,), (1,)), ((), ())),
                            preferred_element_type=F32)
        cost = jnp.maximum(z, 0.0) + jnp.log2(1.0 + jnp.exp2(-jnp.abs(z)))
        if mask is not None:
            cost = jnp.where(mask, cost, 0.0)
        after = jnp.dot(cost.astype(BF16), tri, preferred_element_type=F32) + run_ref[s]
        a = jnp.exp2(z - cost - after)
        if mask is not None:
            a = jnp.where(mask, a, 0.0)
        acc_ref[s] += jnp.dot(a.astype(BF16), v_ref[keys, cols], preferred_element_type=F32)
        run_ref[s] += jnp.sum(cost, axis=-1, keepdims=True)

    def more():
        return jnp.min(run_ref[...]) < SB_UNDERFLOW_LOG2

    run_ref[...] = jnp.zeros_like(run_ref)
    acc_ref[...] = jnp.zeros_like(acc_ref)
    strictly_before = (lax.broadcasted_iota(jnp.int32, (tk, tk), 1)
                       < lax.broadcasted_iota(jnp.int32, (tk, tk), 0))
    for s, (g, r) in enumerate(streams):
        block(s, i * sub + r, strictly_before)

    def cond(c):
        return (c[0] < i * sub + sub - 1) & c[1]

    def body(c):
        for s, (g, r) in enumerate(streams):
            kb = i * sub + r - 1 - c[0]
            run_ref[s] = jnp.where(kb >= 0, run_ref[s], SB_FINISHED_LOG2)
            block(s, jnp.maximum(kb, 0), None)
        return c[0] + 1, more()

    lax.while_loop(cond, body, (jnp.int32(0), more()))
    for s, (g, r) in enumerate(streams):
        o_ref[r * tk:(r + 1) * tk, g * SB_DIM:(g + 1) * SB_DIM] = acc_ref[s]


def _sb_attention(proj, bsz, seq, tq=512, tk=256, heads=2):
    t = proj.shape[0]
    tq = min(tq, seq)
    nq = seq // tq
    tri = (lax.broadcasted_iota(jnp.int32, (tk, tk), 0)
           > lax.broadcasted_iota(jnp.int32, (tk, tk), 1)).astype(BF16)
    w = heads * SB_DIM
    cq, ck, cv = COL_SQ // w, COL_SK // w, COL_SV // w
    n_streams = heads * (tq // tk)
    return pl.pallas_call(
        functools.partial(_sb_attn_kernel, tq=tq, tk=tk, heads=heads),
        grid=(bsz, SB_HEADS // heads, nq),
        in_specs=[pl.BlockSpec((tq, w), lambda b, h, i: (b * nq + i, cq + h)),
                  pl.BlockSpec((seq, w), lambda b, h, i: (b, ck + h)),
                  pl.BlockSpec((seq, w), lambda b, h, i: (b, cv + h)),
                  pl.BlockSpec((tk, tk), lambda b, h, i: (0, 0))],
        out_specs=pl.BlockSpec((tq, w), lambda b, h, i: (b * nq + i, h)),
        out_shape=jax.ShapeDtypeStruct((t, SB_WIDTH), F32),
        scratch_shapes=[pltpu.VMEM((n_streams, tk, 1), F32),
                        pltpu.VMEM((n_streams, tk, SB_DIM), F32)],
        compiler_params=_params("arbitrary", "arbitrary", "arbitrary"),
        name="sb_attention",
    )(proj, proj, proj, tri)


def _out_proj_kernel(om_ref, os_ref, gm_ref, gs_ref, wo_ref, x_ref, gate_ref, g2_ref,
                     sh_ref, sc_ref, rw_ref, xo_ref, h2_ref, lg_ref):
    wm = om_ref.shape[1]
    nm = (_rms(om_ref[...]) * gm_ref[...]).astype(BF16)
    ns = (_rms(os_ref[...]) * gs_ref[...]).astype(BF16)
    y = (jnp.dot(nm, wo_ref[:wm, :], preferred_element_type=F32)
         + jnp.dot(ns, wo_ref[wm:, :], preferred_element_type=F32))
    xn = x_ref[...] + gate_ref[...] * y
    xo_ref[...] = xn
    h2 = _rms(xn) * g2_ref[...]
    h2 = h2 * (1.0 + sc_ref[...]) + sh_ref[...]
    h2_ref[...] = _pack_rows(h2)
    lg_ref[...] = lax.dot_general(rw_ref[...], h2.astype(BF16), (((1,), (1,)), ((), ())),
                                  preferred_element_type=F32)


def _out_proj(o_mla, o_sb, gm, gs, w_o, x, mod, g2, rw_t, layer, seq, tm=256):
    t, d = x.shape
    per_b = seq // tm
    row = lambda i: (i, 0)
    full2 = lambda i: (0, 0)
    modspec = lambda j: pl.BlockSpec((None, 1, d), lambda i: ((i // per_b) * 6 + j, 0, 0))
    return pl.pallas_call(
        _out_proj_kernel,
        grid=(t // tm,),
        in_specs=[pl.BlockSpec((tm, o_mla.shape[1]), row),
                  pl.BlockSpec((tm, o_sb.shape[1]), row),
                  pl.BlockSpec((1, o_mla.shape[1]), full2),
                  pl.BlockSpec((1, o_sb.shape[1]), full2),
                  pl.BlockSpec((None,) + w_o.shape[1:], lambda i: (layer, 0, 0)),
                  pl.BlockSpec((tm, d), row),
                  modspec(2),
                  pl.BlockSpec((1, d), full2),
                  modspec(3),
                  modspec(4),
                  pl.BlockSpec(rw_t.shape, full2)],
        out_specs=[pl.BlockSpec((tm, d), row),
                   pl.BlockSpec((tm, d // 2), row),
                   pl.BlockSpec((N_EXPERTS, tm), lambda i: (0, i))],
        out_shape=[jax.ShapeDtypeStruct((t, d), F32),
                   jax.ShapeDtypeStruct((t, d // 2), jnp.uint32),
                   jax.ShapeDtypeStruct((N_EXPERTS, t), F32)],
        compiler_params=_params("arbitrary"),
        name="out_proj",
    )(o_mla, o_sb, gm, gs, w_o, x, mod, g2, mod, mod, rw_t)


def _route_kernel(lg_ref, rb_ref, tri_ref, e_ref, w_ref, rank_ref, cnt_ref, run_ref):
    i = pl.program_id(0)

    @pl.when(i == 0)
    def _():
        run_ref[...] = jnp.zeros_like(run_ref)

    scores = jax.nn.sigmoid(lg_ref[...])
    sel = scores + rb_ref[...]
    tm = sel.shape[1]
    epg = EXPERTS_PER_GROUP
    srow = [sel[e:e + 1, :] for e in range(N_EXPERTS)]
    prow = [scores[e:e + 1, :] for e in range(N_EXPERTS)]

    gscore = []
    for g in range(N_GROUPS):
        a, b, c, d = srow[g * epg:(g + 1) * epg]
        hi1, lo1 = jnp.maximum(a, b), jnp.minimum(a, b)
        hi2, lo2 = jnp.maximum(c, d), jnp.minimum(c, d)
        gscore.append(jnp.maximum(hi1, hi2)
                      + jnp.maximum(jnp.minimum(hi1, hi2), jnp.maximum(lo1, lo2)))
    best = gscore[0]
    gidx = jnp.zeros((1, tm), jnp.int32)
    for g in range(1, N_GROUPS):
        better = gscore[g] > best
        gidx = jnp.where(better, g, gidx)
        best = jnp.where(better, gscore[g], best)

    def pick(rows, j):
        out = rows[j]
        for g in range(1, N_GROUPS):
            out = jnp.where(gidx == g, rows[g * epg + j], out)
        return out

    v = [pick(srow, j) for j in range(epg)]
    p = [pick(prow, j) for j in range(epg)]
    pos = []
    for j in range(epg):
        r = jnp.zeros((1, tm), jnp.int32)
        for o in range(epg):
            if o == j:
                continue
            ahead = (v[o] > v[j]) | ((v[o] == v[j]) & (o < j))
            r = r + ahead.astype(jnp.int32)
        pos.append(r)
    local, wsel = [], []
    for k in range(TOP_K):
        lk = jnp.zeros((1, tm), jnp.int32)
        wk = jnp.zeros((1, tm), F32)
        for j in range(epg):
            hit = pos[j] == k
            lk = jnp.where(hit, j, lk)
            wk = jnp.where(hit, p[j], wk)
        local.append(lk)
        wsel.append(wk)
    wsum = wsel[0] + wsel[1]
    idx = [gidx * epg + local[k] for k in range(TOP_K)]

    eiota = lax.broadcasted_iota(jnp.int32, (N_EXPERTS, tm), 0)
    hit = [eiota == idx[k] for k in range(TOP_K)]
    onehot = (hit[0] | hit[1]).astype(F32)
    before = jnp.dot(onehot.astype(BF16), tri_ref[...], preferred_element_type=F32) + run_ref[...]
    for k in range(TOP_K):
        e_ref[pl.ds(k, 1), :] = idx[k]
        w_ref[pl.ds(k, 1), :] = wsel[k] / wsum
        rank_ref[pl.ds(k, 1), :] = jnp.sum(jnp.where(hit[k], before, 0.0), axis=0,
                                           keepdims=True).astype(jnp.int32)
    run_ref[...] += jnp.sum(onehot, axis=1, keepdims=True)
    cnt_ref[...] = jnp.broadcast_to(run_ref[...], cnt_ref.shape)


def _route(logits_t, router_b, tm=512):
    t = logits_t.shape[1]
    tm = min(tm, t)
    tri = (lax.broadcasted_iota(jnp.int32, (tm, tm), 0)
           < lax.broadcasted_iota(jnp.int32, (tm, tm), 1)).astype(BF16)
    blk = lambda rows: pl.BlockSpec((rows, tm), lambda i: (0, i))
    return pl.pallas_call(
        _route_kernel,
        grid=(t // tm,),
        in_specs=[blk(N_EXPERTS),
                  pl.BlockSpec((N_EXPERTS, 1), lambda i: (0, 0)),
                  pl.BlockSpec((tm, tm), lambda i: (0, 0))],
        out_specs=[blk(TOP_K), blk(TOP_K), blk(TOP_K),
                   pl.BlockSpec((N_EXPERTS, LANE), lambda i: (0, 0))],
        out_shape=[jax.ShapeDtypeStruct((TOP_K, t), jnp.int32),
                   jax.ShapeDtypeStruct((TOP_K, t), F32),
                   jax.ShapeDtypeStruct((TOP_K, t), jnp.int32),
                   jax.ShapeDtypeStruct((N_EXPERTS, LANE), F32)],
        scratch_shapes=[pltpu.VMEM((N_EXPERTS, 1), F32)],
        compiler_params=_params("arbitrary"),
        name="route",
    )(logits_t, router_b[:, None], tri)


def _dispatch_kernel(dest_ref, pend_ref, h_ref, o_ref, zero_ref, sem, zsem, *, tm, t):
    base = pl.program_id(0) * tm

    @pl.when(pl.program_id(0) == 0)
    def _():
        zero_ref[...] = jnp.zeros_like(zero_ref)

        def fill(e):
            end = pend_ref[e]
            start = pend_ref[e - 1] if e > 0 else 0
            dst = o_ref.at[pl.ds(pl.multiple_of(jnp.maximum(end - MOE_ROWS, 0), MOE_ROWS),
                                 MOE_ROWS), :]
            return end > start, pltpu.make_async_copy(zero_ref, dst, zsem)

        for e in range(N_EXPERTS):
            nonempty, cp = fill(e)
            pl.when(nonempty)(cp.start)
        for e in range(N_EXPERTS):
            nonempty, cp = fill(e)
            pl.when(nonempty)(cp.wait)

        def tail(r):
            dst = o_ref.at[pl.ds(pl.multiple_of(r * MOE_ROWS, MOE_ROWS), MOE_ROWS), :]
            return pltpu.make_async_copy(zero_ref, dst, zsem)

        first_free = pend_ref[N_EXPERTS - 1] // MOE_ROWS
        n_blocks = o_ref.shape[0] // MOE_ROWS
        lax.fori_loop(first_free, n_blocks, lambda r, c: (tail(r).start(), c)[1], 0)
        lax.fori_loop(first_free, n_blocks, lambda r, c: (tail(r).wait(), c)[1], 0)

    def issue(r, carry):
        for k in range(TOP_K):
            d = dest_ref[k * t + base + r]
            pltpu.make_async_copy(h_ref.at[pl.ds(base + r, 1), :], o_ref.at[pl.ds(d, 1), :],
                                  sem).start()
        return carry

    lax.fori_loop(0, tm, issue, 0, unroll=DMA_ISSUE_UNROLL)

    def wait_one_step():
        for k in range(TOP_K):
            pltpu.make_async_copy(h_ref.at[pl.ds(0, tm), :], o_ref.at[pl.ds(0, tm), :], sem).wait()

    pl.when(pl.program_id(0) > 0)(wait_one_step)
    pl.when(pl.program_id(0) == pl.num_programs(0) - 1)(wait_one_step)


def _dispatch(h2, dest_flat, pad_end, rows, tm=256):
    t, d = h2.shape
    return pl.pallas_call(
        functools.partial(_dispatch_kernel, tm=tm, t=t),
        grid_spec=pltpu.PrefetchScalarGridSpec(
            num_scalar_prefetch=2,
            grid=(t // tm,),
            in_specs=[pl.BlockSpec(memory_space=pl.ANY)],
            out_specs=pl.BlockSpec(memory_space=pl.ANY),
            scratch_shapes=[pltpu.VMEM((MOE_ROWS, d), h2.dtype),
                            pltpu.SemaphoreType.DMA(()), pltpu.SemaphoreType.DMA(())]),
        out_shape=jax.ShapeDtypeStruct((rows, d), h2.dtype),
        compiler_params=_params("arbitrary"),
        name="moe_dispatch",
    )(dest_flat, pad_end, h2)


def _expert_kernel(be_ref, nu_ref, x_ref, wg_ref, wu_ref, wd_ref, y_ref, wgb, wub, wdb):
    r = pl.program_id(0)
    live = r < nu_ref[0]
    new_expert = (r == 0) | (be_ref[jnp.maximum(r - 1, 0)] != be_ref[r])

    @pl.when(live & new_expert)
    def _():
        wgb[...] = wg_ref[...].astype(BF16)
        wub[...] = wu_ref[...].astype(BF16)
        wdb[...] = wd_ref[...].astype(BF16)

    @pl.when(live)
    def _():
        lo, hi = _unpack_rows(x_ref[...])
        xb = jnp.concatenate([lo.astype(BF16), hi.astype(BF16)], axis=1)
        g = jnp.dot(xb, wgb[...], preferred_element_type=F32)
        u = jnp.dot(xb, wub[...], preferred_element_type=F32)
        a = (g * jax.nn.sigmoid(g)) * u
        y_ref[...] = _pack_rows(jnp.dot(a.astype(BF16), wdb[...], preferred_element_type=F32))

    @pl.when(jnp.logical_not(live))
    def _():
        y_ref[...] = jnp.zeros_like(y_ref)


def _experts(x_sorted, block_e, n_used, w_gate, w_up, w_down, layer):
    rows, d = x_sorted.shape
    de = w_gate.shape[-1]
    dm = w_gate.shape[-2]
    bm = MOE_ROWS
    blk = lambda r, be, nu: (jnp.minimum(r, nu[0] - 1), 0)
    return pl.pallas_call(
        _expert_kernel,
        grid_spec=pltpu.PrefetchScalarGridSpec(
            num_scalar_prefetch=2,
            grid=(rows // bm,),
            in_specs=[pl.BlockSpec((bm, d), blk),
                      pl.BlockSpec((None, None, dm, de), lambda r, be, nu: (layer, be[r], 0, 0)),
                      pl.BlockSpec((None, None, dm, de), lambda r, be, nu: (layer, be[r], 0, 0)),
                      pl.BlockSpec((None, None, de, dm), lambda r, be, nu: (layer, be[r], 0, 0))],
            out_specs=pl.BlockSpec((bm, d), lambda r, be, nu: (r, 0)),
            scratch_shapes=[pltpu.VMEM((dm, de), BF16), pltpu.VMEM((dm, de), BF16),
                            pltpu.VMEM((de, dm), BF16)]),
        out_shape=jax.ShapeDtypeStruct((rows, d), jnp.uint32),
        compiler_params=_params("arbitrary"),
        name="moe_experts",
    )(block_e, n_used, x_sorted, w_gate, w_up, w_down)


def _combine_kernel(dest_ref, y_ref, x_ref, gate_ref, w_ref, o_ref, buf, sems, *, tm, t):
    i = pl.program_id(0)
    slot = i % 2

    def gather(tile, dst_slot):
        base = tile * tm

        def issue(r, carry):
            for k in range(TOP_K):
                d = dest_ref[k * t + base + r]
                pltpu.make_async_copy(y_ref.at[pl.ds(d, 1), :],
                                      buf.at[dst_slot, k, pl.ds(r, 1), :],
                                      sems.at[dst_slot]).start()
            return carry

        lax.fori_loop(0, tm, issue, 0, unroll=DMA_ISSUE_UNROLL)

    @pl.when(i == 0)
    def _():
        gather(0, 0)

    @pl.when(i + 1 < pl.num_programs(0))
    def _():
        gather(i + 1, 1 - slot)

    for k in range(TOP_K):
        pltpu.make_async_copy(y_ref.at[pl.ds(0, tm), :], buf.at[slot, k], sems.at[slot]).wait()
    w = w_ref[...]
    half = buf.shape[3]
    lo0, hi0 = _unpack_rows(buf[slot, 0])
    lo1, hi1 = _unpack_rows(buf[slot, 1])
    o_ref[:, :half] = x_ref[:, :half] + gate_ref[:, :half] * (w[:, 0:1] * lo0 + w[:, 1:2] * lo1)
    o_ref[:, half:] = x_ref[:, half:] + gate_ref[:, half:] * (w[:, 0:1] * hi0 + w[:, 1:2] * hi1)


def _combine(y_sorted, dest_flat, x, mod, w_tok, seq, tm=256):
    t, d = x.shape
    per_b = seq // tm
    return pl.pallas_call(
        functools.partial(_combine_kernel, tm=tm, t=t),
        grid_spec=pltpu.PrefetchScalarGridSpec(
            num_scalar_prefetch=1,
            grid=(t // tm,),
            in_specs=[pl.BlockSpec(memory_space=pl.ANY),
                      pl.BlockSpec((tm, d), lambda i, dest: (i, 0)),
                      pl.BlockSpec((None, 1, d), lambda i, dest: ((i // per_b) * 6 + 5, 0, 0)),
                      pl.BlockSpec((tm, TOP_K), lambda i, dest: (i, 0))],
            out_specs=pl.BlockSpec((tm, d), lambda i, dest: (i, 0)),
            scratch_shapes=[pltpu.VMEM((2, TOP_K, tm, y_sorted.shape[1]), y_sorted.dtype),
                            pltpu.SemaphoreType.DMA((2,))]),
        out_shape=jax.ShapeDtypeStruct((t, d), F32),
        compiler_params=_params("arbitrary"),
        name="moe_combine",
    )(dest_flat, y_sorted, x, mod, w_tok)


def _rope_tables(seq):
    half = MLA_ROPE // 2
    inv = 1.0 / (ROPE_BASE ** (jnp.arange(0, MLA_ROPE, 2, dtype=F32) / MLA_ROPE))
    ang = jnp.arange(seq, dtype=F32)[:, None] * inv[None, :]
    cos, sin = jnp.cos(ang), jnp.sin(ang)
    z = jnp.zeros((seq, half), F32)
    zz = jnp.zeros((seq, LANE - MLA_ROPE), F32)
    cos_t = jnp.concatenate([cos, cos, zz], axis=1)
    s1_t = jnp.concatenate([z, sin, zz], axis=1)
    s2_t = jnp.concatenate([-sin, z, zz], axis=1)
    return cos_t, s1_t, s2_t


def _pad_last(a, width):
    return jnp.pad(a, [(0, 0)] * (a.ndim - 1) + [(0, width - a.shape[-1])])


def kernel(x, c, norm1_g, norm2_g, w_ada, b_ada, w_in, q_lora_g, kv_lora_g, w_uq, w_ukv,
           q_head_g, k_head_g, mla_out_g, sb_out_g, w_o, router_w, router_b,
           w_gate, w_up, w_down):
    bsz, seq, d = x.shape
    t = bsz * seq
    depth = w_ada.shape[0]

    sb_col0 = COL_KR + MLA_ROPE
    cols = jnp.arange(w_in.shape[-1])
    col_scale = jnp.where((cols >= sb_col0) & (cols < sb_col0 + SB_WIDTH),
                          SB_DIM ** -0.5 * LOG2_E, 1.0).astype(F32)
    w_in_b = (w_in * col_scale).astype(BF16)
    w_in_mla = _pad_last(w_in_b[..., :sb_col0], MLA_IN_COLS)
    w_in_sb = w_in_b[..., sb_col0:]
    wq = w_uq.reshape(depth, Q_LORA, MLA_HEADS, MLA_QK)
    wuq_p = jnp.concatenate(
        [wq[..., :MLA_NOPE].reshape(depth, Q_LORA, -1),
         _pad_last(wq[..., MLA_NOPE:], LANE).reshape(depth, Q_LORA, -1)], axis=-1).astype(BF16)
    wkv = w_ukv.reshape(depth, KV_LORA, MLA_HEADS, MLA_NOPE + MLA_V)
    wukv_p = jnp.concatenate(
        [wkv[..., :MLA_NOPE].reshape(depth, KV_LORA, -1),
         wkv[..., MLA_NOPE:].reshape(depth, KV_LORA, -1)], axis=-1).astype(BF16)
    w_o_b = w_o.astype(BF16)
    q_scale = MLA_QK ** -0.5 * LOG2_E
    gqn = (q_head_g[:, None, :MLA_NOPE] * q_scale)
    gqr = _pad_last(q_head_g[:, None, MLA_NOPE:] * q_scale, LANE)
    gkn = k_head_g[:, None, :MLA_NOPE]
    gkr = _pad_last(k_head_g[:, None, MLA_NOPE:], LANE)
    rope_tabs = _rope_tables(seq)
    rw_t = router_w.T.astype(BF16)

    mod_all = _ada_modulation(c, w_ada, b_ada)
    n_blocks = -(-(t * TOP_K) // MOE_ROWS) + N_EXPERTS
    rows = n_blocks * MOE_ROWS

    xt = x.reshape(t, d)
    for l in range(depth):
        mod = mod_all[l].reshape(bsz * 6, 1, d)
        proj_mla, proj_sb = _in_proj(xt, norm1_g[l][None, :], mod, w_in_mla, w_in_sb, l, seq)
        q2d, k2d, v2d = _mla_prep(proj_mla, rope_tabs, q_lora_g[:, None, :],
                                  kv_lora_g[:, None, :], wuq_p, wukv_p, gqn, gqr, gkn, gkr, l, seq)
        o_mla = _mla_attention(q2d, k2d, v2d, bsz, seq)
        o_sb = _sb_attention(proj_sb, bsz, seq)
        xt, h2, logits_t = _out_proj(o_mla, o_sb, mla_out_g[l][None, :], sb_out_g[l][None, :],
                                     w_o_b, xt, mod, norm2_g[l][None, :], rw_t, l, seq)
        e_idx, w_tok, rank, cnt = _route(logits_t, router_b)
        counts = cnt[:, 0].astype(jnp.int32)
        padded = (counts + MOE_ROWS - 1) // MOE_ROWS * MOE_ROWS
        pad_end = jnp.cumsum(padded)
        pad_start = pad_end - padded
        eids = jnp.arange(N_EXPERTS, dtype=jnp.int32)
        seg_start = jnp.sum(jnp.where(e_idx[None] == eids[:, None, None],
                                      pad_start[:, None, None], 0), axis=0)
        dest = (seg_start + rank).reshape(TOP_K * t).astype(jnp.int32)
        n_used = (pad_end[-1:] // MOE_ROWS).astype(jnp.int32)
        first_row = jnp.minimum(jnp.arange(n_blocks, dtype=jnp.int32), n_used - 1) * MOE_ROWS
        block_e = jnp.sum(pad_end[None, :] <= first_row[:, None], axis=1).astype(jnp.int32)
        x_sorted = _dispatch(h2, dest, pad_end.astype(jnp.int32), rows)
        y_sorted = _experts(x_sorted, block_e, n_used, w_gate, w_up, w_down, l)
        xt = _combine(y_sorted, dest, xt, mod, w_tok.T, seq)
    return xt.reshape(bsz, seq, d)
```

```python
import functools

import jax
import jax.numpy as jnp
from jax import lax
from jax.experimental import pallas as pl
from jax.experimental.pallas import tpu as pltpu

F32 = jnp.float32
BF16 = jnp.bfloat16

EPS = 1e-6
ROPE_BASE = 10000.0
CHUNK = 64

MLA_HEADS = 8
MLA_NOPE = 128
MLA_ROPE = 64
MLA_QK = MLA_NOPE + MLA_ROPE
MLA_V = 128
Q_LORA = 512
KV_LORA = 256
SB_HEADS = 8
SB_DIM = 128
SB_WIDTH = SB_HEADS * SB_DIM
LANE = 128
QK_PAD = 2 * LANE
COL_CQ = 0
COL_CKV = Q_LORA
COL_KR = Q_LORA + KV_LORA
MLA_IN_COLS = COL_KR + LANE
COL_SQ = 0
COL_SK = SB_WIDTH
COL_SV = 2 * SB_WIDTH
SB_IN_COLS = 3 * SB_WIDTH

N_EXPERTS = 16
N_GROUPS = 4
EXPERTS_PER_GROUP = N_EXPERTS // N_GROUPS
TOP_K = 2
D_EXPERT = 512
MOE_ROWS = 256
DMA_ISSUE_UNROLL = 8
LOG2_E = 1.4426950408889634
SB_UNDERFLOW_LOG2 = 150.0
SB_FINISHED_LOG2 = 1e30

VMEM_LIMIT = 56 * 1024 * 1024


def _params(*dims):
    return pltpu.CompilerParams(dimension_semantics=dims, vmem_limit_bytes=VMEM_LIMIT)


def _rms(x):
    return x * lax.rsqrt(jnp.mean(x * x, axis=-1, keepdims=True) + EPS)


def _pack_rows(x):
    n = x.shape[1] // 2
    lo = lax.bitcast_convert_type(x[:, :n].astype(BF16).astype(F32), jnp.uint32)
    hi = lax.bitcast_convert_type(x[:, n:].astype(BF16).astype(F32), jnp.uint32)
    return (lo >> 16) | (hi & jnp.uint32(0xFFFF0000))


def _unpack_rows(w):
    lo = lax.bitcast_convert_type(w << 16, F32)
    hi = lax.bitcast_convert_type(w & jnp.uint32(0xFFFF0000), F32)
    return lo, hi


def _ada_kernel(c_ref, w_ref, b_ref, o_ref):
    w = w_ref[...]
    for b in range(c_ref.shape[0]):
        cb = c_ref[b]
        cb = cb * jax.nn.sigmoid(cb)
        o_ref[pl.ds(b, 1), :] = jnp.sum(cb * w, axis=0, keepdims=True) + b_ref[...]


def _ada_modulation(c, w_ada, b_ada, tn=1024):
    depth, d, n = w_ada.shape
    bsz = c.shape[0]
    return pl.pallas_call(
        _ada_kernel,
        grid=(depth, n // tn),
        in_specs=[pl.BlockSpec((bsz, d, 1), lambda l, j: (0, 0, 0)),
                  pl.BlockSpec((None, d, tn), lambda l, j: (l, 0, j)),
                  pl.BlockSpec((None, 1, tn), lambda l, j: (l, 0, j))],
        out_specs=pl.BlockSpec((None, bsz, tn), lambda l, j: (l, 0, j)),
        out_shape=jax.ShapeDtypeStruct((depth, bsz, n), F32),
        compiler_params=_params("arbitrary", "arbitrary"),
        name="ada_modulation",
    )(c[:, :, None], w_ada, b_ada[:, None, :])


def _in_proj_kernel(x_ref, g_ref, sh_ref, sc_ref, wa_ref, wb_ref, oa_ref, ob_ref):
    h = _rms(x_ref[...]) * g_ref[...]
    h = (h * (1.0 + sc_ref[...]) + sh_ref[...]).astype(BF16)
    oa_ref[...] = jnp.dot(h, wa_ref[...], preferred_element_type=F32).astype(oa_ref.dtype)
    ob_ref[...] = jnp.dot(h, wb_ref[...], preferred_element_type=F32).astype(ob_ref.dtype)


def _in_proj(x, g, mod, w_mla, w_sb, layer, seq, tm=256):
    t, d = x.shape
    na, nb = w_mla.shape[-1], w_sb.shape[-1]
    per_b = seq // tm
    return pl.pallas_call(
        _in_proj_kernel,
        grid=(t // tm,),
        in_specs=[pl.BlockSpec((tm, d), lambda i: (i, 0)),
                  pl.BlockSpec((1, d), lambda i: (0, 0)),
                  pl.BlockSpec((None, 1, d), lambda i: ((i // per_b) * 6 + 0, 0, 0)),
                  pl.BlockSpec((None, 1, d), lambda i: ((i // per_b) * 6 + 1, 0, 0)),
                  pl.BlockSpec((None, d, na), lambda i: (layer, 0, 0)),
                  pl.BlockSpec((None, d, nb), lambda i: (layer, 0, 0))],
        out_specs=[pl.BlockSpec((tm, na), lambda i: (i, 0)),
                   pl.BlockSpec((tm, nb), lambda i: (i, 0))],
        out_shape=[jax.ShapeDtypeStruct((t, na), BF16),
                   jax.ShapeDtypeStruct((t, nb), BF16)],
        compiler_params=_params("arbitrary"),
        name="in_proj",
    )(x, g, mod, mod, w_mla, w_sb)


def _mla_prep_kernel(cq_ref, ckv_ref, kr_ref, cos_ref, s1_ref, s2_ref, gql_ref, gkvl_ref,
                     wuq_ref, wukv_ref, gqn_ref, gqr_ref, gkn_ref, gkr_ref,
                     q_ref, k_ref, v_ref):
    nope_w = MLA_HEADS * MLA_NOPE
    cqn = _rms(cq_ref[...].astype(F32)) * gql_ref[...]
    q = jnp.dot(cqn.astype(BF16), wuq_ref[...], preferred_element_type=F32)
    ckvn = _rms(ckv_ref[...].astype(F32)) * gkvl_ref[...]
    kv = jnp.dot(ckvn.astype(BF16), wukv_ref[...], preferred_element_type=F32)
    cosv, s1, s2 = cos_ref[...], s1_ref[...], s2_ref[...]

    def rope(x):
        return (x * cosv + pltpu.roll(x, MLA_ROPE // 2, 1) * s1
                + pltpu.roll(x, LANE - MLA_ROPE // 2, 1) * s2)

    kr = kr_ref[...].astype(F32)
    kr_ss = jnp.sum(kr * kr, axis=-1, keepdims=True)
    kr_roped = rope(kr * gkr_ref[...])
    gqn, gqr, gkn = gqn_ref[...], gqr_ref[...], gkn_ref[...]
    for h in range(MLA_HEADS):
        qn = q[:, h * LANE:(h + 1) * LANE]
        qr = q[:, nope_w + h * LANE:nope_w + (h + 1) * LANE]
        ss = jnp.sum(qn * qn, axis=-1, keepdims=True) + jnp.sum(qr * qr, axis=-1, keepdims=True)
        r = lax.rsqrt(ss * (1.0 / MLA_QK) + EPS)
        q_ref[:, h * QK_PAD:h * QK_PAD + LANE] = (qn * r * gqn).astype(BF16)
        q_ref[:, h * QK_PAD + LANE:(h + 1) * QK_PAD] = (rope(qr * gqr) * r).astype(BF16)
        kn = kv[:, h * LANE:(h + 1) * LANE]
        rk = lax.rsqrt((jnp.sum(kn * kn, axis=-1, keepdims=True) + kr_ss) * (1.0 / MLA_QK) + EPS)
        k_ref[:, h * QK_PAD:h * QK_PAD + LANE] = (kn * rk * gkn).astype(BF16)
        k_ref[:, h * QK_PAD + LANE:(h + 1) * QK_PAD] = (kr_roped * rk).astype(BF16)
        v_ref[:, 2 * h * MLA_V:(2 * h + 1) * MLA_V] = (
            kv[:, nope_w + h * MLA_V:nope_w + (h + 1) * MLA_V].astype(BF16))
        v_ref[:, (2 * h + 1) * MLA_V:(2 * h + 2) * MLA_V] = jnp.ones((kn.shape[0], MLA_V), BF16)


def _mla_prep(proj, rope_tabs, gql, gkvl, wuq, wukv, gqn, gqr, gkn, gkr, layer, seq, tm=256):
    t = proj.shape[0]
    per_b = seq // tm
    cos_t, s1_t, s2_t = rope_tabs
    row = lambda i: (i, 0)
    full2 = lambda i: (0, 0)
    lay = lambda i: (layer, 0, 0)
    tab = lambda i: (i % per_b, 0)
    return pl.pallas_call(
        _mla_prep_kernel,
        grid=(t // tm,),
        in_specs=[pl.BlockSpec((tm, Q_LORA), lambda i: (i, COL_CQ // Q_LORA)),
                  pl.BlockSpec((tm, KV_LORA), lambda i: (i, COL_CKV // KV_LORA)),
                  pl.BlockSpec((tm, LANE), lambda i: (i, COL_KR // LANE)),
                  pl.BlockSpec((tm, LANE), tab),
                  pl.BlockSpec((tm, LANE), tab),
                  pl.BlockSpec((tm, LANE), tab),
                  pl.BlockSpec((None, 1, Q_LORA), lay),
                  pl.BlockSpec((None, 1, KV_LORA), lay),
                  pl.BlockSpec((None,) + wuq.shape[1:], lay),
                  pl.BlockSpec((None,) + wukv.shape[1:], lay),
                  pl.BlockSpec((None, 1, LANE), lay),
                  pl.BlockSpec((None, 1, LANE), lay),
                  pl.BlockSpec((None, 1, LANE), lay),
                  pl.BlockSpec((None, 1, LANE), lay)],
        out_specs=[pl.BlockSpec((tm, MLA_HEADS * QK_PAD), row),
                   pl.BlockSpec((tm, MLA_HEADS * QK_PAD), row),
                   pl.BlockSpec((tm, MLA_HEADS * 2 * MLA_V), row)],
        out_shape=[jax.ShapeDtypeStruct((t, MLA_HEADS * QK_PAD), BF16),
                   jax.ShapeDtypeStruct((t, MLA_HEADS * QK_PAD), BF16),
                   jax.ShapeDtypeStruct((t, MLA_HEADS * 2 * MLA_V), BF16)],
        compiler_params=_params("arbitrary"),
        name="mla_prep",
    )(proj, proj, proj, cos_t, s1_t, s2_t, gql, gkvl, wuq, wukv, gqn, gqr, gkn, gkr)


def _mla_attn_kernel(q_ref, k_ref, v_ref, o_ref, m_ref, l_ref, acc_ref, sa_ref, sb_ref, *,
                     tq, heads):
    i = pl.program_id(2)
    vw = 2 * MLA_V

    def scores(g, kb):
        q = q_ref[:, g * QK_PAD:(g + 1) * QK_PAD]
        k = k_ref[pl.ds(pl.multiple_of(kb * tq, tq), tq), g * QK_PAD:(g + 1) * QK_PAD]
        return lax.dot_general(q, k, (((1,), (1,)), ((), ())), preferred_element_type=F32)

    def probs(s, m):
        return jnp.concatenate(
            [jnp.exp2(s[:, c * LANE:(c + 1) * LANE] - m) for c in range(tq // LANE)],
            axis=1).astype(BF16)

    def pv(g, p, kb):
        v = v_ref[pl.ds(pl.multiple_of(kb * tq, tq), tq), g * vw:(g + 1) * vw]
        return jnp.dot(p, v, preferred_element_type=F32)

    def update(g, s, kb):
        m_old = m_ref[g]
        m_new = jnp.maximum(m_old, jnp.max(s, axis=-1, keepdims=True))
        alpha = jnp.exp2(m_old - m_new)
        r = pv(g, probs(s, m_new), kb)
        acc_ref[g] = alpha * acc_ref[g] + r[:, :MLA_V]
        l_ref[g] = alpha * l_ref[g] + r[:, MLA_V:]
        m_ref[g] = m_new

    q_chunk = lax.broadcasted_iota(jnp.int32, (tq, 1), 0) // CHUNK
    k_chunk = lax.broadcasted_iota(jnp.int32, (1, tq), 1) // CHUNK
    visible = k_chunk <= q_chunk
    for g in range(heads):
        s = jnp.where(visible, scores(g, i), -jnp.inf)
        m = jnp.broadcast_to(jnp.max(s, axis=-1, keepdims=True), (tq, LANE))
        r = pv(g, probs(s, m), i)
        m_ref[g] = m
        acc_ref[g] = r[:, :MLA_V]
        l_ref[g] = r[:, MLA_V:]
        sa_ref[g] = scores(g, 0)

    last = jnp.maximum(i - 1, 0)

    def pair(t, carry):
        j = 2 * t
        for g in range(heads):
            sb_ref[g] = scores(g, jnp.minimum(j + 1, last))
            update(g, sa_ref[g], j)
        for g in range(heads):
            sa_ref[g] = scores(g, jnp.minimum(j + 2, last))
            update(g, sb_ref[g], j + 1)
        return carry

    lax.fori_loop(0, i // 2, pair, 0)

    @pl.when(i % 2 == 1)
    def _():
        for g in range(heads):
            update(g, sa_ref[g], i - 1)

    for g in range(heads):
        o_ref[:, g * MLA_V:(g + 1) * MLA_V] = acc_ref[g] / l_ref[g]


def _mla_attention(q2d, k2d, v2d, bsz, seq, tq=512, heads=2):
    t = q2d.shape[0]
    tq = min(tq, seq)
    nq = seq // tq
    vw = 2 * MLA_V
    return pl.pallas_call(
        functools.partial(_mla_attn_kernel, tq=tq, heads=heads),
        grid=(bsz, MLA_HEADS // heads, nq),
        in_specs=[pl.BlockSpec((tq, heads * QK_PAD), lambda b, h, i: (b * nq + i, h)),
                  pl.BlockSpec((seq, heads * QK_PAD), lambda b, h, i: (b, h)),
                  pl.BlockSpec((seq, heads * vw), lambda b, h, i: (b, h))],
        out_specs=pl.BlockSpec((tq, heads * MLA_V), lambda b, h, i: (b * nq + i, h)),
        out_shape=jax.ShapeDtypeStruct((t, MLA_HEADS * MLA_V), F32),
        scratch_shapes=[pltpu.VMEM((heads, tq, LANE), F32), pltpu.VMEM((heads, tq, MLA_V), F32),
                        pltpu.VMEM((heads, tq, MLA_V), F32),
                        pltpu.VMEM((heads, tq, tq), F32), pltpu.VMEM((heads, tq, tq), F32)],
        compiler_params=_params("arbitrary", "arbitrary", "arbitrary"),
        name="mla_attention",
    )(q2d, k2d, v2d)


def _sb_attn_kernel(q_ref, k_ref, v_ref, tri_ref, o_ref, run_ref, acc_ref, *, tq, tk, heads):
    i = pl.program_id(2)
    tri = tri_ref[...]
    sub = tq // tk
    streams = [(g, r) for g in range(heads) for r in range(sub)]

    def block(s, kb, mask):
        g, r = streams[s]
        cols = slice(g * SB_DIM, (g + 1) * SB_DIM)
        keys = pl.ds(pl.multiple_of(kb * tk, tk), tk)
        q = q_ref[r * tk:(r + 1) * tk, cols]
        z = lax.dot_general(q, k_ref[keys, cols], (((1,), (1,)), ((), ())),
                            preferred_element_type=F32)
        neg_abs = lax.bitcast_convert_type(
            lax.bitcast_convert_type(z, jnp.uint32) | jnp.uint32(0x80000000), F32)
        cost = jnp.maximum(z, 0.0) + jnp.log2(1.0 + jnp.exp2(neg_abs))
        if mask is not None:
            cost = jnp.where(mask, cost, 0.0)
        after = jnp.dot(cost.astype(BF16), tri, preferred_element_type=F32) + run_ref[s]
        a = jnp.exp2(z - cost - after)
        if mask is not None:
            a = jnp.where(mask, a, 0.0)
        acc_ref[s] += jnp.dot(a.astype(BF16), v_ref[keys, cols], preferred_element_type=F32)
        run_ref[s] += jnp.sum(cost, axis=-1, keepdims=True)

    def more():
        return jnp.min(run_ref[...]) < SB_UNDERFLOW_LOG2

    run_ref[...] = jnp.zeros_like(run_ref)
    acc_ref[...] = jnp.zeros_like(acc_ref)
    strictly_before = (lax.broadcasted_iota(jnp.int32, (tk, tk), 1)
                       < lax.broadcasted_iota(jnp.int32, (tk, tk), 0))
    for s, (g, r) in enumerate(streams):
        block(s, i * sub + r, strictly_before)

    def cond(c):
        return (c[0] < i * sub + sub - 1) & c[1]

    def body(c):
        for s, (g, r) in enumerate(streams):
            kb = i * sub + r - 1 - c[0]
            run_ref[s] = jnp.where(kb >= 0, run_ref[s], SB_FINISHED_LOG2)
            block(s, jnp.maximum(kb, 0), None)
        return c[0] + 1, more()

    lax.while_loop(cond, body, (jnp.int32(0), more()))
    for s, (g, r) in enumerate(streams):
        o_ref[r * tk:(r + 1) * tk, g * SB_DIM:(g + 1) * SB_DIM] = acc_ref[s]


def _sb_attention(proj, bsz, seq, tq=512, tk=256, heads=4):
    t = proj.shape[0]
    tq = min(tq, seq)
    nq = seq // tq
    tri = (lax.broadcasted_iota(jnp.int32, (tk, tk), 0)
           > lax.broadcasted_iota(jnp.int32, (tk, tk), 1)).astype(BF16)
    w = heads * SB_DIM
    cq, ck, cv = COL_SQ // w, COL_SK // w, COL_SV // w
    n_streams = heads * (tq // tk)
    return pl.pallas_call(
        functools.partial(_sb_attn_kernel, tq=tq, tk=tk, heads=heads),
        grid=(bsz, SB_HEADS // heads, nq),
        in_specs=[pl.BlockSpec((tq, w), lambda b, h, i: (b * nq + i, cq + h)),
                  pl.BlockSpec((seq, w), lambda b, h, i: (b, ck + h)),
                  pl.BlockSpec((seq, w), lambda b, h, i: (b, cv + h)),
                  pl.BlockSpec((tk, tk), lambda b, h, i: (0, 0))],
        out_specs=pl.BlockSpec((tq, w), lambda b, h, i: (b * nq + i, h)),
        out_shape=jax.ShapeDtypeStruct((t, SB_WIDTH), F32),
        scratch_shapes=[pltpu.VMEM((n_streams, tk, 1), F32),
                        pltpu.VMEM((n_streams, tk, SB_DIM), F32)],
        compiler_params=_params("arbitrary", "arbitrary", "arbitrary"),
        name="sb_attention",
    )(proj, proj, proj, tri)


def _out_proj_kernel(om_ref, os_ref, gm_ref, gs_ref, wo_ref, x_ref, gate_ref, g2_ref,
                     sh_ref, sc_ref, rw_ref, xo_ref, h2_ref, lg_ref):
    wm = om_ref.shape[1]
    nm = (_rms(om_ref[...]) * gm_ref[...]).astype(BF16)
    ns = (_rms(os_ref[...]) * gs_ref[...]).astype(BF16)
    y = (jnp.dot(nm, wo_ref[:wm, :], preferred_element_type=F32)
         + jnp.dot(ns, wo_ref[wm:, :], preferred_element_type=F32))
    xn = x_ref[...] + gate_ref[...] * y
    xo_ref[...] = xn
    h2 = _rms(xn) * g2_ref[...]
    h2 = h2 * (1.0 + sc_ref[...]) + sh_ref[...]
    h2_ref[...] = _pack_rows(h2)
    lg_ref[...] = lax.dot_general(rw_ref[...], h2.astype(BF16), (((1,), (1,)), ((), ())),
                                  preferred_element_type=F32)


def _out_proj(o_mla, o_sb, gm, gs, w_o, x, mod, g2, rw_t, layer, seq, tm=256):
    t, d = x.shape
    per_b = seq // tm
    row = lambda i: (i, 0)
    full2 = lambda i: (0, 0)
    modspec = lambda j: pl.BlockSpec((None, 1, d), lambda i: ((i // per_b) * 6 + j, 0, 0))
    return pl.pallas_call(
        _out_proj_kernel,
        grid=(t // tm,),
        in_specs=[pl.BlockSpec((tm, o_mla.shape[1]), row),
                  pl.BlockSpec((tm, o_sb.shape[1]), row),
                  pl.BlockSpec((1, o_mla.shape[1]), full2),
                  pl.BlockSpec((1, o_sb.shape[1]), full2),
                  pl.BlockSpec((None,) + w_o.shape[1:], lambda i: (layer, 0, 0)),
                  pl.BlockSpec((tm, d), row),
                  modspec(2),
                  pl.BlockSpec((1, d), full2),
                  modspec(3),
                  modspec(4),
                  pl.BlockSpec(rw_t.shape, full2)],
        out_specs=[pl.BlockSpec((tm, d), row),
                   pl.BlockSpec((tm, d // 2), row),
                   pl.BlockSpec((N_EXPERTS, tm), lambda i: (0, i))],
        out_shape=[jax.ShapeDtypeStruct((t, d), F32),
                   jax.ShapeDtypeStruct((t, d // 2), jnp.uint32),
                   jax.ShapeDtypeStruct((N_EXPERTS, t), F32)],
        compiler_params=_params("arbitrary"),
        name="out_proj",
    )(o_mla, o_sb, gm, gs, w_o, x, mod, g2, mod, mod, rw_t)


def _route_kernel(lg_ref, rb_ref, tri_ref, e_ref, w_ref, rank_ref, cnt_ref, run_ref):
    i = pl.program_id(0)

    @pl.when(i == 0)
    def _():
        run_ref[...] = jnp.zeros_like(run_ref)

    scores = jax.nn.sigmoid(lg_ref[...])
    sel = scores + rb_ref[...]
    tm = sel.shape[1]
    epg = EXPERTS_PER_GROUP
    srow = [sel[e:e + 1, :] for e in range(N_EXPERTS)]
    prow = [scores[e:e + 1, :] for e in range(N_EXPERTS)]

    gscore = []
    for g in range(N_GROUPS):
        a, b, c, d = srow[g * epg:(g + 1) * epg]
        hi1, lo1 = jnp.maximum(a, b), jnp.minimum(a, b)
        hi2, lo2 = jnp.maximum(c, d), jnp.minimum(c, d)
        gscore.append(jnp.maximum(hi1, hi2)
                      + jnp.maximum(jnp.minimum(hi1, hi2), jnp.maximum(lo1, lo2)))
    best = gscore[0]
    gidx = jnp.zeros((1, tm), jnp.int32)
    for g in range(1, N_GROUPS):
        better = gscore[g] > best
        gidx = jnp.where(better, g, gidx)
        best = jnp.where(better, gscore[g], best)

    def pick(rows, j):
        out = rows[j]
        for g in range(1, N_GROUPS):
            out = jnp.where(gidx == g, rows[g * epg + j], out)
        return out

    v = [pick(srow, j) for j in range(epg)]
    p = [pick(prow, j) for j in range(epg)]
    pos = []
    for j in range(epg):
        r = jnp.zeros((1, tm), jnp.int32)
        for o in range(epg):
            if o == j:
                continue
            ahead = (v[o] > v[j]) | ((v[o] == v[j]) & (o < j))
            r = r + ahead.astype(jnp.int32)
        pos.append(r)
    local, wsel = [], []
    for k in range(TOP_K):
        lk = jnp.zeros((1, tm), jnp.int32)
        wk = jnp.zeros((1, tm), F32)
        for j in range(epg):
            hit = pos[j] == k
            lk = jnp.where(hit, j, lk)
            wk = jnp.where(hit, p[j], wk)
        local.append(lk)
        wsel.append(wk)
    wsum = wsel[0] + wsel[1]
    idx = [gidx * epg + local[k] for k in range(TOP_K)]

    eiota = lax.broadcasted_iota(jnp.int32, (N_EXPERTS, tm), 0)
    hit = [eiota == idx[k] for k in range(TOP_K)]
    onehot = (hit[0] | hit[1]).astype(F32)
    before = jnp.dot(onehot.astype(BF16), tri_ref[...], preferred_element_type=F32) + run_ref[...]
    for k in range(TOP_K):
        e_ref[pl.ds(k, 1), :] = idx[k]
        w_ref[pl.ds(k, 1), :] = wsel[k] / wsum
        rank_ref[pl.ds(k, 1), :] = jnp.sum(jnp.where(hit[k], before, 0.0), axis=0,
                                           keepdims=True).astype(jnp.int32)
    run_ref[...] += jnp.sum(onehot, axis=1, keepdims=True)
    cnt_ref[...] = jnp.broadcast_to(run_ref[...], cnt_ref.shape)


def _route(logits_t, router_b, tm=512):
    t = logits_t.shape[1]
    tm = min(tm, t)
    tri = (lax.broadcasted_iota(jnp.int32, (tm, tm), 0)
           < lax.broadcasted_iota(jnp.int32, (tm, tm), 1)).astype(BF16)
    blk = lambda rows: pl.BlockSpec((rows, tm), lambda i: (0, i))
    return pl.pallas_call(
        _route_kernel,
        grid=(t // tm,),
        in_specs=[blk(N_EXPERTS),
                  pl.BlockSpec((N_EXPERTS, 1), lambda i: (0, 0)),
                  pl.BlockSpec((tm, tm), lambda i: (0, 0))],
        out_specs=[blk(TOP_K), blk(TOP_K), blk(TOP_K),
                   pl.BlockSpec((N_EXPERTS, LANE), lambda i: (0, 0))],
        out_shape=[jax.ShapeDtypeStruct((TOP_K, t), jnp.int32),
                   jax.ShapeDtypeStruct((TOP_K, t), F32),
                   jax.ShapeDtypeStruct((TOP_K, t), jnp.int32),
                   jax.ShapeDtypeStruct((N_EXPERTS, LANE), F32)],
        scratch_shapes=[pltpu.VMEM((N_EXPERTS, 1), F32)],
        compiler_params=_params("arbitrary"),
        name="route",
    )(logits_t, router_b[:, None], tri)


def _dispatch_kernel(dest_ref, pend_ref, h_ref, o_ref, zero_ref, sem, zsem, *, tm, t):
    base = pl.program_id(0) * tm

    @pl.when(pl.program_id(0) == 0)
    def _():
        zero_ref[...] = jnp.zeros_like(zero_ref)

        def fill(e):
            end = pend_ref[e]
            start = pend_ref[e - 1] if e > 0 else 0
            dst = o_ref.at[pl.ds(pl.multiple_of(jnp.maximum(end - MOE_ROWS, 0), MOE_ROWS),
                                 MOE_ROWS), :]
            return end > start, pltpu.make_async_copy(zero_ref, dst, zsem)

        for e in range(N_EXPERTS):
            nonempty, cp = fill(e)
            pl.when(nonempty)(cp.start)
        for e in range(N_EXPERTS):
            nonempty, cp = fill(e)
            pl.when(nonempty)(cp.wait)

        def tail(r):
            dst = o_ref.at[pl.ds(pl.multiple_of(r * MOE_ROWS, MOE_ROWS), MOE_ROWS), :]
            return pltpu.make_async_copy(zero_ref, dst, zsem)

        first_free = pend_ref[N_EXPERTS - 1] // MOE_ROWS
        n_blocks = o_ref.shape[0] // MOE_ROWS
        lax.fori_loop(first_free, n_blocks, lambda r, c: (tail(r).start(), c)[1], 0)
        lax.fori_loop(first_free, n_blocks, lambda r, c: (tail(r).wait(), c)[1], 0)

    def issue(r, carry):
        for k in range(TOP_K):
            d = dest_ref[k * t + base + r]
            pltpu.make_async_copy(h_ref.at[pl.ds(r, 1), :], o_ref.at[pl.ds(d, 1), :], sem).start()
        return carry

    lax.fori_loop(0, tm, issue, 0, unroll=DMA_ISSUE_UNROLL)
    for k in range(TOP_K):
        pltpu.make_async_copy(h_ref, o_ref.at[pl.ds(0, tm), :], sem).wait()


def _dispatch(h2, dest_flat, pad_end, rows, tm=256):
    t, d = h2.shape
    return pl.pallas_call(
        functools.partial(_dispatch_kernel, tm=tm, t=t),
        grid_spec=pltpu.PrefetchScalarGridSpec(
            num_scalar_prefetch=2,
            grid=(t // tm,),
            in_specs=[pl.BlockSpec((tm, d), lambda i, dest, pend: (i, 0))],
            out_specs=pl.BlockSpec(memory_space=pl.ANY),
            scratch_shapes=[pltpu.VMEM((MOE_ROWS, d), h2.dtype),
                            pltpu.SemaphoreType.DMA(()), pltpu.SemaphoreType.DMA(())]),
        out_shape=jax.ShapeDtypeStruct((rows, d), h2.dtype),
        compiler_params=_params("arbitrary"),
        name="moe_dispatch",
    )(dest_flat, pad_end, h2)


def _expert_kernel(be_ref, nu_ref, x_ref, wg_ref, wu_ref, wd_ref, y_ref, wgb, wub, wdb):
    r = pl.program_id(0)
    live = r < nu_ref[0]
    new_expert = (r == 0) | (be_ref[jnp.maximum(r - 1, 0)] != be_ref[r])

    @pl.when(live & new_expert)
    def _():
        wgb[...] = wg_ref[...].astype(BF16)
        wub[...] = wu_ref[...].astype(BF16)
        wdb[...] = wd_ref[...].astype(BF16)

    @pl.when(live)
    def _():
        lo, hi = _unpack_rows(x_ref[...])
        xb = jnp.concatenate([lo.astype(BF16), hi.astype(BF16)], axis=1)
        g = jnp.dot(xb, wgb[...], preferred_element_type=F32)
        u = jnp.dot(xb, wub[...], preferred_element_type=F32)
        a = (g * jax.nn.sigmoid(g)) * u
        y_ref[...] = _pack_rows(jnp.dot(a.astype(BF16), wdb[...], preferred_element_type=F32))

    @pl.when(jnp.logical_not(live))
    def _():
        y_ref[...] = jnp.zeros_like(y_ref)


def _experts(x_sorted, block_e, n_used, w_gate, w_up, w_down, layer):
    rows, d = x_sorted.shape
    de = w_gate.shape[-1]
    dm = w_gate.shape[-2]
    bm = MOE_ROWS
    blk = lambda r, be, nu: (jnp.minimum(r, nu[0] - 1), 0)
    return pl.pallas_call(
        _expert_kernel,
        grid_spec=pltpu.PrefetchScalarGridSpec(
            num_scalar_prefetch=2,
            grid=(rows // bm,),
            in_specs=[pl.BlockSpec((bm, d), blk),
                      pl.BlockSpec((None, None, dm, de), lambda r, be, nu: (layer, be[r], 0, 0)),
                      pl.BlockSpec((None, None, dm, de), lambda r, be, nu: (layer, be[r], 0, 0)),
                      pl.BlockSpec((None, None, de, dm), lambda r, be, nu: (layer, be[r], 0, 0))],
            out_specs=pl.BlockSpec((bm, d), lambda r, be, nu: (r, 0)),
            scratch_shapes=[pltpu.VMEM((dm, de), BF16), pltpu.VMEM((dm, de), BF16),
                            pltpu.VMEM((de, dm), BF16)]),
        out_shape=jax.ShapeDtypeStruct((rows, d), jnp.uint32),
        compiler_params=_params("arbitrary"),
        name="moe_experts",
    )(block_e, n_used, x_sorted, w_gate, w_up, w_down)


def _combine_kernel(dest_ref, y_ref, x_ref, gate_ref, w_ref, o_ref, buf, sems, *, tm, t):
    i = pl.program_id(0)
    slot = i % 2

    def gather(tile, dst_slot):
        base = tile * tm

        def issue(r, carry):
            for k in range(TOP_K):
                d = dest_ref[k * t + base + r]
                pltpu.make_async_copy(y_ref.at[pl.ds(d, 1), :],
                                      buf.at[dst_slot, k, pl.ds(r, 1), :],
                                      sems.at[dst_slot]).start()
            return carry

        lax.fori_loop(0, tm, issue, 0, unroll=DMA_ISSUE_UNROLL)

    @pl.when(i == 0)
    def _():
        gather(0, 0)

    @pl.when(i + 1 < pl.num_programs(0))
    def _():
        gather(i + 1, 1 - slot)

    for k in range(TOP_K):
        pltpu.make_async_copy(y_ref.at[pl.ds(0, tm), :], buf.at[slot, k], sems.at[slot]).wait()
    w = w_ref[...]
    half = buf.shape[3]
    lo0, hi0 = _unpack_rows(buf[slot, 0])
    lo1, hi1 = _unpack_rows(buf[slot, 1])
    o_ref[:, :half] = x_ref[:, :half] + gate_ref[:, :half] * (w[:, 0:1] * lo0 + w[:, 1:2] * lo1)
    o_ref[:, half:] = x_ref[:, half:] + gate_ref[:, half:] * (w[:, 0:1] * hi0 + w[:, 1:2] * hi1)


def _combine(y_sorted, dest_flat, x, mod, w_tok, seq, tm=256):
    t, d = x.shape
    per_b = seq // tm
    return pl.pallas_call(
        functools.partial(_combine_kernel, tm=tm, t=t),
        grid_spec=pltpu.PrefetchScalarGridSpec(
            num_scalar_prefetch=1,
            grid=(t // tm,),
            in_specs=[pl.BlockSpec(memory_space=pl.ANY),
                      pl.BlockSpec((tm, d), lambda i, dest: (i, 0)),
                      pl.BlockSpec((None, 1, d), lambda i, dest: ((i // per_b) * 6 + 5, 0, 0)),
                      pl.BlockSpec((tm, TOP_K), lambda i, dest: (i, 0))],
            out_specs=pl.BlockSpec((tm, d), lambda i, dest: (i, 0)),
            scratch_shapes=[pltpu.VMEM((2, TOP_K, tm, y_sorted.shape[1]), y_sorted.dtype),
                            pltpu.SemaphoreType.DMA((2,))]),
        out_shape=jax.ShapeDtypeStruct((t, d), F32),
        compiler_params=_params("arbitrary"),
        name="moe_combine",
    )(dest_flat, y_sorted, x, mod, w_tok)


def _rope_tables(seq):
    half = MLA_ROPE // 2
    inv = 1.0 / (ROPE_BASE ** (jnp.arange(0, MLA_ROPE, 2, dtype=F32) / MLA_ROPE))
    ang = jnp.arange(seq, dtype=F32)[:, None] * inv[None, :]
    cos, sin = jnp.cos(ang), jnp.sin(ang)
    z = jnp.zeros((seq, half), F32)
    zz = jnp.zeros((seq, LANE - MLA_ROPE), F32)
    cos_t = jnp.concatenate([cos, cos, zz], axis=1)
    s1_t = jnp.concatenate([z, sin, zz], axis=1)
    s2_t = jnp.concatenate([-sin, z, zz], axis=1)
    return cos_t, s1_t, s2_t


def _pad_last(a, width):
    return jnp.pad(a, [(0, 0)] * (a.ndim - 1) + [(0, width - a.shape[-1])])


def kernel(x, c, norm1_g, norm2_g, w_ada, b_ada, w_in, q_lora_g, kv_lora_g, w_uq, w_ukv,
           q_head_g, k_head_g, mla_out_g, sb_out_g, w_o, router_w, router_b,
           w_gate, w_up, w_down):
    bsz, seq, d = x.shape
    t = bsz * seq
    depth = w_ada.shape[0]

    sb_col0 = COL_KR + MLA_ROPE
    w_in_mla = _pad_last(w_in[..., :sb_col0], MLA_IN_COLS).astype(BF16)
    sb_col_scale = jnp.where(jnp.arange(SB_IN_COLS) < SB_WIDTH, SB_DIM ** -0.5 * LOG2_E, 1.0)
    w_in_sb = (w_in[..., sb_col0:] * sb_col_scale.astype(F32)).astype(BF16)
    wq = w_uq.reshape(depth, Q_LORA, MLA_HEADS, MLA_QK)
    wuq_p = jnp.concatenate(
        [wq[..., :MLA_NOPE].reshape(depth, Q_LORA, -1),
         _pad_last(wq[..., MLA_NOPE:], LANE).reshape(depth, Q_LORA, -1)], axis=-1).astype(BF16)
    wkv = w_ukv.reshape(depth, KV_LORA, MLA_HEADS, MLA_NOPE + MLA_V)
    wukv_p = jnp.concatenate(
        [wkv[..., :MLA_NOPE].reshape(depth, KV_LORA, -1),
         wkv[..., MLA_NOPE:].reshape(depth, KV_LORA, -1)], axis=-1).astype(BF16)
    w_o_b = w_o.astype(BF16)
    q_scale = MLA_QK ** -0.5 * LOG2_E
    gqn = (q_head_g[:, None, :MLA_NOPE] * q_scale)
    gqr = _pad_last(q_head_g[:, None, MLA_NOPE:] * q_scale, LANE)
    gkn = k_head_g[:, None, :MLA_NOPE]
    gkr = _pad_last(k_head_g[:, None, MLA_NOPE:], LANE)
    rope_tabs = _rope_tables(seq)
    rw_t = router_w.T.astype(BF16)

    mod_all = _ada_modulation(c, w_ada, b_ada)
    n_blocks = -(-(t * TOP_K) // MOE_ROWS) + N_EXPERTS
    rows = n_blocks * MOE_ROWS

    xt = x.reshape(t, d)
    for l in range(depth):
        mod = mod_all[l].reshape(bsz * 6, 1, d)
        proj_mla, proj_sb = _in_proj(xt, norm1_g[l][None, :], mod, w_in_mla, w_in_sb, l, seq)
        q2d, k2d, v2d = _mla_prep(proj_mla, rope_tabs, q_lora_g[:, None, :],
                                  kv_lora_g[:, None, :], wuq_p, wukv_p, gqn, gqr, gkn, gkr, l, seq)
        o_mla = _mla_attention(q2d, k2d, v2d, bsz, seq)
        o_sb = _sb_attention(proj_sb, bsz, seq)
        xt, h2, logits_t = _out_proj(o_mla, o_sb, mla_out_g[l][None, :], sb_out_g[l][None, :],
                                     w_o_b, xt, mod, norm2_g[l][None, :], rw_t, l, seq)
        e_idx, w_tok, rank, cnt = _route(logits_t, router_b)
        counts = cnt[:, 0].astype(jnp.int32)
        padded = (counts + MOE_ROWS - 1) // MOE_ROWS * MOE_ROWS
        pad_end = jnp.cumsum(padded)
        pad_start = pad_end - padded
        eids = jnp.arange(N_EXPERTS, dtype=jnp.int32)
        seg_start = jnp.sum(jnp.where(e_idx[None] == eids[:, None, None],
                                      pad_start[:, None, None], 0), axis=0)
        dest = (seg_start + rank).reshape(TOP_K * t).astype(jnp.int32)
        n_used = (pad_end[-1:] // MOE_ROWS).astype(jnp.int32)
        first_row = jnp.minimum(jnp.arange(n_blocks, dtype=jnp.int32), n_used - 1) * MOE_ROWS
        block_e = jnp.sum(pad_end[None, :] <= first_row[:, None], axis=1).astype(jnp.int32)
        x_sorted = _dispatch(h2, dest, pad_end.astype(jnp.int32), rows)
        y_sorted = _experts(x_sorted, block_e, n_used, w_gate, w_up, w_down, l)
        xt = _combine(y_sorted, dest, xt, mod, w_tok.T, seq)
    return xt.reshape(bsz, seq, d)
```

```python
import functools

import jax
import jax.numpy as jnp
from jax import lax
from jax.experimental import pallas as pl
from jax.experimental.pallas import tpu as pltpu

F32 = jnp.float32
BF16 = jnp.bfloat16

EPS = 1e-6
ROPE_BASE = 10000.0
CHUNK = 64

MLA_HEADS = 8
MLA_NOPE = 128
MLA_ROPE = 64
MLA_QK = MLA_NOPE + MLA_ROPE
MLA_V = 128
Q_LORA = 512
KV_LORA = 256
SB_HEADS = 8
SB_DIM = 128
SB_WIDTH = SB_HEADS * SB_DIM
LANE = 128
QK_PAD = 2 * LANE
COL_CQ = 0
COL_CKV = Q_LORA
COL_KR = Q_LORA + KV_LORA
MLA_IN_COLS = COL_KR + LANE
COL_SQ = 0
COL_SK = SB_WIDTH
COL_SV = 2 * SB_WIDTH
SB_IN_COLS = 3 * SB_WIDTH

N_EXPERTS = 16
N_GROUPS = 4
EXPERTS_PER_GROUP = N_EXPERTS // N_GROUPS
TOP_K = 2
D_EXPERT = 512
MOE_ROWS = 256
DMA_ISSUE_UNROLL = 8
LOG2_E = 1.4426950408889634
SB_UNDERFLOW_LOG2 = 150.0
SB_FINISHED_LOG2 = 1e30

VMEM_LIMIT = 56 * 1024 * 1024


def _params(*dims):
    return pltpu.CompilerParams(dimension_semantics=dims, vmem_limit_bytes=VMEM_LIMIT)


def _rms(x):
    return x * lax.rsqrt(jnp.mean(x * x, axis=-1, keepdims=True) + EPS)


def _pack_rows(x):
    n = x.shape[1] // 2
    lo = lax.bitcast_convert_type(x[:, :n].astype(BF16).astype(F32), jnp.uint32)
    hi = lax.bitcast_convert_type(x[:, n:].astype(BF16).astype(F32), jnp.uint32)
    return (lo >> 16) | (hi & jnp.uint32(0xFFFF0000))


def _unpack_rows(w):
    lo = lax.bitcast_convert_type(w << 16, F32)
    hi = lax.bitcast_convert_type(w & jnp.uint32(0xFFFF0000), F32)
    return lo, hi


def _ada_kernel(c_ref, w_ref, b_ref, o_ref):
    w = w_ref[...]
    for b in range(c_ref.shape[0]):
        cb = c_ref[b]
        cb = cb * jax.nn.sigmoid(cb)
        o_ref[pl.ds(b, 1), :] = jnp.sum(cb * w, axis=0, keepdims=True) + b_ref[...]


def _ada_modulation(c_col, w_ada, b_ada3, layers, tn=1024):
    _, d, n = w_ada.shape
    bsz = c_col.shape[0]
    return pl.pallas_call(
        _ada_kernel,
        grid=(layers, n // tn),
        in_specs=[pl.BlockSpec((bsz, d, 1), lambda l, j: (0, 0, 0)),
                  pl.BlockSpec((None, d, tn), lambda l, j: (l, 0, j)),
                  pl.BlockSpec((None, 1, tn), lambda l, j: (l, 0, j))],
        out_specs=pl.BlockSpec((None, bsz, tn), lambda l, j: (l, 0, j)),
        out_shape=jax.ShapeDtypeStruct((layers, bsz, n), F32),
        compiler_params=_params("arbitrary", "arbitrary"),
        name="ada_modulation",
    )(c_col, w_ada, b_ada3)


def _split_in_weights_kernel(w_ref, scale_ref, oa_ref, ob_ref):
    sb_col0 = COL_KR + MLA_ROPE
    w = w_ref[...]
    head = w[:, :MLA_IN_COLS]
    col = lax.broadcasted_iota(jnp.int32, head.shape, 1)
    oa_ref[...] = jnp.where(col < sb_col0, head, 0.0).astype(BF16)
    ob_ref[...] = (w[:, sb_col0:] * scale_ref[...]).astype(BF16)


def _split_in_weights(w_in, sb_col_scale, tk=256):
    depth, d, n = w_in.shape
    return pl.pallas_call(
        _split_in_weights_kernel,
        grid=(depth, d // tk),
        in_specs=[pl.BlockSpec((None, tk, n), lambda l, i: (l, i, 0)),
                  pl.BlockSpec((1, SB_IN_COLS), lambda l, i: (0, 0))],
        out_specs=[pl.BlockSpec((None, tk, MLA_IN_COLS), lambda l, i: (l, i, 0)),
                   pl.BlockSpec((None, tk, SB_IN_COLS), lambda l, i: (l, i, 0))],
        out_shape=[jax.ShapeDtypeStruct((depth, d, MLA_IN_COLS), BF16),
                   jax.ShapeDtypeStruct((depth, d, SB_IN_COLS), BF16)],
        compiler_params=_params("arbitrary", "arbitrary"),
        name="split_in_weights",
    )(w_in, sb_col_scale)


def _in_proj_kernel(x_ref, g_ref, sh_ref, sc_ref, wa_ref, wb_ref, oa_ref, ob_ref):
    h = _rms(x_ref[...]) * g_ref[...]
    h = (h * (1.0 + sc_ref[...]) + sh_ref[...]).astype(BF16)
    oa_ref[...] = jnp.dot(h, wa_ref[...], preferred_element_type=F32).astype(oa_ref.dtype)
    ob_ref[...] = jnp.dot(h, wb_ref[...], preferred_element_type=F32).astype(ob_ref.dtype)


def _in_proj(x, g, mod, w_mla, w_sb, layer, seq, tm=256):
    t, d = x.shape
    na, nb = w_mla.shape[-1], w_sb.shape[-1]
    per_b = seq // tm
    return pl.pallas_call(
        _in_proj_kernel,
        grid=(t // tm,),
        in_specs=[pl.BlockSpec((tm, d), lambda i: (i, 0)),
                  pl.BlockSpec((1, d), lambda i: (0, 0)),
                  pl.BlockSpec((None, 1, d), lambda i: ((i // per_b) * 6 + 0, 0, 0)),
                  pl.BlockSpec((None, 1, d), lambda i: ((i // per_b) * 6 + 1, 0, 0)),
                  pl.BlockSpec((None, d, na), lambda i: (layer, 0, 0)),
                  pl.BlockSpec((None, d, nb), lambda i: (layer, 0, 0))],
        out_specs=[pl.BlockSpec((tm, na), lambda i: (i, 0)),
                   pl.BlockSpec((tm, nb), lambda i: (i, 0))],
        out_shape=[jax.ShapeDtypeStruct((t, na), BF16),
                   jax.ShapeDtypeStruct((t, nb), BF16)],
        compiler_params=_params("arbitrary"),
        name="in_proj",
    )(x, g, mod, mod, w_mla, w_sb)


def _mla_prep_kernel(cq_ref, ckv_ref, kr_ref, cos_ref, s1_ref, s2_ref, gql_ref, gkvl_ref,
                     wuq_ref, wukv_ref, gqn_ref, gqr_ref, gkn_ref, gkr_ref,
                     q_ref, k_ref, v_ref):
    nope_w = MLA_HEADS * MLA_NOPE
    cqn = _rms(cq_ref[...].astype(F32)) * gql_ref[...]
    q = jnp.dot(cqn.astype(BF16), wuq_ref[...], preferred_element_type=F32)
    ckvn = _rms(ckv_ref[...].astype(F32)) * gkvl_ref[...]
    kv = jnp.dot(ckvn.astype(BF16), wukv_ref[...], preferred_element_type=F32)
    cosv, s1, s2 = cos_ref[...], s1_ref[...], s2_ref[...]

    def rope(x):
        return (x * cosv + pltpu.roll(x, MLA_ROPE // 2, 1) * s1
                + pltpu.roll(x, LANE - MLA_ROPE // 2, 1) * s2)

    kr = kr_ref[...].astype(F32)
    kr_ss = jnp.sum(kr * kr, axis=-1, keepdims=True)
    kr_roped = rope(kr * gkr_ref[...])
    gqn, gqr, gkn = gqn_ref[...], gqr_ref[...], gkn_ref[...]
    for h in range(MLA_HEADS):
        qn = q[:, h * LANE:(h + 1) * LANE]
        qr = q[:, nope_w + h * LANE:nope_w + (h + 1) * LANE]
        ss = jnp.sum(qn * qn + qr * qr, axis=-1, keepdims=True)
        r = lax.rsqrt(ss * (1.0 / MLA_QK) + EPS)
        q_ref[:, h * QK_PAD:h * QK_PAD + LANE] = (qn * r * gqn).astype(BF16)
        q_ref[:, h * QK_PAD + LANE:(h + 1) * QK_PAD] = (rope(qr * gqr) * r).astype(BF16)
        kn = kv[:, h * LANE:(h + 1) * LANE]
        rk = lax.rsqrt((jnp.sum(kn * kn, axis=-1, keepdims=True) + kr_ss) * (1.0 / MLA_QK) + EPS)
        k_ref[:, h * QK_PAD:h * QK_PAD + LANE] = (kn * rk * gkn).astype(BF16)
        k_ref[:, h * QK_PAD + LANE:(h + 1) * QK_PAD] = (kr_roped * rk).astype(BF16)
        v_ref[:, 2 * h * MLA_V:(2 * h + 1) * MLA_V] = (
            kv[:, nope_w + h * MLA_V:nope_w + (h + 1) * MLA_V].astype(BF16))
        v_ref[:, (2 * h + 1) * MLA_V:(2 * h + 2) * MLA_V] = jnp.ones((kn.shape[0], MLA_V), BF16)


def _mla_prep(proj, rope_tabs, gql, gkvl, wuq, wukv, gqn, gqr, gkn, gkr, layer, seq, tm=256):
    t = proj.shape[0]
    per_b = seq // tm
    cos_t, s1_t, s2_t = rope_tabs
    row = lambda i: (i, 0)
    full2 = lambda i: (0, 0)
    lay = lambda i: (layer, 0, 0)
    tab = lambda i: (i % per_b, 0)
    return pl.pallas_call(
        _mla_prep_kernel,
        grid=(t // tm,),
        in_specs=[pl.BlockSpec((tm, Q_LORA), lambda i: (i, COL_CQ // Q_LORA)),
                  pl.BlockSpec((tm, KV_LORA), lambda i: (i, COL_CKV // KV_LORA)),
                  pl.BlockSpec((tm, LANE), lambda i: (i, COL_KR // LANE)),
                  pl.BlockSpec((tm, LANE), tab),
                  pl.BlockSpec((tm, LANE), tab),
                  pl.BlockSpec((tm, LANE), tab),
                  pl.BlockSpec((None, 1, Q_LORA), lay),
                  pl.BlockSpec((None, 1, KV_LORA), lay),
                  pl.BlockSpec((None,) + wuq.shape[1:], lay),
                  pl.BlockSpec((None,) + wukv.shape[1:], lay),
                  pl.BlockSpec((None, 1, LANE), lay),
                  pl.BlockSpec((None, 1, LANE), lay),
                  pl.BlockSpec((None, 1, LANE), lay),
                  pl.BlockSpec((None, 1, LANE), lay)],
        out_specs=[pl.BlockSpec((tm, MLA_HEADS * QK_PAD), row),
                   pl.BlockSpec((tm, MLA_HEADS * QK_PAD), row),
                   pl.BlockSpec((tm, MLA_HEADS * 2 * MLA_V), row)],
        out_shape=[jax.ShapeDtypeStruct((t, MLA_HEADS * QK_PAD), BF16),
                   jax.ShapeDtypeStruct((t, MLA_HEADS * QK_PAD), BF16),
                   jax.ShapeDtypeStruct((t, MLA_HEADS * 2 * MLA_V), BF16)],
        compiler_params=_params("arbitrary"),
        name="mla_prep",
    )(proj, proj, proj, cos_t, s1_t, s2_t, gql, gkvl, wuq, wukv, gqn, gqr, gkn, gkr)


def _mla_attn_kernel(*refs, tq, heads, ada_tiles):
    if ada_tiles:
        (q_ref, k_ref, v_ref, c_ref, wada_ref, bada_ref, o_ref, mod_ref,
         m_ref, l_ref, acc_ref, sa_ref, sb_ref) = refs
        step = ((pl.program_id(0) * pl.num_programs(1) + pl.program_id(1)) * pl.num_programs(2)
                + pl.program_id(2))

        @pl.when(step < ada_tiles)
        def _():
            _ada_kernel(c_ref, wada_ref, bada_ref, mod_ref)
    else:
        q_ref, k_ref, v_ref, o_ref, m_ref, l_ref, acc_ref, sa_ref, sb_ref = refs
    i = pl.program_id(2)
    vw = 2 * MLA_V

    def scores(g, kb):
        q = q_ref[:, g * QK_PAD:(g + 1) * QK_PAD]
        k = k_ref[pl.ds(pl.multiple_of(kb * tq, tq), tq), g * QK_PAD:(g + 1) * QK_PAD]
        return lax.dot_general(q, k, (((1,), (1,)), ((), ())), preferred_element_type=F32)

    def probs(s, m):
        return jnp.concatenate(
            [jnp.exp2(s[:, c * LANE:(c + 1) * LANE] - m) for c in range(tq // LANE)],
            axis=1).astype(BF16)

    def pv(g, p, kb):
        v = v_ref[pl.ds(pl.multiple_of(kb * tq, tq), tq), g * vw:(g + 1) * vw]
        return jnp.dot(p, v, preferred_element_type=F32)

    q_chunk = lax.broadcasted_iota(jnp.int32, (tq, 1), 0) // CHUNK
    k_chunk = lax.broadcasted_iota(jnp.int32, (1, tq), 1) // CHUNK
    all_chunks = jnp.int32(tq // CHUNK)

    def update(g, s, kb):
        s = jnp.where(k_chunk <= jnp.where(kb == i, q_chunk, all_chunks), s, -jnp.inf)
        m_old = m_ref[g]
        m_new = jnp.maximum(m_old, jnp.max(s, axis=-1, keepdims=True))
        alpha = jnp.exp2(m_old - m_new)
        r = pv(g, probs(s, m_new), kb)
        acc_ref[g] = alpha * acc_ref[g] + r[:, :MLA_V]
        l_ref[g] = alpha * l_ref[g] + r[:, MLA_V:]
        m_ref[g] = m_new

    for g in range(heads):
        m_ref[g] = jnp.full((tq, LANE), -jnp.inf, F32)
        l_ref[g] = jnp.zeros((tq, MLA_V), F32)
        acc_ref[g] = jnp.zeros((tq, MLA_V), F32)
        sa_ref[g] = scores(g, 0)

    def pair(t, carry):
        j = 2 * t
        for g in range(heads):
            sb_ref[g] = scores(g, jnp.minimum(j + 1, i))
            update(g, sa_ref[g], j)
        for g in range(heads):
            sa_ref[g] = scores(g, jnp.minimum(j + 2, i))
            update(g, sb_ref[g], j + 1)
        return carry

    lax.fori_loop(0, (i + 1) // 2, pair, 0)

    @pl.when(i % 2 == 0)
    def _():
        for g in range(heads):
            update(g, sa_ref[g], i)

    for g in range(heads):
        o_ref[:, g * MLA_V:(g + 1) * MLA_V] = acc_ref[g] / l_ref[g]


def _mla_attention(q2d, k2d, v2d, bsz, seq, ada=None, tq=512, heads=2):
    t = q2d.shape[0]
    tq = min(tq, seq)
    nq = seq // tq
    vw = 2 * MLA_V
    hg = MLA_HEADS // heads
    in_specs = [pl.BlockSpec((tq, heads * QK_PAD), lambda b, h, i: (b * nq + i, h)),
                pl.BlockSpec((seq, heads * QK_PAD), lambda b, h, i: (b, h)),
                pl.BlockSpec((seq, heads * vw), lambda b, h, i: (b, h))]
    out_specs = [pl.BlockSpec((tq, heads * MLA_V), lambda b, h, i: (b * nq + i, h))]
    out_shape = [jax.ShapeDtypeStruct((t, MLA_HEADS * MLA_V), F32)]
    operands = [q2d, k2d, v2d]
    ada_tiles = 0
    if ada is not None:
        c_col, w_ada, b_ada3, layer = ada
        _, d, n = w_ada.shape
        steps = bsz * hg * nq
        tn = LANE * -(-n // (LANE * steps))
        assert n % tn == 0
        ada_tiles = n // tn
        tile = lambda b, h, i: jnp.minimum((b * hg + h) * nq + i, ada_tiles - 1)
        in_specs += [pl.BlockSpec(c_col.shape, lambda b, h, i: (0, 0, 0)),
                     pl.BlockSpec((None, d, tn), lambda b, h, i: (layer, 0, tile(b, h, i))),
                     pl.BlockSpec((None, 1, tn), lambda b, h, i: (layer, 0, tile(b, h, i)))]
        out_specs += [pl.BlockSpec((c_col.shape[0], tn), lambda b, h, i: (0, tile(b, h, i)))]
        out_shape += [jax.ShapeDtypeStruct((c_col.shape[0], n), F32)]
        operands += [c_col, w_ada, b_ada3]
    outs = pl.pallas_call(
        functools.partial(_mla_attn_kernel, tq=tq, heads=heads, ada_tiles=ada_tiles),
        grid=(bsz, hg, nq),
        in_specs=in_specs,
        out_specs=out_specs,
        out_shape=out_shape,
        scratch_shapes=[pltpu.VMEM((heads, tq, LANE), F32), pltpu.VMEM((heads, tq, MLA_V), F32),
                        pltpu.VMEM((heads, tq, MLA_V), F32),
                        pltpu.VMEM((heads, tq, tq), F32), pltpu.VMEM((heads, tq, tq), F32)],
        compiler_params=_params("arbitrary", "arbitrary", "arbitrary"),
        name="mla_attention",
    )(*operands)
    return outs if ada is not None else (outs[0], None)


def _sb_attn_kernel(q_ref, k_ref, v_ref, tri_ref, o_ref, run_ref, acc_ref, *, tq, tk, heads):
    i = pl.program_id(2)
    tri = tri_ref[...]
    sub = tq // tk
    streams = [(g, r) for g in range(heads) for r in range(sub)]

    def block(s, kb, mask):
        g, r = streams[s]
        cols = slice(g * SB_DIM, (g + 1) * SB_DIM)
        keys = pl.ds(pl.multiple_of(kb * tk, tk), tk)
        q = q_ref[r * tk:(r + 1) * tk, cols]
        z = lax.dot_general(q, k_ref[keys, cols], (((1,), (1,)), ((), ())),
                            preferred_element_type=F32)
        neg_abs = lax.bitcast_convert_type(
            lax.bitcast_convert_type(z, jnp.uint32) | jnp.uint32(0x80000000), F32)
        cost = jnp.maximum(z, 0.0) + jnp.log2(1.0 + jnp.exp2(neg_abs))
        if mask is not None:
            cost = jnp.where(mask, cost, 0.0)
        after = jnp.dot(cost.astype(BF16), tri, preferred_element_type=F32) + run_ref[s]
        a = jnp.exp2(z - cost - after)
        if mask is not None:
            a = jnp.where(mask, a, 0.0)
        acc_ref[s] += jnp.dot(a.astype(BF16), v_ref[keys, cols], preferred_element_type=F32)
        run_ref[s] += jnp.sum(cost, axis=-1, keepdims=True)

    def more():
        return jnp.min(run_ref[...]) < SB_UNDERFLOW_LOG2

    run_ref[...] = jnp.zeros_like(run_ref)
    acc_ref[...] = jnp.zeros_like(acc_ref)
    strictly_before = (lax.broadcasted_iota(jnp.int32, (tk, tk), 1)
                       < lax.broadcasted_iota(jnp.int32, (tk, tk), 0))
    for s, (g, r) in enumerate(streams):
        block(s, i * sub + r, strictly_before)

    def cond(c):
        return (c[0] < i * sub + sub - 1) & c[1]

    def body(c):
        for s, (g, r) in enumerate(streams):
            kb = i * sub + r - 1 - c[0]
            run_ref[s] = jnp.where(kb >= 0, run_ref[s], SB_FINISHED_LOG2)
            block(s, jnp.maximum(kb, 0), None)
        return c[0] + 1, more()

    lax.while_loop(cond, body, (jnp.int32(0), more()))
    for s, (g, r) in enumerate(streams):
        o_ref[r * tk:(r + 1) * tk, g * SB_DIM:(g + 1) * SB_DIM] = acc_ref[s]


def _sb_attention(proj, bsz, seq, tq=512, tk=256, heads=4):
    t = proj.shape[0]
    tq = min(tq, seq)
    nq = seq // tq
    tri = (lax.broadcasted_iota(jnp.int32, (tk, tk), 0)
           > lax.broadcasted_iota(jnp.int32, (tk, tk), 1)).astype(BF16)
    w = heads * SB_DIM
    cq, ck, cv = COL_SQ // w, COL_SK // w, COL_SV // w
    n_streams = heads * (tq // tk)
    return pl.pallas_call(
        functools.partial(_sb_attn_kernel, tq=tq, tk=tk, heads=heads),
        grid=(bsz, SB_HEADS // heads, nq),
        in_specs=[pl.BlockSpec((tq, w), lambda b, h, i: (b * nq + i, cq + h)),
                  pl.BlockSpec((seq, w), lambda b, h, i: (b, ck + h)),
                  pl.BlockSpec((seq, w), lambda b, h, i: (b, cv + h)),
                  pl.BlockSpec((tk, tk), lambda b, h, i: (0, 0))],
        out_specs=pl.BlockSpec((tq, w), lambda b, h, i: (b * nq + i, h)),
        out_shape=jax.ShapeDtypeStruct((t, SB_WIDTH), F32),
        scratch_shapes=[pltpu.VMEM((n_streams, tk, 1), F32),
                        pltpu.VMEM((n_streams, tk, SB_DIM), F32)],
        compiler_params=_params("arbitrary", "arbitrary", "arbitrary"),
        name="sb_attention",
    )(proj, proj, proj, tri)


def _out_proj_kernel(om_ref, os_ref, gm_ref, gs_ref, wo_ref, x_ref, gate_ref, g2_ref,
                     sh_ref, sc_ref, rw_ref, xo_ref, h2_ref, lg_ref):
    wm = om_ref.shape[1]
    nm = (_rms(om_ref[...]) * gm_ref[...]).astype(BF16)
    ns = (_rms(os_ref[...]) * gs_ref[...]).astype(BF16)
    y = (jnp.dot(nm, wo_ref[:wm, :], preferred_element_type=F32)
         + jnp.dot(ns, wo_ref[wm:, :], preferred_element_type=F32))
    xn = x_ref[...] + gate_ref[...] * y
    xo_ref[...] = xn
    h2 = _rms(xn) * g2_ref[...]
    h2 = h2 * (1.0 + sc_ref[...]) + sh_ref[...]
    h2_ref[...] = _pack_rows(h2)
    lg_ref[...] = lax.dot_general(rw_ref[...], h2.astype(BF16), (((1,), (1,)), ((), ())),
                                  preferred_element_type=F32)


def _out_proj(o_mla, o_sb, gm, gs, w_o, x, mod, g2, rw_t, layer, seq, tm=256):
    t, d = x.shape
    per_b = seq // tm
    row = lambda i: (i, 0)
    full2 = lambda i: (0, 0)
    modspec = lambda j: pl.BlockSpec((None, 1, d), lambda i: ((i // per_b) * 6 + j, 0, 0))
    return pl.pallas_call(
        _out_proj_kernel,
        grid=(t // tm,),
        in_specs=[pl.BlockSpec((tm, o_mla.shape[1]), row),
                  pl.BlockSpec((tm, o_sb.shape[1]), row),
                  pl.BlockSpec((1, o_mla.shape[1]), full2),
                  pl.BlockSpec((1, o_sb.shape[1]), full2),
                  pl.BlockSpec((None,) + w_o.shape[1:], lambda i: (layer, 0, 0)),
                  pl.BlockSpec((tm, d), row),
                  modspec(2),
                  pl.BlockSpec((1, d), full2),
                  modspec(3),
                  modspec(4),
                  pl.BlockSpec(rw_t.shape, full2)],
        out_specs=[pl.BlockSpec((tm, d), row),
                   pl.BlockSpec((tm, d // 2), row),
                   pl.BlockSpec((N_EXPERTS, tm), lambda i: (0, i))],
        out_shape=[jax.ShapeDtypeStruct((t, d), F32),
                   jax.ShapeDtypeStruct((t, d // 2), jnp.uint32),
                   jax.ShapeDtypeStruct((N_EXPERTS, t), F32)],
        compiler_params=_params("arbitrary"),
        name="out_proj",
    )(o_mla, o_sb, gm, gs, w_o, x, mod, g2, mod, mod, rw_t)


def _route_kernel(lg_ref, rb_ref, tri_ref, e_ref, w_ref, rank_ref, cnt_ref, run_ref):
    i = pl.program_id(0)

    @pl.when(i == 0)
    def _():
        run_ref[...] = jnp.zeros_like(run_ref)

    scores = jax.nn.sigmoid(lg_ref[...])
    sel = scores + rb_ref[...]
    tm = sel.shape[1]
    epg = EXPERTS_PER_GROUP
    srow = [sel[e:e + 1, :] for e in range(N_EXPERTS)]
    prow = [scores[e:e + 1, :] for e in range(N_EXPERTS)]

    gscore = []
    for g in range(N_GROUPS):
        a, b, c, d = srow[g * epg:(g + 1) * epg]
        hi1, lo1 = jnp.maximum(a, b), jnp.minimum(a, b)
        hi2, lo2 = jnp.maximum(c, d), jnp.minimum(c, d)
        gscore.append(jnp.maximum(hi1, hi2)
                      + jnp.maximum(jnp.minimum(hi1, hi2), jnp.maximum(lo1, lo2)))
    best = gscore[0]
    gidx = jnp.zeros((1, tm), jnp.int32)
    for g in range(1, N_GROUPS):
        better = gscore[g] > best
        gidx = jnp.where(better, g, gidx)
        best = jnp.where(better, gscore[g], best)

    def pick(rows, j):
        out = rows[j]
        for g in range(1, N_GROUPS):
            out = jnp.where(gidx == g, rows[g * epg + j], out)
        return out

    v = [pick(srow, j) for j in range(epg)]
    p = [pick(prow, j) for j in range(epg)]
    pos = []
    for j in range(epg):
        r = jnp.zeros((1, tm), jnp.int32)
        for o in range(epg):
            if o == j:
                continue
            ahead = (v[o] > v[j]) | ((v[o] == v[j]) & (o < j))
            r = r + ahead.astype(jnp.int32)
        pos.append(r)
    local, wsel = [], []
    for k in range(TOP_K):
        lk = jnp.zeros((1, tm), jnp.int32)
        wk = jnp.zeros((1, tm), F32)
        for j in range(epg):
            hit = pos[j] == k
            lk = jnp.where(hit, j, lk)
            wk = jnp.where(hit, p[j], wk)
        local.append(lk)
        wsel.append(wk)
    wsum = wsel[0] + wsel[1]
    idx = [gidx * epg + local[k] for k in range(TOP_K)]

    eiota = lax.broadcasted_iota(jnp.int32, (N_EXPERTS, tm), 0)
    hit = [eiota == idx[k] for k in range(TOP_K)]
    onehot = (hit[0] | hit[1]).astype(F32)
    before = jnp.dot(onehot.astype(BF16), tri_ref[...], preferred_element_type=F32) + run_ref[...]
    for k in range(TOP_K):
        e_ref[pl.ds(k, 1), :] = idx[k]
        w_ref[pl.ds(k, 1), :] = wsel[k] / wsum
        rank_ref[pl.ds(k, 1), :] = jnp.sum(jnp.where(hit[k], before, 0.0), axis=0,
                                           keepdims=True).astype(jnp.int32)
    run_ref[...] += jnp.sum(onehot, axis=1, keepdims=True)
    cnt_ref[...] = jnp.broadcast_to(run_ref[...], cnt_ref.shape)


def _route(logits_t, router_b, tm=512):
    t = logits_t.shape[1]
    tm = min(tm, t)
    tri = (lax.broadcasted_iota(jnp.int32, (tm, tm), 0)
           < lax.broadcasted_iota(jnp.int32, (tm, tm), 1)).astype(BF16)
    blk = lambda rows: pl.BlockSpec((rows, tm), lambda i: (0, i))
    return pl.pallas_call(
        _route_kernel,
        grid=(t // tm,),
        in_specs=[blk(N_EXPERTS),
                  pl.BlockSpec((N_EXPERTS, 1), lambda i: (0, 0)),
                  pl.BlockSpec((tm, tm), lambda i: (0, 0))],
        out_specs=[blk(TOP_K), blk(TOP_K), blk(TOP_K),
                   pl.BlockSpec((N_EXPERTS, LANE), lambda i: (0, 0))],
        out_shape=[jax.ShapeDtypeStruct((TOP_K, t), jnp.int32),
                   jax.ShapeDtypeStruct((TOP_K, t), F32),
                   jax.ShapeDtypeStruct((TOP_K, t), jnp.int32),
                   jax.ShapeDtypeStruct((N_EXPERTS, LANE), F32)],
        scratch_shapes=[pltpu.VMEM((N_EXPERTS, 1), F32)],
        compiler_params=_params("arbitrary"),
        name="route",
    )(logits_t, router_b[:, None], tri)


def _dispatch_kernel(dest_ref, pend_ref, h_ref, o_ref, zero_ref, sem, zsem, *, tm, t):
    base = pl.program_id(0) * tm

    @pl.when(pl.program_id(0) == 0)
    def _():
        zero_ref[...] = jnp.zeros_like(zero_ref)

        def fill(e):
            end = pend_ref[e]
            start = pend_ref[e - 1] if e > 0 else 0
            dst = o_ref.at[pl.ds(pl.multiple_of(jnp.maximum(end - MOE_ROWS, 0), MOE_ROWS),
                                 MOE_ROWS), :]
            return end > start, pltpu.make_async_copy(zero_ref, dst, zsem)

        for e in range(N_EXPERTS):
            nonempty, cp = fill(e)
            pl.when(nonempty)(cp.start)
        for e in range(N_EXPERTS):
            nonempty, cp = fill(e)
            pl.when(nonempty)(cp.wait)

        def tail(r):
            dst = o_ref.at[pl.ds(pl.multiple_of(r * MOE_ROWS, MOE_ROWS), MOE_ROWS), :]
            return pltpu.make_async_copy(zero_ref, dst, zsem)

        first_free = pend_ref[N_EXPERTS - 1] // MOE_ROWS
        n_blocks = o_ref.shape[0] // MOE_ROWS
        lax.fori_loop(first_free, n_blocks, lambda r, c: (tail(r).start(), c)[1], 0)
        lax.fori_loop(first_free, n_blocks, lambda r, c: (tail(r).wait(), c)[1], 0)

    def issue(r, carry):
        for k in range(TOP_K):
            d = dest_ref[k * t + base + r]
            pltpu.make_async_copy(h_ref.at[pl.ds(r, 1), :], o_ref.at[pl.ds(d, 1), :], sem).start()
        return carry

    lax.fori_loop(0, tm, issue, 0, unroll=DMA_ISSUE_UNROLL)
    for k in range(TOP_K):
        pltpu.make_async_copy(h_ref, o_ref.at[pl.ds(0, tm), :], sem).wait()


def _dispatch(h2, dest_flat, pad_end, rows, tm=256):
    t, d = h2.shape
    return pl.pallas_call(
        functools.partial(_dispatch_kernel, tm=tm, t=t),
        grid_spec=pltpu.PrefetchScalarGridSpec(
            num_scalar_prefetch=2,
            grid=(t // tm,),
            in_specs=[pl.BlockSpec((tm, d), lambda i, dest, pend: (i, 0))],
            out_specs=pl.BlockSpec(memory_space=pl.ANY),
            scratch_shapes=[pltpu.VMEM((MOE_ROWS, d), h2.dtype),
                            pltpu.SemaphoreType.DMA(()), pltpu.SemaphoreType.DMA(())]),
        out_shape=jax.ShapeDtypeStruct((rows, d), h2.dtype),
        compiler_params=_params("arbitrary"),
        name="moe_dispatch",
    )(dest_flat, pad_end, h2)


def _expert_kernel(be_ref, nu_ref, x_ref, wg_ref, wu_ref, wd_ref, y_ref, wgb, wub, wdb):
    r = pl.program_id(0)
    live = r < nu_ref[0]
    new_expert = (r == 0) | (be_ref[jnp.maximum(r - 1, 0)] != be_ref[r])

    @pl.when(live & new_expert)
    def _():
        wgb[...] = wg_ref[...].astype(BF16)
        wub[...] = wu_ref[...].astype(BF16)
        wdb[...] = wd_ref[...].astype(BF16)

    @pl.when(live)
    def _():
        lo, hi = _unpack_rows(x_ref[...])
        xb = jnp.concatenate([lo.astype(BF16), hi.astype(BF16)], axis=1)
        g = jnp.dot(xb, wgb[...], preferred_element_type=F32)
        u = jnp.dot(xb, wub[...], preferred_element_type=F32)
        a = (g * jax.nn.sigmoid(g)) * u
        y_ref[...] = _pack_rows(jnp.dot(a.astype(BF16), wdb[...], preferred_element_type=F32))

    @pl.when(jnp.logical_not(live))
    def _():
        y_ref[...] = jnp.zeros_like(y_ref)


def _experts(x_sorted, block_e, n_used, w_gate, w_up, w_down, layer):
    rows, d = x_sorted.shape
    de = w_gate.shape[-1]
    dm = w_gate.shape[-2]
    bm = MOE_ROWS
    blk = lambda r, be, nu: (jnp.minimum(r, nu[0] - 1), 0)
    return pl.pallas_call(
        _expert_kernel,
        grid_spec=pltpu.PrefetchScalarGridSpec(
            num_scalar_prefetch=2,
            grid=(rows // bm,),
            in_specs=[pl.BlockSpec((bm, d), blk),
                      pl.BlockSpec((None, None, dm, de), lambda r, be, nu: (layer, be[r], 0, 0)),
                      pl.BlockSpec((None, None, dm, de), lambda r, be, nu: (layer, be[r], 0, 0)),
                      pl.BlockSpec((None, None, de, dm), lambda r, be, nu: (layer, be[r], 0, 0))],
            out_specs=pl.BlockSpec((bm, d), lambda r, be, nu: (r, 0)),
            scratch_shapes=[pltpu.VMEM((dm, de), BF16), pltpu.VMEM((dm, de), BF16),
                            pltpu.VMEM((de, dm), BF16)]),
        out_shape=jax.ShapeDtypeStruct((rows, d), jnp.uint32),
        compiler_params=_params("arbitrary"),
        name="moe_experts",
    )(block_e, n_used, x_sorted, w_gate, w_up, w_down)


def _combine_kernel(dest_ref, y_ref, x_ref, gate_ref, w_ref, o_ref, buf, sems, *, tm, t):
    i = pl.program_id(0)
    slot = i % 2

    def gather(tile, dst_slot):
        base = tile * tm

        def issue(r, carry):
            for k in range(TOP_K):
                d = dest_ref[k * t + base + r]
                pltpu.make_async_copy(y_ref.at[pl.ds(d, 1), :],
                                      buf.at[dst_slot, k, pl.ds(r, 1), :],
                                      sems.at[dst_slot]).start()
            return carry

        lax.fori_loop(0, tm, issue, 0, unroll=DMA_ISSUE_UNROLL)

    @pl.when(i == 0)
    def _():
        gather(0, 0)

    @pl.when(i + 1 < pl.num_programs(0))
    def _():
        gather(i + 1, 1 - slot)

    for k in range(TOP_K):
        pltpu.make_async_copy(y_ref.at[pl.ds(0, tm), :], buf.at[slot, k], sems.at[slot]).wait()
    w = w_ref[...]
    half = buf.shape[3]
    lo0, hi0 = _unpack_rows(buf[slot, 0])
    lo1, hi1 = _unpack_rows(buf[slot, 1])
    o_ref[:, :half] = x_ref[:, :half] + gate_ref[:, :half] * (w[:, 0:1] * lo0 + w[:, 1:2] * lo1)
    o_ref[:, half:] = x_ref[:, half:] + gate_ref[:, half:] * (w[:, 0:1] * hi0 + w[:, 1:2] * hi1)


def _combine(y_sorted, dest_flat, x, mod, w_tok, seq, tm=256):
    t, d = x.shape
    per_b = seq // tm
    return pl.pallas_call(
        functools.partial(_combine_kernel, tm=tm, t=t),
        grid_spec=pltpu.PrefetchScalarGridSpec(
            num_scalar_prefetch=1,
            grid=(t // tm,),
            in_specs=[pl.BlockSpec(memory_space=pl.ANY),
                      pl.BlockSpec((tm, d), lambda i, dest: (i, 0)),
                      pl.BlockSpec((None, 1, d), lambda i, dest: ((i // per_b) * 6 + 5, 0, 0)),
                      pl.BlockSpec((tm, TOP_K), lambda i, dest: (i, 0))],
            out_specs=pl.BlockSpec((tm, d), lambda i, dest: (i, 0)),
            scratch_shapes=[pltpu.VMEM((2, TOP_K, tm, y_sorted.shape[1]), y_sorted.dtype),
                            pltpu.SemaphoreType.DMA((2,))]),
        out_shape=jax.ShapeDtypeStruct((t, d), F32),
        compiler_params=_params("arbitrary"),
        name="moe_combine",
    )(dest_flat, y_sorted, x, mod, w_tok)


def _rope_tables(seq):
    half = MLA_ROPE // 2
    inv = 1.0 / (ROPE_BASE ** (jnp.arange(0, MLA_ROPE, 2, dtype=F32) / MLA_ROPE))
    ang = jnp.arange(seq, dtype=F32)[:, None] * inv[None, :]
    cos, sin = jnp.cos(ang), jnp.sin(ang)
    z = jnp.zeros((seq, half), F32)
    zz = jnp.zeros((seq, LANE - MLA_ROPE), F32)
    cos_t = jnp.concatenate([cos, cos, zz], axis=1)
    s1_t = jnp.concatenate([z, sin, zz], axis=1)
    s2_t = jnp.concatenate([-sin, z, zz], axis=1)
    return cos_t, s1_t, s2_t


def _pad_last(a, width):
    return jnp.pad(a, [(0, 0)] * (a.ndim - 1) + [(0, width - a.shape[-1])])


def kernel(x, c, norm1_g, norm2_g, w_ada, b_ada, w_in, q_lora_g, kv_lora_g, w_uq, w_ukv,
           q_head_g, k_head_g, mla_out_g, sb_out_g, w_o, router_w, router_b,
           w_gate, w_up, w_down):
    bsz, seq, d = x.shape
    t = bsz * seq
    depth = w_ada.shape[0]

    sb_col_scale = jnp.where(jnp.arange(SB_IN_COLS) < SB_WIDTH, SB_DIM ** -0.5 * LOG2_E, 1.0)
    w_in_mla, w_in_sb = _split_in_weights(w_in, sb_col_scale.astype(F32)[None, :])
    wq = w_uq.reshape(depth, Q_LORA, MLA_HEADS, MLA_QK)
    wuq_p = jnp.concatenate(
        [wq[..., :MLA_NOPE].reshape(depth, Q_LORA, -1),
         _pad_last(wq[..., MLA_NOPE:], LANE).reshape(depth, Q_LORA, -1)], axis=-1).astype(BF16)
    wkv = w_ukv.reshape(depth, KV_LORA, MLA_HEADS, MLA_NOPE + MLA_V)
    wukv_p = jnp.concatenate(
        [wkv[..., :MLA_NOPE].reshape(depth, KV_LORA, -1),
         wkv[..., MLA_NOPE:].reshape(depth, KV_LORA, -1)], axis=-1).astype(BF16)
    w_o_b = w_o.astype(BF16)
    q_scale = MLA_QK ** -0.5 * LOG2_E
    gqn = (q_head_g[:, None, :MLA_NOPE] * q_scale)
    gqr = _pad_last(q_head_g[:, None, MLA_NOPE:] * q_scale, LANE)
    gkn = k_head_g[:, None, :MLA_NOPE]
    gkr = _pad_last(k_head_g[:, None, MLA_NOPE:], LANE)
    rope_tabs = _rope_tables(seq)
    rw_t = router_w.T.astype(BF16)

    c_col, b_ada3 = c[:, :, None], b_ada[:, None, :]
    mod_l = _ada_modulation(c_col, w_ada, b_ada3, layers=1)[0]
    n_blocks = -(-(t * TOP_K) // MOE_ROWS) + N_EXPERTS
    rows = n_blocks * MOE_ROWS

    xt = x.reshape(t, d)
    for l in range(depth):
        mod = mod_l.reshape(bsz * 6, 1, d)
        proj_mla, proj_sb = _in_proj(xt, norm1_g[l][None, :], mod, w_in_mla, w_in_sb, l, seq)
        q2d, k2d, v2d = _mla_prep(proj_mla, rope_tabs, q_lora_g[:, None, :],
                                  kv_lora_g[:, None, :], wuq_p, wukv_p, gqn, gqr, gkn, gkr, l, seq)
        next_ada = (c_col, w_ada, b_ada3, l + 1) if l + 1 < depth else None
        o_mla, mod_l = _mla_attention(q2d, k2d, v2d, bsz, seq, ada=next_ada)
        o_sb = _sb_attention(proj_sb, bsz, seq)
        xt, h2, logits_t = _out_proj(o_mla, o_sb, mla_out_g[l][None, :], sb_out_g[l][None, :],
                                     w_o_b, xt, mod, norm2_g[l][None, :], rw_t, l, seq)
        e_idx, w_tok, rank, cnt = _route(logits_t, router_b)
        counts = cnt[:, 0].astype(jnp.int32)
        padded = (counts + MOE_ROWS - 1) // MOE_ROWS * MOE_ROWS
        pad_end = jnp.cumsum(padded)
        pad_start = pad_end - padded
        eids = jnp.arange(N_EXPERTS, dtype=jnp.int32)
        seg_start = jnp.sum(jnp.where(e_idx[None] == eids[:, None, None],
                                      pad_start[:, None, None], 0), axis=0)
        dest = (seg_start + rank).reshape(TOP_K * t).astype(jnp.int32)
        n_used = (pad_end[-1:] // MOE_ROWS).astype(jnp.int32)
        first_row = jnp.minimum(jnp.arange(n_blocks, dtype=jnp.int32), n_used - 1) * MOE_ROWS
        block_e = jnp.sum(pad_end[None, :] <= first_row[:, None], axis=1).astype(jnp.int32)
        x_sorted = _dispatch(h2, dest, pad_end.astype(jnp.int32), rows)
        y_sorted = _experts(x_sorted, block_e, n_used, w_gate, w_up, w_down, l)
        xt = _combine(y_sorted, dest, xt, mod, w_tok.T, seq)
    return xt.reshape(bsz, seq, d)
```

```python
import functools

import jax
import jax.numpy as jnp
from jax import lax
from jax.experimental import pallas as pl
from jax.experimental.pallas import tpu as pltpu

F32 = jnp.float32
BF16 = jnp.bfloat16

EPS = 1e-6
ROPE_BASE = 10000.0
CHUNK = 64

MLA_HEADS = 8
MLA_NOPE = 128
MLA_ROPE = 64
MLA_QK = MLA_NOPE + MLA_ROPE
MLA_V = 128
Q_LORA = 512
KV_LORA = 256
SB_HEADS = 8
SB_DIM = 128
SB_WIDTH = SB_HEADS * SB_DIM
LANE = 128
QK_PAD = 2 * LANE
COL_CQ = 0
COL_CKV = Q_LORA
COL_KR = Q_LORA + KV_LORA
MLA_IN_COLS = COL_KR + LANE
COL_SQ = 0
COL_SK = SB_WIDTH
COL_SV = 2 * SB_WIDTH
SB_IN_COLS = 3 * SB_WIDTH

N_EXPERTS = 16
N_GROUPS = 4
EXPERTS_PER_GROUP = N_EXPERTS // N_GROUPS
TOP_K = 2
D_EXPERT = 512
MOE_ROWS = 256
DMA_ISSUE_UNROLL = 8
LOG2_E = 1.4426950408889634
SB_UNDERFLOW_LOG2 = 150.0
SB_FINISHED_LOG2 = 1e30

VMEM_LIMIT = 56 * 1024 * 1024


def _params(*dims):
    return pltpu.CompilerParams(dimension_semantics=dims, vmem_limit_bytes=VMEM_LIMIT)


def _rms(x):
    return x * lax.rsqrt(jnp.mean(x * x, axis=-1, keepdims=True) + EPS)


def _pack_rows(x):
    n = x.shape[1] // 2
    lo = lax.bitcast_convert_type(x[:, :n].astype(BF16).astype(F32), jnp.uint32)
    hi = lax.bitcast_convert_type(x[:, n:].astype(BF16).astype(F32), jnp.uint32)
    return (lo >> 16) | (hi & jnp.uint32(0xFFFF0000))


def _unpack_rows(w):
    lo = lax.bitcast_convert_type(w << 16, F32)
    hi = lax.bitcast_convert_type(w & jnp.uint32(0xFFFF0000), F32)
    return lo, hi


def _ada_kernel(c_ref, w_ref, b_ref, o_ref):
    tn = w_ref.shape[1]
    for b in range(c_ref.shape[0]):
        cb = c_ref[b]
        cb = cb * jax.nn.sigmoid(cb)
        cols = [jnp.sum(cb * w_ref[:, j * LANE:(j + 1) * LANE], axis=0, keepdims=True)
                for j in range(tn // LANE)]
        o_ref[pl.ds(b, 1), :] = jnp.concatenate(cols, axis=1) + b_ref[...]


def _ada_modulation(c_col, w_ada, b_ada3, layers, tn=1024):
    _, d, n = w_ada.shape
    bsz = c_col.shape[0]
    return pl.pallas_call(
        _ada_kernel,
        grid=(layers, n // tn),
        in_specs=[pl.BlockSpec(c_col.shape, lambda l, j: (0, 0, 0)),
                  pl.BlockSpec((None, d, tn), lambda l, j: (l, 0, j)),
                  pl.BlockSpec((None, 1, tn), lambda l, j: (l, 0, j))],
        out_specs=pl.BlockSpec((None, bsz, tn), lambda l, j: (l, 0, j)),
        out_shape=jax.ShapeDtypeStruct((layers, bsz, n), F32),
        compiler_params=_params("arbitrary", "arbitrary"),
        name="ada_modulation",
    )(c_col, w_ada, b_ada3)


def _in_proj_kernel(x_ref, g_ref, sh_ref, sc_ref, wa_ref, wb_ref, oa_ref, ob_ref):
    h = _rms(x_ref[...]) * g_ref[...]
    h = (h * (1.0 + sc_ref[...]) + sh_ref[...]).astype(BF16)
    oa_ref[...] = jnp.dot(h, wa_ref[...], preferred_element_type=F32).astype(oa_ref.dtype)
    ob_ref[...] = jnp.dot(h, wb_ref[...], preferred_element_type=F32).astype(ob_ref.dtype)


def _in_proj(x, g, mod, w_mla, w_sb, layer, seq, tm=256):
    t, d = x.shape
    na, nb = w_mla.shape[-1], w_sb.shape[-1]
    per_b = seq // tm
    return pl.pallas_call(
        _in_proj_kernel,
        grid=(t // tm,),
        in_specs=[pl.BlockSpec((tm, d), lambda i: (i, 0)),
                  pl.BlockSpec((1, d), lambda i: (0, 0)),
                  pl.BlockSpec((None, 1, d), lambda i: ((i // per_b) * 6 + 0, 0, 0)),
                  pl.BlockSpec((None, 1, d), lambda i: ((i // per_b) * 6 + 1, 0, 0)),
                  pl.BlockSpec((None, d, na), lambda i: (layer, 0, 0)),
                  pl.BlockSpec((None, d, nb), lambda i: (layer, 0, 0))],
        out_specs=[pl.BlockSpec((tm, na), lambda i: (i, 0)),
                   pl.BlockSpec((tm, nb), lambda i: (i, 0))],
        out_shape=[jax.ShapeDtypeStruct((t, na), BF16),
                   jax.ShapeDtypeStruct((t, nb), BF16)],
        compiler_params=_params("arbitrary"),
        name="in_proj",
    )(x, g, mod, mod, w_mla, w_sb)


def _mla_prep_kernel(cq_ref, ckv_ref, kr_ref, cos_ref, s1_ref, s2_ref, gql_ref, gkvl_ref,
                     wuq_ref, wukv_ref, gqn_ref, gqr_ref, gkn_ref, gkr_ref,
                     q_ref, k_ref, v_ref):
    nope_w = MLA_HEADS * MLA_NOPE
    cqn = _rms(cq_ref[...].astype(F32)) * gql_ref[...]
    q = jnp.dot(cqn.astype(BF16), wuq_ref[...], preferred_element_type=F32)
    ckvn = _rms(ckv_ref[...].astype(F32)) * gkvl_ref[...]
    kv = jnp.dot(ckvn.astype(BF16), wukv_ref[...], preferred_element_type=F32)
    cosv, s1, s2 = cos_ref[...], s1_ref[...], s2_ref[...]

    def rope(x):
        return (x * cosv + pltpu.roll(x, MLA_ROPE // 2, 1) * s1
                + pltpu.roll(x, LANE - MLA_ROPE // 2, 1) * s2)

    kr = kr_ref[...].astype(F32)
    kr_ss = jnp.sum(kr * kr, axis=-1, keepdims=True)
    kr_roped = rope(kr * gkr_ref[...])
    gqn, gkn = gqn_ref[...], gkn_ref[...]
    gqr, gqs = gqr_ref[:, :LANE], gqr_ref[:, LANE:]
    s_comb = s1 + s2
    for h in range(MLA_HEADS):
        qn = q[:, h * LANE:(h + 1) * LANE]
        qr = q[:, nope_w + h * LANE:nope_w + (h + 1) * LANE]
        ss = jnp.sum(qn * qn + qr * qr, axis=-1, keepdims=True)
        r = lax.rsqrt(ss * (1.0 / MLA_QK) + EPS)
        q_ref[:, h * QK_PAD:h * QK_PAD + LANE] = (qn * r * gqn).astype(BF16)
        qs = q[:, 2 * nope_w + h * LANE:2 * nope_w + (h + 1) * LANE]
        q_rope = (qr * gqr) * cosv + (qs * gqs) * s_comb
        q_ref[:, h * QK_PAD + LANE:(h + 1) * QK_PAD] = (q_rope * r).astype(BF16)
        kn = kv[:, h * LANE:(h + 1) * LANE]
        rk = lax.rsqrt((jnp.sum(kn * kn, axis=-1, keepdims=True) + kr_ss) * (1.0 / MLA_QK) + EPS)
        k_ref[:, h * QK_PAD:h * QK_PAD + LANE] = (kn * rk * gkn).astype(BF16)
        k_ref[:, h * QK_PAD + LANE:(h + 1) * QK_PAD] = (kr_roped * rk).astype(BF16)
        v_ref[:, 2 * h * MLA_V:(2 * h + 1) * MLA_V] = (
            kv[:, nope_w + h * MLA_V:nope_w + (h + 1) * MLA_V].astype(BF16))
        v_ref[:, (2 * h + 1) * MLA_V:(2 * h + 2) * MLA_V] = jnp.ones((kn.shape[0], MLA_V), BF16)


def _mla_prep(proj, rope_tabs, gql, gkvl, wuq, wukv, gqn, gqr, gkn, gkr, layer, seq, tm=256):
    t = proj.shape[0]
    per_b = seq // tm
    cos_t, s1_t, s2_t = rope_tabs
    row = lambda i: (i, 0)
    full2 = lambda i: (0, 0)
    lay = lambda i: (layer, 0, 0)
    tab = lambda i: (i % per_b, 0)
    return pl.pallas_call(
        _mla_prep_kernel,
        grid=(t // tm,),
        in_specs=[pl.BlockSpec((tm, Q_LORA), lambda i: (i, COL_CQ // Q_LORA)),
                  pl.BlockSpec((tm, KV_LORA), lambda i: (i, COL_CKV // KV_LORA)),
                  pl.BlockSpec((tm, LANE), lambda i: (i, COL_KR // LANE)),
                  pl.BlockSpec((tm, LANE), tab),
                  pl.BlockSpec((tm, LANE), tab),
                  pl.BlockSpec((tm, LANE), tab),
                  pl.BlockSpec((None, 1, Q_LORA), lay),
                  pl.BlockSpec((None, 1, KV_LORA), lay),
                  pl.BlockSpec((None,) + wuq.shape[1:], lay),
                  pl.BlockSpec((None,) + wukv.shape[1:], lay),
                  pl.BlockSpec((None, 1, LANE), lay),
                  pl.BlockSpec((None, 1, 2 * LANE), lay),
                  pl.BlockSpec((None, 1, LANE), lay),
                  pl.BlockSpec((None, 1, LANE), lay)],
        out_specs=[pl.BlockSpec((tm, MLA_HEADS * QK_PAD), row),
                   pl.BlockSpec((tm, MLA_HEADS * QK_PAD), row),
                   pl.BlockSpec((tm, MLA_HEADS * 2 * MLA_V), row)],
        out_shape=[jax.ShapeDtypeStruct((t, MLA_HEADS * QK_PAD), BF16),
                   jax.ShapeDtypeStruct((t, MLA_HEADS * QK_PAD), BF16),
                   jax.ShapeDtypeStruct((t, MLA_HEADS * 2 * MLA_V), BF16)],
        compiler_params=_params("arbitrary"),
        name="mla_prep",
    )(proj, proj, proj, cos_t, s1_t, s2_t, gql, gkvl, wuq, wukv, gqn, gqr, gkn, gkr)


def _mla_attn_kernel(*refs, tq, heads, ada_tiles):
    if ada_tiles:
        (q_ref, k_ref, v_ref, c_ref, wada_ref, bada_ref, o_ref, mod_ref,
         m_ref, l_ref, acc_ref, sa_ref, sb_ref) = refs
        step = ((pl.program_id(0) * pl.num_programs(1) + pl.program_id(1)) * pl.num_programs(2)
                + pl.program_id(2))

        @pl.when(step < ada_tiles)
        def _():
            _ada_kernel(c_ref, wada_ref, bada_ref, mod_ref)
    else:
        q_ref, k_ref, v_ref, o_ref, m_ref, l_ref, acc_ref, sa_ref, sb_ref = refs
    i = pl.program_id(2)
    vw = 2 * MLA_V

    def scores(g, kb):
        q = q_ref[:, g * QK_PAD:(g + 1) * QK_PAD]
        k = k_ref[pl.ds(pl.multiple_of(kb * tq, tq), tq), g * QK_PAD:(g + 1) * QK_PAD]
        return lax.dot_general(q, k, (((1,), (1,)), ((), ())), preferred_element_type=F32)

    def probs(s, m):
        return jnp.concatenate(
            [jnp.exp2(s[:, c * LANE:(c + 1) * LANE] - m) for c in range(tq // LANE)],
            axis=1).astype(BF16)

    def pv(g, p, kb):
        v = v_ref[pl.ds(pl.multiple_of(kb * tq, tq), tq), g * vw:(g + 1) * vw]
        return jnp.dot(p, v, preferred_element_type=F32)

    def update(g, s, kb):
        m_old = m_ref[g]
        m_new = jnp.maximum(m_old, jnp.max(s, axis=-1, keepdims=True))
        alpha = jnp.exp2(m_old - m_new)
        r = pv(g, probs(s, m_new), kb)
        acc_ref[g] = alpha * acc_ref[g] + r[:, :MLA_V]
        l_ref[g] = alpha * l_ref[g] + r[:, MLA_V:]
        m_ref[g] = m_new

    q_chunk = lax.broadcasted_iota(jnp.int32, (tq, 1), 0) // CHUNK
    k_chunk = lax.broadcasted_iota(jnp.int32, (1, tq), 1) // CHUNK
    visible = k_chunk <= q_chunk
    for g in range(heads):
        s = jnp.where(visible, scores(g, i), -jnp.inf)
        m = jnp.broadcast_to(jnp.max(s, axis=-1, keepdims=True), (tq, LANE))
        r = pv(g, probs(s, m), i)
        m_ref[g] = m
        acc_ref[g] = r[:, :MLA_V]
        l_ref[g] = r[:, MLA_V:]
        sa_ref[g] = scores(g, 0)

    last = jnp.maximum(i - 1, 0)

    def pair(t, carry):
        j = 2 * t
        for g in range(heads):
            sb_ref[g] = scores(g, jnp.minimum(j + 1, last))
            update(g, sa_ref[g], j)
        for g in range(heads):
            sa_ref[g] = scores(g, jnp.minimum(j + 2, last))
            update(g, sb_ref[g], j + 1)
        return carry

    lax.fori_loop(0, i // 2, pair, 0)

    @pl.when(i % 2 == 1)
    def _():
        for g in range(heads):
            update(g, sa_ref[g], i - 1)

    for g in range(heads):
        o_ref[:, g * MLA_V:(g + 1) * MLA_V] = acc_ref[g] / l_ref[g]


def _mla_attention(q2d, k2d, v2d, bsz, seq, ada=None, tq=512, heads=2):
    t = q2d.shape[0]
    tq = min(tq, seq)
    nq = seq // tq
    vw = 2 * MLA_V
    hg = MLA_HEADS // heads
    in_specs = [pl.BlockSpec((tq, heads * QK_PAD), lambda b, h, i: (b * nq + i, h)),
                pl.BlockSpec((seq, heads * QK_PAD), lambda b, h, i: (b, h)),
                pl.BlockSpec((seq, heads * vw), lambda b, h, i: (b, h))]
    out_specs = [pl.BlockSpec((tq, heads * MLA_V), lambda b, h, i: (b * nq + i, h))]
    out_shape = [jax.ShapeDtypeStruct((t, MLA_HEADS * MLA_V), F32)]
    operands = [q2d, k2d, v2d]
    ada_tiles = 0
    if ada is not None:
        c_col, w_ada, b_ada3, layer = ada
        _, d, n = w_ada.shape
        steps = bsz * hg * nq
        tn = LANE * -(-n // (LANE * steps))
        assert n % tn == 0
        ada_tiles = n // tn
        tile = lambda b, h, i: jnp.minimum((b * hg + h) * nq + i, ada_tiles - 1)
        in_specs += [pl.BlockSpec(c_col.shape, lambda b, h, i: (0, 0, 0)),
                     pl.BlockSpec((None, d, tn), lambda b, h, i: (layer, 0, tile(b, h, i))),
                     pl.BlockSpec((None, 1, tn), lambda b, h, i: (layer, 0, tile(b, h, i)))]
        out_specs += [pl.BlockSpec((c_col.shape[0], tn), lambda b, h, i: (0, tile(b, h, i)))]
        out_shape += [jax.ShapeDtypeStruct((c_col.shape[0], n), F32)]
        operands += [c_col, w_ada, b_ada3]
    outs = pl.pallas_call(
        functools.partial(_mla_attn_kernel, tq=tq, heads=heads, ada_tiles=ada_tiles),
        grid=(bsz, hg, nq),
        in_specs=in_specs,
        out_specs=out_specs,
        out_shape=out_shape,
        scratch_shapes=[pltpu.VMEM((heads, tq, LANE), F32), pltpu.VMEM((heads, tq, MLA_V), F32),
                        pltpu.VMEM((heads, tq, MLA_V), F32),
                        pltpu.VMEM((heads, tq, tq), F32), pltpu.VMEM((heads, tq, tq), F32)],
        compiler_params=_params("arbitrary", "arbitrary", "arbitrary"),
        name="mla_attention",
    )(*operands)
    return outs if ada is not None else (outs[0], None)


def _sb_attn_kernel(q_ref, k_ref, v_ref, tri_ref, o_ref, run_ref, acc_ref, *, tq, tk, heads):
    i = pl.program_id(2)
    tri = tri_ref[...]
    sub = tq // tk
    streams = [(g, r) for g in range(heads) for r in range(sub)]

    def block(s, kb, mask):
        g, r = streams[s]
        cols = slice(g * SB_DIM, (g + 1) * SB_DIM)
        keys = pl.ds(pl.multiple_of(kb * tk, tk), tk)
        q = q_ref[r * tk:(r + 1) * tk, cols]
        z = lax.dot_general(q, k_ref[keys, cols], (((1,), (1,)), ((), ())),
                            preferred_element_type=F32)
        neg_abs = lax.bitcast_convert_type(
            lax.bitcast_convert_type(z, jnp.uint32) | jnp.uint32(0x80000000), F32)
        cost = jnp.maximum(z, 0.0) + jnp.log2(1.0 + jnp.exp2(neg_abs))
        if mask is not None:
            cost = jnp.where(mask, cost, 0.0)
        after = jnp.dot(cost.astype(BF16), tri, preferred_element_type=F32) + run_ref[s]
        a = jnp.exp2(z - cost - after)
        if mask is not None:
            a = jnp.where(mask, a, 0.0)
        acc_ref[s] += jnp.dot(a.astype(BF16), v_ref[keys, cols], preferred_element_type=F32)
        run_ref[s] += jnp.sum(cost, axis=-1, keepdims=True)

    def more():
        return jnp.min(run_ref[...]) < SB_UNDERFLOW_LOG2

    run_ref[...] = jnp.zeros_like(run_ref)
    acc_ref[...] = jnp.zeros_like(acc_ref)
    strictly_before = (lax.broadcasted_iota(jnp.int32, (tk, tk), 1)
                       < lax.broadcasted_iota(jnp.int32, (tk, tk), 0))
    for s, (g, r) in enumerate(streams):
        block(s, i * sub + r, strictly_before)

    def cond(c):
        return (c[0] < i * sub + sub - 1) & c[1]

    def body(c):
        for s, (g, r) in enumerate(streams):
            kb = i * sub + r - 1 - c[0]
            run_ref[s] = jnp.where(kb >= 0, run_ref[s], SB_FINISHED_LOG2)
            block(s, jnp.maximum(kb, 0), None)
        return c[0] + 1, more()

    lax.while_loop(cond, body, (jnp.int32(0), more()))
    for s, (g, r) in enumerate(streams):
        o_ref[r * tk:(r + 1) * tk, g * SB_DIM:(g + 1) * SB_DIM] = acc_ref[s]


def _sb_attention(proj, bsz, seq, tq=512, tk=256, heads=4):
    t = proj.shape[0]
    tq = min(tq, seq)
    nq = seq // tq
    tri = (lax.broadcasted_iota(jnp.int32, (tk, tk), 0)
           > lax.broadcasted_iota(jnp.int32, (tk, tk), 1)).astype(BF16)
    w = heads * SB_DIM
    cq, ck, cv = COL_SQ // w, COL_SK // w, COL_SV // w
    n_streams = heads * (tq // tk)
    return pl.pallas_call(
        functools.partial(_sb_attn_kernel, tq=tq, tk=tk, heads=heads),
        grid=(bsz, SB_HEADS // heads, nq),
        in_specs=[pl.BlockSpec((tq, w), lambda b, h, i: (b * nq + i, cq + h)),
                  pl.BlockSpec((seq, w), lambda b, h, i: (b, ck + h)),
                  pl.BlockSpec((seq, w), lambda b, h, i: (b, cv + h)),
                  pl.BlockSpec((tk, tk), lambda b, h, i: (0, 0))],
        out_specs=pl.BlockSpec((tq, w), lambda b, h, i: (b * nq + i, h)),
        out_shape=jax.ShapeDtypeStruct((t, SB_WIDTH), F32),
        scratch_shapes=[pltpu.VMEM((n_streams, tk, 1), F32),
                        pltpu.VMEM((n_streams, tk, SB_DIM), F32)],
        compiler_params=_params("arbitrary", "arbitrary", "arbitrary"),
        name="sb_attention",
    )(proj, proj, proj, tri)


def _out_proj_kernel(om_ref, os_ref, gm_ref, gs_ref, wo_ref, x_ref, gate_ref, g2_ref,
                     sh_ref, sc_ref, rw_ref, xo_ref, h2_ref, lg_ref):
    wm = om_ref.shape[1]
    nm = (_rms(om_ref[...]) * gm_ref[...]).astype(BF16)
    ns = (_rms(os_ref[...]) * gs_ref[...]).astype(BF16)
    y = (jnp.dot(nm, wo_ref[:wm, :], preferred_element_type=F32)
         + jnp.dot(ns, wo_ref[wm:, :], preferred_element_type=F32))
    xn = x_ref[...] + gate_ref[...] * y
    xo_ref[...] = xn
    h2 = _rms(xn) * g2_ref[...]
    h2 = h2 * (1.0 + sc_ref[...]) + sh_ref[...]
    h2_ref[...] = _pack_rows(h2)
    lg_ref[...] = lax.dot_general(rw_ref[...], h2.astype(BF16), (((1,), (1,)), ((), ())),
                                  preferred_element_type=F32)


def _out_proj(o_mla, o_sb, gm, gs, w_o, x, mod, g2, rw_t, layer, seq, tm=256):
    t, d = x.shape
    per_b = seq // tm
    row = lambda i: (i, 0)
    full2 = lambda i: (0, 0)
    modspec = lambda j: pl.BlockSpec((None, 1, d), lambda i: ((i // per_b) * 6 + j, 0, 0))
    return pl.pallas_call(
        _out_proj_kernel,
        grid=(t // tm,),
        in_specs=[pl.BlockSpec((tm, o_mla.shape[1]), row),
                  pl.BlockSpec((tm, o_sb.shape[1]), row),
                  pl.BlockSpec((1, o_mla.shape[1]), full2),
                  pl.BlockSpec((1, o_sb.shape[1]), full2),
                  pl.BlockSpec((None,) + w_o.shape[1:], lambda i: (layer, 0, 0)),
                  pl.BlockSpec((tm, d), row),
                  modspec(2),
                  pl.BlockSpec((1, d), full2),
                  modspec(3),
                  modspec(4),
                  pl.BlockSpec(rw_t.shape, full2)],
        out_specs=[pl.BlockSpec((tm, d), row),
                   pl.BlockSpec((tm, d // 2), row),
                   pl.BlockSpec((N_EXPERTS, tm), lambda i: (0, i))],
        out_shape=[jax.ShapeDtypeStruct((t, d), F32),
                   jax.ShapeDtypeStruct((t, d // 2), jnp.uint32),
                   jax.ShapeDtypeStruct((N_EXPERTS, t), F32)],
        compiler_params=_params("arbitrary"),
        name="out_proj",
    )(o_mla, o_sb, gm, gs, w_o, x, mod, g2, mod, mod, rw_t)


def _route_kernel(lg_ref, rb_ref, tri_ref, e_ref, w_ref, rank_ref, cnt_ref, run_ref):
    i = pl.program_id(0)

    @pl.when(i == 0)
    def _():
        run_ref[...] = jnp.zeros_like(run_ref)

    scores = jax.nn.sigmoid(lg_ref[...])
    sel = scores + rb_ref[...]
    tm = sel.shape[1]
    epg = EXPERTS_PER_GROUP
    srow = [sel[e:e + 1, :] for e in range(N_EXPERTS)]
    prow = [scores[e:e + 1, :] for e in range(N_EXPERTS)]

    gscore = []
    for g in range(N_GROUPS):
        a, b, c, d = srow[g * epg:(g + 1) * epg]
        hi1, lo1 = jnp.maximum(a, b), jnp.minimum(a, b)
        hi2, lo2 = jnp.maximum(c, d), jnp.minimum(c, d)
        gscore.append(jnp.maximum(hi1, hi2)
                      + jnp.maximum(jnp.minimum(hi1, hi2), jnp.maximum(lo1, lo2)))
    best = gscore[0]
    gidx = jnp.zeros((1, tm), jnp.int32)
    for g in range(1, N_GROUPS):
        better = gscore[g] > best
        gidx = jnp.where(better, g, gidx)
        best = jnp.where(better, gscore[g], best)

    def pick(rows, j):
        out = rows[j]
        for g in range(1, N_GROUPS):
            out = jnp.where(gidx == g, rows[g * epg + j], out)
        return out

    v = [pick(srow, j) for j in range(epg)]
    p = [pick(prow, j) for j in range(epg)]
    pos = []
    for j in range(epg):
        r = jnp.zeros((1, tm), jnp.int32)
        for o in range(epg):
            if o == j:
                continue
            ahead = (v[o] > v[j]) | ((v[o] == v[j]) & (o < j))
            r = r + ahead.astype(jnp.int32)
        pos.append(r)
    local, wsel = [], []
    for k in range(TOP_K):
        lk = jnp.zeros((1, tm), jnp.int32)
        wk = jnp.zeros((1, tm), F32)
        for j in range(epg):
            hit = pos[j] == k
            lk = jnp.where(hit, j, lk)
            wk = jnp.where(hit, p[j], wk)
        local.append(lk)
        wsel.append(wk)
    wsum = wsel[0] + wsel[1]
    idx = [gidx * epg + local[k] for k in range(TOP_K)]

    eiota = lax.broadcasted_iota(jnp.int32, (N_EXPERTS, tm), 0)
    hit = [eiota == idx[k] for k in range(TOP_K)]
    onehot = (hit[0] | hit[1]).astype(F32)
    before = jnp.dot(onehot.astype(BF16), tri_ref[...], preferred_element_type=F32) + run_ref[...]
    for k in range(TOP_K):
        e_ref[pl.ds(k, 1), :] = idx[k]
        w_ref[pl.ds(k, 1), :] = wsel[k] / wsum
        rank_ref[pl.ds(k, 1), :] = jnp.sum(jnp.where(hit[k], before, 0.0), axis=0,
                                           keepdims=True).astype(jnp.int32)
    run_ref[...] += jnp.sum(onehot, axis=1, keepdims=True)
    cnt_ref[...] = jnp.broadcast_to(run_ref[...], cnt_ref.shape)


def _route(logits_t, router_b, tm=512):
    t = logits_t.shape[1]
    tm = min(tm, t)
    tri = (lax.broadcasted_iota(jnp.int32, (tm, tm), 0)
           < lax.broadcasted_iota(jnp.int32, (tm, tm), 1)).astype(BF16)
    blk = lambda rows: pl.BlockSpec((rows, tm), lambda i: (0, i))
    return pl.pallas_call(
        _route_kernel,
        grid=(t // tm,),
        in_specs=[blk(N_EXPERTS),
                  pl.BlockSpec((N_EXPERTS, 1), lambda i: (0, 0)),
                  pl.BlockSpec((tm, tm), lambda i: (0, 0))],
        out_specs=[blk(TOP_K), blk(TOP_K), blk(TOP_K),
                   pl.BlockSpec((N_EXPERTS, LANE), lambda i: (0, 0))],
        out_shape=[jax.ShapeDtypeStruct((TOP_K, t), jnp.int32),
                   jax.ShapeDtypeStruct((TOP_K, t), F32),
                   jax.ShapeDtypeStruct((TOP_K, t), jnp.int32),
                   jax.ShapeDtypeStruct((N_EXPERTS, LANE), F32)],
        scratch_shapes=[pltpu.VMEM((N_EXPERTS, 1), F32)],
        compiler_params=_params("arbitrary"),
        name="route",
    )(logits_t, router_b[:, None], tri)


def _dispatch_kernel(dest_ref, pend_ref, h_ref, o_ref, zero_ref, sem, zsem, *, tm, t):
    base = pl.program_id(0) * tm

    @pl.when(pl.program_id(0) == 0)
    def _():
        zero_ref[...] = jnp.zeros_like(zero_ref)

        def fill(e):
            end = pend_ref[e]
            start = pend_ref[e - 1] if e > 0 else 0
            dst = o_ref.at[pl.ds(pl.multiple_of(jnp.maximum(end - MOE_ROWS, 0), MOE_ROWS),
                                 MOE_ROWS), :]
            return end > start, pltpu.make_async_copy(zero_ref, dst, zsem)

        for e in range(N_EXPERTS):
            nonempty, cp = fill(e)
            pl.when(nonempty)(cp.start)
        for e in range(N_EXPERTS):
            nonempty, cp = fill(e)
            pl.when(nonempty)(cp.wait)

        def tail(r):
            dst = o_ref.at[pl.ds(pl.multiple_of(r * MOE_ROWS, MOE_ROWS), MOE_ROWS), :]
            return pltpu.make_async_copy(zero_ref, dst, zsem)

        first_free = pend_ref[N_EXPERTS - 1] // MOE_ROWS
        n_blocks = o_ref.shape[0] // MOE_ROWS
        lax.fori_loop(first_free, n_blocks, lambda r, c: (tail(r).start(), c)[1], 0)
        lax.fori_loop(first_free, n_blocks, lambda r, c: (tail(r).wait(), c)[1], 0)

    def issue(r, carry):
        for k in range(TOP_K):
            d = dest_ref[k * t + base + r]
            pltpu.make_async_copy(h_ref.at[pl.ds(r, 1), :], o_ref.at[pl.ds(d, 1), :], sem).start()
        return carry

    lax.fori_loop(0, tm, issue, 0, unroll=DMA_ISSUE_UNROLL)
    for k in range(TOP_K):
        pltpu.make_async_copy(h_ref, o_ref.at[pl.ds(0, tm), :], sem).wait()


def _dispatch(h2, dest_flat, pad_end, rows, tm=256):
    t, d = h2.shape
    return pl.pallas_call(
        functools.partial(_dispatch_kernel, tm=tm, t=t),
        grid_spec=pltpu.PrefetchScalarGridSpec(
            num_scalar_prefetch=2,
            grid=(t // tm,),
            in_specs=[pl.BlockSpec((tm, d), lambda i, dest, pend: (i, 0))],
            out_specs=pl.BlockSpec(memory_space=pl.ANY),
            scratch_shapes=[pltpu.VMEM((MOE_ROWS, d), h2.dtype),
                            pltpu.SemaphoreType.DMA(()), pltpu.SemaphoreType.DMA(())]),
        out_shape=jax.ShapeDtypeStruct((rows, d), h2.dtype),
        compiler_params=_params("arbitrary"),
        name="moe_dispatch",
    )(dest_flat, pad_end, h2)


def _expert_kernel(be_ref, nu_ref, x_ref, wg_ref, wu_ref, wd_ref, y_ref, wgb, wub, wdb):
    r = pl.program_id(0)
    live = r < nu_ref[0]
    new_expert = (r == 0) | (be_ref[jnp.maximum(r - 1, 0)] != be_ref[r])

    @pl.when(live & new_expert)
    def _():
        wgb[...] = wg_ref[...].astype(BF16)
        wub[...] = wu_ref[...].astype(BF16)
        wdb[...] = wd_ref[...].astype(BF16)

    @pl.when(live)
    def _():
        lo, hi = _unpack_rows(x_ref[...])
        xb = jnp.concatenate([lo.astype(BF16), hi.astype(BF16)], axis=1)
        g = jnp.dot(xb, wgb[...], preferred_element_type=F32)
        u = jnp.dot(xb, wub[...], preferred_element_type=F32)
        a = (g * jax.nn.sigmoid(g)) * u
        y_ref[...] = _pack_rows(jnp.dot(a.astype(BF16), wdb[...], preferred_element_type=F32))

    @pl.when(jnp.logical_not(live))
    def _():
        y_ref[...] = jnp.zeros_like(y_ref)


def _experts(x_sorted, block_e, n_used, w_gate, w_up, w_down, layer):
    rows, d = x_sorted.shape
    de = w_gate.shape[-1]
    dm = w_gate.shape[-2]
    bm = MOE_ROWS
    blk = lambda r, be, nu: (jnp.minimum(r, nu[0] - 1), 0)
    return pl.pallas_call(
        _expert_kernel,
        grid_spec=pltpu.PrefetchScalarGridSpec(
            num_scalar_prefetch=2,
            grid=(rows // bm,),
            in_specs=[pl.BlockSpec((bm, d), blk),
                      pl.BlockSpec((None, None, dm, de), lambda r, be, nu: (layer, be[r], 0, 0)),
                      pl.BlockSpec((None, None, dm, de), lambda r, be, nu: (layer, be[r], 0, 0)),
                      pl.BlockSpec((None, None, de, dm), lambda r, be, nu: (layer, be[r], 0, 0))],
            out_specs=pl.BlockSpec((bm, d), lambda r, be, nu: (r, 0)),
            scratch_shapes=[pltpu.VMEM((dm, de), BF16), pltpu.VMEM((dm, de), BF16),
                            pltpu.VMEM((de, dm), BF16)]),
        out_shape=jax.ShapeDtypeStruct((rows, d), jnp.uint32),
        compiler_params=_params("arbitrary"),
        name="moe_experts",
    )(block_e, n_used, x_sorted, w_gate, w_up, w_down)


def _combine_kernel(dest_ref, y_ref, x_ref, gate_ref, w_ref, o_ref, buf, sems, *, tm, t):
    i = pl.program_id(0)
    slot = i % 2

    def gather(tile, dst_slot):
        base = tile * tm

        def issue(r, carry):
            for k in range(TOP_K):
                d = dest_ref[k * t + base + r]
                pltpu.make_async_copy(y_ref.at[pl.ds(d, 1), :],
                                      buf.at[dst_slot, k, pl.ds(r, 1), :],
                                      sems.at[dst_slot]).start()
            return carry

        lax.fori_loop(0, tm, issue, 0, unroll=DMA_ISSUE_UNROLL)

    @pl.when(i == 0)
    def _():
        gather(0, 0)

    @pl.when(i + 1 < pl.num_programs(0))
    def _():
        gather(i + 1, 1 - slot)

    for k in range(TOP_K):
        pltpu.make_async_copy(y_ref.at[pl.ds(0, tm), :], buf.at[slot, k], sems.at[slot]).wait()
    w = w_ref[...]
    half = buf.shape[3]
    lo0, hi0 = _unpack_rows(buf[slot, 0])
    lo1, hi1 = _unpack_rows(buf[slot, 1])
    o_ref[:, :half] = x_ref[:, :half] + gate_ref[:, :half] * (w[:, 0:1] * lo0 + w[:, 1:2] * lo1)
    o_ref[:, half:] = x_ref[:, half:] + gate_ref[:, half:] * (w[:, 0:1] * hi0 + w[:, 1:2] * hi1)


def _combine(y_sorted, dest_flat, x, mod, w_tok, seq, tm=256):
    t, d = x.shape
    per_b = seq // tm
    return pl.pallas_call(
        functools.partial(_combine_kernel, tm=tm, t=t),
        grid_spec=pltpu.PrefetchScalarGridSpec(
            num_scalar_prefetch=1,
            grid=(t // tm,),
            in_specs=[pl.BlockSpec(memory_space=pl.ANY),
                      pl.BlockSpec((tm, d), lambda i, dest: (i, 0)),
                      pl.BlockSpec((None, 1, d), lambda i, dest: ((i // per_b) * 6 + 5, 0, 0)),
                      pl.BlockSpec((tm, TOP_K), lambda i, dest: (i, 0))],
            out_specs=pl.BlockSpec((tm, d), lambda i, dest: (i, 0)),
            scratch_shapes=[pltpu.VMEM((2, TOP_K, tm, y_sorted.shape[1]), y_sorted.dtype),
                            pltpu.SemaphoreType.DMA((2,))]),
        out_shape=jax.ShapeDtypeStruct((t, d), F32),
        compiler_params=_params("arbitrary"),
        name="moe_combine",
    )(dest_flat, y_sorted, x, mod, w_tok)


def _rope_tables(seq):
    half = MLA_ROPE // 2
    inv = 1.0 / (ROPE_BASE ** (jnp.arange(0, MLA_ROPE, 2, dtype=F32) / MLA_ROPE))
    ang = jnp.arange(seq, dtype=F32)[:, None] * inv[None, :]
    cos, sin = jnp.cos(ang), jnp.sin(ang)
    z = jnp.zeros((seq, half), F32)
    zz = jnp.zeros((seq, LANE - MLA_ROPE), F32)
    cos_t = jnp.concatenate([cos, cos, zz], axis=1)
    s1_t = jnp.concatenate([z, sin, zz], axis=1)
    s2_t = jnp.concatenate([-sin, z, zz], axis=1)
    return cos_t, s1_t, s2_t


def _pad_last(a, width):
    return jnp.pad(a, [(0, 0)] * (a.ndim - 1) + [(0, width - a.shape[-1])])


def kernel(x, c, norm1_g, norm2_g, w_ada, b_ada, w_in, q_lora_g, kv_lora_g, w_uq, w_ukv,
           q_head_g, k_head_g, mla_out_g, sb_out_g, w_o, router_w, router_b,
           w_gate, w_up, w_down):
    bsz, seq, d = x.shape
    t = bsz * seq
    depth = w_ada.shape[0]

    sb_col0 = COL_KR + MLA_ROPE
    w_in_mla = _pad_last(w_in[..., :sb_col0], MLA_IN_COLS).astype(BF16)
    sb_col_scale = jnp.where(jnp.arange(SB_IN_COLS) < SB_WIDTH, SB_DIM ** -0.5 * LOG2_E, 1.0)
    w_in_sb = (w_in[..., sb_col0:] * sb_col_scale.astype(F32)).astype(BF16)
    wq = w_uq.reshape(depth, Q_LORA, MLA_HEADS, MLA_QK)
    half = MLA_ROPE // 2
    swap = lambda a: jnp.concatenate([a[..., half:], a[..., :half]], axis=-1)
    wq_rope = wq[..., MLA_NOPE:]
    wuq_p = jnp.concatenate(
        [wq[..., :MLA_NOPE].reshape(depth, Q_LORA, -1),
         _pad_last(wq_rope, LANE).reshape(depth, Q_LORA, -1),
         _pad_last(swap(wq_rope), LANE).reshape(depth, Q_LORA, -1)], axis=-1).astype(BF16)
    wkv = w_ukv.reshape(depth, KV_LORA, MLA_HEADS, MLA_NOPE + MLA_V)
    wukv_p = jnp.concatenate(
        [wkv[..., :MLA_NOPE].reshape(depth, KV_LORA, -1),
         wkv[..., MLA_NOPE:].reshape(depth, KV_LORA, -1)], axis=-1).astype(BF16)
    w_o_b = w_o.astype(BF16)
    q_scale = MLA_QK ** -0.5 * LOG2_E
    gqn = (q_head_g[:, None, :MLA_NOPE] * q_scale)
    gq_rope = q_head_g[:, None, MLA_NOPE:] * q_scale
    gqr = jnp.concatenate([_pad_last(gq_rope, LANE), _pad_last(swap(gq_rope), LANE)], axis=-1)
    gkn = k_head_g[:, None, :MLA_NOPE]
    gkr = _pad_last(k_head_g[:, None, MLA_NOPE:], LANE)
    rope_tabs = _rope_tables(seq)
    rw_t = router_w.T.astype(BF16)

    c_col = jnp.broadcast_to(c[:, :, None], c.shape + (LANE,))
    b_ada3 = b_ada[:, None, :]
    mod_l = _ada_modulation(c_col, w_ada, b_ada3, layers=1)[0]
    n_blocks = -(-(t * TOP_K) // MOE_ROWS) + N_EXPERTS
    rows = n_blocks * MOE_ROWS

    xt = x.reshape(t, d)
    for l in range(depth):
        mod = mod_l.reshape(bsz * 6, 1, d)
        proj_mla, proj_sb = _in_proj(xt, norm1_g[l][None, :], mod, w_in_mla, w_in_sb, l, seq)
        q2d, k2d, v2d = _mla_prep(proj_mla, rope_tabs, q_lora_g[:, None, :],
                                  kv_lora_g[:, None, :], wuq_p, wukv_p, gqn, gqr, gkn, gkr, l, seq)
        next_ada = (c_col, w_ada, b_ada3, l + 1) if l + 1 < depth else None
        o_mla, mod_l = _mla_attention(q2d, k2d, v2d, bsz, seq, ada=next_ada)
        o_sb = _sb_attention(proj_sb, bsz, seq)
        xt, h2, logits_t = _out_proj(o_mla, o_sb, mla_out_g[l][None, :], sb_out_g[l][None, :],
                                     w_o_b, xt, mod, norm2_g[l][None, :], rw_t, l, seq)
        e_idx, w_tok, rank, cnt = _route(logits_t, router_b)
        counts = cnt[:, 0].astype(jnp.int32)
        padded = (counts + MOE_ROWS - 1) // MOE_ROWS * MOE_ROWS
        pad_end = jnp.cumsum(padded)
        pad_start = pad_end - padded
        eids = jnp.arange(N_EXPERTS, dtype=jnp.int32)
        seg_start = jnp.sum(jnp.where(e_idx[None] == eids[:, None, None],
                                      pad_start[:, None, None], 0), axis=0)
        dest = (seg_start + rank).reshape(TOP_K * t).astype(jnp.int32)
        n_used = (pad_end[-1:] // MOE_ROWS).astype(jnp.int32)
        first_row = jnp.minimum(jnp.arange(n_blocks, dtype=jnp.int32), n_used - 1) * MOE_ROWS
        block_e = jnp.sum(pad_end[None, :] <= first_row[:, None], axis=1).astype(jnp.int32)
        x_sorted = _dispatch(h2, dest, pad_end.astype(jnp.int32), rows)
        y_sorted = _experts(x_sorted, block_e, n_used, w_gate, w_up, w_down, l)
        xt = _combine(y_sorted, dest, xt, mod, w_tok.T, seq)
    return xt.reshape(bsz, seq, d)
```

```python
import functools

import jax
import jax.numpy as jnp
from jax import lax
from jax.experimental import pallas as pl
from jax.experimental.pallas import tpu as pltpu

F32 = jnp.float32
BF16 = jnp.bfloat16

EPS = 1e-6
ROPE_BASE = 10000.0
CHUNK = 64

MLA_HEADS = 8
MLA_NOPE = 128
MLA_ROPE = 64
MLA_QK = MLA_NOPE + MLA_ROPE
MLA_V = 128
Q_LORA = 512
KV_LORA = 256
SB_HEADS = 8
SB_DIM = 128
SB_WIDTH = SB_HEADS * SB_DIM
LANE = 128
QK_PAD = 2 * LANE
COL_CQ = 0
COL_CKV = Q_LORA
COL_KR = Q_LORA + KV_LORA
MLA_IN_COLS = COL_KR + LANE
COL_SQ = 0
COL_SK = SB_WIDTH
COL_SV = 2 * SB_WIDTH
SB_IN_COLS = 3 * SB_WIDTH

N_EXPERTS = 16
N_GROUPS = 4
EXPERTS_PER_GROUP = N_EXPERTS // N_GROUPS
TOP_K = 2
D_EXPERT = 512
MOE_ROWS = 512
DMA_ISSUE_UNROLL = 8
LOG2_E = 1.4426950408889634
SB_UNDERFLOW_LOG2 = 150.0
SB_FINISHED_LOG2 = 1e30

VMEM_LIMIT = 56 * 1024 * 1024


def _params(*dims):
    return pltpu.CompilerParams(dimension_semantics=dims, vmem_limit_bytes=VMEM_LIMIT)


def _rms(x):
    return x * lax.rsqrt(jnp.mean(x * x, axis=-1, keepdims=True) + EPS)


def _pack_rows(x):
    n = x.shape[1] // 2
    lo = lax.bitcast_convert_type(x[:, :n].astype(BF16).astype(F32), jnp.uint32)
    hi = lax.bitcast_convert_type(x[:, n:].astype(BF16).astype(F32), jnp.uint32)
    return (lo >> 16) | (hi & jnp.uint32(0xFFFF0000))


def _unpack_rows(w):
    lo = lax.bitcast_convert_type(w << 16, F32)
    hi = lax.bitcast_convert_type(w & jnp.uint32(0xFFFF0000), F32)
    return lo, hi


def _ada_kernel(c_ref, w_ref, b_ref, o_ref):
    w = w_ref[...]
    for b in range(c_ref.shape[0]):
        cb = c_ref[b]
        cb = cb * jax.nn.sigmoid(cb)
        o_ref[pl.ds(b, 1), :] = jnp.sum(cb * w, axis=0, keepdims=True) + b_ref[...]


def _ada_modulation(c_col, w_ada, b_ada3, layers, tn=1024):
    _, d, n = w_ada.shape
    bsz = c_col.shape[0]
    return pl.pallas_call(
        _ada_kernel,
        grid=(layers, n // tn),
        in_specs=[pl.BlockSpec(c_col.shape, lambda l, j: (0, 0, 0)),
                  pl.BlockSpec((None, d, tn), lambda l, j: (l, 0, j)),
                  pl.BlockSpec((None, 1, tn), lambda l, j: (l, 0, j))],
        out_specs=pl.BlockSpec((None, bsz, tn), lambda l, j: (l, 0, j)),
        out_shape=jax.ShapeDtypeStruct((layers, bsz, n), F32),
        compiler_params=_params("arbitrary", "arbitrary"),
        name="ada_modulation",
    )(c_col, w_ada, b_ada3)


def _in_proj_kernel(x_ref, g_ref, sh_ref, sc_ref, wa_ref, wb_ref, oa_ref, ob_ref):
    h = _rms(x_ref[...]) * g_ref[...]
    h = (h * (1.0 + sc_ref[...]) + sh_ref[...]).astype(BF16)
    oa_ref[...] = jnp.dot(h, wa_ref[...], preferred_element_type=F32).astype(oa_ref.dtype)
    ob_ref[...] = jnp.dot(h, wb_ref[...], preferred_element_type=F32).astype(ob_ref.dtype)


def _in_proj(x, g, mod, w_mla, w_sb, layer, seq, tm=256):
    t, d = x.shape
    na, nb = w_mla.shape[-1], w_sb.shape[-1]
    per_b = seq // tm
    return pl.pallas_call(
        _in_proj_kernel,
        grid=(t // tm,),
        in_specs=[pl.BlockSpec((tm, d), lambda i: (i, 0)),
                  pl.BlockSpec((1, d), lambda i: (0, 0)),
                  pl.BlockSpec((None, 1, d), lambda i: ((i // per_b) * 6 + 0, 0, 0)),
                  pl.BlockSpec((None, 1, d), lambda i: ((i // per_b) * 6 + 1, 0, 0)),
                  pl.BlockSpec((None, d, na), lambda i: (layer, 0, 0)),
                  pl.BlockSpec((None, d, nb), lambda i: (layer, 0, 0))],
        out_specs=[pl.BlockSpec((tm, na), lambda i: (i, 0)),
                   pl.BlockSpec((tm, nb), lambda i: (i, 0))],
        out_shape=[jax.ShapeDtypeStruct((t, na), BF16),
                   jax.ShapeDtypeStruct((t, nb), BF16)],
        compiler_params=_params("arbitrary"),
        name="in_proj",
    )(x, g, mod, mod, w_mla, w_sb)


def _mla_prep_kernel(cq_ref, ckv_ref, kr_ref, cos_ref, s1_ref, s2_ref, gql_ref, gkvl_ref,
                     wuq_ref, wukv_ref, gqn_ref, gqr_ref, gkn_ref, gkr_ref,
                     q_ref, k_ref, v_ref):
    nope_w = MLA_HEADS * MLA_NOPE
    cqn = _rms(cq_ref[...].astype(F32)) * gql_ref[...]
    q = jnp.dot(cqn.astype(BF16), wuq_ref[...], preferred_element_type=F32)
    ckvn = _rms(ckv_ref[...].astype(F32)) * gkvl_ref[...]
    kv = jnp.dot(ckvn.astype(BF16), wukv_ref[...], preferred_element_type=F32)
    cosv, s1, s2 = cos_ref[...], s1_ref[...], s2_ref[...]

    def rope(x):
        return (x * cosv + pltpu.roll(x, MLA_ROPE // 2, 1) * s1
                + pltpu.roll(x, LANE - MLA_ROPE // 2, 1) * s2)

    kr = kr_ref[...].astype(F32)
    kr_ss = jnp.sum(kr * kr, axis=-1, keepdims=True)
    kr_roped = rope(kr * gkr_ref[...])
    gqn, gkn = gqn_ref[...], gkn_ref[...]
    gqr, gqs = gqr_ref[:, :LANE], gqr_ref[:, LANE:]
    s_comb = s1 + s2
    for h in range(MLA_HEADS):
        qn = q[:, h * LANE:(h + 1) * LANE]
        qr = q[:, nope_w + h * LANE:nope_w + (h + 1) * LANE]
        ss = jnp.sum(qn * qn + qr * qr, axis=-1, keepdims=True)
        r = lax.rsqrt(ss * (1.0 / MLA_QK) + EPS)
        q_ref[:, h * QK_PAD:h * QK_PAD + LANE] = (qn * r * gqn).astype(BF16)
        qs = q[:, 2 * nope_w + h * LANE:2 * nope_w + (h + 1) * LANE]
        q_rope = (qr * gqr) * cosv + (qs * gqs) * s_comb
        q_ref[:, h * QK_PAD + LANE:(h + 1) * QK_PAD] = (q_rope * r).astype(BF16)
        kn = kv[:, h * LANE:(h + 1) * LANE]
        rk = lax.rsqrt((jnp.sum(kn * kn, axis=-1, keepdims=True) + kr_ss) * (1.0 / MLA_QK) + EPS)
        k_ref[:, h * QK_PAD:h * QK_PAD + LANE] = (kn * rk * gkn).astype(BF16)
        k_ref[:, h * QK_PAD + LANE:(h + 1) * QK_PAD] = (kr_roped * rk).astype(BF16)
        v_ref[:, 2 * h * MLA_V:(2 * h + 1) * MLA_V] = (
            kv[:, nope_w + h * MLA_V:nope_w + (h + 1) * MLA_V].astype(BF16))
        v_ref[:, (2 * h + 1) * MLA_V:(2 * h + 2) * MLA_V] = jnp.ones((kn.shape[0], MLA_V), BF16)


def _mla_prep(proj, rope_tabs, gql, gkvl, wuq, wukv, gqn, gqr, gkn, gkr, layer, seq, tm=256):
    t = proj.shape[0]
    per_b = seq // tm
    cos_t, s1_t, s2_t = rope_tabs
    row = lambda i: (i, 0)
    full2 = lambda i: (0, 0)
    lay = lambda i: (layer, 0, 0)
    tab = lambda i: (i % per_b, 0)
    return pl.pallas_call(
        _mla_prep_kernel,
        grid=(t // tm,),
        in_specs=[pl.BlockSpec((tm, Q_LORA), lambda i: (i, COL_CQ // Q_LORA)),
                  pl.BlockSpec((tm, KV_LORA), lambda i: (i, COL_CKV // KV_LORA)),
                  pl.BlockSpec((tm, LANE), lambda i: (i, COL_KR // LANE)),
                  pl.BlockSpec((tm, LANE), tab),
                  pl.BlockSpec((tm, LANE), tab),
                  pl.BlockSpec((tm, LANE), tab),
                  pl.BlockSpec((None, 1, Q_LORA), lay),
                  pl.BlockSpec((None, 1, KV_LORA), lay),
                  pl.BlockSpec((None,) + wuq.shape[1:], lay),
                  pl.BlockSpec((None,) + wukv.shape[1:], lay),
                  pl.BlockSpec((None, 1, LANE), lay),
                  pl.BlockSpec((None, 1, 2 * LANE), lay),
                  pl.BlockSpec((None, 1, LANE), lay),
                  pl.BlockSpec((None, 1, LANE), lay)],
        out_specs=[pl.BlockSpec((tm, MLA_HEADS * QK_PAD), row),
                   pl.BlockSpec((tm, MLA_HEADS * QK_PAD), row),
                   pl.BlockSpec((tm, MLA_HEADS * 2 * MLA_V), row)],
        out_shape=[jax.ShapeDtypeStruct((t, MLA_HEADS * QK_PAD), BF16),
                   jax.ShapeDtypeStruct((t, MLA_HEADS * QK_PAD), BF16),
                   jax.ShapeDtypeStruct((t, MLA_HEADS * 2 * MLA_V), BF16)],
        compiler_params=_params("arbitrary"),
        name="mla_prep",
    )(proj, proj, proj, cos_t, s1_t, s2_t, gql, gkvl, wuq, wukv, gqn, gqr, gkn, gkr)


def _mla_attn_kernel(q_ref, k_ref, v_ref, o_ref, m_ref, l_ref, acc_ref, sa_ref, sb_ref, *,
                     tq, heads):
    i = pl.program_id(2)
    vw = 2 * MLA_V

    def scores(g, kb):
        q = q_ref[:, g * QK_PAD:(g + 1) * QK_PAD]
        k = k_ref[pl.ds(pl.multiple_of(kb * tq, tq), tq), g * QK_PAD:(g + 1) * QK_PAD]
        return lax.dot_general(q, k, (((1,), (1,)), ((), ())), preferred_element_type=F32)

    def probs(s, m):
        return jnp.concatenate(
            [jnp.exp2(s[:, c * LANE:(c + 1) * LANE] - m) for c in range(tq // LANE)],
            axis=1).astype(BF16)

    def pv(g, p, kb):
        v = v_ref[pl.ds(pl.multiple_of(kb * tq, tq), tq), g * vw:(g + 1) * vw]
        return jnp.dot(p, v, preferred_element_type=F32)

    def update(g, s, kb):
        m_old = m_ref[g]
        m_new = jnp.maximum(m_old, jnp.max(s, axis=-1, keepdims=True))
        alpha = jnp.exp2(m_old - m_new)
        r = pv(g, probs(s, m_new), kb)
        acc_ref[g] = alpha * acc_ref[g] + r[:, :MLA_V]
        l_ref[g] = alpha * l_ref[g] + r[:, MLA_V:]
        m_ref[g] = m_new

    q_chunk = lax.broadcasted_iota(jnp.int32, (tq, 1), 0) // CHUNK
    k_chunk = lax.broadcasted_iota(jnp.int32, (1, tq), 1) // CHUNK
    visible = k_chunk <= q_chunk
    for g in range(heads):
        s = jnp.where(visible, scores(g, i), -jnp.inf)
        m = jnp.broadcast_to(jnp.max(s, axis=-1, keepdims=True), (tq, LANE))
        r = pv(g, probs(s, m), i)
        m_ref[g] = m
        acc_ref[g] = r[:, :MLA_V]
        l_ref[g] = r[:, MLA_V:]
        sa_ref[g] = scores(g, 0)

    last = jnp.maximum(i - 1, 0)

    def pair(t, carry):
        j = 2 * t
        for g in range(heads):
            sb_ref[g] = scores(g, jnp.minimum(j + 1, last))
            update(g, sa_ref[g], j)
        for g in range(heads):
            sa_ref[g] = scores(g, jnp.minimum(j + 2, last))
            update(g, sb_ref[g], j + 1)
        return carry

    lax.fori_loop(0, i // 2, pair, 0)

    @pl.when(i % 2 == 1)
    def _():
        for g in range(heads):
            update(g, sa_ref[g], i - 1)

    for g in range(heads):
        o_ref[:, g * MLA_V:(g + 1) * MLA_V] = acc_ref[g] / l_ref[g]


def _mla_attention(q2d, k2d, v2d, bsz, seq, tq=512, heads=2):
    t = q2d.shape[0]
    tq = min(tq, seq)
    nq = seq // tq
    vw = 2 * MLA_V
    return pl.pallas_call(
        functools.partial(_mla_attn_kernel, tq=tq, heads=heads),
        grid=(bsz, MLA_HEADS // heads, nq),
        in_specs=[pl.BlockSpec((tq, heads * QK_PAD), lambda b, h, i: (b * nq + i, h)),
                  pl.BlockSpec((seq, heads * QK_PAD), lambda b, h, i: (b, h)),
                  pl.BlockSpec((seq, heads * vw), lambda b, h, i: (b, h))],
        out_specs=pl.BlockSpec((tq, heads * MLA_V), lambda b, h, i: (b * nq + i, h)),
        out_shape=jax.ShapeDtypeStruct((t, MLA_HEADS * MLA_V), F32),
        scratch_shapes=[pltpu.VMEM((heads, tq, LANE), F32), pltpu.VMEM((heads, tq, MLA_V), F32),
                        pltpu.VMEM((heads, tq, MLA_V), F32),
                        pltpu.VMEM((heads, tq, tq), F32), pltpu.VMEM((heads, tq, tq), F32)],
        compiler_params=_params("arbitrary", "arbitrary", "arbitrary"),
        name="mla_attention",
    )(q2d, k2d, v2d)


def _sb_attn_kernel(q_ref, k_ref, v_ref, tri_ref, o_ref, run_ref, acc_ref, *, tq, tk, heads):
    i = pl.program_id(2)
    tri = tri_ref[...]
    sub = tq // tk
    streams = [(g, r) for g in range(heads) for r in range(sub)]

    def block(s, kb, mask):
        g, r = streams[s]
        cols = slice(g * SB_DIM, (g + 1) * SB_DIM)
        keys = pl.ds(pl.multiple_of(kb * tk, tk), tk)
        q = q_ref[r * tk:(r + 1) * tk, cols]
        z = lax.dot_general(q, k_ref[keys, cols], (((1,), (1,)), ((), ())),
                            preferred_element_type=F32)
        neg_abs = lax.bitcast_convert_type(
            lax.bitcast_convert_type(z, jnp.uint32) | jnp.uint32(0x80000000), F32)
        cost = jnp.maximum(z, 0.0) + jnp.log2(1.0 + jnp.exp2(neg_abs))
        if mask is not None:
            cost = jnp.where(mask, cost, 0.0)
        after = jnp.dot(cost.astype(BF16), tri, preferred_element_type=F32) + run_ref[s]
        a = jnp.exp2(z - cost - after)
        if mask is not None:
            a = jnp.where(mask, a, 0.0)
        acc_ref[s] += jnp.dot(a.astype(BF16), v_ref[keys, cols], preferred_element_type=F32)
        run_ref[s] += jnp.sum(cost, axis=-1, keepdims=True)

    def more():
        return jnp.min(run_ref[...]) < SB_UNDERFLOW_LOG2

    run_ref[...] = jnp.zeros_like(run_ref)
    acc_ref[...] = jnp.zeros_like(acc_ref)
    strictly_before = (lax.broadcasted_iota(jnp.int32, (tk, tk), 1)
                       < lax.broadcasted_iota(jnp.int32, (tk, tk), 0))
    for s, (g, r) in enumerate(streams):
        block(s, i * sub + r, strictly_before)

    def cond(c):
        return (c[0] < i * sub + sub - 1) & c[1]

    def body(c):
        for s, (g, r) in enumerate(streams):
            kb = i * sub + r - 1 - c[0]
            run_ref[s] = jnp.where(kb >= 0, run_ref[s], SB_FINISHED_LOG2)
            block(s, jnp.maximum(kb, 0), None)
        return c[0] + 1, more()

    lax.while_loop(cond, body, (jnp.int32(0), more()))
    for s, (g, r) in enumerate(streams):
        o_ref[r * tk:(r + 1) * tk, g * SB_DIM:(g + 1) * SB_DIM] = acc_ref[s]


def _sb_attention(proj, bsz, seq, tq=512, tk=256, heads=4):
    t = proj.shape[0]
    tq = min(tq, seq)
    nq = seq // tq
    tri = (lax.broadcasted_iota(jnp.int32, (tk, tk), 0)
           > lax.broadcasted_iota(jnp.int32, (tk, tk), 1)).astype(BF16)
    w = heads * SB_DIM
    cq, ck, cv = COL_SQ // w, COL_SK // w, COL_SV // w
    n_streams = heads * (tq // tk)
    return pl.pallas_call(
        functools.partial(_sb_attn_kernel, tq=tq, tk=tk, heads=heads),
        grid=(bsz, SB_HEADS // heads, nq),
        in_specs=[pl.BlockSpec((tq, w), lambda b, h, i: (b * nq + i, cq + h)),
                  pl.BlockSpec((seq, w), lambda b, h, i: (b, ck + h)),
                  pl.BlockSpec((seq, w), lambda b, h, i: (b, cv + h)),
                  pl.BlockSpec((tk, tk), lambda b, h, i: (0, 0))],
        out_specs=pl.BlockSpec((tq, w), lambda b, h, i: (b * nq + i, h)),
        out_shape=jax.ShapeDtypeStruct((t, SB_WIDTH), F32),
        scratch_shapes=[pltpu.VMEM((n_streams, tk, 1), F32),
                        pltpu.VMEM((n_streams, tk, SB_DIM), F32)],
        compiler_params=_params("arbitrary", "arbitrary", "arbitrary"),
        name="sb_attention",
    )(proj, proj, proj, tri)


def _out_proj_kernel(om_ref, os_ref, gm_ref, gs_ref, wo_ref, x_ref, gate_ref, g2_ref,
                     sh_ref, sc_ref, rw_ref, xo_ref, h2_ref, lg_ref):
    wm = om_ref.shape[1]
    nm = (_rms(om_ref[...]) * gm_ref[...]).astype(BF16)
    ns = (_rms(os_ref[...]) * gs_ref[...]).astype(BF16)
    y = (jnp.dot(nm, wo_ref[:wm, :], preferred_element_type=F32)
         + jnp.dot(ns, wo_ref[wm:, :], preferred_element_type=F32))
    xn = x_ref[...] + gate_ref[...] * y
    xo_ref[...] = xn
    h2 = _rms(xn) * g2_ref[...]
    h2 = h2 * (1.0 + sc_ref[...]) + sh_ref[...]
    h2_ref[...] = _pack_rows(h2)
    lg_ref[...] = lax.dot_general(rw_ref[...], h2.astype(BF16), (((1,), (1,)), ((), ())),
                                  preferred_element_type=F32)


def _out_proj(o_mla, o_sb, gm, gs, w_o, x, mod, g2, rw_t, layer, seq, tm=256):
    t, d = x.shape
    per_b = seq // tm
    row = lambda i: (i, 0)
    full2 = lambda i: (0, 0)
    modspec = lambda j: pl.BlockSpec((None, 1, d), lambda i: ((i // per_b) * 6 + j, 0, 0))
    return pl.pallas_call(
        _out_proj_kernel,
        grid=(t // tm,),
        in_specs=[pl.BlockSpec((tm, o_mla.shape[1]), row),
                  pl.BlockSpec((tm, o_sb.shape[1]), row),
                  pl.BlockSpec((1, o_mla.shape[1]), full2),
                  pl.BlockSpec((1, o_sb.shape[1]), full2),
                  pl.BlockSpec((None,) + w_o.shape[1:], lambda i: (layer, 0, 0)),
                  pl.BlockSpec((tm, d), row),
                  modspec(2),
                  pl.BlockSpec((1, d), full2),
                  modspec(3),
                  modspec(4),
                  pl.BlockSpec(rw_t.shape, full2)],
        out_specs=[pl.BlockSpec((tm, d), row),
                   pl.BlockSpec((tm, d // 2), row),
                   pl.BlockSpec((N_EXPERTS, tm), lambda i: (0, i))],
        out_shape=[jax.ShapeDtypeStruct((t, d), F32),
                   jax.ShapeDtypeStruct((t, d // 2), jnp.uint32),
                   jax.ShapeDtypeStruct((N_EXPERTS, t), F32)],
        compiler_params=_params("arbitrary"),
        name="out_proj",
    )(o_mla, o_sb, gm, gs, w_o, x, mod, g2, mod, mod, rw_t)


def _route_kernel(lg_ref, rb_ref, tri_ref, e_ref, w_ref, rank_ref, cnt_ref, run_ref):
    i = pl.program_id(0)

    @pl.when(i == 0)
    def _():
        run_ref[...] = jnp.zeros_like(run_ref)

    scores = jax.nn.sigmoid(lg_ref[...])
    sel = scores + rb_ref[...]
    tm = sel.shape[1]
    epg = EXPERTS_PER_GROUP
    srow = [sel[e:e + 1, :] for e in range(N_EXPERTS)]
    prow = [scores[e:e + 1, :] for e in range(N_EXPERTS)]

    gscore = []
    for g in range(N_GROUPS):
        a, b, c, d = srow[g * epg:(g + 1) * epg]
        hi1, lo1 = jnp.maximum(a, b), jnp.minimum(a, b)
        hi2, lo2 = jnp.maximum(c, d), jnp.minimum(c, d)
        gscore.append(jnp.maximum(hi1, hi2)
                      + jnp.maximum(jnp.minimum(hi1, hi2), jnp.maximum(lo1, lo2)))
    best = gscore[0]
    gidx = jnp.zeros((1, tm), jnp.int32)
    for g in range(1, N_GROUPS):
        better = gscore[g] > best
        gidx = jnp.where(better, g, gidx)
        best = jnp.where(better, gscore[g], best)

    def pick(rows, j):
        out = rows[j]
        for g in range(1, N_GROUPS):
            out = jnp.where(gidx == g, rows[g * epg + j], out)
        return out

    v = [pick(srow, j) for j in range(epg)]
    p = [pick(prow, j) for j in range(epg)]
    pos = []
    for j in range(epg):
        r = jnp.zeros((1, tm), jnp.int32)
        for o in range(epg):
            if o == j:
                continue
            ahead = (v[o] > v[j]) | ((v[o] == v[j]) & (o < j))
            r = r + ahead.astype(jnp.int32)
        pos.append(r)
    local, wsel = [], []
    for k in range(TOP_K):
        lk = jnp.zeros((1, tm), jnp.int32)
        wk = jnp.zeros((1, tm), F32)
        for j in range(epg):
            hit = pos[j] == k
            lk = jnp.where(hit, j, lk)
            wk = jnp.where(hit, p[j], wk)
        local.append(lk)
        wsel.append(wk)
    wsum = wsel[0] + wsel[1]
    idx = [gidx * epg + local[k] for k in range(TOP_K)]

    eiota = lax.broadcasted_iota(jnp.int32, (N_EXPERTS, tm), 0)
    hit = [eiota == idx[k] for k in range(TOP_K)]
    onehot = (hit[0] | hit[1]).astype(F32)
    before = jnp.dot(onehot.astype(BF16), tri_ref[...], preferred_element_type=F32) + run_ref[...]
    for k in range(TOP_K):
        e_ref[pl.ds(k, 1), :] = idx[k]
        w_ref[pl.ds(k, 1), :] = wsel[k] / wsum
        rank_ref[pl.ds(k, 1), :] = jnp.sum(jnp.where(hit[k], before, 0.0), axis=0,
                                           keepdims=True).astype(jnp.int32)
    run_ref[...] += jnp.sum(onehot, axis=1, keepdims=True)
    cnt_ref[...] = jnp.broadcast_to(run_ref[...], cnt_ref.shape)


def _route(logits_t, router_b, tm=512):
    t = logits_t.shape[1]
    tm = min(tm, t)
    tri = (lax.broadcasted_iota(jnp.int32, (tm, tm), 0)
           < lax.broadcasted_iota(jnp.int32, (tm, tm), 1)).astype(BF16)
    blk = lambda rows: pl.BlockSpec((rows, tm), lambda i: (0, i))
    return pl.pallas_call(
        _route_kernel,
        grid=(t // tm,),
        in_specs=[blk(N_EXPERTS),
                  pl.BlockSpec((N_EXPERTS, 1), lambda i: (0, 0)),
                  pl.BlockSpec((tm, tm), lambda i: (0, 0))],
        out_specs=[blk(TOP_K), blk(TOP_K), blk(TOP_K),
                   pl.BlockSpec((N_EXPERTS, LANE), lambda i: (0, 0))],
        out_shape=[jax.ShapeDtypeStruct((TOP_K, t), jnp.int32),
                   jax.ShapeDtypeStruct((TOP_K, t), F32),
                   jax.ShapeDtypeStruct((TOP_K, t), jnp.int32),
                   jax.ShapeDtypeStruct((N_EXPERTS, LANE), F32)],
        scratch_shapes=[pltpu.VMEM((N_EXPERTS, 1), F32)],
        compiler_params=_params("arbitrary"),
        name="route",
    )(logits_t, router_b[:, None], tri)


def _dispatch_kernel(dest_ref, pend_ref, h_ref, o_ref, zero_ref, sem, zsem, *, tm, t):
    base = pl.program_id(0) * tm

    @pl.when(pl.program_id(0) == 0)
    def _():
        zero_ref[...] = jnp.zeros_like(zero_ref)

        def fill(e):
            end = pend_ref[e]
            start = pend_ref[e - 1] if e > 0 else 0
            dst = o_ref.at[pl.ds(pl.multiple_of(jnp.maximum(end - MOE_ROWS, 0), MOE_ROWS),
                                 MOE_ROWS), :]
            return end > start, pltpu.make_async_copy(zero_ref, dst, zsem)

        for e in range(N_EXPERTS):
            nonempty, cp = fill(e)
            pl.when(nonempty)(cp.start)
        for e in range(N_EXPERTS):
            nonempty, cp = fill(e)
            pl.when(nonempty)(cp.wait)

        def tail(r):
            dst = o_ref.at[pl.ds(pl.multiple_of(r * MOE_ROWS, MOE_ROWS), MOE_ROWS), :]
            return pltpu.make_async_copy(zero_ref, dst, zsem)

        first_free = pend_ref[N_EXPERTS - 1] // MOE_ROWS
        n_blocks = o_ref.shape[0] // MOE_ROWS
        lax.fori_loop(first_free, n_blocks, lambda r, c: (tail(r).start(), c)[1], 0)
        lax.fori_loop(first_free, n_blocks, lambda r, c: (tail(r).wait(), c)[1], 0)

    def issue(r, carry):
        for k in range(TOP_K):
            d = dest_ref[k * t + base + r]
            pltpu.make_async_copy(h_ref.at[pl.ds(r, 1), :], o_ref.at[pl.ds(d, 1), :], sem).start()
        return carry

    lax.fori_loop(0, tm, issue, 0, unroll=DMA_ISSUE_UNROLL)
    for k in range(TOP_K):
        pltpu.make_async_copy(h_ref, o_ref.at[pl.ds(0, tm), :], sem).wait()


def _dispatch(h2, dest_flat, pad_end, rows, tm=256):
    t, d = h2.shape
    return pl.pallas_call(
        functools.partial(_dispatch_kernel, tm=tm, t=t),
        grid_spec=pltpu.PrefetchScalarGridSpec(
            num_scalar_prefetch=2,
            grid=(t // tm,),
            in_specs=[pl.BlockSpec((tm, d), lambda i, dest, pend: (i, 0))],
            out_specs=pl.BlockSpec(memory_space=pl.ANY),
            scratch_shapes=[pltpu.VMEM((MOE_ROWS, d), h2.dtype),
                            pltpu.SemaphoreType.DMA(()), pltpu.SemaphoreType.DMA(())]),
        out_shape=jax.ShapeDtypeStruct((rows, d), h2.dtype),
        compiler_params=_params("arbitrary"),
        name="moe_dispatch",
    )(dest_flat, pad_end, h2)


def _expert_kernel(be_ref, nu_ref, x_ref, wg_ref, wu_ref, wd_ref, y_ref, wgb, wub, wdb):
    r = pl.program_id(0)
    live = r < nu_ref[0]
    new_expert = (r == 0) | (be_ref[jnp.maximum(r - 1, 0)] != be_ref[r])

    @pl.when(live & new_expert)
    def _():
        wgb[...] = wg_ref[...].astype(BF16)
        wub[...] = wu_ref[...].astype(BF16)
        wdb[...] = wd_ref[...].astype(BF16)

    @pl.when(live)
    def _():
        lo, hi = _unpack_rows(x_ref[...])
        xb = jnp.concatenate([lo.astype(BF16), hi.astype(BF16)], axis=1)
        g = jnp.dot(xb, wgb[...], preferred_element_type=F32)
        u = jnp.dot(xb, wub[...], preferred_element_type=F32)
        a = (g * jax.nn.sigmoid(g)) * u
        y_ref[...] = _pack_rows(jnp.dot(a.astype(BF16), wdb[...], preferred_element_type=F32))

    @pl.when(jnp.logical_not(live))
    def _():
        y_ref[...] = jnp.zeros_like(y_ref)


def _experts(x_sorted, block_e, n_used, w_gate, w_up, w_down, layer):
    rows, d = x_sorted.shape
    de = w_gate.shape[-1]
    dm = w_gate.shape[-2]
    bm = MOE_ROWS
    blk = lambda r, be, nu: (jnp.minimum(r, nu[0] - 1), 0)
    return pl.pallas_call(
        _expert_kernel,
        grid_spec=pltpu.PrefetchScalarGridSpec(
            num_scalar_prefetch=2,
            grid=(rows // bm,),
            in_specs=[pl.BlockSpec((bm, d), blk),
                      pl.BlockSpec((None, None, dm, de), lambda r, be, nu: (layer, be[r], 0, 0)),
                      pl.BlockSpec((None, None, dm, de), lambda r, be, nu: (layer, be[r], 0, 0)),
                      pl.BlockSpec((None, None, de, dm), lambda r, be, nu: (layer, be[r], 0, 0))],
            out_specs=pl.BlockSpec((bm, d), lambda r, be, nu: (r, 0)),
            scratch_shapes=[pltpu.VMEM((dm, de), BF16), pltpu.VMEM((dm, de), BF16),
                            pltpu.VMEM((de, dm), BF16)]),
        out_shape=jax.ShapeDtypeStruct((rows, d), jnp.uint32),
        compiler_params=_params("arbitrary"),
        name="moe_experts",
    )(block_e, n_used, x_sorted, w_gate, w_up, w_down)


def _combine_kernel(dest_ref, y_ref, x_ref, gate_ref, w_ref, o_ref, buf, sems, *, tm, t):
    i = pl.program_id(0)
    slot = i % 2

    def gather(tile, dst_slot):
        base = tile * tm

        def issue(r, carry):
            for k in range(TOP_K):
                d = dest_ref[k * t + base + r]
                pltpu.make_async_copy(y_ref.at[pl.ds(d, 1), :],
                                      buf.at[dst_slot, k, pl.ds(r, 1), :],
                                      sems.at[dst_slot]).start()
            return carry

        lax.fori_loop(0, tm, issue, 0, unroll=DMA_ISSUE_UNROLL)

    @pl.when(i == 0)
    def _():
        gather(0, 0)

    @pl.when(i + 1 < pl.num_programs(0))
    def _():
        gather(i + 1, 1 - slot)

    for k in range(TOP_K):
        pltpu.make_async_copy(y_ref.at[pl.ds(0, tm), :], buf.at[slot, k], sems.at[slot]).wait()
    w = w_ref[...]
    half = buf.shape[3]
    lo0, hi0 = _unpack_rows(buf[slot, 0])
    lo1, hi1 = _unpack_rows(buf[slot, 1])
    o_ref[:, :half] = x_ref[:, :half] + gate_ref[:, :half] * (w[:, 0:1] * lo0 + w[:, 1:2] * lo1)
    o_ref[:, half:] = x_ref[:, half:] + gate_ref[:, half:] * (w[:, 0:1] * hi0 + w[:, 1:2] * hi1)


def _combine(y_sorted, dest_flat, x, mod, w_tok, seq, tm=256):
    t, d = x.shape
    per_b = seq // tm
    return pl.pallas_call(
        functools.partial(_combine_kernel, tm=tm, t=t),
        grid_spec=pltpu.PrefetchScalarGridSpec(
            num_scalar_prefetch=1,
            grid=(t // tm,),
            in_specs=[pl.BlockSpec(memory_space=pl.ANY),
                      pl.BlockSpec((tm, d), lambda i, dest: (i, 0)),
                      pl.BlockSpec((None, 1, d), lambda i, dest: ((i // per_b) * 6 + 5, 0, 0)),
                      pl.BlockSpec((tm, TOP_K), lambda i, dest: (i, 0))],
            out_specs=pl.BlockSpec((tm, d), lambda i, dest: (i, 0)),
            scratch_shapes=[pltpu.VMEM((2, TOP_K, tm, y_sorted.shape[1]), y_sorted.dtype),
                            pltpu.SemaphoreType.DMA((2,))]),
        out_shape=jax.ShapeDtypeStruct((t, d), F32),
        compiler_params=_params("arbitrary"),
        name="moe_combine",
    )(dest_flat, y_sorted, x, mod, w_tok)


def _rope_tables(seq):
    half = MLA_ROPE // 2
    inv = 1.0 / (ROPE_BASE ** (jnp.arange(0, MLA_ROPE, 2, dtype=F32) / MLA_ROPE))
    ang = jnp.arange(seq, dtype=F32)[:, None] * inv[None, :]
    cos, sin = jnp.cos(ang), jnp.sin(ang)
    z = jnp.zeros((seq, half), F32)
    zz = jnp.zeros((seq, LANE - MLA_ROPE), F32)
    cos_t = jnp.concatenate([cos, cos, zz], axis=1)
    s1_t = jnp.concatenate([z, sin, zz], axis=1)
    s2_t = jnp.concatenate([-sin, z, zz], axis=1)
    return cos_t, s1_t, s2_t


def _pad_last(a, width):
    return jnp.pad(a, [(0, 0)] * (a.ndim - 1) + [(0, width - a.shape[-1])])


def kernel(x, c, norm1_g, norm2_g, w_ada, b_ada, w_in, q_lora_g, kv_lora_g, w_uq, w_ukv,
           q_head_g, k_head_g, mla_out_g, sb_out_g, w_o, router_w, router_b,
           w_gate, w_up, w_down):
    bsz, seq, d = x.shape
    t = bsz * seq
    depth = w_ada.shape[0]

    sb_col0 = COL_KR + MLA_ROPE
    w_in_mla = _pad_last(w_in[..., :sb_col0], MLA_IN_COLS).astype(BF16)
    sb_col_scale = jnp.where(jnp.arange(SB_IN_COLS) < SB_WIDTH, SB_DIM ** -0.5 * LOG2_E, 1.0)
    w_in_sb = (w_in[..., sb_col0:] * sb_col_scale.astype(F32)).astype(BF16)
    wq = w_uq.reshape(depth, Q_LORA, MLA_HEADS, MLA_QK)
    half = MLA_ROPE // 2
    swap = lambda a: jnp.concatenate([a[..., half:], a[..., :half]], axis=-1)
    wq_rope = wq[..., MLA_NOPE:]
    wuq_p = jnp.concatenate(
        [wq[..., :MLA_NOPE].reshape(depth, Q_LORA, -1),
         _pad_last(wq_rope, LANE).reshape(depth, Q_LORA, -1),
         _pad_last(swap(wq_rope), LANE).reshape(depth, Q_LORA, -1)], axis=-1).astype(BF16)
    wkv = w_ukv.reshape(depth, KV_LORA, MLA_HEADS, MLA_NOPE + MLA_V)
    wukv_p = jnp.concatenate(
        [wkv[..., :MLA_NOPE].reshape(depth, KV_LORA, -1),
         wkv[..., MLA_NOPE:].reshape(depth, KV_LORA, -1)], axis=-1).astype(BF16)
    w_o_b = w_o.astype(BF16)
    q_scale = MLA_QK ** -0.5 * LOG2_E
    gqn = (q_head_g[:, None, :MLA_NOPE] * q_scale)
    gq_rope = q_head_g[:, None, MLA_NOPE:] * q_scale
    gqr = jnp.concatenate([_pad_last(gq_rope, LANE), _pad_last(swap(gq_rope), LANE)], axis=-1)
    gkn = k_head_g[:, None, :MLA_NOPE]
    gkr = _pad_last(k_head_g[:, None, MLA_NOPE:], LANE)
    rope_tabs = _rope_tables(seq)
    rw_t = router_w.T.astype(BF16)

    mod_all = _ada_modulation(c[:, :, None], w_ada, b_ada[:, None, :], layers=depth)
    n_blocks = -(-(t * TOP_K) // MOE_ROWS) + N_EXPERTS
    rows = n_blocks * MOE_ROWS

    xt = x.reshape(t, d)
    for l in range(depth):
        mod = mod_all[l].reshape(bsz * 6, 1, d)
        proj_mla, proj_sb = _in_proj(xt, norm1_g[l][None, :], mod, w_in_mla, w_in_sb, l, seq)
        q2d, k2d, v2d = _mla_prep(proj_mla, rope_tabs, q_lora_g[:, None, :],
                                  kv_lora_g[:, None, :], wuq_p, wukv_p, gqn, gqr, gkn, gkr, l, seq)
        o_mla = _mla_attention(q2d, k2d, v2d, bsz, seq)
        o_sb = _sb_attention(proj_sb, bsz, seq)
        xt, h2, logits_t = _out_proj(o_mla, o_sb, mla_out_g[l][None, :], sb_out_g[l][None, :],
                                     w_o_b, xt, mod, norm2_g[l][None, :], rw_t, l, seq)
        e_idx, w_tok, rank, cnt = _route(logits_t, router_b)
        counts = cnt[:, 0].astype(jnp.int32)
        padded = (counts + MOE_ROWS - 1) // MOE_ROWS * MOE_ROWS
        pad_end = jnp.cumsum(padded)
        pad_start = pad_end - padded
        eids = jnp.arange(N_EXPERTS, dtype=jnp.int32)
        seg_start = jnp.sum(jnp.where(e_idx[None] == eids[:, None, None],
                                      pad_start[:, None, None], 0), axis=0)
        dest = (seg_start + rank).reshape(TOP_K * t).astype(jnp.int32)
        n_used = (pad_end[-1:] // MOE_ROWS).astype(jnp.int32)
        first_row = jnp.minimum(jnp.arange(n_blocks, dtype=jnp.int32), n_used - 1) * MOE_ROWS
        block_e = jnp.sum(pad_end[None, :] <= first_row[:, None], axis=1).astype(jnp.int32)
        x_sorted = _dispatch(h2, dest, pad_end.astype(jnp.int32), rows)
        y_sorted = _experts(x_sorted, block_e, n_used, w_gate, w_up, w_down, l)
        xt = _combine(y_sorted, dest, xt, mod, w_tok.T, seq)
    return xt.reshape(bsz, seq, d)
```

```python
import functools

import jax
import jax.numpy as jnp
from jax import lax
from jax.experimental import pallas as pl
from jax.experimental.pallas import tpu as pltpu

F32 = jnp.float32
BF16 = jnp.bfloat16

EPS = 1e-6
ROPE_BASE = 10000.0
CHUNK = 64

MLA_HEADS = 8
MLA_NOPE = 128
MLA_ROPE = 64
MLA_QK = MLA_NOPE + MLA_ROPE
MLA_V = 128
Q_LORA = 512
KV_LORA = 256
SB_HEADS = 8
SB_DIM = 128
SB_WIDTH = SB_HEADS * SB_DIM
LANE = 128
QK_PAD = 2 * LANE
COL_CQ = 0
COL_CKV = Q_LORA
COL_KR = Q_LORA + KV_LORA
MLA_IN_COLS = COL_KR + LANE
COL_SQ = 0
COL_SK = SB_WIDTH
COL_SV = 2 * SB_WIDTH
SB_IN_COLS = 3 * SB_WIDTH

N_EXPERTS = 16
N_GROUPS = 4
EXPERTS_PER_GROUP = N_EXPERTS // N_GROUPS
TOP_K = 2
D_EXPERT = 512
MOE_ROWS = 512
DMA_ISSUE_UNROLL = 8
LOG2_E = 1.4426950408889634
SB_UNDERFLOW_LOG2 = 150.0
SB_FINISHED_LOG2 = 1e30

VMEM_LIMIT = 56 * 1024 * 1024


def _params(*dims):
    return pltpu.CompilerParams(dimension_semantics=dims, vmem_limit_bytes=VMEM_LIMIT)


def _rms(x):
    return x * lax.rsqrt(jnp.mean(x * x, axis=-1, keepdims=True) + EPS)


def _pack_rows(x):
    n = x.shape[1] // 2
    lo = lax.bitcast_convert_type(x[:, :n].astype(BF16).astype(F32), jnp.uint32)
    hi = lax.bitcast_convert_type(x[:, n:].astype(BF16).astype(F32), jnp.uint32)
    return (lo >> 16) | (hi & jnp.uint32(0xFFFF0000))


def _unpack_rows(w):
    lo = lax.bitcast_convert_type(w << 16, F32)
    hi = lax.bitcast_convert_type(w & jnp.uint32(0xFFFF0000), F32)
    return lo, hi


def _ada_kernel(c_ref, w_ref, b_ref, o_ref):
    w = w_ref[...]
    for b in range(c_ref.shape[0]):
        cb = c_ref[b]
        cb = cb * jax.nn.sigmoid(cb)
        o_ref[pl.ds(b, 1), :] = jnp.sum(cb * w, axis=0, keepdims=True) + b_ref[...]


def _ada_modulation(c_col, w_ada, b_ada3, layers, tn=1024):
    _, d, n = w_ada.shape
    bsz = c_col.shape[0]
    return pl.pallas_call(
        _ada_kernel,
        grid=(layers, n // tn),
        in_specs=[pl.BlockSpec(c_col.shape, lambda l, j: (0, 0, 0)),
                  pl.BlockSpec((None, d, tn), lambda l, j: (l, 0, j)),
                  pl.BlockSpec((None, 1, tn), lambda l, j: (l, 0, j))],
        out_specs=pl.BlockSpec((None, bsz, tn), lambda l, j: (l, 0, j)),
        out_shape=jax.ShapeDtypeStruct((layers, bsz, n), F32),
        compiler_params=_params("arbitrary", "arbitrary"),
        name="ada_modulation",
    )(c_col, w_ada, b_ada3)


def _in_proj_kernel(x_ref, g_ref, sh_ref, sc_ref, wa_ref, wb_ref, oa_ref, ob_ref):
    h = _rms(x_ref[...]) * g_ref[...]
    h = (h * (1.0 + sc_ref[...]) + sh_ref[...]).astype(BF16)
    oa_ref[...] = jnp.dot(h, wa_ref[...], preferred_element_type=F32).astype(oa_ref.dtype)
    sb = jnp.dot(h, wb_ref[...], preferred_element_type=F32)
    ob_ref[:, :SB_WIDTH] = (sb[:, :SB_WIDTH] * (SB_DIM ** -0.5 * LOG2_E)).astype(ob_ref.dtype)
    ob_ref[:, SB_WIDTH:] = sb[:, SB_WIDTH:].astype(ob_ref.dtype)


def _in_proj(x, g, mod, w_mla, w_sb, layer, seq, tm=256):
    t, d = x.shape
    na, nb = w_mla.shape[-1], w_sb.shape[-1]
    per_b = seq // tm
    return pl.pallas_call(
        _in_proj_kernel,
        grid=(t // tm,),
        in_specs=[pl.BlockSpec((tm, d), lambda i: (i, 0)),
                  pl.BlockSpec((1, d), lambda i: (0, 0)),
                  pl.BlockSpec((None, 1, d), lambda i: ((i // per_b) * 6 + 0, 0, 0)),
                  pl.BlockSpec((None, 1, d), lambda i: ((i // per_b) * 6 + 1, 0, 0)),
                  pl.BlockSpec((None, d, na), lambda i: (layer, 0, 0)),
                  pl.BlockSpec((None, d, nb), lambda i: (layer, 0, 0))],
        out_specs=[pl.BlockSpec((tm, na), lambda i: (i, 0)),
                   pl.BlockSpec((tm, nb), lambda i: (i, 0))],
        out_shape=[jax.ShapeDtypeStruct((t, na), BF16),
                   jax.ShapeDtypeStruct((t, nb), BF16)],
        compiler_params=_params("arbitrary"),
        name="in_proj",
    )(x, g, mod, mod, w_mla, w_sb)


def _mla_prep_kernel(cq_ref, ckv_ref, kr_ref, cos_ref, s1_ref, s2_ref, gql_ref, gkvl_ref,
                     wuq_ref, wukv_ref, gqn_ref, gqr_ref, gkn_ref, gkr_ref,
                     q_ref, k_ref, v_ref):
    nope_w = MLA_HEADS * MLA_NOPE
    cqn = _rms(cq_ref[...].astype(F32)) * gql_ref[...]
    q = jnp.dot(cqn.astype(BF16), wuq_ref[...], preferred_element_type=F32)
    ckvn = _rms(ckv_ref[...].astype(F32)) * gkvl_ref[...]
    kv = jnp.dot(ckvn.astype(BF16), wukv_ref[...], preferred_element_type=F32)
    cosv, s1, s2 = cos_ref[...], s1_ref[...], s2_ref[...]

    def rope(x):
        return (x * cosv + pltpu.roll(x, MLA_ROPE // 2, 1) * s1
                + pltpu.roll(x, LANE - MLA_ROPE // 2, 1) * s2)

    kr = kr_ref[...].astype(F32)
    kr_ss = jnp.sum(kr * kr, axis=-1, keepdims=True)
    kr_roped = rope(kr * gkr_ref[...])
    gqn, gkn = gqn_ref[...], gkn_ref[...]
    gqr, gqs = gqr_ref[:, :LANE], gqr_ref[:, LANE:]
    s_comb = s1 + s2
    for h in range(MLA_HEADS):
        qn = q[:, h * LANE:(h + 1) * LANE]
        qr = q[:, nope_w + h * LANE:nope_w + (h + 1) * LANE]
        ss = jnp.sum(qn * qn + qr * qr, axis=-1, keepdims=True)
        r = lax.rsqrt(ss * (1.0 / MLA_QK) + EPS)
        q_ref[:, h * QK_PAD:h * QK_PAD + LANE] = (qn * r * gqn).astype(BF16)
        qs = q[:, 2 * nope_w + h * LANE:2 * nope_w + (h + 1) * LANE]
        q_rope = (qr * gqr) * cosv + (qs * gqs) * s_comb
        q_ref[:, h * QK_PAD + LANE:(h + 1) * QK_PAD] = (q_rope * r).astype(BF16)
        kn = kv[:, h * LANE:(h + 1) * LANE]
        rk = lax.rsqrt((jnp.sum(kn * kn, axis=-1, keepdims=True) + kr_ss) * (1.0 / MLA_QK) + EPS)
        k_ref[:, h * QK_PAD:h * QK_PAD + LANE] = (kn * rk * gkn).astype(BF16)
        k_ref[:, h * QK_PAD + LANE:(h + 1) * QK_PAD] = (kr_roped * rk).astype(BF16)
        v_ref[:, 2 * h * MLA_V:(2 * h + 1) * MLA_V] = (
            kv[:, nope_w + h * MLA_V:nope_w + (h + 1) * MLA_V].astype(BF16))
        v_ref[:, (2 * h + 1) * MLA_V:(2 * h + 2) * MLA_V] = jnp.ones((kn.shape[0], MLA_V), BF16)


def _mla_prep(proj, rope_tabs, gql, gkvl, wuq, wukv, gqn, gqr, gkn, gkr, layer, seq, tm=256):
    t = proj.shape[0]
    per_b = seq // tm
    cos_t, s1_t, s2_t = rope_tabs
    row = lambda i: (i, 0)
    full2 = lambda i: (0, 0)
    lay = lambda i: (layer, 0, 0)
    tab = lambda i: (i % per_b, 0)
    return pl.pallas_call(
        _mla_prep_kernel,
        grid=(t // tm,),
        in_specs=[pl.BlockSpec((tm, Q_LORA), lambda i: (i, COL_CQ // Q_LORA)),
                  pl.BlockSpec((tm, KV_LORA), lambda i: (i, COL_CKV // KV_LORA)),
                  pl.BlockSpec((tm, LANE), lambda i: (i, COL_KR // LANE)),
                  pl.BlockSpec((tm, LANE), tab),
                  pl.BlockSpec((tm, LANE), tab),
                  pl.BlockSpec((tm, LANE), tab),
                  pl.BlockSpec((None, 1, Q_LORA), lay),
                  pl.BlockSpec((None, 1, KV_LORA), lay),
                  pl.BlockSpec((None,) + wuq.shape[1:], lay),
                  pl.BlockSpec((None,) + wukv.shape[1:], lay),
                  pl.BlockSpec((None, 1, LANE), lay),
                  pl.BlockSpec((None, 1, 2 * LANE), lay),
                  pl.BlockSpec((None, 1, LANE), lay),
                  pl.BlockSpec((None, 1, LANE), lay)],
        out_specs=[pl.BlockSpec((tm, MLA_HEADS * QK_PAD), row),
                   pl.BlockSpec((tm, MLA_HEADS * QK_PAD), row),
                   pl.BlockSpec((tm, MLA_HEADS * 2 * MLA_V), row)],
        out_shape=[jax.ShapeDtypeStruct((t, MLA_HEADS * QK_PAD), BF16),
                   jax.ShapeDtypeStruct((t, MLA_HEADS * QK_PAD), BF16),
                   jax.ShapeDtypeStruct((t, MLA_HEADS * 2 * MLA_V), BF16)],
        compiler_params=_params("arbitrary"),
        name="mla_prep",
    )(proj, proj, proj, cos_t, s1_t, s2_t, gql, gkvl, wuq, wukv, gqn, gqr, gkn, gkr)


def _mla_attn_kernel(q_ref, k_ref, v_ref, o_ref, m_ref, l_ref, acc_ref, sa_ref, sb_ref, *,
                     tq, heads):
    i = pl.program_id(2)
    vw = 2 * MLA_V

    def scores(g, kb):
        q = q_ref[:, g * QK_PAD:(g + 1) * QK_PAD]
        k = k_ref[pl.ds(pl.multiple_of(kb * tq, tq), tq), g * QK_PAD:(g + 1) * QK_PAD]
        return lax.dot_general(q, k, (((1,), (1,)), ((), ())), preferred_element_type=F32)

    def probs(s, m):
        return jnp.concatenate(
            [jnp.exp2(s[:, c * LANE:(c + 1) * LANE] - m) for c in range(tq // LANE)],
            axis=1).astype(BF16)

    def pv(g, p, kb):
        v = v_ref[pl.ds(pl.multiple_of(kb * tq, tq), tq), g * vw:(g + 1) * vw]
        return jnp.dot(p, v, preferred_element_type=F32)

    def update(g, s, kb):
        m_old = m_ref[g]
        m_new = jnp.maximum(m_old, jnp.max(s, axis=-1, keepdims=True))
        alpha = jnp.exp2(m_old - m_new)
        r = pv(g, probs(s, m_new), kb)
        acc_ref[g] = alpha * acc_ref[g] + r[:, :MLA_V]
        l_ref[g] = alpha * l_ref[g] + r[:, MLA_V:]
        m_ref[g] = m_new

    q_chunk = lax.broadcasted_iota(jnp.int32, (tq, 1), 0) // CHUNK
    k_chunk = lax.broadcasted_iota(jnp.int32, (1, tq), 1) // CHUNK
    visible = k_chunk <= q_chunk
    for g in range(heads):
        s = jnp.where(visible, scores(g, i), -jnp.inf)
        m = jnp.broadcast_to(jnp.max(s, axis=-1, keepdims=True), (tq, LANE))
        r = pv(g, probs(s, m), i)
        m_ref[g] = m
        acc_ref[g] = r[:, :MLA_V]
        l_ref[g] = r[:, MLA_V:]
        sa_ref[g] = scores(g, 0)

    last = jnp.maximum(i - 1, 0)

    def pair(t, carry):
        j = 2 * t
        for g in range(heads):
            sb_ref[g] = scores(g, jnp.minimum(j + 1, last))
            update(g, sa_ref[g], j)
        for g in range(heads):
            sa_ref[g] = scores(g, jnp.minimum(j + 2, last))
            update(g, sb_ref[g], j + 1)
        return carry

    lax.fori_loop(0, i // 2, pair, 0)

    @pl.when(i % 2 == 1)
    def _():
        for g in range(heads):
            update(g, sa_ref[g], i - 1)

    for g in range(heads):
        o_ref[:, g * MLA_V:(g + 1) * MLA_V] = acc_ref[g] / l_ref[g]


def _mla_attention(q2d, k2d, v2d, bsz, seq, tq=512, heads=2):
    t = q2d.shape[0]
    tq = min(tq, seq)
    nq = seq // tq
    vw = 2 * MLA_V
    return pl.pallas_call(
        functools.partial(_mla_attn_kernel, tq=tq, heads=heads),
        grid=(bsz, MLA_HEADS // heads, nq),
        in_specs=[pl.BlockSpec((tq, heads * QK_PAD), lambda b, h, i: (b * nq + i, h)),
                  pl.BlockSpec((seq, heads * QK_PAD), lambda b, h, i: (b, h)),
                  pl.BlockSpec((seq, heads * vw), lambda b, h, i: (b, h))],
        out_specs=pl.BlockSpec((tq, heads * MLA_V), lambda b, h, i: (b * nq + i, h)),
        out_shape=jax.ShapeDtypeStruct((t, MLA_HEADS * MLA_V), F32),
        scratch_shapes=[pltpu.VMEM((heads, tq, LANE), F32), pltpu.VMEM((heads, tq, MLA_V), F32),
                        pltpu.VMEM((heads, tq, MLA_V), F32),
                        pltpu.VMEM((heads, tq, tq), F32), pltpu.VMEM((heads, tq, tq), F32)],
        compiler_params=_params("arbitrary", "arbitrary", "arbitrary"),
        name="mla_attention",
    )(q2d, k2d, v2d)


def _sb_attn_kernel(q_ref, k_ref, v_ref, tri_ref, o_ref, run_ref, acc_ref, *, tq, tk, heads):
    i = pl.program_id(2)
    tri = tri_ref[...]
    sub = tq // tk
    streams = [(g, r) for g in range(heads) for r in range(sub)]

    def block(s, kb, mask):
        g, r = streams[s]
        cols = slice(g * SB_DIM, (g + 1) * SB_DIM)
        keys = pl.ds(pl.multiple_of(kb * tk, tk), tk)
        q = q_ref[r * tk:(r + 1) * tk, cols]
        z = lax.dot_general(q, k_ref[keys, cols], (((1,), (1,)), ((), ())),
                            preferred_element_type=F32)
        neg_abs = lax.bitcast_convert_type(
            lax.bitcast_convert_type(z, jnp.uint32) | jnp.uint32(0x80000000), F32)
        cost = jnp.maximum(z, 0.0) + jnp.log2(1.0 + jnp.exp2(neg_abs))
        if mask is not None:
            cost = jnp.where(mask, cost, 0.0)
        after = jnp.dot(cost.astype(BF16), tri, preferred_element_type=F32) + run_ref[s]
        a = jnp.exp2(z - cost - after)
        if mask is not None:
            a = jnp.where(mask, a, 0.0)
        acc_ref[s] += jnp.dot(a.astype(BF16), v_ref[keys, cols], preferred_element_type=F32)
        run_ref[s] += jnp.sum(cost, axis=-1, keepdims=True)

    def more():
        return jnp.min(run_ref[...]) < SB_UNDERFLOW_LOG2

    run_ref[...] = jnp.zeros_like(run_ref)
    acc_ref[...] = jnp.zeros_like(acc_ref)
    strictly_before = (lax.broadcasted_iota(jnp.int32, (tk, tk), 1)
                       < lax.broadcasted_iota(jnp.int32, (tk, tk), 0))
    for s, (g, r) in enumerate(streams):
        block(s, i * sub + r, strictly_before)

    def cond(c):
        return (c[0] < i * sub + sub - 1) & c[1]

    def body(c):
        for s, (g, r) in enumerate(streams):
            kb = i * sub + r - 1 - c[0]
            run_ref[s] = jnp.where(kb >= 0, run_ref[s], SB_FINISHED_LOG2)
            block(s, jnp.maximum(kb, 0), None)
        return c[0] + 1, more()

    lax.while_loop(cond, body, (jnp.int32(0), more()))
    for s, (g, r) in enumerate(streams):
        o_ref[r * tk:(r + 1) * tk, g * SB_DIM:(g + 1) * SB_DIM] = acc_ref[s]


def _sb_attention(proj, bsz, seq, tq=512, tk=256, heads=4):
    t = proj.shape[0]
    tq = min(tq, seq)
    nq = seq // tq
    tri = (lax.broadcasted_iota(jnp.int32, (tk, tk), 0)
           > lax.broadcasted_iota(jnp.int32, (tk, tk), 1)).astype(BF16)
    w = heads * SB_DIM
    cq, ck, cv = COL_SQ // w, COL_SK // w, COL_SV // w
    n_streams = heads * (tq // tk)
    return pl.pallas_call(
        functools.partial(_sb_attn_kernel, tq=tq, tk=tk, heads=heads),
        grid=(bsz, SB_HEADS // heads, nq),
        in_specs=[pl.BlockSpec((tq, w), lambda b, h, i: (b * nq + i, cq + h)),
                  pl.BlockSpec((seq, w), lambda b, h, i: (b, ck + h)),
                  pl.BlockSpec((seq, w), lambda b, h, i: (b, cv + h)),
                  pl.BlockSpec((tk, tk), lambda b, h, i: (0, 0))],
        out_specs=pl.BlockSpec((tq, w), lambda b, h, i: (b * nq + i, h)),
        out_shape=jax.ShapeDtypeStruct((t, SB_WIDTH), F32),
        scratch_shapes=[pltpu.VMEM((n_streams, tk, 1), F32),
                        pltpu.VMEM((n_streams, tk, SB_DIM), F32)],
        compiler_params=_params("arbitrary", "arbitrary", "arbitrary"),
        name="sb_attention",
    )(proj, proj, proj, tri)


def _out_proj_kernel(om_ref, os_ref, gm_ref, gs_ref, wo_ref, x_ref, gate_ref, g2_ref,
                     sh_ref, sc_ref, rw_ref, xo_ref, h2_ref, lg_ref):
    wm = om_ref.shape[1]
    nm = (_rms(om_ref[...]) * gm_ref[...]).astype(BF16)
    ns = (_rms(os_ref[...]) * gs_ref[...]).astype(BF16)
    y = (jnp.dot(nm, wo_ref[:wm, :], preferred_element_type=F32)
         + jnp.dot(ns, wo_ref[wm:, :], preferred_element_type=F32))
    xn = x_ref[...] + gate_ref[...] * y
    xo_ref[...] = xn
    h2 = _rms(xn) * g2_ref[...]
    h2 = h2 * (1.0 + sc_ref[...]) + sh_ref[...]
    h2_ref[...] = _pack_rows(h2)
    lg_ref[...] = lax.dot_general(rw_ref[...], h2.astype(BF16), (((1,), (1,)), ((), ())),
                                  preferred_element_type=F32)


def _out_proj(o_mla, o_sb, gm, gs, w_o, x, mod, g2, rw_t, layer, seq, tm=256):
    t, d = x.shape
    per_b = seq // tm
    row = lambda i: (i, 0)
    full2 = lambda i: (0, 0)
    modspec = lambda j: pl.BlockSpec((None, 1, d), lambda i: ((i // per_b) * 6 + j, 0, 0))
    return pl.pallas_call(
        _out_proj_kernel,
        grid=(t // tm,),
        in_specs=[pl.BlockSpec((tm, o_mla.shape[1]), row),
                  pl.BlockSpec((tm, o_sb.shape[1]), row),
                  pl.BlockSpec((1, o_mla.shape[1]), full2),
                  pl.BlockSpec((1, o_sb.shape[1]), full2),
                  pl.BlockSpec((None,) + w_o.shape[1:], lambda i: (layer, 0, 0)),
                  pl.BlockSpec((tm, d), row),
                  modspec(2),
                  pl.BlockSpec((1, d), full2),
                  modspec(3),
                  modspec(4),
                  pl.BlockSpec(rw_t.shape, full2)],
        out_specs=[pl.BlockSpec((tm, d), row),
                   pl.BlockSpec((tm, d // 2), row),
                   pl.BlockSpec((N_EXPERTS, tm), lambda i: (0, i))],
        out_shape=[jax.ShapeDtypeStruct((t, d), F32),
                   jax.ShapeDtypeStruct((t, d // 2), jnp.uint32),
                   jax.ShapeDtypeStruct((N_EXPERTS, t), F32)],
        compiler_params=_params("arbitrary"),
        name="out_proj",
    )(o_mla, o_sb, gm, gs, w_o, x, mod, g2, mod, mod, rw_t)


def _route_kernel(lg_ref, rb_ref, tri_ref, e_ref, w_ref, rank_ref, cnt_ref, run_ref):
    i = pl.program_id(0)

    @pl.when(i == 0)
    def _():
        run_ref[...] = jnp.zeros_like(run_ref)

    scores = jax.nn.sigmoid(lg_ref[...])
    sel = scores + rb_ref[...]
    tm = sel.shape[1]
    epg = EXPERTS_PER_GROUP
    srow = [sel[e:e + 1, :] for e in range(N_EXPERTS)]
    prow = [scores[e:e + 1, :] for e in range(N_EXPERTS)]

    gscore = []
    for g in range(N_GROUPS):
        a, b, c, d = srow[g * epg:(g + 1) * epg]
        hi1, lo1 = jnp.maximum(a, b), jnp.minimum(a, b)
        hi2, lo2 = jnp.maximum(c, d), jnp.minimum(c, d)
        gscore.append(jnp.maximum(hi1, hi2)
                      + jnp.maximum(jnp.minimum(hi1, hi2), jnp.maximum(lo1, lo2)))
    best = gscore[0]
    gidx = jnp.zeros((1, tm), jnp.int32)
    for g in range(1, N_GROUPS):
        better = gscore[g] > best
        gidx = jnp.where(better, g, gidx)
        best = jnp.where(better, gscore[g], best)

    def pick(rows, j):
        out = rows[j]
        for g in range(1, N_GROUPS):
            out = jnp.where(gidx == g, rows[g * epg + j], out)
        return out

    v = [pick(srow, j) for j in range(epg)]
    p = [pick(prow, j) for j in range(epg)]
    pos = []
    for j in range(epg):
        r = jnp.zeros((1, tm), jnp.int32)
        for o in range(epg):
            if o == j:
                continue
            ahead = (v[o] > v[j]) | ((v[o] == v[j]) & (o < j))
            r = r + ahead.astype(jnp.int32)
        pos.append(r)
    local, wsel = [], []
    for k in range(TOP_K):
        lk = jnp.zeros((1, tm), jnp.int32)
        wk = jnp.zeros((1, tm), F32)
        for j in range(epg):
            hit = pos[j] == k
            lk = jnp.where(hit, j, lk)
            wk = jnp.where(hit, p[j], wk)
        local.append(lk)
        wsel.append(wk)
    wsum = wsel[0] + wsel[1]
    idx = [gidx * epg + local[k] for k in range(TOP_K)]

    eiota = lax.broadcasted_iota(jnp.int32, (N_EXPERTS, tm), 0)
    hit = [eiota == idx[k] for k in range(TOP_K)]
    onehot = (hit[0] | hit[1]).astype(F32)
    before = jnp.dot(onehot.astype(BF16), tri_ref[...], preferred_element_type=F32) + run_ref[...]
    for k in range(TOP_K):
        e_ref[pl.ds(k, 1), :] = idx[k]
        w_ref[pl.ds(k, 1), :] = wsel[k] / wsum
        rank_ref[pl.ds(k, 1), :] = jnp.sum(jnp.where(hit[k], before, 0.0), axis=0,
                                           keepdims=True).astype(jnp.int32)
    run_ref[...] += jnp.sum(onehot, axis=1, keepdims=True)
    cnt_ref[...] = jnp.broadcast_to(run_ref[...], cnt_ref.shape)


def _route(logits_t, router_b, tm=512):
    t = logits_t.shape[1]
    tm = min(tm, t)
    tri = (lax.broadcasted_iota(jnp.int32, (tm, tm), 0)
           < lax.broadcasted_iota(jnp.int32, (tm, tm), 1)).astype(BF16)
    blk = lambda rows: pl.BlockSpec((rows, tm), lambda i: (0, i))
    return pl.pallas_call(
        _route_kernel,
        grid=(t // tm,),
        in_specs=[blk(N_EXPERTS),
                  pl.BlockSpec((N_EXPERTS, 1), lambda i: (0, 0)),
                  pl.BlockSpec((tm, tm), lambda i: (0, 0))],
        out_specs=[blk(TOP_K), blk(TOP_K), blk(TOP_K),
                   pl.BlockSpec((N_EXPERTS, LANE), lambda i: (0, 0))],
        out_shape=[jax.ShapeDtypeStruct((TOP_K, t), jnp.int32),
                   jax.ShapeDtypeStruct((TOP_K, t), F32),
                   jax.ShapeDtypeStruct((TOP_K, t), jnp.int32),
                   jax.ShapeDtypeStruct((N_EXPERTS, LANE), F32)],
        scratch_shapes=[pltpu.VMEM((N_EXPERTS, 1), F32)],
        compiler_params=_params("arbitrary"),
        name="route",
    )(logits_t, router_b[:, None], tri)


def _dispatch_kernel(dest_ref, pend_ref, h_ref, o_ref, zero_ref, sem, zsem, *, tm, t):
    base = pl.program_id(0) * tm

    @pl.when(pl.program_id(0) == 0)
    def _():
        zero_ref[...] = jnp.zeros_like(zero_ref)

        def fill(e):
            end = pend_ref[e]
            start = pend_ref[e - 1] if e > 0 else 0
            dst = o_ref.at[pl.ds(pl.multiple_of(jnp.maximum(end - MOE_ROWS, 0), MOE_ROWS),
                                 MOE_ROWS), :]
            return end > start, pltpu.make_async_copy(zero_ref, dst, zsem)

        for e in range(N_EXPERTS):
            nonempty, cp = fill(e)
            pl.when(nonempty)(cp.start)
        for e in range(N_EXPERTS):
            nonempty, cp = fill(e)
            pl.when(nonempty)(cp.wait)

        def tail(r):
            dst = o_ref.at[pl.ds(pl.multiple_of(r * MOE_ROWS, MOE_ROWS), MOE_ROWS), :]
            return pltpu.make_async_copy(zero_ref, dst, zsem)

        first_free = pend_ref[N_EXPERTS - 1] // MOE_ROWS
        n_blocks = o_ref.shape[0] // MOE_ROWS
        lax.fori_loop(first_free, n_blocks, lambda r, c: (tail(r).start(), c)[1], 0)
        lax.fori_loop(first_free, n_blocks, lambda r, c: (tail(r).wait(), c)[1], 0)

    def issue(r, carry):
        for k in range(TOP_K):
            d = dest_ref[k * t + base + r]
            pltpu.make_async_copy(h_ref.at[pl.ds(r, 1), :], o_ref.at[pl.ds(d, 1), :], sem).start()
        return carry

    lax.fori_loop(0, tm, issue, 0, unroll=DMA_ISSUE_UNROLL)
    for k in range(TOP_K):
        pltpu.make_async_copy(h_ref, o_ref.at[pl.ds(0, tm), :], sem).wait()


def _dispatch(h2, dest_flat, pad_end, rows, tm=256):
    t, d = h2.shape
    return pl.pallas_call(
        functools.partial(_dispatch_kernel, tm=tm, t=t),
        grid_spec=pltpu.PrefetchScalarGridSpec(
            num_scalar_prefetch=2,
            grid=(t // tm,),
            in_specs=[pl.BlockSpec((tm, d), lambda i, dest, pend: (i, 0))],
            out_specs=pl.BlockSpec(memory_space=pl.ANY),
            scratch_shapes=[pltpu.VMEM((MOE_ROWS, d), h2.dtype),
                            pltpu.SemaphoreType.DMA(()), pltpu.SemaphoreType.DMA(())]),
        out_shape=jax.ShapeDtypeStruct((rows, d), h2.dtype),
        compiler_params=_params("arbitrary"),
        name="moe_dispatch",
    )(dest_flat, pad_end, h2)


def _expert_kernel(be_ref, nu_ref, next_ref, slot_ref, x_ref, wg_hbm, wu_hbm, wd_hbm, y_ref,
                   wgf, wuf, wdf, wgb, wub, wdb, sems, *, layer):
    r = pl.program_id(0)
    live = r < nu_ref[0]
    e = be_ref[r]
    new_expert = (r == 0) | (be_ref[jnp.maximum(r - 1, 0)] != e)
    slot = slot_ref[e]

    def fetch(expert, s):
        pairs = ((wg_hbm, wgf), (wu_hbm, wuf), (wd_hbm, wdf))
        return [pltpu.make_async_copy(w.at[layer, expert], buf.at[s], sems.at[s, j])
                for j, (w, buf) in enumerate(pairs)]

    @pl.when(r == 0)
    def _():
        for cp in fetch(e, slot):
            cp.start()

    @pl.when(live & new_expert)
    def _():
        nxt = next_ref[e]

        @pl.when(nxt != e)
        def _():
            for cp in fetch(nxt, 1 - slot):
                cp.start()

        for cp in fetch(e, slot):
            cp.wait()
        wgb[...] = wgf[slot].astype(BF16)
        wub[...] = wuf[slot].astype(BF16)
        wdb[...] = wdf[slot].astype(BF16)

    @pl.when(live)
    def _():
        lo, hi = _unpack_rows(x_ref[...])
        xb = jnp.concatenate([lo.astype(BF16), hi.astype(BF16)], axis=1)
        g = jnp.dot(xb, wgb[...], preferred_element_type=F32)
        u = jnp.dot(xb, wub[...], preferred_element_type=F32)
        a = (g * jax.nn.sigmoid(g)) * u
        y_ref[...] = _pack_rows(jnp.dot(a.astype(BF16), wdb[...], preferred_element_type=F32))

    @pl.when(jnp.logical_not(live))
    def _():
        y_ref[...] = jnp.zeros_like(y_ref)


def _experts(x_sorted, block_e, n_used, next_expert, expert_slot, w_gate, w_up, w_down, layer):
    rows, d = x_sorted.shape
    de = w_gate.shape[-1]
    dm = w_gate.shape[-2]
    bm = MOE_ROWS
    blk = lambda r, be, nu, nx, sl: (jnp.minimum(r, nu[0] - 1), 0)
    hbm = pl.BlockSpec(memory_space=pl.ANY)
    return pl.pallas_call(
        functools.partial(_expert_kernel, layer=layer),
        grid_spec=pltpu.PrefetchScalarGridSpec(
            num_scalar_prefetch=4,
            grid=(rows // bm,),
            in_specs=[pl.BlockSpec((bm, d), blk), hbm, hbm, hbm],
            out_specs=pl.BlockSpec((bm, d), lambda r, be, nu, nx, sl: (r, 0)),
            scratch_shapes=[pltpu.VMEM((2, dm, de), F32), pltpu.VMEM((2, dm, de), F32),
                            pltpu.VMEM((2, de, dm), F32),
                            pltpu.VMEM((dm, de), BF16), pltpu.VMEM((dm, de), BF16),
                            pltpu.VMEM((de, dm), BF16),
                            pltpu.SemaphoreType.DMA((2, 3))]),
        out_shape=jax.ShapeDtypeStruct((rows, d), jnp.uint32),
        compiler_params=_params("arbitrary"),
        name="moe_experts",
    )(block_e, n_used, next_expert, expert_slot, x_sorted, w_gate, w_up, w_down)


def _combine_kernel(dest_ref, y_ref, x_ref, gate_ref, w_ref, o_ref, buf, sems, *, tm, t):
    i = pl.program_id(0)
    slot = i % 2

    def gather(tile, dst_slot):
        base = tile * tm

        def issue(r, carry):
            for k in range(TOP_K):
                d = dest_ref[k * t + base + r]
                pltpu.make_async_copy(y_ref.at[pl.ds(d, 1), :],
                                      buf.at[dst_slot, k, pl.ds(r, 1), :],
                                      sems.at[dst_slot]).start()
            return carry

        lax.fori_loop(0, tm, issue, 0, unroll=DMA_ISSUE_UNROLL)

    @pl.when(i == 0)
    def _():
        gather(0, 0)

    @pl.when(i + 1 < pl.num_programs(0))
    def _():
        gather(i + 1, 1 - slot)

    for k in range(TOP_K):
        pltpu.make_async_copy(y_ref.at[pl.ds(0, tm), :], buf.at[slot, k], sems.at[slot]).wait()
    w = w_ref[...]
    half = buf.shape[3]
    lo0, hi0 = _unpack_rows(buf[slot, 0])
    lo1, hi1 = _unpack_rows(buf[slot, 1])
    o_ref[:, :half] = x_ref[:, :half] + gate_ref[:, :half] * (w[:, 0:1] * lo0 + w[:, 1:2] * lo1)
    o_ref[:, half:] = x_ref[:, half:] + gate_ref[:, half:] * (w[:, 0:1] * hi0 + w[:, 1:2] * hi1)


def _combine(y_sorted, dest_flat, x, mod, w_tok, seq, tm=256):
    t, d = x.shape
    per_b = seq // tm
    return pl.pallas_call(
        functools.partial(_combine_kernel, tm=tm, t=t),
        grid_spec=pltpu.PrefetchScalarGridSpec(
            num_scalar_prefetch=1,
            grid=(t // tm,),
            in_specs=[pl.BlockSpec(memory_space=pl.ANY),
                      pl.BlockSpec((tm, d), lambda i, dest: (i, 0)),
                      pl.BlockSpec((None, 1, d), lambda i, dest: ((i // per_b) * 6 + 5, 0, 0)),
                      pl.BlockSpec((tm, TOP_K), lambda i, dest: (i, 0))],
            out_specs=pl.BlockSpec((tm, d), lambda i, dest: (i, 0)),
            scratch_shapes=[pltpu.VMEM((2, TOP_K, tm, y_sorted.shape[1]), y_sorted.dtype),
                            pltpu.SemaphoreType.DMA((2,))]),
        out_shape=jax.ShapeDtypeStruct((t, d), F32),
        compiler_params=_params("arbitrary"),
        name="moe_combine",
    )(dest_flat, y_sorted, x, mod, w_tok)


def _rope_tables(seq):
    half = MLA_ROPE // 2
    inv = 1.0 / (ROPE_BASE ** (jnp.arange(0, MLA_ROPE, 2, dtype=F32) / MLA_ROPE))
    ang = jnp.arange(seq, dtype=F32)[:, None] * inv[None, :]
    cos, sin = jnp.cos(ang), jnp.sin(ang)
    z = jnp.zeros((seq, half), F32)
    zz = jnp.zeros((seq, LANE - MLA_ROPE), F32)
    cos_t = jnp.concatenate([cos, cos, zz], axis=1)
    s1_t = jnp.concatenate([z, sin, zz], axis=1)
    s2_t = jnp.concatenate([-sin, z, zz], axis=1)
    return cos_t, s1_t, s2_t


def _pad_last(a, width):
    return jnp.pad(a, [(0, 0)] * (a.ndim - 1) + [(0, width - a.shape[-1])])


def kernel(x, c, norm1_g, norm2_g, w_ada, b_ada, w_in, q_lora_g, kv_lora_g, w_uq, w_ukv,
           q_head_g, k_head_g, mla_out_g, sb_out_g, w_o, router_w, router_b,
           w_gate, w_up, w_down):
    bsz, seq, d = x.shape
    t = bsz * seq
    depth = w_ada.shape[0]

    sb_col0 = COL_KR + MLA_ROPE
    w_in_mla = _pad_last(w_in[..., :sb_col0], MLA_IN_COLS).astype(BF16)
    w_in_sb = w_in[..., sb_col0:].astype(BF16)
    wq = w_uq.reshape(depth, Q_LORA, MLA_HEADS, MLA_QK)
    half = MLA_ROPE // 2
    swap = lambda a: jnp.concatenate([a[..., half:], a[..., :half]], axis=-1)
    wq_rope = wq[..., MLA_NOPE:]
    wuq_p = jnp.concatenate(
        [wq[..., :MLA_NOPE].reshape(depth, Q_LORA, -1),
         _pad_last(wq_rope, LANE).reshape(depth, Q_LORA, -1),
         _pad_last(swap(wq_rope), LANE).reshape(depth, Q_LORA, -1)], axis=-1).astype(BF16)
    wkv = w_ukv.reshape(depth, KV_LORA, MLA_HEADS, MLA_NOPE + MLA_V)
    wukv_p = jnp.concatenate(
        [wkv[..., :MLA_NOPE].reshape(depth, KV_LORA, -1),
         wkv[..., MLA_NOPE:].reshape(depth, KV_LORA, -1)], axis=-1).astype(BF16)
    w_o_b = w_o.astype(BF16)
    q_scale = MLA_QK ** -0.5 * LOG2_E
    gqn = (q_head_g[:, None, :MLA_NOPE] * q_scale)
    gq_rope = q_head_g[:, None, MLA_NOPE:] * q_scale
    gqr = jnp.concatenate([_pad_last(gq_rope, LANE), _pad_last(swap(gq_rope), LANE)], axis=-1)
    gkn = k_head_g[:, None, :MLA_NOPE]
    gkr = _pad_last(k_head_g[:, None, MLA_NOPE:], LANE)
    rope_tabs = _rope_tables(seq)
    rw_t = router_w.T.astype(BF16)

    mod_all = _ada_modulation(c[:, :, None], w_ada, b_ada[:, None, :], layers=depth)
    n_blocks = -(-(t * TOP_K) // MOE_ROWS) + N_EXPERTS
    rows = n_blocks * MOE_ROWS

    xt = x.reshape(t, d)
    for l in range(depth):
        mod = mod_all[l].reshape(bsz * 6, 1, d)
        proj_mla, proj_sb = _in_proj(xt, norm1_g[l][None, :], mod, w_in_mla, w_in_sb, l, seq)
        q2d, k2d, v2d = _mla_prep(proj_mla, rope_tabs, q_lora_g[:, None, :],
                                  kv_lora_g[:, None, :], wuq_p, wukv_p, gqn, gqr, gkn, gkr, l, seq)
        o_mla = _mla_attention(q2d, k2d, v2d, bsz, seq)
        o_sb = _sb_attention(proj_sb, bsz, seq)
        xt, h2, logits_t = _out_proj(o_mla, o_sb, mla_out_g[l][None, :], sb_out_g[l][None, :],
                                     w_o_b, xt, mod, norm2_g[l][None, :], rw_t, l, seq)
        e_idx, w_tok, rank, cnt = _route(logits_t, router_b)
        counts = cnt[:, 0].astype(jnp.int32)
        padded = (counts + MOE_ROWS - 1) // MOE_ROWS * MOE_ROWS
        pad_end = jnp.cumsum(padded)
        pad_start = pad_end - padded
        eids = jnp.arange(N_EXPERTS, dtype=jnp.int32)
        seg_start = jnp.sum(jnp.where(e_idx[None] == eids[:, None, None],
                                      pad_start[:, None, None], 0), axis=0)
        dest = (seg_start + rank).reshape(TOP_K * t).astype(jnp.int32)
        n_used = (pad_end[-1:] // MOE_ROWS).astype(jnp.int32)
        first_row = jnp.minimum(jnp.arange(n_blocks, dtype=jnp.int32), n_used - 1) * MOE_ROWS
        block_e = jnp.sum(pad_end[None, :] <= first_row[:, None], axis=1).astype(jnp.int32)
        nonempty = counts > 0
        later = (eids[None, :] > eids[:, None]) & nonempty[None, :]
        next_expert = jnp.min(jnp.where(later, eids[None, :], N_EXPERTS), axis=1)
        next_expert = jnp.where(next_expert == N_EXPERTS, eids, next_expert).astype(jnp.int32)
        expert_slot = ((jnp.cumsum(nonempty) - nonempty) % 2).astype(jnp.int32)
        x_sorted = _dispatch(h2, dest, pad_end.astype(jnp.int32), rows)
        y_sorted = _experts(x_sorted, block_e, n_used, next_expert, expert_slot,
                            w_gate, w_up, w_down, l)
        xt = _combine(y_sorted, dest, xt, mod, w_tok.T, seq)
    return xt.reshape(bsz, seq, d)
```

```python
import functools

import jax
import jax.numpy as jnp
from jax import lax
from jax.experimental import pallas as pl
from jax.experimental.pallas import tpu as pltpu

F32 = jnp.float32
BF16 = jnp.bfloat16

EPS = 1e-6
ROPE_BASE = 10000.0
CHUNK = 64

MLA_HEADS = 8
MLA_NOPE = 128
MLA_ROPE = 64
MLA_QK = MLA_NOPE + MLA_ROPE
MLA_V = 128
Q_LORA = 512
KV_LORA = 256
SB_HEADS = 8
SB_DIM = 128
SB_WIDTH = SB_HEADS * SB_DIM
LANE = 128
QK_PAD = 2 * LANE
COL_CQ = 0
COL_CKV = Q_LORA
COL_KR = Q_LORA + KV_LORA
MLA_IN_COLS = COL_KR + LANE
COL_SQ = 0
COL_SK = SB_WIDTH
COL_SV = 2 * SB_WIDTH
SB_IN_COLS = 3 * SB_WIDTH

N_EXPERTS = 16
N_GROUPS = 4
EXPERTS_PER_GROUP = N_EXPERTS // N_GROUPS
TOP_K = 2
D_EXPERT = 512
MOE_ROWS = 512
DMA_ISSUE_UNROLL = 8
LOG2_E = 1.4426950408889634
SB_UNDERFLOW_LOG2 = 150.0
SB_FINISHED_LOG2 = 1e30

VMEM_LIMIT = 56 * 1024 * 1024


def _params(*dims):
    return pltpu.CompilerParams(dimension_semantics=dims, vmem_limit_bytes=VMEM_LIMIT)


def _rms(x):
    return x * lax.rsqrt(jnp.mean(x * x, axis=-1, keepdims=True) + EPS)


def _pack_rows(x):
    n = x.shape[1] // 2
    lo = lax.bitcast_convert_type(x[:, :n].astype(BF16).astype(F32), jnp.uint32)
    hi = lax.bitcast_convert_type(x[:, n:].astype(BF16).astype(F32), jnp.uint32)
    return (lo >> 16) | (hi & jnp.uint32(0xFFFF0000))


def _unpack_rows(w):
    lo = lax.bitcast_convert_type(w << 16, F32)
    hi = lax.bitcast_convert_type(w & jnp.uint32(0xFFFF0000), F32)
    return lo, hi


def _ada_kernel(c_ref, w_ref, b_ref, o_ref):
    w = w_ref[...]
    for b in range(c_ref.shape[0]):
        cb = c_ref[b]
        cb = cb * jax.nn.sigmoid(cb)
        o_ref[pl.ds(b, 1), :] = jnp.sum(cb * w, axis=0, keepdims=True) + b_ref[...]


def _ada_modulation(c_col, w_ada, b_ada3, layers, tn=1024):
    _, d, n = w_ada.shape
    bsz = c_col.shape[0]
    return pl.pallas_call(
        _ada_kernel,
        grid=(layers, n // tn),
        in_specs=[pl.BlockSpec(c_col.shape, lambda l, j: (0, 0, 0)),
                  pl.BlockSpec((None, d, tn), lambda l, j: (l, 0, j)),
                  pl.BlockSpec((None, 1, tn), lambda l, j: (l, 0, j))],
        out_specs=pl.BlockSpec((None, bsz, tn), lambda l, j: (l, 0, j)),
        out_shape=jax.ShapeDtypeStruct((layers, bsz, n), F32),
        compiler_params=_params("arbitrary", "arbitrary"),
        name="ada_modulation",
    )(c_col, w_ada, b_ada3)


def _in_proj_kernel(x_ref, g_ref, sh_ref, sc_ref, wa_ref, wb_ref, oa_ref, ob_ref):
    h = _rms(x_ref[...]) * g_ref[...]
    h = (h * (1.0 + sc_ref[...]) + sh_ref[...]).astype(BF16)
    oa_ref[...] = jnp.dot(h, wa_ref[...], preferred_element_type=F32).astype(oa_ref.dtype)
    sb = jnp.dot(h, wb_ref[...], preferred_element_type=F32)
    ob_ref[:, :SB_WIDTH] = (sb[:, :SB_WIDTH] * (SB_DIM ** -0.5 * LOG2_E)).astype(ob_ref.dtype)
    ob_ref[:, SB_WIDTH:] = sb[:, SB_WIDTH:].astype(ob_ref.dtype)


def _in_proj(x, g, mod, w_mla, w_sb, layer, seq, tm=256):
    t, d = x.shape
    na, nb = w_mla.shape[-1], w_sb.shape[-1]
    per_b = seq // tm
    return pl.pallas_call(
        _in_proj_kernel,
        grid=(t // tm,),
        in_specs=[pl.BlockSpec((tm, d), lambda i: (i, 0)),
                  pl.BlockSpec((1, d), lambda i: (0, 0)),
                  pl.BlockSpec((None, 1, d), lambda i: ((i // per_b) * 6 + 0, 0, 0)),
                  pl.BlockSpec((None, 1, d), lambda i: ((i // per_b) * 6 + 1, 0, 0)),
                  pl.BlockSpec((None, d, na), lambda i: (layer, 0, 0)),
                  pl.BlockSpec((None, d, nb), lambda i: (layer, 0, 0))],
        out_specs=[pl.BlockSpec((tm, na), lambda i: (i, 0)),
                   pl.BlockSpec((tm, nb), lambda i: (i, 0))],
        out_shape=[jax.ShapeDtypeStruct((t, na), BF16),
                   jax.ShapeDtypeStruct((t, nb), BF16)],
        compiler_params=_params("arbitrary"),
        name="in_proj",
    )(x, g, mod, mod, w_mla, w_sb)


def _mla_prep_kernel(cq_ref, ckv_ref, kr_ref, cos_ref, s1_ref, s2_ref, gql_ref, gkvl_ref,
                     wuq_ref, wukv_ref, gqn_ref, gqr_ref, gkn_ref, gkr_ref,
                     q_ref, k_ref, v_ref):
    nope_w = MLA_HEADS * MLA_NOPE
    cqn = _rms(cq_ref[...].astype(F32)) * gql_ref[...]
    q = jnp.dot(cqn.astype(BF16), wuq_ref[...], preferred_element_type=F32)
    ckvn = _rms(ckv_ref[...].astype(F32)) * gkvl_ref[...]
    kv = jnp.dot(ckvn.astype(BF16), wukv_ref[...], preferred_element_type=F32)
    cosv, s1, s2 = cos_ref[...], s1_ref[...], s2_ref[...]

    def rope(x):
        return (x * cosv + pltpu.roll(x, MLA_ROPE // 2, 1) * s1
                + pltpu.roll(x, LANE - MLA_ROPE // 2, 1) * s2)

    kr = kr_ref[...].astype(F32)
    kr_ss = jnp.sum(kr * kr, axis=-1, keepdims=True)
    kr_roped = rope(kr * gkr_ref[...])
    gqn, gkn = gqn_ref[...], gkn_ref[...]
    gqr, gqs = gqr_ref[:, :LANE], gqr_ref[:, LANE:]
    s_comb = s1 + s2
    for h in range(MLA_HEADS):
        qn = q[:, h * LANE:(h + 1) * LANE]
        qr = q[:, nope_w + h * LANE:nope_w + (h + 1) * LANE]
        ss = jnp.sum(qn * qn + qr * qr, axis=-1, keepdims=True)
        r = lax.rsqrt(ss * (1.0 / MLA_QK) + EPS)
        q_ref[:, h * QK_PAD:h * QK_PAD + LANE] = (qn * r * gqn).astype(BF16)
        qs = q[:, 2 * nope_w + h * LANE:2 * nope_w + (h + 1) * LANE]
        q_rope = (qr * gqr) * cosv + (qs * gqs) * s_comb
        q_ref[:, h * QK_PAD + LANE:(h + 1) * QK_PAD] = (q_rope * r).astype(BF16)
        kn = kv[:, h * LANE:(h + 1) * LANE]
        rk = lax.rsqrt((jnp.sum(kn * kn, axis=-1, keepdims=True) + kr_ss) * (1.0 / MLA_QK) + EPS)
        k_ref[:, h * QK_PAD:h * QK_PAD + LANE] = (kn * rk * gkn).astype(BF16)
        k_ref[:, h * QK_PAD + LANE:(h + 1) * QK_PAD] = (kr_roped * rk).astype(BF16)
        v_ref[:, 2 * h * MLA_V:(2 * h + 1) * MLA_V] = (
            kv[:, nope_w + h * MLA_V:nope_w + (h + 1) * MLA_V].astype(BF16))
        v_ref[:, (2 * h + 1) * MLA_V:(2 * h + 2) * MLA_V] = jnp.ones((kn.shape[0], MLA_V), BF16)


def _mla_prep(proj, rope_tabs, gql, gkvl, wuq, wukv, gqn, gqr, gkn, gkr, layer, seq, tm=256):
    t = proj.shape[0]
    per_b = seq // tm
    cos_t, s1_t, s2_t = rope_tabs
    row = lambda i: (i, 0)
    full2 = lambda i: (0, 0)
    lay = lambda i: (layer, 0, 0)
    tab = lambda i: (i % per_b, 0)
    return pl.pallas_call(
        _mla_prep_kernel,
        grid=(t // tm,),
        in_specs=[pl.BlockSpec((tm, Q_LORA), lambda i: (i, COL_CQ // Q_LORA)),
                  pl.BlockSpec((tm, KV_LORA), lambda i: (i, COL_CKV // KV_LORA)),
                  pl.BlockSpec((tm, LANE), lambda i: (i, COL_KR // LANE)),
                  pl.BlockSpec((tm, LANE), tab),
                  pl.BlockSpec((tm, LANE), tab),
                  pl.BlockSpec((tm, LANE), tab),
                  pl.BlockSpec((None, 1, Q_LORA), lay),
                  pl.BlockSpec((None, 1, KV_LORA), lay),
                  pl.BlockSpec((None,) + wuq.shape[1:], lay),
                  pl.BlockSpec((None,) + wukv.shape[1:], lay),
                  pl.BlockSpec((None, 1, LANE), lay),
                  pl.BlockSpec((None, 1, 2 * LANE), lay),
                  pl.BlockSpec((None, 1, LANE), lay),
                  pl.BlockSpec((None, 1, LANE), lay)],
        out_specs=[pl.BlockSpec((tm, MLA_HEADS * QK_PAD), row),
                   pl.BlockSpec((tm, MLA_HEADS * QK_PAD), row),
                   pl.BlockSpec((tm, MLA_HEADS * 2 * MLA_V), row)],
        out_shape=[jax.ShapeDtypeStruct((t, MLA_HEADS * QK_PAD), BF16),
                   jax.ShapeDtypeStruct((t, MLA_HEADS * QK_PAD), BF16),
                   jax.ShapeDtypeStruct((t, MLA_HEADS * 2 * MLA_V), BF16)],
        compiler_params=_params("arbitrary"),
        name="mla_prep",
    )(proj, proj, proj, cos_t, s1_t, s2_t, gql, gkvl, wuq, wukv, gqn, gqr, gkn, gkr)


def _mla_attn_kernel(q_ref, k_ref, v_ref, o_ref, m_ref, l_ref, acc_ref, sa_ref, sb_ref, *,
                     tq, heads):
    i = pl.program_id(2)
    vw = 2 * MLA_V

    def scores(g, kb):
        q = q_ref[:, g * QK_PAD:(g + 1) * QK_PAD]
        k = k_ref[pl.ds(pl.multiple_of(kb * tq, tq), tq), g * QK_PAD:(g + 1) * QK_PAD]
        return lax.dot_general(q, k, (((1,), (1,)), ((), ())), preferred_element_type=F32)

    def probs(s, m):
        return jnp.concatenate(
            [jnp.exp2(s[:, c * LANE:(c + 1) * LANE] - m) for c in range(tq // LANE)],
            axis=1).astype(BF16)

    def pv(g, p, kb):
        v = v_ref[pl.ds(pl.multiple_of(kb * tq, tq), tq), g * vw:(g + 1) * vw]
        return jnp.dot(p, v, preferred_element_type=F32)

    def update(g, s, kb):
        m_old = m_ref[g]
        m_new = jnp.maximum(m_old, jnp.max(s, axis=-1, keepdims=True))
        alpha = jnp.exp2(m_old - m_new)
        r = pv(g, probs(s, m_new), kb)
        acc_ref[g] = alpha * acc_ref[g] + r[:, :MLA_V]
        l_ref[g] = alpha * l_ref[g] + r[:, MLA_V:]
        m_ref[g] = m_new

    for g in range(heads):
        m_ref[g] = jnp.full((tq, LANE), -jnp.inf, F32)
        l_ref[g] = jnp.zeros((tq, MLA_V), F32)
        acc_ref[g] = jnp.zeros((tq, MLA_V), F32)
        sa_ref[g] = scores(g, 0)

    def pair(t, carry):
        j = 2 * t
        for g in range(heads):
            sb_ref[g] = scores(g, j + 1)
            update(g, sa_ref[g], j)
        for g in range(heads):
            sa_ref[g] = scores(g, j + 2)
            update(g, sb_ref[g], j + 1)
        return carry

    lax.fori_loop(0, i // 2, pair, 0)

    q_chunk = lax.broadcasted_iota(jnp.int32, (tq, 1), 0) // CHUNK
    k_chunk = lax.broadcasted_iota(jnp.int32, (1, tq), 1) // CHUNK
    visible = k_chunk <= q_chunk

    @pl.when(i % 2 == 1)
    def _():
        for g in range(heads):
            sb_ref[g] = scores(g, i)
            update(g, sa_ref[g], i - 1)
        for g in range(heads):
            update(g, jnp.where(visible, sb_ref[g], -jnp.inf), i)

    @pl.when(i % 2 == 0)
    def _():
        for g in range(heads):
            update(g, jnp.where(visible, sa_ref[g], -jnp.inf), i)

    for g in range(heads):
        o_ref[:, g * MLA_V:(g + 1) * MLA_V] = acc_ref[g] / l_ref[g]


def _mla_attention(q2d, k2d, v2d, bsz, seq, tq=512, heads=2):
    t = q2d.shape[0]
    tq = min(tq, seq)
    nq = seq // tq
    vw = 2 * MLA_V
    return pl.pallas_call(
        functools.partial(_mla_attn_kernel, tq=tq, heads=heads),
        grid=(bsz, MLA_HEADS // heads, nq),
        in_specs=[pl.BlockSpec((tq, heads * QK_PAD), lambda b, h, i: (b * nq + i, h)),
                  pl.BlockSpec((seq, heads * QK_PAD), lambda b, h, i: (b, h)),
                  pl.BlockSpec((seq, heads * vw), lambda b, h, i: (b, h))],
        out_specs=pl.BlockSpec((tq, heads * MLA_V), lambda b, h, i: (b * nq + i, h)),
        out_shape=jax.ShapeDtypeStruct((t, MLA_HEADS * MLA_V), F32),
        scratch_shapes=[pltpu.VMEM((heads, tq, LANE), F32), pltpu.VMEM((heads, tq, MLA_V), F32),
                        pltpu.VMEM((heads, tq, MLA_V), F32),
                        pltpu.VMEM((heads, tq, tq), F32), pltpu.VMEM((heads, tq, tq), F32)],
        compiler_params=_params("arbitrary", "arbitrary", "arbitrary"),
        name="mla_attention",
    )(q2d, k2d, v2d)


def _sb_attn_kernel(q_ref, k_ref, v_ref, tri_ref, o_ref, run_ref, acc_ref, *, tq, tk, heads):
    i = pl.program_id(2)
    tri = tri_ref[...]
    sub = tq // tk
    streams = [(g, r) for g in range(heads) for r in range(sub)]

    def block(s, kb, mask):
        g, r = streams[s]
        cols = slice(g * SB_DIM, (g + 1) * SB_DIM)
        keys = pl.ds(pl.multiple_of(kb * tk, tk), tk)
        q = q_ref[r * tk:(r + 1) * tk, cols]
        z = lax.dot_general(q, k_ref[keys, cols], (((1,), (1,)), ((), ())),
                            preferred_element_type=F32)
        neg_abs = lax.bitcast_convert_type(
            lax.bitcast_convert_type(z, jnp.uint32) | jnp.uint32(0x80000000), F32)
        cost = jnp.maximum(z, 0.0) + jnp.log2(1.0 + jnp.exp2(neg_abs))
        if mask is not None:
            cost = jnp.where(mask, cost, 0.0)
        after = jnp.dot(cost.astype(BF16), tri, preferred_element_type=F32) + run_ref[s]
        a = jnp.exp2(z - cost - after)
        if mask is not None:
            a = jnp.where(mask, a, 0.0)
        acc_ref[s] += jnp.dot(a.astype(BF16), v_ref[keys, cols], preferred_element_type=F32)
        run_ref[s] += jnp.sum(cost, axis=-1, keepdims=True)

    def more():
        return jnp.min(run_ref[...]) < SB_UNDERFLOW_LOG2

    run_ref[...] = jnp.zeros_like(run_ref)
    acc_ref[...] = jnp.zeros_like(acc_ref)
    strictly_before = (lax.broadcasted_iota(jnp.int32, (tk, tk), 1)
                       < lax.broadcasted_iota(jnp.int32, (tk, tk), 0))
    for s, (g, r) in enumerate(streams):
        block(s, i * sub + r, strictly_before)

    def cond(c):
        return (c[0] < i * sub + sub - 1) & c[1]

    def body(c):
        for s, (g, r) in enumerate(streams):
            kb = i * sub + r - 1 - c[0]
            run_ref[s] = jnp.where(kb >= 0, run_ref[s], SB_FINISHED_LOG2)
            block(s, jnp.maximum(kb, 0), None)
        return c[0] + 1, more()

    lax.while_loop(cond, body, (jnp.int32(0), more()))
    for s, (g, r) in enumerate(streams):
        o_ref[r * tk:(r + 1) * tk, g * SB_DIM:(g + 1) * SB_DIM] = acc_ref[s]


def _sb_attention(proj, bsz, seq, tq=512, tk=256, heads=4):
    t = proj.shape[0]
    tq = min(tq, seq)
    nq = seq // tq
    tri = (lax.broadcasted_iota(jnp.int32, (tk, tk), 0)
           > lax.broadcasted_iota(jnp.int32, (tk, tk), 1)).astype(BF16)
    w = heads * SB_DIM
    cq, ck, cv = COL_SQ // w, COL_SK // w, COL_SV // w
    n_streams = heads * (tq // tk)
    return pl.pallas_call(
        functools.partial(_sb_attn_kernel, tq=tq, tk=tk, heads=heads),
        grid=(bsz, SB_HEADS // heads, nq),
        in_specs=[pl.BlockSpec((tq, w), lambda b, h, i: (b * nq + i, cq + h)),
                  pl.BlockSpec((seq, w), lambda b, h, i: (b, ck + h)),
                  pl.BlockSpec((seq, w), lambda b, h, i: (b, cv + h)),
                  pl.BlockSpec((tk, tk), lambda b, h, i: (0, 0))],
        out_specs=pl.BlockSpec((tq, w), lambda b, h, i: (b * nq + i, h)),
        out_shape=jax.ShapeDtypeStruct((t, SB_WIDTH), F32),
        scratch_shapes=[pltpu.VMEM((n_streams, tk, 1), F32),
                        pltpu.VMEM((n_streams, tk, SB_DIM), F32)],
        compiler_params=_params("arbitrary", "arbitrary", "arbitrary"),
        name="sb_attention",
    )(proj, proj, proj, tri)


def _out_proj_kernel(om_ref, os_ref, gm_ref, gs_ref, wo_ref, x_ref, gate_ref, g2_ref,
                     sh_ref, sc_ref, rw_ref, xo_ref, h2_ref, lg_ref):
    wm = om_ref.shape[1]
    nm = (_rms(om_ref[...]) * gm_ref[...]).astype(BF16)
    ns = (_rms(os_ref[...]) * gs_ref[...]).astype(BF16)
    y = (jnp.dot(nm, wo_ref[:wm, :], preferred_element_type=F32)
         + jnp.dot(ns, wo_ref[wm:, :], preferred_element_type=F32))
    xn = x_ref[...] + gate_ref[...] * y
    xo_ref[...] = xn
    h2 = _rms(xn) * g2_ref[...]
    h2 = h2 * (1.0 + sc_ref[...]) + sh_ref[...]
    h2_ref[...] = _pack_rows(h2)
    lg_ref[...] = lax.dot_general(rw_ref[...], h2.astype(BF16), (((1,), (1,)), ((), ())),
                                  preferred_element_type=F32)


def _out_proj(o_mla, o_sb, gm, gs, w_o, x, mod, g2, rw_t, layer, seq, tm=512):
    t, d = x.shape
    tm = min(tm, seq)
    per_b = seq // tm
    row = lambda i: (i, 0)
    full2 = lambda i: (0, 0)
    modspec = lambda j: pl.BlockSpec((None, 1, d), lambda i: ((i // per_b) * 6 + j, 0, 0))
    return pl.pallas_call(
        _out_proj_kernel,
        grid=(t // tm,),
        in_specs=[pl.BlockSpec((tm, o_mla.shape[1]), row),
                  pl.BlockSpec((tm, o_sb.shape[1]), row),
                  pl.BlockSpec((1, o_mla.shape[1]), full2),
                  pl.BlockSpec((1, o_sb.shape[1]), full2),
                  pl.BlockSpec((None,) + w_o.shape[1:], lambda i: (layer, 0, 0)),
                  pl.BlockSpec((tm, d), row),
                  modspec(2),
                  pl.BlockSpec((1, d), full2),
                  modspec(3),
                  modspec(4),
                  pl.BlockSpec(rw_t.shape, full2)],
        out_specs=[pl.BlockSpec((tm, d), row),
                   pl.BlockSpec((tm, d // 2), row),
                   pl.BlockSpec((N_EXPERTS, tm), lambda i: (0, i))],
        out_shape=[jax.ShapeDtypeStruct((t, d), F32),
                   jax.ShapeDtypeStruct((t, d // 2), jnp.uint32),
                   jax.ShapeDtypeStruct((N_EXPERTS, t), F32)],
        compiler_params=_params("arbitrary"),
        name="out_proj",
    )(o_mla, o_sb, gm, gs, w_o, x, mod, g2, mod, mod, rw_t)


def _route_kernel(lg_ref, rb_ref, tri_ref, e_ref, w_ref, rank_ref, cnt_ref, run_ref):
    i = pl.program_id(0)

    @pl.when(i == 0)
    def _():
        run_ref[...] = jnp.zeros_like(run_ref)

    scores = jax.nn.sigmoid(lg_ref[...])
    sel = scores + rb_ref[...]
    tm = sel.shape[1]
    epg = EXPERTS_PER_GROUP
    srow = [sel[e:e + 1, :] for e in range(N_EXPERTS)]
    prow = [scores[e:e + 1, :] for e in range(N_EXPERTS)]

    gscore = []
    for g in range(N_GROUPS):
        a, b, c, d = srow[g * epg:(g + 1) * epg]
        hi1, lo1 = jnp.maximum(a, b), jnp.minimum(a, b)
        hi2, lo2 = jnp.maximum(c, d), jnp.minimum(c, d)
        gscore.append(jnp.maximum(hi1, hi2)
                      + jnp.maximum(jnp.minimum(hi1, hi2), jnp.maximum(lo1, lo2)))
    best = gscore[0]
    gidx = jnp.zeros((1, tm), jnp.int32)
    for g in range(1, N_GROUPS):
        better = gscore[g] > best
        gidx = jnp.where(better, g, gidx)
        best = jnp.where(better, gscore[g], best)

    def pick(rows, j):
        out = rows[j]
        for g in range(1, N_GROUPS):
            out = jnp.where(gidx == g, rows[g * epg + j], out)
        return out

    v = [pick(srow, j) for j in range(epg)]
    p = [pick(prow, j) for j in range(epg)]
    pos = []
    for j in range(epg):
        r = jnp.zeros((1, tm), jnp.int32)
        for o in range(epg):
            if o == j:
                continue
            ahead = (v[o] > v[j]) | ((v[o] == v[j]) & (o < j))
            r = r + ahead.astype(jnp.int32)
        pos.append(r)
    local, wsel = [], []
    for k in range(TOP_K):
        lk = jnp.zeros((1, tm), jnp.int32)
        wk = jnp.zeros((1, tm), F32)
        for j in range(epg):
            hit = pos[j] == k
            lk = jnp.where(hit, j, lk)
            wk = jnp.where(hit, p[j], wk)
        local.append(lk)
        wsel.append(wk)
    wsum = wsel[0] + wsel[1]
    idx = [gidx * epg + local[k] for k in range(TOP_K)]

    eiota = lax.broadcasted_iota(jnp.int32, (N_EXPERTS, tm), 0)
    hit = [eiota == idx[k] for k in range(TOP_K)]
    onehot = (hit[0] | hit[1]).astype(F32)
    before = jnp.dot(onehot.astype(BF16), tri_ref[...], preferred_element_type=F32) + run_ref[...]
    for k in range(TOP_K):
        e_ref[pl.ds(k, 1), :] = idx[k]
        w_ref[pl.ds(k, 1), :] = wsel[k] / wsum
        rank_ref[pl.ds(k, 1), :] = jnp.sum(jnp.where(hit[k], before, 0.0), axis=0,
                                           keepdims=True).astype(jnp.int32)
    run_ref[...] += jnp.sum(onehot, axis=1, keepdims=True)
    cnt_ref[...] = jnp.broadcast_to(run_ref[...], cnt_ref.shape)


def _route(logits_t, router_b, tm=512):
    t = logits_t.shape[1]
    tm = min(tm, t)
    tri = (lax.broadcasted_iota(jnp.int32, (tm, tm), 0)
           < lax.broadcasted_iota(jnp.int32, (tm, tm), 1)).astype(BF16)
    blk = lambda rows: pl.BlockSpec((rows, tm), lambda i: (0, i))
    return pl.pallas_call(
        _route_kernel,
        grid=(t // tm,),
        in_specs=[blk(N_EXPERTS),
                  pl.BlockSpec((N_EXPERTS, 1), lambda i: (0, 0)),
                  pl.BlockSpec((tm, tm), lambda i: (0, 0))],
        out_specs=[blk(TOP_K), blk(TOP_K), blk(TOP_K),
                   pl.BlockSpec((N_EXPERTS, LANE), lambda i: (0, 0))],
        out_shape=[jax.ShapeDtypeStruct((TOP_K, t), jnp.int32),
                   jax.ShapeDtypeStruct((TOP_K, t), F32),
                   jax.ShapeDtypeStruct((TOP_K, t), jnp.int32),
                   jax.ShapeDtypeStruct((N_EXPERTS, LANE), F32)],
        scratch_shapes=[pltpu.VMEM((N_EXPERTS, 1), F32)],
        compiler_params=_params("arbitrary"),
        name="route",
    )(logits_t, router_b[:, None], tri)


def _dispatch_kernel(dest_ref, pend_ref, h_ref, o_ref, zero_ref, sem, zsem, *, tm, t):
    base = pl.program_id(0) * tm

    @pl.when(pl.program_id(0) == 0)
    def _():
        zero_ref[...] = jnp.zeros_like(zero_ref)

        def fill(e):
            end = pend_ref[e]
            start = pend_ref[e - 1] if e > 0 else 0
            dst = o_ref.at[pl.ds(pl.multiple_of(jnp.maximum(end - MOE_ROWS, 0), MOE_ROWS),
                                 MOE_ROWS), :]
            return end > start, pltpu.make_async_copy(zero_ref, dst, zsem)

        for e in range(N_EXPERTS):
            nonempty, cp = fill(e)
            pl.when(nonempty)(cp.start)
        for e in range(N_EXPERTS):
            nonempty, cp = fill(e)
            pl.when(nonempty)(cp.wait)

        def tail(r):
            dst = o_ref.at[pl.ds(pl.multiple_of(r * MOE_ROWS, MOE_ROWS), MOE_ROWS), :]
            return pltpu.make_async_copy(zero_ref, dst, zsem)

        first_free = pend_ref[N_EXPERTS - 1] // MOE_ROWS
        n_blocks = o_ref.shape[0] // MOE_ROWS
        lax.fori_loop(first_free, n_blocks, lambda r, c: (tail(r).start(), c)[1], 0)
        lax.fori_loop(first_free, n_blocks, lambda r, c: (tail(r).wait(), c)[1], 0)

    def issue(r, carry):
        for k in range(TOP_K):
            d = dest_ref[k * t + base + r]
            pltpu.make_async_copy(h_ref.at[pl.ds(r, 1), :], o_ref.at[pl.ds(d, 1), :], sem).start()
        return carry

    lax.fori_loop(0, tm, issue, 0, unroll=DMA_ISSUE_UNROLL)
    for k in range(TOP_K):
        pltpu.make_async_copy(h_ref, o_ref.at[pl.ds(0, tm), :], sem).wait()


def _dispatch(h2, dest_flat, pad_end, rows, tm=256):
    t, d = h2.shape
    return pl.pallas_call(
        functools.partial(_dispatch_kernel, tm=tm, t=t),
        grid_spec=pltpu.PrefetchScalarGridSpec(
            num_scalar_prefetch=2,
            grid=(t // tm,),
            in_specs=[pl.BlockSpec((tm, d), lambda i, dest, pend: (i, 0))],
            out_specs=pl.BlockSpec(memory_space=pl.ANY),
            scratch_shapes=[pltpu.VMEM((MOE_ROWS, d), h2.dtype),
                            pltpu.SemaphoreType.DMA(()), pltpu.SemaphoreType.DMA(())]),
        out_shape=jax.ShapeDtypeStruct((rows, d), h2.dtype),
        compiler_params=_params("arbitrary"),
        name="moe_dispatch",
    )(dest_flat, pad_end, h2)


def _expert_kernel(be_ref, nu_ref, next_ref, slot_ref, x_ref, wg_hbm, wu_hbm, wd_hbm, y_ref,
                   wgf, wuf, wdf, wgb, wub, wdb, sems, *, layer):
    r = pl.program_id(0)
    live = r < nu_ref[0]
    e = be_ref[r]
    new_expert = (r == 0) | (be_ref[jnp.maximum(r - 1, 0)] != e)
    slot = slot_ref[e]

    def fetch(expert, s):
        pairs = ((wg_hbm, wgf), (wu_hbm, wuf), (wd_hbm, wdf))
        return [pltpu.make_async_copy(w.at[layer, expert], buf.at[s], sems.at[s, j])
                for j, (w, buf) in enumerate(pairs)]

    @pl.when(r == 0)
    def _():
        for cp in fetch(e, slot):
            cp.start()

    @pl.when(live & new_expert)
    def _():
        nxt = next_ref[e]

        @pl.when(nxt != e)
        def _():
            for cp in fetch(nxt, 1 - slot):
                cp.start()

        for cp in fetch(e, slot):
            cp.wait()
        wgb[...] = wgf[slot].astype(BF16)
        wub[...] = wuf[slot].astype(BF16)
        wdb[...] = wdf[slot].astype(BF16)

    @pl.when(live)
    def _():
        lo, hi = _unpack_rows(x_ref[...])
        xb = jnp.concatenate([lo.astype(BF16), hi.astype(BF16)], axis=1)
        g = jnp.dot(xb, wgb[...], preferred_element_type=F32)
        u = jnp.dot(xb, wub[...], preferred_element_type=F32)
        a = (g * jax.nn.sigmoid(g)) * u
        y_ref[...] = _pack_rows(jnp.dot(a.astype(BF16), wdb[...], preferred_element_type=F32))

    @pl.when(jnp.logical_not(live))
    def _():
        y_ref[...] = jnp.zeros_like(y_ref)


def _experts(x_sorted, block_e, n_used, next_expert, expert_slot, w_gate, w_up, w_down, layer):
    rows, d = x_sorted.shape
    de = w_gate.shape[-1]
    dm = w_gate.shape[-2]
    bm = MOE_ROWS
    blk = lambda r, be, nu, nx, sl: (jnp.minimum(r, nu[0] - 1), 0)
    hbm = pl.BlockSpec(memory_space=pl.ANY)
    return pl.pallas_call(
        functools.partial(_expert_kernel, layer=layer),
        grid_spec=pltpu.PrefetchScalarGridSpec(
            num_scalar_prefetch=4,
            grid=(rows // bm,),
            in_specs=[pl.BlockSpec((bm, d), blk), hbm, hbm, hbm],
            out_specs=pl.BlockSpec((bm, d), lambda r, be, nu, nx, sl: (r, 0)),
            scratch_shapes=[pltpu.VMEM((2, dm, de), F32), pltpu.VMEM((2, dm, de), F32),
                            pltpu.VMEM((2, de, dm), F32),
                            pltpu.VMEM((dm, de), BF16), pltpu.VMEM((dm, de), BF16),
                            pltpu.VMEM((de, dm), BF16),
                            pltpu.SemaphoreType.DMA((2, 3))]),
        out_shape=jax.ShapeDtypeStruct((rows, d), jnp.uint32),
        compiler_params=_params("arbitrary"),
        name="moe_experts",
    )(block_e, n_used, next_expert, expert_slot, x_sorted, w_gate, w_up, w_down)


def _combine_kernel(dest_ref, y_ref, x_ref, gate_ref, w_ref, o_ref, buf, sems, *, tm, t):
    i = pl.program_id(0)
    slot = i % 2

    def gather(tile, dst_slot):
        base = tile * tm

        def issue(r, carry):
            for k in range(TOP_K):
                d = dest_ref[k * t + base + r]
                pltpu.make_async_copy(y_ref.at[pl.ds(d, 1), :],
                                      buf.at[dst_slot, k, pl.ds(r, 1), :],
                                      sems.at[dst_slot]).start()
            return carry

        lax.fori_loop(0, tm, issue, 0, unroll=DMA_ISSUE_UNROLL)

    @pl.when(i == 0)
    def _():
        gather(0, 0)

    @pl.when(i + 1 < pl.num_programs(0))
    def _():
        gather(i + 1, 1 - slot)

    for k in range(TOP_K):
        pltpu.make_async_copy(y_ref.at[pl.ds(0, tm), :], buf.at[slot, k], sems.at[slot]).wait()
    w = w_ref[...]
    half = buf.shape[3]
    lo0, hi0 = _unpack_rows(buf[slot, 0])
    lo1, hi1 = _unpack_rows(buf[slot, 1])
    o_ref[:, :half] = x_ref[:, :half] + gate_ref[:, :half] * (w[:, 0:1] * lo0 + w[:, 1:2] * lo1)
    o_ref[:, half:] = x_ref[:, half:] + gate_ref[:, half:] * (w[:, 0:1] * hi0 + w[:, 1:2] * hi1)


def _combine(y_sorted, dest_flat, x, mod, w_tok, seq, tm=256):
    t, d = x.shape
    per_b = seq // tm
    return pl.pallas_call(
        functools.partial(_combine_kernel, tm=tm, t=t),
        grid_spec=pltpu.PrefetchScalarGridSpec(
            num_scalar_prefetch=1,
            grid=(t // tm,),
            in_specs=[pl.BlockSpec(memory_space=pl.ANY),
                      pl.BlockSpec((tm, d), lambda i, dest: (i, 0)),
                      pl.BlockSpec((None, 1, d), lambda i, dest: ((i // per_b) * 6 + 5, 0, 0)),
                      pl.BlockSpec((tm, TOP_K), lambda i, dest: (i, 0))],
            out_specs=pl.BlockSpec((tm, d), lambda i, dest: (i, 0)),
            scratch_shapes=[pltpu.VMEM((2, TOP_K, tm, y_sorted.shape[1]), y_sorted.dtype),
                            pltpu.SemaphoreType.DMA((2,))]),
        out_shape=jax.ShapeDtypeStruct((t, d), F32),
        compiler_params=_params("arbitrary"),
        name="moe_combine",
    )(dest_flat, y_sorted, x, mod, w_tok)


def _rope_tables(seq):
    half = MLA_ROPE // 2
    inv = 1.0 / (ROPE_BASE ** (jnp.arange(0, MLA_ROPE, 2, dtype=F32) / MLA_ROPE))
    ang = jnp.arange(seq, dtype=F32)[:, None] * inv[None, :]
    cos, sin = jnp.cos(ang), jnp.sin(ang)
    z = jnp.zeros((seq, half), F32)
    zz = jnp.zeros((seq, LANE - MLA_ROPE), F32)
    cos_t = jnp.concatenate([cos, cos, zz], axis=1)
    s1_t = jnp.concatenate([z, sin, zz], axis=1)
    s2_t = jnp.concatenate([-sin, z, zz], axis=1)
    return cos_t, s1_t, s2_t


def _pad_last(a, width):
    return jnp.pad(a, [(0, 0)] * (a.ndim - 1) + [(0, width - a.shape[-1])])


def kernel(x, c, norm1_g, norm2_g, w_ada, b_ada, w_in, q_lora_g, kv_lora_g, w_uq, w_ukv,
           q_head_g, k_head_g, mla_out_g, sb_out_g, w_o, router_w, router_b,
           w_gate, w_up, w_down):
    bsz, seq, d = x.shape
    t = bsz * seq
    depth = w_ada.shape[0]

    sb_col0 = COL_KR + MLA_ROPE
    w_in_mla = _pad_last(w_in[..., :sb_col0], MLA_IN_COLS).astype(BF16)
    w_in_sb = w_in[..., sb_col0:].astype(BF16)
    wq = w_uq.reshape(depth, Q_LORA, MLA_HEADS, MLA_QK)
    half = MLA_ROPE // 2
    swap = lambda a: jnp.concatenate([a[..., half:], a[..., :half]], axis=-1)
    wq_rope = wq[..., MLA_NOPE:]
    wuq_p = jnp.concatenate(
        [wq[..., :MLA_NOPE].reshape(depth, Q_LORA, -1),
         _pad_last(wq_rope, LANE).reshape(depth, Q_LORA, -1),
         _pad_last(swap(wq_rope), LANE).reshape(depth, Q_LORA, -1)], axis=-1).astype(BF16)
    wkv = w_ukv.reshape(depth, KV_LORA, MLA_HEADS, MLA_NOPE + MLA_V)
    wukv_p = jnp.concatenate(
        [wkv[..., :MLA_NOPE].reshape(depth, KV_LORA, -1),
         wkv[..., MLA_NOPE:].reshape(depth, KV_LORA, -1)], axis=-1).astype(BF16)
    w_o_b = w_o.astype(BF16)
    q_scale = MLA_QK ** -0.5 * LOG2_E
    gqn = (q_head_g[:, None, :MLA_NOPE] * q_scale)
    gq_rope = q_head_g[:, None, MLA_NOPE:] * q_scale
    gqr = jnp.concatenate([_pad_last(gq_rope, LANE), _pad_last(swap(gq_rope), LANE)], axis=-1)
    gkn = k_head_g[:, None, :MLA_NOPE]
    gkr = _pad_last(k_head_g[:, None, MLA_NOPE:], LANE)
    rope_tabs = _rope_tables(seq)
    rw_t = router_w.T.astype(BF16)

    mod_all = _ada_modulation(c[:, :, None], w_ada, b_ada[:, None, :], layers=depth)
    n_blocks = -(-(t * TOP_K) // MOE_ROWS) + N_EXPERTS
    rows = n_blocks * MOE_ROWS

    xt = x.reshape(t, d)
    for l in range(depth):
        mod = mod_all[l].reshape(bsz * 6, 1, d)
        proj_mla, proj_sb = _in_proj(xt, norm1_g[l][None, :], mod, w_in_mla, w_in_sb, l, seq)
        q2d, k2d, v2d = _mla_prep(proj_mla, rope_tabs, q_lora_g[:, None, :],
                                  kv_lora_g[:, None, :], wuq_p, wukv_p, gqn, gqr, gkn, gkr, l, seq)
        o_mla = _mla_attention(q2d, k2d, v2d, bsz, seq)
        o_sb = _sb_attention(proj_sb, bsz, seq)
        xt, h2, logits_t = _out_proj(o_mla, o_sb, mla_out_g[l][None, :], sb_out_g[l][None, :],
                                     w_o_b, xt, mod, norm2_g[l][None, :], rw_t, l, seq)
        e_idx, w_tok, rank, cnt = _route(logits_t, router_b)
        counts = cnt[:, 0].astype(jnp.int32)
        padded = (counts + MOE_ROWS - 1) // MOE_ROWS * MOE_ROWS
        pad_end = jnp.cumsum(padded)
        pad_start = pad_end - padded
        eids = jnp.arange(N_EXPERTS, dtype=jnp.int32)
        seg_start = jnp.sum(jnp.where(e_idx[None] == eids[:, None, None],
                                      pad_start[:, None, None], 0), axis=0)
        dest = (seg_start + rank).reshape(TOP_K * t).astype(jnp.int32)
        n_used = (pad_end[-1:] // MOE_ROWS).astype(jnp.int32)
        first_row = jnp.minimum(jnp.arange(n_blocks, dtype=jnp.int32), n_used - 1) * MOE_ROWS
        block_e = jnp.sum(pad_end[None, :] <= first_row[:, None], axis=1).astype(jnp.int32)
        nonempty = counts > 0
        later = (eids[None, :] > eids[:, None]) & nonempty[None, :]
        next_expert = jnp.min(jnp.where(later, eids[None, :], N_EXPERTS), axis=1)
        next_expert = jnp.where(next_expert == N_EXPERTS, eids, next_expert).astype(jnp.int32)
        expert_slot = ((jnp.cumsum(nonempty) - nonempty) % 2).astype(jnp.int32)
        x_sorted = _dispatch(h2, dest, pad_end.astype(jnp.int32), rows)
        y_sorted = _experts(x_sorted, block_e, n_used, next_expert, expert_slot,
                            w_gate, w_up, w_down, l)
        xt = _combine(y_sorted, dest, xt, mod, w_tok.T, seq)
    return xt.reshape(bsz, seq, d)
```

```python
import functools

import jax
import jax.numpy as jnp
from jax import lax
from jax.experimental import pallas as pl
from jax.experimental.pallas import tpu as pltpu

F32 = jnp.float32
BF16 = jnp.bfloat16

EPS = 1e-6
ROPE_BASE = 10000.0
CHUNK = 64

MLA_HEADS = 8
MLA_NOPE = 128
MLA_ROPE = 64
MLA_QK = MLA_NOPE + MLA_ROPE
MLA_V = 128
Q_LORA = 512
KV_LORA = 256
SB_HEADS = 8
SB_DIM = 128
SB_WIDTH = SB_HEADS * SB_DIM
LANE = 128
QK_PAD = 2 * LANE
COL_CQ = 0
COL_CKV = Q_LORA
COL_KR = Q_LORA + KV_LORA
MLA_IN_COLS = COL_KR + LANE
COL_SQ = 0
COL_SK = SB_WIDTH
COL_SV = 2 * SB_WIDTH
SB_IN_COLS = 3 * SB_WIDTH

N_EXPERTS = 16
N_GROUPS = 4
EXPERTS_PER_GROUP = N_EXPERTS // N_GROUPS
TOP_K = 2
D_EXPERT = 512
MOE_ROWS = 512
DMA_ISSUE_UNROLL = 8
LOG2_E = 1.4426950408889634
SB_UNDERFLOW_LOG2 = 150.0
SB_FINISHED_LOG2 = 1e30

VMEM_LIMIT = 56 * 1024 * 1024


def _params(*dims):
    return pltpu.CompilerParams(dimension_semantics=dims, vmem_limit_bytes=VMEM_LIMIT)


def _rms(x):
    return x * lax.rsqrt(jnp.mean(x * x, axis=-1, keepdims=True) + EPS)


def _pack_rows(x):
    n = x.shape[1] // 2
    lo = lax.bitcast_convert_type(x[:, :n].astype(BF16).astype(F32), jnp.uint32)
    hi = lax.bitcast_convert_type(x[:, n:].astype(BF16).astype(F32), jnp.uint32)
    return (lo >> 16) | (hi & jnp.uint32(0xFFFF0000))


def _unpack_rows(w):
    lo = lax.bitcast_convert_type(w << 16, F32)
    hi = lax.bitcast_convert_type(w & jnp.uint32(0xFFFF0000), F32)
    return lo, hi


def _ada_kernel(c_ref, w_ref, b_ref, o_ref):
    w = w_ref[...]
    for b in range(c_ref.shape[0]):
        cb = c_ref[b]
        cb = cb * jax.nn.sigmoid(cb)
        o_ref[pl.ds(b, 1), :] = jnp.sum(cb * w, axis=0, keepdims=True) + b_ref[...]


def _ada_modulation(c_col, w_ada, b_ada3, layers, tn=1024):
    _, d, n = w_ada.shape
    bsz = c_col.shape[0]
    return pl.pallas_call(
        _ada_kernel,
        grid=(layers, n // tn),
        in_specs=[pl.BlockSpec(c_col.shape, lambda l, j: (0, 0, 0)),
                  pl.BlockSpec((None, d, tn), lambda l, j: (l, 0, j)),
                  pl.BlockSpec((None, 1, tn), lambda l, j: (l, 0, j))],
        out_specs=pl.BlockSpec((None, bsz, tn), lambda l, j: (l, 0, j)),
        out_shape=jax.ShapeDtypeStruct((layers, bsz, n), F32),
        compiler_params=_params("arbitrary", "arbitrary"),
        name="ada_modulation",
    )(c_col, w_ada, b_ada3)


def _in_proj_kernel(x_ref, g_ref, sh_ref, sc_ref, wa_ref, wb_ref, oa_ref, ob_ref):
    h = _rms(x_ref[...]) * g_ref[...]
    h = (h * (1.0 + sc_ref[...]) + sh_ref[...]).astype(BF16)
    oa_ref[...] = jnp.dot(h, wa_ref[...], preferred_element_type=F32).astype(oa_ref.dtype)
    sb = jnp.dot(h, wb_ref[...], preferred_element_type=F32)
    ob_ref[:, :SB_WIDTH] = (sb[:, :SB_WIDTH] * (SB_DIM ** -0.5 * LOG2_E)).astype(ob_ref.dtype)
    ob_ref[:, SB_WIDTH:] = sb[:, SB_WIDTH:].astype(ob_ref.dtype)


def _in_proj(x, g, mod, w_mla, w_sb, layer, seq, tm=256):
    t, d = x.shape
    na, nb = w_mla.shape[-1], w_sb.shape[-1]
    per_b = seq // tm
    return pl.pallas_call(
        _in_proj_kernel,
        grid=(t // tm,),
        in_specs=[pl.BlockSpec((tm, d), lambda i: (i, 0)),
                  pl.BlockSpec((1, d), lambda i: (0, 0)),
                  pl.BlockSpec((None, 1, d), lambda i: ((i // per_b) * 6 + 0, 0, 0)),
                  pl.BlockSpec((None, 1, d), lambda i: ((i // per_b) * 6 + 1, 0, 0)),
                  pl.BlockSpec((None, d, na), lambda i: (layer, 0, 0)),
                  pl.BlockSpec((None, d, nb), lambda i: (layer, 0, 0))],
        out_specs=[pl.BlockSpec((tm, na), lambda i: (i, 0)),
                   pl.BlockSpec((tm, nb), lambda i: (i, 0))],
        out_shape=[jax.ShapeDtypeStruct((t, na), BF16),
                   jax.ShapeDtypeStruct((t, nb), BF16)],
        compiler_params=_params("arbitrary"),
        name="in_proj",
    )(x, g, mod, mod, w_mla, w_sb)


def _mla_prep_kernel(cq_ref, ckv_ref, kr_ref, cos_ref, s1_ref, s2_ref, gql_ref, gkvl_ref,
                     wuq_ref, wukv_ref, gqn_ref, gqr_ref, gkn_ref, gkr_ref,
                     q_ref, k_ref, v_ref):
    nope_w = MLA_HEADS * MLA_NOPE
    cqn = _rms(cq_ref[...].astype(F32)) * gql_ref[...]
    q = jnp.dot(cqn.astype(BF16), wuq_ref[...], preferred_element_type=F32)
    ckvn = _rms(ckv_ref[...].astype(F32)) * gkvl_ref[...]
    kv = jnp.dot(ckvn.astype(BF16), wukv_ref[...], preferred_element_type=F32)
    cosv, s1, s2 = cos_ref[...], s1_ref[...], s2_ref[...]

    def rope(x):
        return (x * cosv + pltpu.roll(x, MLA_ROPE // 2, 1) * s1
                + pltpu.roll(x, LANE - MLA_ROPE // 2, 1) * s2)

    kr = kr_ref[...].astype(F32)
    kr_ss = jnp.sum(kr * kr, axis=-1, keepdims=True)
    kr_roped = rope(kr * gkr_ref[...])
    gqn, gkn = gqn_ref[...], gkn_ref[...]
    gqr, gqs = gqr_ref[:, :LANE], gqr_ref[:, LANE:]
    s_comb = s1 + s2
    for h in range(MLA_HEADS):
        qn = q[:, h * LANE:(h + 1) * LANE]
        qr = q[:, nope_w + h * LANE:nope_w + (h + 1) * LANE]
        ss = jnp.sum(qn * qn + qr * qr, axis=-1, keepdims=True)
        r = lax.rsqrt(ss * (1.0 / MLA_QK) + EPS)
        q_ref[:, h * QK_PAD:h * QK_PAD + LANE] = (qn * r * gqn).astype(BF16)
        qs = q[:, 2 * nope_w + h * LANE:2 * nope_w + (h + 1) * LANE]
        q_rope = (qr * gqr) * cosv + (qs * gqs) * s_comb
        q_ref[:, h * QK_PAD + LANE:(h + 1) * QK_PAD] = (q_rope * r).astype(BF16)
        kn = kv[:, h * LANE:(h + 1) * LANE]
        rk = lax.rsqrt((jnp.sum(kn * kn, axis=-1, keepdims=True) + kr_ss) * (1.0 / MLA_QK) + EPS)
        k_ref[:, h * QK_PAD:h * QK_PAD + LANE] = (kn * rk * gkn).astype(BF16)
        k_ref[:, h * QK_PAD + LANE:(h + 1) * QK_PAD] = (kr_roped * rk).astype(BF16)
        v_ref[:, 2 * h * MLA_V:(2 * h + 1) * MLA_V] = (
            kv[:, nope_w + h * MLA_V:nope_w + (h + 1) * MLA_V].astype(BF16))
        v_ref[:, (2 * h + 1) * MLA_V:(2 * h + 2) * MLA_V] = jnp.ones((kn.shape[0], MLA_V), BF16)


def _mla_prep(proj, rope_tabs, gql, gkvl, wuq, wukv, gqn, gqr, gkn, gkr, layer, seq, tm=256):
    t = proj.shape[0]
    per_b = seq // tm
    cos_t, s1_t, s2_t = rope_tabs
    row = lambda i: (i, 0)
    full2 = lambda i: (0, 0)
    lay = lambda i: (layer, 0, 0)
    tab = lambda i: (i % per_b, 0)
    return pl.pallas_call(
        _mla_prep_kernel,
        grid=(t // tm,),
        in_specs=[pl.BlockSpec((tm, Q_LORA), lambda i: (i, COL_CQ // Q_LORA)),
                  pl.BlockSpec((tm, KV_LORA), lambda i: (i, COL_CKV // KV_LORA)),
                  pl.BlockSpec((tm, LANE), lambda i: (i, COL_KR // LANE)),
                  pl.BlockSpec((tm, LANE), tab),
                  pl.BlockSpec((tm, LANE), tab),
                  pl.BlockSpec((tm, LANE), tab),
                  pl.BlockSpec((None, 1, Q_LORA), lay),
                  pl.BlockSpec((None, 1, KV_LORA), lay),
                  pl.BlockSpec((None,) + wuq.shape[1:], lay),
                  pl.BlockSpec((None,) + wukv.shape[1:], lay),
                  pl.BlockSpec((None, 1, LANE), lay),
                  pl.BlockSpec((None, 1, 2 * LANE), lay),
                  pl.BlockSpec((None, 1, LANE), lay),
                  pl.BlockSpec((None, 1, LANE), lay)],
        out_specs=[pl.BlockSpec((tm, MLA_HEADS * QK_PAD), row),
                   pl.BlockSpec((tm, MLA_HEADS * QK_PAD), row),
                   pl.BlockSpec((tm, MLA_HEADS * 2 * MLA_V), row)],
        out_shape=[jax.ShapeDtypeStruct((t, MLA_HEADS * QK_PAD), BF16),
                   jax.ShapeDtypeStruct((t, MLA_HEADS * QK_PAD), BF16),
                   jax.ShapeDtypeStruct((t, MLA_HEADS * 2 * MLA_V), BF16)],
        compiler_params=_params("arbitrary"),
        name="mla_prep",
    )(proj, proj, proj, cos_t, s1_t, s2_t, gql, gkvl, wuq, wukv, gqn, gqr, gkn, gkr)


def _mla_attn_kernel(q_ref, k_ref, v_ref, o_ref, m_ref, l_ref, acc_ref, sa_ref, sb_ref, *,
                     tq, heads):
    i = pl.program_id(2)
    vw = 2 * MLA_V

    def scores(g, kb):
        q = q_ref[:, g * QK_PAD:(g + 1) * QK_PAD]
        k = k_ref[pl.ds(pl.multiple_of(kb * tq, tq), tq), g * QK_PAD:(g + 1) * QK_PAD]
        return lax.dot_general(q, k, (((1,), (1,)), ((), ())), preferred_element_type=F32)

    def probs(s, m):
        return jnp.concatenate(
            [jnp.exp2(s[:, c * LANE:(c + 1) * LANE] - m) for c in range(tq // LANE)],
            axis=1).astype(BF16)

    def pv(g, p, kb):
        v = v_ref[pl.ds(pl.multiple_of(kb * tq, tq), tq), g * vw:(g + 1) * vw]
        return jnp.dot(p, v, preferred_element_type=F32)

    def update(g, s, kb):
        m_old = m_ref[g]
        m_new = jnp.maximum(m_old, jnp.max(s, axis=-1, keepdims=True))
        alpha = jnp.exp2(m_old - m_new)
        r = pv(g, probs(s, m_new), kb)
        acc_ref[g] = alpha * acc_ref[g] + r[:, :MLA_V]
        l_ref[g] = alpha * l_ref[g] + r[:, MLA_V:]
        m_ref[g] = m_new

    for g in range(heads):
        m_ref[g] = jnp.full((tq, LANE), -jnp.inf, F32)
        l_ref[g] = jnp.zeros((tq, MLA_V), F32)
        acc_ref[g] = jnp.zeros((tq, MLA_V), F32)
        sa_ref[g] = scores(g, 0)

    def pair(t, carry):
        j = 2 * t
        for g in range(heads):
            sb_ref[g] = scores(g, j + 1)
            update(g, sa_ref[g], j)
        for g in range(heads):
            sa_ref[g] = scores(g, j + 2)
            update(g, sb_ref[g], j + 1)
        return carry

    lax.fori_loop(0, i // 2, pair, 0)

    q_chunk = lax.broadcasted_iota(jnp.int32, (tq, 1), 0) // CHUNK
    k_chunk = lax.broadcasted_iota(jnp.int32, (1, tq), 1) // CHUNK
    visible = k_chunk <= q_chunk

    @pl.when(i % 2 == 1)
    def _():
        for g in range(heads):
            sb_ref[g] = scores(g, i)
            update(g, sa_ref[g], i - 1)
        for g in range(heads):
            update(g, jnp.where(visible, sb_ref[g], -jnp.inf), i)

    @pl.when(i % 2 == 0)
    def _():
        for g in range(heads):
            update(g, jnp.where(visible, sa_ref[g], -jnp.inf), i)

    for g in range(heads):
        o_ref[:, g * MLA_V:(g + 1) * MLA_V] = acc_ref[g] / l_ref[g]


def _mla_attention(q2d, k2d, v2d, bsz, seq, tq=512, heads=4):
    t = q2d.shape[0]
    tq = min(tq, seq)
    nq = seq // tq
    vw = 2 * MLA_V
    return pl.pallas_call(
        functools.partial(_mla_attn_kernel, tq=tq, heads=heads),
        grid=(bsz, MLA_HEADS // heads, nq),
        in_specs=[pl.BlockSpec((tq, heads * QK_PAD), lambda b, h, i: (b * nq + i, h)),
                  pl.BlockSpec((seq, heads * QK_PAD), lambda b, h, i: (b, h)),
                  pl.BlockSpec((seq, heads * vw), lambda b, h, i: (b, h))],
        out_specs=pl.BlockSpec((tq, heads * MLA_V), lambda b, h, i: (b * nq + i, h)),
        out_shape=jax.ShapeDtypeStruct((t, MLA_HEADS * MLA_V), F32),
        scratch_shapes=[pltpu.VMEM((heads, tq, LANE), F32), pltpu.VMEM((heads, tq, MLA_V), F32),
                        pltpu.VMEM((heads, tq, MLA_V), F32),
                        pltpu.VMEM((heads, tq, tq), F32), pltpu.VMEM((heads, tq, tq), F32)],
        compiler_params=_params("arbitrary", "arbitrary", "arbitrary"),
        name="mla_attention",
    )(q2d, k2d, v2d)


def _sb_attn_kernel(q_ref, k_ref, v_ref, tri_ref, o_ref, run_ref, acc_ref, *, tq, tk, heads):
    i = pl.program_id(2)
    tri = tri_ref[...]
    sub = tq // tk
    streams = [(g, r) for g in range(heads) for r in range(sub)]

    def block(s, kb, mask):
        g, r = streams[s]
        cols = slice(g * SB_DIM, (g + 1) * SB_DIM)
        keys = pl.ds(pl.multiple_of(kb * tk, tk), tk)
        q = q_ref[r * tk:(r + 1) * tk, cols]
        z = lax.dot_general(q, k_ref[keys, cols], (((1,), (1,)), ((), ())),
                            preferred_element_type=F32)
        neg_abs = lax.bitcast_convert_type(
            lax.bitcast_convert_type(z, jnp.uint32) | jnp.uint32(0x80000000), F32)
        cost = jnp.maximum(z, 0.0) + jnp.log2(1.0 + jnp.exp2(neg_abs))
        if mask is not None:
            cost = jnp.where(mask, cost, 0.0)
        after = jnp.dot(cost.astype(BF16), tri, preferred_element_type=F32) + run_ref[s]
        a = jnp.exp2(z - cost - after)
        if mask is not None:
            a = jnp.where(mask, a, 0.0)
        acc_ref[s] += jnp.dot(a.astype(BF16), v_ref[keys, cols], preferred_element_type=F32)
        run_ref[s] += jnp.sum(cost, axis=-1, keepdims=True)

    def more():
        return jnp.min(run_ref[...]) < SB_UNDERFLOW_LOG2

    run_ref[...] = jnp.zeros_like(run_ref)
    acc_ref[...] = jnp.zeros_like(acc_ref)
    strictly_before = (lax.broadcasted_iota(jnp.int32, (tk, tk), 1)
                       < lax.broadcasted_iota(jnp.int32, (tk, tk), 0))
    for s, (g, r) in enumerate(streams):
        block(s, i * sub + r, strictly_before)

    def cond(c):
        return (c[0] < i * sub + sub - 1) & c[1]

    def body(c):
        for s, (g, r) in enumerate(streams):
            kb = i * sub + r - 1 - c[0]
            run_ref[s] = jnp.where(kb >= 0, run_ref[s], SB_FINISHED_LOG2)
            block(s, jnp.maximum(kb, 0), None)
        return c[0] + 1, more()

    lax.while_loop(cond, body, (jnp.int32(0), more()))
    for s, (g, r) in enumerate(streams):
        o_ref[r * tk:(r + 1) * tk, g * SB_DIM:(g + 1) * SB_DIM] = acc_ref[s]


def _sb_attention(proj, bsz, seq, tq=512, tk=256, heads=4):
    t = proj.shape[0]
    tq = min(tq, seq)
    nq = seq // tq
    tri = (lax.broadcasted_iota(jnp.int32, (tk, tk), 0)
           > lax.broadcasted_iota(jnp.int32, (tk, tk), 1)).astype(BF16)
    w = heads * SB_DIM
    cq, ck, cv = COL_SQ // w, COL_SK // w, COL_SV // w
    n_streams = heads * (tq // tk)
    return pl.pallas_call(
        functools.partial(_sb_attn_kernel, tq=tq, tk=tk, heads=heads),
        grid=(bsz, SB_HEADS // heads, nq),
        in_specs=[pl.BlockSpec((tq, w), lambda b, h, i: (b * nq + i, cq + h)),
                  pl.BlockSpec((seq, w), lambda b, h, i: (b, ck + h)),
                  pl.BlockSpec((seq, w), lambda b, h, i: (b, cv + h)),
                  pl.BlockSpec((tk, tk), lambda b, h, i: (0, 0))],
        out_specs=pl.BlockSpec((tq, w), lambda b, h, i: (b * nq + i, h)),
        out_shape=jax.ShapeDtypeStruct((t, SB_WIDTH), F32),
        scratch_shapes=[pltpu.VMEM((n_streams, tk, 1), F32),
                        pltpu.VMEM((n_streams, tk, SB_DIM), F32)],
        compiler_params=_params("arbitrary", "arbitrary", "arbitrary"),
        name="sb_attention",
    )(proj, proj, proj, tri)


def _out_proj_kernel(om_ref, os_ref, gm_ref, gs_ref, wo_ref, x_ref, gate_ref, g2_ref,
                     sh_ref, sc_ref, rw_ref, xo_ref, h2_ref, lg_ref):
    wm = om_ref.shape[1]
    nm = (_rms(om_ref[...]) * gm_ref[...]).astype(BF16)
    ns = (_rms(os_ref[...]) * gs_ref[...]).astype(BF16)
    y = (jnp.dot(nm, wo_ref[:wm, :], preferred_element_type=F32)
         + jnp.dot(ns, wo_ref[wm:, :], preferred_element_type=F32))
    xn = x_ref[...] + gate_ref[...] * y
    xo_ref[...] = xn
    h2 = _rms(xn) * g2_ref[...]
    h2 = h2 * (1.0 + sc_ref[...]) + sh_ref[...]
    h2_ref[...] = _pack_rows(h2)
    lg_ref[...] = lax.dot_general(rw_ref[...], h2.astype(BF16), (((1,), (1,)), ((), ())),
                                  preferred_element_type=F32)


def _out_proj(o_mla, o_sb, gm, gs, w_o, x, mod, g2, rw_t, layer, seq, tm=512):
    t, d = x.shape
    tm = min(tm, seq)
    per_b = seq // tm
    row = lambda i: (i, 0)
    full2 = lambda i: (0, 0)
    modspec = lambda j: pl.BlockSpec((None, 1, d), lambda i: ((i // per_b) * 6 + j, 0, 0))
    return pl.pallas_call(
        _out_proj_kernel,
        grid=(t // tm,),
        in_specs=[pl.BlockSpec((tm, o_mla.shape[1]), row),
                  pl.BlockSpec((tm, o_sb.shape[1]), row),
                  pl.BlockSpec((1, o_mla.shape[1]), full2),
                  pl.BlockSpec((1, o_sb.shape[1]), full2),
                  pl.BlockSpec((None,) + w_o.shape[1:], lambda i: (layer, 0, 0)),
                  pl.BlockSpec((tm, d), row),
                  modspec(2),
                  pl.BlockSpec((1, d), full2),
                  modspec(3),
                  modspec(4),
                  pl.BlockSpec(rw_t.shape, full2)],
        out_specs=[pl.BlockSpec((tm, d), row),
                   pl.BlockSpec((tm, d // 2), row),
                   pl.BlockSpec((N_EXPERTS, tm), lambda i: (0, i))],
        out_shape=[jax.ShapeDtypeStruct((t, d), F32),
                   jax.ShapeDtypeStruct((t, d // 2), jnp.uint32),
                   jax.ShapeDtypeStruct((N_EXPERTS, t), F32)],
        compiler_params=_params("arbitrary"),
        name="out_proj",
    )(o_mla, o_sb, gm, gs, w_o, x, mod, g2, mod, mod, rw_t)


def _route_kernel(lg_ref, rb_ref, tri_ref, e_ref, w_ref, rank_ref, cnt_ref, run_ref):
    i = pl.program_id(0)

    @pl.when(i == 0)
    def _():
        run_ref[...] = jnp.zeros_like(run_ref)

    scores = jax.nn.sigmoid(lg_ref[...])
    sel = scores + rb_ref[...]
    tm = sel.shape[1]
    epg = EXPERTS_PER_GROUP
    srow = [sel[e:e + 1, :] for e in range(N_EXPERTS)]
    prow = [scores[e:e + 1, :] for e in range(N_EXPERTS)]

    gscore = []
    for g in range(N_GROUPS):
        a, b, c, d = srow[g * epg:(g + 1) * epg]
        hi1, lo1 = jnp.maximum(a, b), jnp.minimum(a, b)
        hi2, lo2 = jnp.maximum(c, d), jnp.minimum(c, d)
        gscore.append(jnp.maximum(hi1, hi2)
                      + jnp.maximum(jnp.minimum(hi1, hi2), jnp.maximum(lo1, lo2)))
    best = gscore[0]
    gidx = jnp.zeros((1, tm), jnp.int32)
    for g in range(1, N_GROUPS):
        better = gscore[g] > best
        gidx = jnp.where(better, g, gidx)
        best = jnp.where(better, gscore[g], best)

    def pick(rows, j):
        out = rows[j]
        for g in range(1, N_GROUPS):
            out = jnp.where(gidx == g, rows[g * epg + j], out)
        return out

    v = [pick(srow, j) for j in range(epg)]
    p = [pick(prow, j) for j in range(epg)]
    pos = []
    for j in range(epg):
        r = jnp.zeros((1, tm), jnp.int32)
        for o in range(epg):
            if o == j:
                continue
            ahead = (v[o] > v[j]) | ((v[o] == v[j]) & (o < j))
            r = r + ahead.astype(jnp.int32)
        pos.append(r)
    local, wsel = [], []
    for k in range(TOP_K):
        lk = jnp.zeros((1, tm), jnp.int32)
        wk = jnp.zeros((1, tm), F32)
        for j in range(epg):
            hit = pos[j] == k
            lk = jnp.where(hit, j, lk)
            wk = jnp.where(hit, p[j], wk)
        local.append(lk)
        wsel.append(wk)
    wsum = wsel[0] + wsel[1]
    idx = [gidx * epg + local[k] for k in range(TOP_K)]

    eiota = lax.broadcasted_iota(jnp.int32, (N_EXPERTS, tm), 0)
    hit = [eiota == idx[k] for k in range(TOP_K)]
    onehot = (hit[0] | hit[1]).astype(F32)
    before = jnp.dot(onehot.astype(BF16), tri_ref[...], preferred_element_type=F32) + run_ref[...]
    for k in range(TOP_K):
        e_ref[pl.ds(k, 1), :] = idx[k]
        w_ref[pl.ds(k, 1), :] = wsel[k] / wsum
        rank_ref[pl.ds(k, 1), :] = jnp.sum(jnp.where(hit[k], before, 0.0), axis=0,
                                           keepdims=True).astype(jnp.int32)
    run_ref[...] += jnp.sum(onehot, axis=1, keepdims=True)
    cnt_ref[...] = jnp.broadcast_to(run_ref[...], cnt_ref.shape)


def _route(logits_t, router_b, tm=512):
    t = logits_t.shape[1]
    tm = min(tm, t)
    tri = (lax.broadcasted_iota(jnp.int32, (tm, tm), 0)
           < lax.broadcasted_iota(jnp.int32, (tm, tm), 1)).astype(BF16)
    blk = lambda rows: pl.BlockSpec((rows, tm), lambda i: (0, i))
    return pl.pallas_call(
        _route_kernel,
        grid=(t // tm,),
        in_specs=[blk(N_EXPERTS),
                  pl.BlockSpec((N_EXPERTS, 1), lambda i: (0, 0)),
                  pl.BlockSpec((tm, tm), lambda i: (0, 0))],
        out_specs=[blk(TOP_K), blk(TOP_K), blk(TOP_K),
                   pl.BlockSpec((N_EXPERTS, LANE), lambda i: (0, 0))],
        out_shape=[jax.ShapeDtypeStruct((TOP_K, t), jnp.int32),
                   jax.ShapeDtypeStruct((TOP_K, t), F32),
                   jax.ShapeDtypeStruct((TOP_K, t), jnp.int32),
                   jax.ShapeDtypeStruct((N_EXPERTS, LANE), F32)],
        scratch_shapes=[pltpu.VMEM((N_EXPERTS, 1), F32)],
        compiler_params=_params("arbitrary"),
        name="route",
    )(logits_t, router_b[:, None], tri)


def _dispatch_kernel(dest_ref, pend_ref, h_ref, o_ref, zero_ref, ring, in_sems, row_sems, zsem,
                     *, tm, t):
    base = pl.program_id(0) * tm

    @pl.when(pl.program_id(0) == 0)
    def _():
        zero_ref[...] = jnp.zeros_like(zero_ref)

        def fill(e):
            end = pend_ref[e]
            start = pend_ref[e - 1] if e > 0 else 0
            dst = o_ref.at[pl.ds(pl.multiple_of(jnp.maximum(end - MOE_ROWS, 0), MOE_ROWS),
                                 MOE_ROWS), :]
            return end > start, pltpu.make_async_copy(zero_ref, dst, zsem)

        for e in range(N_EXPERTS):
            nonempty, cp = fill(e)
            pl.when(nonempty)(cp.start)
        for e in range(N_EXPERTS):
            nonempty, cp = fill(e)
            pl.when(nonempty)(cp.wait)

        def tail(r):
            dst = o_ref.at[pl.ds(pl.multiple_of(r * MOE_ROWS, MOE_ROWS), MOE_ROWS), :]
            return pltpu.make_async_copy(zero_ref, dst, zsem)

        first_free = pend_ref[N_EXPERTS - 1] // MOE_ROWS
        n_blocks = o_ref.shape[0] // MOE_ROWS
        lax.fori_loop(first_free, n_blocks, lambda r, c: (tail(r).start(), c)[1], 0)
        lax.fori_loop(first_free, n_blocks, lambda r, c: (tail(r).wait(), c)[1], 0)

    i = pl.program_id(0)
    n = pl.num_programs(0)

    def stage(tile):
        return pltpu.make_async_copy(h_ref.at[pl.ds(tile * tm, tm), :], ring.at[tile % 3],
                                     in_sems.at[tile % 3])

    @pl.when(i == 0)
    def _():
        stage(0).start()

    @pl.when(i + 1 < n)
    def _():
        stage(i + 1).start()

    stage(i).wait()
    src = ring.at[i % 3]

    def issue(r, carry):
        for k in range(TOP_K):
            d = dest_ref[k * t + base + r]
            pltpu.make_async_copy(src.at[pl.ds(r, 1), :], o_ref.at[pl.ds(d, 1), :],
                                  row_sems.at[i % 2]).start()
        return carry

    lax.fori_loop(0, tm, issue, 0, unroll=DMA_ISSUE_UNROLL)

    def wait_rows(step):
        for k in range(TOP_K):
            pltpu.make_async_copy(ring.at[0], o_ref.at[pl.ds(0, tm), :],
                                  row_sems.at[step % 2]).wait()

    pl.when(i > 0)(lambda: wait_rows(i - 1))
    pl.when(i == n - 1)(lambda: wait_rows(i))


def _dispatch(h2, dest_flat, pad_end, rows, tm=256):
    t, d = h2.shape
    return pl.pallas_call(
        functools.partial(_dispatch_kernel, tm=tm, t=t),
        grid_spec=pltpu.PrefetchScalarGridSpec(
            num_scalar_prefetch=2,
            grid=(t // tm,),
            in_specs=[pl.BlockSpec(memory_space=pl.ANY)],
            out_specs=pl.BlockSpec(memory_space=pl.ANY),
            scratch_shapes=[pltpu.VMEM((MOE_ROWS, d), h2.dtype),
                            pltpu.VMEM((3, tm, d), h2.dtype),
                            pltpu.SemaphoreType.DMA((3,)), pltpu.SemaphoreType.DMA((2,)),
                            pltpu.SemaphoreType.DMA(())]),
        out_shape=jax.ShapeDtypeStruct((rows, d), h2.dtype),
        compiler_params=_params("arbitrary"),
        name="moe_dispatch",
    )(dest_flat, pad_end, h2)


def _expert_kernel(be_ref, nu_ref, next_ref, slot_ref, x_ref, wg_hbm, wu_hbm, wd_hbm, y_ref,
                   wgf, wuf, wdf, wgb, wub, wdb, sems, *, layer):
    r = pl.program_id(0)
    live = r < nu_ref[0]
    e = be_ref[r]
    new_expert = (r == 0) | (be_ref[jnp.maximum(r - 1, 0)] != e)
    slot = slot_ref[e]

    def fetch(expert, s):
        pairs = ((wg_hbm, wgf), (wu_hbm, wuf), (wd_hbm, wdf))
        return [pltpu.make_async_copy(w.at[layer, expert], buf.at[s], sems.at[s, j])
                for j, (w, buf) in enumerate(pairs)]

    @pl.when(r == 0)
    def _():
        for cp in fetch(e, slot):
            cp.start()

    @pl.when(live & new_expert)
    def _():
        nxt = next_ref[e]

        @pl.when(nxt != e)
        def _():
            for cp in fetch(nxt, 1 - slot):
                cp.start()

        for cp in fetch(e, slot):
            cp.wait()
        wgb[...] = wgf[slot].astype(BF16)
        wub[...] = wuf[slot].astype(BF16)
        wdb[...] = wdf[slot].astype(BF16)

    @pl.when(live)
    def _():
        lo, hi = _unpack_rows(x_ref[...])
        xb = jnp.concatenate([lo.astype(BF16), hi.astype(BF16)], axis=1)
        g = jnp.dot(xb, wgb[...], preferred_element_type=F32)
        u = jnp.dot(xb, wub[...], preferred_element_type=F32)
        a = (g * jax.nn.sigmoid(g)) * u
        y_ref[...] = _pack_rows(jnp.dot(a.astype(BF16), wdb[...], preferred_element_type=F32))

    @pl.when(jnp.logical_not(live))
    def _():
        y_ref[...] = jnp.zeros_like(y_ref)


def _experts(x_sorted, block_e, n_used, next_expert, expert_slot, w_gate, w_up, w_down, layer):
    rows, d = x_sorted.shape
    de = w_gate.shape[-1]
    dm = w_gate.shape[-2]
    bm = MOE_ROWS
    blk = lambda r, be, nu, nx, sl: (jnp.minimum(r, nu[0] - 1), 0)
    hbm = pl.BlockSpec(memory_space=pl.ANY)
    return pl.pallas_call(
        functools.partial(_expert_kernel, layer=layer),
        grid_spec=pltpu.PrefetchScalarGridSpec(
            num_scalar_prefetch=4,
            grid=(rows // bm,),
            in_specs=[pl.BlockSpec((bm, d), blk), hbm, hbm, hbm],
            out_specs=pl.BlockSpec((bm, d), lambda r, be, nu, nx, sl: (r, 0)),
            scratch_shapes=[pltpu.VMEM((2, dm, de), F32), pltpu.VMEM((2, dm, de), F32),
                            pltpu.VMEM((2, de, dm), F32),
                            pltpu.VMEM((dm, de), BF16), pltpu.VMEM((dm, de), BF16),
                            pltpu.VMEM((de, dm), BF16),
                            pltpu.SemaphoreType.DMA((2, 3))]),
        out_shape=jax.ShapeDtypeStruct((rows, d), jnp.uint32),
        compiler_params=_params("arbitrary"),
        name="moe_experts",
    )(block_e, n_used, next_expert, expert_slot, x_sorted, w_gate, w_up, w_down)


def _combine_kernel(dest_ref, y_ref, x_ref, gate_ref, w_ref, o_ref, buf, sems, *, tm, t):
    i = pl.program_id(0)
    slot = i % 2

    def gather(tile, dst_slot):
        base = tile * tm

        def issue(r, carry):
            for k in range(TOP_K):
                d = dest_ref[k * t + base + r]
                pltpu.make_async_copy(y_ref.at[pl.ds(d, 1), :],
                                      buf.at[dst_slot, k, pl.ds(r, 1), :],
                                      sems.at[dst_slot]).start()
            return carry

        lax.fori_loop(0, tm, issue, 0, unroll=DMA_ISSUE_UNROLL)

    @pl.when(i == 0)
    def _():
        gather(0, 0)

    @pl.when(i + 1 < pl.num_programs(0))
    def _():
        gather(i + 1, 1 - slot)

    for k in range(TOP_K):
        pltpu.make_async_copy(y_ref.at[pl.ds(0, tm), :], buf.at[slot, k], sems.at[slot]).wait()
    w = w_ref[...]
    half = buf.shape[3]
    lo0, hi0 = _unpack_rows(buf[slot, 0])
    lo1, hi1 = _unpack_rows(buf[slot, 1])
    o_ref[:, :half] = x_ref[:, :half] + gate_ref[:, :half] * (w[:, 0:1] * lo0 + w[:, 1:2] * lo1)
    o_ref[:, half:] = x_ref[:, half:] + gate_ref[:, half:] * (w[:, 0:1] * hi0 + w[:, 1:2] * hi1)


def _combine(y_sorted, dest_flat, x, mod, w_tok, seq, tm=256):
    t, d = x.shape
    per_b = seq // tm
    return pl.pallas_call(
        functools.partial(_combine_kernel, tm=tm, t=t),
        grid_spec=pltpu.PrefetchScalarGridSpec(
            num_scalar_prefetch=1,
            grid=(t // tm,),
            in_specs=[pl.BlockSpec(memory_space=pl.ANY),
                      pl.BlockSpec((tm, d), lambda i, dest: (i, 0)),
                      pl.BlockSpec((None, 1, d), lambda i, dest: ((i // per_b) * 6 + 5, 0, 0)),
                      pl.BlockSpec((tm, TOP_K), lambda i, dest: (i, 0))],
            out_specs=pl.BlockSpec((tm, d), lambda i, dest: (i, 0)),
            scratch_shapes=[pltpu.VMEM((2, TOP_K, tm, y_sorted.shape[1]), y_sorted.dtype),
                            pltpu.SemaphoreType.DMA((2,))]),
        out_shape=jax.ShapeDtypeStruct((t, d), F32),
        compiler_params=_params("arbitrary"),
        name="moe_combine",
    )(dest_flat, y_sorted, x, mod, w_tok)


def _rope_tables(seq):
    half = MLA_ROPE // 2
    inv = 1.0 / (ROPE_BASE ** (jnp.arange(0, MLA_ROPE, 2, dtype=F32) / MLA_ROPE))
    ang = jnp.arange(seq, dtype=F32)[:, None] * inv[None, :]
    cos, sin = jnp.cos(ang), jnp.sin(ang)
    z = jnp.zeros((seq, half), F32)
    zz = jnp.zeros((seq, LANE - MLA_ROPE), F32)
    cos_t = jnp.concatenate([cos, cos, zz], axis=1)
    s1_t = jnp.concatenate([z, sin, zz], axis=1)
    s2_t = jnp.concatenate([-sin, z, zz], axis=1)
    return cos_t, s1_t, s2_t


def _pad_last(a, width):
    return jnp.pad(a, [(0, 0)] * (a.ndim - 1) + [(0, width - a.shape[-1])])


def kernel(x, c, norm1_g, norm2_g, w_ada, b_ada, w_in, q_lora_g, kv_lora_g, w_uq, w_ukv,
           q_head_g, k_head_g, mla_out_g, sb_out_g, w_o, router_w, router_b,
           w_gate, w_up, w_down):
    bsz, seq, d = x.shape
    t = bsz * seq
    depth = w_ada.shape[0]

    sb_col0 = COL_KR + MLA_ROPE
    w_in_mla = _pad_last(w_in[..., :sb_col0], MLA_IN_COLS).astype(BF16)
    w_in_sb = w_in[..., sb_col0:].astype(BF16)
    wq = w_uq.reshape(depth, Q_LORA, MLA_HEADS, MLA_QK)
    half = MLA_ROPE // 2
    swap = lambda a: jnp.concatenate([a[..., half:], a[..., :half]], axis=-1)
    wq_rope = wq[..., MLA_NOPE:]
    wuq_p = jnp.concatenate(
        [wq[..., :MLA_NOPE].reshape(depth, Q_LORA, -1),
         _pad_last(wq_rope, LANE).reshape(depth, Q_LORA, -1),
         _pad_last(swap(wq_rope), LANE).reshape(depth, Q_LORA, -1)], axis=-1).astype(BF16)
    wkv = w_ukv.reshape(depth, KV_LORA, MLA_HEADS, MLA_NOPE + MLA_V)
    wukv_p = jnp.concatenate(
        [wkv[..., :MLA_NOPE].reshape(depth, KV_LORA, -1),
         wkv[..., MLA_NOPE:].reshape(depth, KV_LORA, -1)], axis=-1).astype(BF16)
    w_o_b = w_o.astype(BF16)
    q_scale = MLA_QK ** -0.5 * LOG2_E
    gqn = (q_head_g[:, None, :MLA_NOPE] * q_scale)
    gq_rope = q_head_g[:, None, MLA_NOPE:] * q_scale
    gqr = jnp.concatenate([_pad_last(gq_rope, LANE), _pad_last(swap(gq_rope), LANE)], axis=-1)
    gkn = k_head_g[:, None, :MLA_NOPE]
    gkr = _pad_last(k_head_g[:, None, MLA_NOPE:], LANE)
    rope_tabs = _rope_tables(seq)
    rw_t = router_w.T.astype(BF16)

    mod_all = _ada_modulation(c[:, :, None], w_ada, b_ada[:, None, :], layers=depth)
    n_blocks = -(-(t * TOP_K) // MOE_ROWS) + N_EXPERTS
    rows = n_blocks * MOE_ROWS

    xt = x.reshape(t, d)
    for l in range(depth):
        mod = mod_all[l].reshape(bsz * 6, 1, d)
        proj_mla, proj_sb = _in_proj(xt, norm1_g[l][None, :], mod, w_in_mla, w_in_sb, l, seq)
        q2d, k2d, v2d = _mla_prep(proj_mla, rope_tabs, q_lora_g[:, None, :],
                                  kv_lora_g[:, None, :], wuq_p, wukv_p, gqn, gqr, gkn, gkr, l, seq)
        o_mla = _mla_attention(q2d, k2d, v2d, bsz, seq)
        o_sb = _sb_attention(proj_sb, bsz, seq)
        xt, h2, logits_t = _out_proj(o_mla, o_sb, mla_out_g[l][None, :], sb_out_g[l][None, :],
                                     w_o_b, xt, mod, norm2_g[l][None, :], rw_t, l, seq)
        e_idx, w_tok, rank, cnt = _route(logits_t, router_b)
        counts = cnt[:, 0].astype(jnp.int32)
        padded = (counts + MOE_ROWS - 1) // MOE_ROWS * MOE_ROWS
        pad_end = jnp.cumsum(padded)
        pad_start = pad_end - padded
        eids = jnp.arange(N_EXPERTS, dtype=jnp.int32)
        seg_start = jnp.sum(jnp.where(e_idx[None] == eids[:, None, None],
                                      pad_start[:, None, None], 0), axis=0)
        dest = (seg_start + rank).reshape(TOP_K * t).astype(jnp.int32)
        n_used = (pad_end[-1:] // MOE_ROWS).astype(jnp.int32)
        first_row = jnp.minimum(jnp.arange(n_blocks, dtype=jnp.int32), n_used - 1) * MOE_ROWS
        block_e = jnp.sum(pad_end[None, :] <= first_row[:, None], axis=1).astype(jnp.int32)
        nonempty = counts > 0
        later = (eids[None, :] > eids[:, None]) & nonempty[None, :]
        next_expert = jnp.min(jnp.where(later, eids[None, :], N_EXPERTS), axis=1)
        next_expert = jnp.where(next_expert == N_EXPERTS, eids, next_expert).astype(jnp.int32)
        expert_slot = ((jnp.cumsum(nonempty) - nonempty) % 2).astype(jnp.int32)
        x_sorted = _dispatch(h2, dest, pad_end.astype(jnp.int32), rows)
        y_sorted = _experts(x_sorted, block_e, n_used, next_expert, expert_slot,
                            w_gate, w_up, w_down, l)
        xt = _combine(y_sorted, dest, xt, mod, w_tok.T, seq)
    return xt.reshape(bsz, seq, d)
```

```python
import functools

import jax
import jax.numpy as jnp
from jax import lax
from jax.experimental import pallas as pl
from jax.experimental.pallas import tpu as pltpu

F32 = jnp.float32
BF16 = jnp.bfloat16

EPS = 1e-6
ROPE_BASE = 10000.0
CHUNK = 64

MLA_HEADS = 8
MLA_NOPE = 128
MLA_ROPE = 64
MLA_QK = MLA_NOPE + MLA_ROPE
MLA_V = 128
Q_LORA = 512
KV_LORA = 256
SB_HEADS = 8
SB_DIM = 128
SB_WIDTH = SB_HEADS * SB_DIM
LANE = 128
QK_PAD = 2 * LANE
COL_CQ = 0
COL_CKV = Q_LORA
COL_KR = Q_LORA + KV_LORA
MLA_IN_COLS = COL_KR + LANE
COL_SQ = 0
COL_SK = SB_WIDTH
COL_SV = 2 * SB_WIDTH
SB_IN_COLS = 3 * SB_WIDTH

N_EXPERTS = 16
N_GROUPS = 4
EXPERTS_PER_GROUP = N_EXPERTS // N_GROUPS
TOP_K = 2
D_EXPERT = 512
MOE_ROWS = 512
DMA_ISSUE_UNROLL = 8
LOG2_E = 1.4426950408889634
SB_UNDERFLOW_LOG2 = 150.0
SB_FINISHED_LOG2 = 1e30

VMEM_LIMIT = 56 * 1024 * 1024


def _params(*dims):
    return pltpu.CompilerParams(dimension_semantics=dims, vmem_limit_bytes=VMEM_LIMIT)


def _rms(x):
    return x * lax.rsqrt(jnp.mean(x * x, axis=-1, keepdims=True) + EPS)


def _pack_rows(x):
    n = x.shape[1] // 2
    lo = lax.bitcast_convert_type(x[:, :n].astype(BF16).astype(F32), jnp.uint32)
    hi = lax.bitcast_convert_type(x[:, n:].astype(BF16).astype(F32), jnp.uint32)
    return (lo >> 16) | (hi & jnp.uint32(0xFFFF0000))


def _unpack_rows(w):
    lo = lax.bitcast_convert_type(w << 16, F32)
    hi = lax.bitcast_convert_type(w & jnp.uint32(0xFFFF0000), F32)
    return lo, hi


def _ada_kernel(c_ref, w_ref, b_ref, o_ref):
    w = w_ref[...]
    for b in range(c_ref.shape[0]):
        cb = c_ref[b]
        cb = cb * jax.nn.sigmoid(cb)
        o_ref[pl.ds(b, 1), :] = jnp.sum(cb * w, axis=0, keepdims=True) + b_ref[...]


def _ada_modulation(c_col, w_ada, b_ada3, layers, tn=1024):
    _, d, n = w_ada.shape
    bsz = c_col.shape[0]
    return pl.pallas_call(
        _ada_kernel,
        grid=(layers, n // tn),
        in_specs=[pl.BlockSpec(c_col.shape, lambda l, j: (0, 0, 0)),
                  pl.BlockSpec((None, d, tn), lambda l, j: (l, 0, j)),
                  pl.BlockSpec((None, 1, tn), lambda l, j: (l, 0, j))],
        out_specs=pl.BlockSpec((None, bsz, tn), lambda l, j: (l, 0, j)),
        out_shape=jax.ShapeDtypeStruct((layers, bsz, n), F32),
        compiler_params=_params("arbitrary", "arbitrary"),
        name="ada_modulation",
    )(c_col, w_ada, b_ada3)


def _in_proj_kernel(x_ref, g_ref, sh_ref, sc_ref, wa_ref, wb_ref, cos_ref, s1_ref, s2_ref,
                    gql_ref, gkvl_ref, wuq_ref, wukv_ref, gqn_ref, gqr_ref, gkn_ref, gkr_ref,
                    q_ref, k_ref, v_ref, ob_ref):
    h = _rms(x_ref[...]) * g_ref[...]
    h = (h * (1.0 + sc_ref[...]) + sh_ref[...]).astype(BF16)
    sb = jnp.dot(h, wb_ref[...], preferred_element_type=F32)
    ob_ref[:, :SB_WIDTH] = (sb[:, :SB_WIDTH] * (SB_DIM ** -0.5 * LOG2_E)).astype(ob_ref.dtype)
    ob_ref[:, SB_WIDTH:] = sb[:, SB_WIDTH:].astype(ob_ref.dtype)

    pa = jnp.dot(h, wa_ref[...], preferred_element_type=F32)
    nope_w = MLA_HEADS * MLA_NOPE
    cqn = _rms(pa[:, COL_CQ:COL_CKV]) * gql_ref[...]
    q = jnp.dot(cqn.astype(BF16), wuq_ref[...], preferred_element_type=F32)
    ckvn = _rms(pa[:, COL_CKV:COL_KR]) * gkvl_ref[...]
    kv = jnp.dot(ckvn.astype(BF16), wukv_ref[...], preferred_element_type=F32)
    cosv, s1, s2 = cos_ref[...], s1_ref[...], s2_ref[...]

    def rope(x):
        return (x * cosv + pltpu.roll(x, MLA_ROPE // 2, 1) * s1
                + pltpu.roll(x, LANE - MLA_ROPE // 2, 1) * s2)

    kr = pa[:, COL_KR:]
    kr_ss = jnp.sum(kr * kr, axis=-1, keepdims=True)
    kr_roped = rope(kr * gkr_ref[...])
    gqn, gkn = gqn_ref[...], gkn_ref[...]
    gqr, gqs = gqr_ref[:, :LANE], gqr_ref[:, LANE:]
    s_comb = s1 + s2
    for h in range(MLA_HEADS):
        qn = q[:, h * LANE:(h + 1) * LANE]
        qr = q[:, nope_w + h * LANE:nope_w + (h + 1) * LANE]
        ss = jnp.sum(qn * qn + qr * qr, axis=-1, keepdims=True)
        r = lax.rsqrt(ss * (1.0 / MLA_QK) + EPS)
        q_ref[:, h * QK_PAD:h * QK_PAD + LANE] = (qn * r * gqn).astype(BF16)
        qs = q[:, 2 * nope_w + h * LANE:2 * nope_w + (h + 1) * LANE]
        q_rope = (qr * gqr) * cosv + (qs * gqs) * s_comb
        q_ref[:, h * QK_PAD + LANE:(h + 1) * QK_PAD] = (q_rope * r).astype(BF16)
        kn = kv[:, h * LANE:(h + 1) * LANE]
        rk = lax.rsqrt((jnp.sum(kn * kn, axis=-1, keepdims=True) + kr_ss) * (1.0 / MLA_QK) + EPS)
        k_ref[:, h * QK_PAD:h * QK_PAD + LANE] = (kn * rk * gkn).astype(BF16)
        k_ref[:, h * QK_PAD + LANE:(h + 1) * QK_PAD] = (kr_roped * rk).astype(BF16)
        v_ref[:, 2 * h * MLA_V:(2 * h + 1) * MLA_V] = (
            kv[:, nope_w + h * MLA_V:nope_w + (h + 1) * MLA_V].astype(BF16))
        v_ref[:, (2 * h + 1) * MLA_V:(2 * h + 2) * MLA_V] = jnp.ones((kn.shape[0], MLA_V), BF16)


def _in_proj(x, g, mod, w_mla, w_sb, rope_tabs, gql, gkvl, wuq, wukv, gqn, gqr, gkn, gkr,
             layer, seq, tm=256):
    t, d = x.shape
    na, nb = w_mla.shape[-1], w_sb.shape[-1]
    per_b = seq // tm
    cos_t, s1_t, s2_t = rope_tabs
    row = lambda i: (i, 0)
    lay = lambda i: (layer, 0, 0)
    tab = lambda i: (i % per_b, 0)
    return pl.pallas_call(
        _in_proj_kernel,
        grid=(t // tm,),
        in_specs=[pl.BlockSpec((tm, d), row),
                  pl.BlockSpec((1, d), lambda i: (0, 0)),
                  pl.BlockSpec((None, 1, d), lambda i: ((i // per_b) * 6 + 0, 0, 0)),
                  pl.BlockSpec((None, 1, d), lambda i: ((i // per_b) * 6 + 1, 0, 0)),
                  pl.BlockSpec((None, d, na), lay),
                  pl.BlockSpec((None, d, nb), lay),
                  pl.BlockSpec((tm, LANE), tab),
                  pl.BlockSpec((tm, LANE), tab),
                  pl.BlockSpec((tm, LANE), tab),
                  pl.BlockSpec((None, 1, Q_LORA), lay),
                  pl.BlockSpec((None, 1, KV_LORA), lay),
                  pl.BlockSpec((None,) + wuq.shape[1:], lay),
                  pl.BlockSpec((None,) + wukv.shape[1:], lay),
                  pl.BlockSpec((None, 1, LANE), lay),
                  pl.BlockSpec((None, 1, 2 * LANE), lay),
                  pl.BlockSpec((None, 1, LANE), lay),
                  pl.BlockSpec((None, 1, LANE), lay)],
        out_specs=[pl.BlockSpec((tm, MLA_HEADS * QK_PAD), row),
                   pl.BlockSpec((tm, MLA_HEADS * QK_PAD), row),
                   pl.BlockSpec((tm, MLA_HEADS * 2 * MLA_V), row),
                   pl.BlockSpec((tm, nb), row)],
        out_shape=[jax.ShapeDtypeStruct((t, MLA_HEADS * QK_PAD), BF16),
                   jax.ShapeDtypeStruct((t, MLA_HEADS * QK_PAD), BF16),
                   jax.ShapeDtypeStruct((t, MLA_HEADS * 2 * MLA_V), BF16),
                   jax.ShapeDtypeStruct((t, nb), BF16)],
        compiler_params=_params("arbitrary"),
        name="in_proj",
    )(x, g, mod, mod, w_mla, w_sb, cos_t, s1_t, s2_t, gql, gkvl, wuq, wukv, gqn, gqr, gkn, gkr)


def _mla_attn_kernel(q_ref, k_ref, v_ref, o_ref, m_ref, l_ref, acc_ref, sa_ref, sb_ref, *,
                     tq, heads):
    i = pl.program_id(2)
    vw = 2 * MLA_V

    def scores(g, kb):
        q = q_ref[:, g * QK_PAD:(g + 1) * QK_PAD]
        k = k_ref[pl.ds(pl.multiple_of(kb * tq, tq), tq), g * QK_PAD:(g + 1) * QK_PAD]
        return lax.dot_general(q, k, (((1,), (1,)), ((), ())), preferred_element_type=F32)

    def probs(s, m):
        return jnp.concatenate(
            [jnp.exp2(s[:, c * LANE:(c + 1) * LANE] - m) for c in range(tq // LANE)],
            axis=1).astype(BF16)

    def pv(g, p, kb):
        v = v_ref[pl.ds(pl.multiple_of(kb * tq, tq), tq), g * vw:(g + 1) * vw]
        return jnp.dot(p, v, preferred_element_type=F32)

    def update(g, s, kb):
        m_old = m_ref[g]
        m_new = jnp.maximum(m_old, jnp.max(s, axis=-1, keepdims=True))
        alpha = jnp.exp2(m_old - m_new)
        r = pv(g, probs(s, m_new), kb)
        acc_ref[g] = alpha * acc_ref[g] + r[:, :MLA_V]
        l_ref[g] = alpha * l_ref[g] + r[:, MLA_V:]
        m_ref[g] = m_new

    for g in range(heads):
        m_ref[g] = jnp.full((tq, LANE), -jnp.inf, F32)
        l_ref[g] = jnp.zeros((tq, MLA_V), F32)
        acc_ref[g] = jnp.zeros((tq, MLA_V), F32)
        sa_ref[g] = scores(g, 0)

    def pair(t, carry):
        j = 2 * t
        for g in range(heads):
            sb_ref[g] = scores(g, j + 1)
            update(g, sa_ref[g], j)
        for g in range(heads):
            sa_ref[g] = scores(g, j + 2)
            update(g, sb_ref[g], j + 1)
        return carry

    lax.fori_loop(0, i // 2, pair, 0)

    q_chunk = lax.broadcasted_iota(jnp.int32, (tq, 1), 0) // CHUNK
    k_chunk = lax.broadcasted_iota(jnp.int32, (1, tq), 1) // CHUNK
    visible = k_chunk <= q_chunk

    @pl.when(i % 2 == 1)
    def _():
        for g in range(heads):
            sb_ref[g] = scores(g, i)
            update(g, sa_ref[g], i - 1)
        for g in range(heads):
            update(g, jnp.where(visible, sb_ref[g], -jnp.inf), i)

    @pl.when(i % 2 == 0)
    def _():
        for g in range(heads):
            update(g, jnp.where(visible, sa_ref[g], -jnp.inf), i)

    for g in range(heads):
        o_ref[:, g * MLA_V:(g + 1) * MLA_V] = acc_ref[g] / l_ref[g]


def _mla_attention(q2d, k2d, v2d, bsz, seq, tq=512, heads=4):
    t = q2d.shape[0]
    tq = min(tq, seq)
    nq = seq // tq
    vw = 2 * MLA_V
    return pl.pallas_call(
        functools.partial(_mla_attn_kernel, tq=tq, heads=heads),
        grid=(bsz, MLA_HEADS // heads, nq),
        in_specs=[pl.BlockSpec((tq, heads * QK_PAD), lambda b, h, i: (b * nq + i, h)),
                  pl.BlockSpec((seq, heads * QK_PAD), lambda b, h, i: (b, h)),
                  pl.BlockSpec((seq, heads * vw), lambda b, h, i: (b, h))],
        out_specs=pl.BlockSpec((tq, heads * MLA_V), lambda b, h, i: (b * nq + i, h)),
        out_shape=jax.ShapeDtypeStruct((t, MLA_HEADS * MLA_V), F32),
        scratch_shapes=[pltpu.VMEM((heads, tq, LANE), F32), pltpu.VMEM((heads, tq, MLA_V), F32),
                        pltpu.VMEM((heads, tq, MLA_V), F32),
                        pltpu.VMEM((heads, tq, tq), F32), pltpu.VMEM((heads, tq, tq), F32)],
        compiler_params=_params("arbitrary", "arbitrary", "arbitrary"),
        name="mla_attention",
    )(q2d, k2d, v2d)


def _sb_attn_kernel(q_ref, k_ref, v_ref, tri_ref, o_ref, run_ref, acc_ref, *, tq, tk, heads):
    i = pl.program_id(2)
    tri = tri_ref[...]
    sub = tq // tk
    streams = [(g, r) for g in range(heads) for r in range(sub)]

    def block(s, kb, mask):
        g, r = streams[s]
        cols = slice(g * SB_DIM, (g + 1) * SB_DIM)
        keys = pl.ds(pl.multiple_of(kb * tk, tk), tk)
        q = q_ref[r * tk:(r + 1) * tk, cols]
        z = lax.dot_general(q, k_ref[keys, cols], (((1,), (1,)), ((), ())),
                            preferred_element_type=F32)
        neg_abs = lax.bitcast_convert_type(
            lax.bitcast_convert_type(z, jnp.uint32) | jnp.uint32(0x80000000), F32)
        cost = jnp.maximum(z, 0.0) + jnp.log2(1.0 + jnp.exp2(neg_abs))
        if mask is not None:
            cost = jnp.where(mask, cost, 0.0)
        after = jnp.dot(cost.astype(BF16), tri, preferred_element_type=F32) + run_ref[s]
        a = jnp.exp2(z - cost - after)
        if mask is not None:
            a = jnp.where(mask, a, 0.0)
        acc_ref[s] += jnp.dot(a.astype(BF16), v_ref[keys, cols], preferred_element_type=F32)
        run_ref[s] += jnp.sum(cost, axis=-1, keepdims=True)

    def more():
        return jnp.min(run_ref[...]) < SB_UNDERFLOW_LOG2

    run_ref[...] = jnp.zeros_like(run_ref)
    acc_ref[...] = jnp.zeros_like(acc_ref)
    strictly_before = (lax.broadcasted_iota(jnp.int32, (tk, tk), 1)
                       < lax.broadcasted_iota(jnp.int32, (tk, tk), 0))
    for s, (g, r) in enumerate(streams):
        block(s, i * sub + r, strictly_before)

    def cond(c):
        return (c[0] < i * sub + sub - 1) & c[1]

    def body(c):
        for s, (g, r) in enumerate(streams):
            kb = i * sub + r - 1 - c[0]
            run_ref[s] = jnp.where(kb >= 0, run_ref[s], SB_FINISHED_LOG2)
            block(s, jnp.maximum(kb, 0), None)
        return c[0] + 1, more()

    lax.while_loop(cond, body, (jnp.int32(0), more()))
    for s, (g, r) in enumerate(streams):
        o_ref[r * tk:(r + 1) * tk, g * SB_DIM:(g + 1) * SB_DIM] = acc_ref[s]


def _sb_attention(proj, bsz, seq, tq=512, tk=256, heads=4):
    t = proj.shape[0]
    tq = min(tq, seq)
    nq = seq // tq
    tri = (lax.broadcasted_iota(jnp.int32, (tk, tk), 0)
           > lax.broadcasted_iota(jnp.int32, (tk, tk), 1)).astype(BF16)
    w = heads * SB_DIM
    cq, ck, cv = COL_SQ // w, COL_SK // w, COL_SV // w
    n_streams = heads * (tq // tk)
    return pl.pallas_call(
        functools.partial(_sb_attn_kernel, tq=tq, tk=tk, heads=heads),
        grid=(bsz, SB_HEADS // heads, nq),
        in_specs=[pl.BlockSpec((tq, w), lambda b, h, i: (b * nq + i, cq + h)),
                  pl.BlockSpec((seq, w), lambda b, h, i: (b, ck + h)),
                  pl.BlockSpec((seq, w), lambda b, h, i: (b, cv + h)),
                  pl.BlockSpec((tk, tk), lambda b, h, i: (0, 0))],
        out_specs=pl.BlockSpec((tq, w), lambda b, h, i: (b * nq + i, h)),
        out_shape=jax.ShapeDtypeStruct((t, SB_WIDTH), F32),
        scratch_shapes=[pltpu.VMEM((n_streams, tk, 1), F32),
                        pltpu.VMEM((n_streams, tk, SB_DIM), F32)],
        compiler_params=_params("arbitrary", "arbitrary", "arbitrary"),
        name="sb_attention",
    )(proj, proj, proj, tri)


def _out_proj_kernel(om_ref, os_ref, gm_ref, gs_ref, wo_ref, x_ref, gate_ref, g2_ref,
                     sh_ref, sc_ref, rw_ref, xo_ref, h2_ref, lg_ref):
    wm = om_ref.shape[1]
    nm = (_rms(om_ref[...]) * gm_ref[...]).astype(BF16)
    ns = (_rms(os_ref[...]) * gs_ref[...]).astype(BF16)
    y = (jnp.dot(nm, wo_ref[:wm, :], preferred_element_type=F32)
         + jnp.dot(ns, wo_ref[wm:, :], preferred_element_type=F32))
    xn = x_ref[...] + gate_ref[...] * y
    xo_ref[...] = xn
    h2 = _rms(xn) * g2_ref[...]
    h2 = h2 * (1.0 + sc_ref[...]) + sh_ref[...]
    h2_ref[...] = _pack_rows(h2)
    lg_ref[...] = lax.dot_general(rw_ref[...], h2.astype(BF16), (((1,), (1,)), ((), ())),
                                  preferred_element_type=F32)


def _out_proj(o_mla, o_sb, gm, gs, w_o, x, mod, g2, rw_t, layer, seq, tm=512):
    t, d = x.shape
    tm = min(tm, seq)
    per_b = seq // tm
    row = lambda i: (i, 0)
    full2 = lambda i: (0, 0)
    modspec = lambda j: pl.BlockSpec((None, 1, d), lambda i: ((i // per_b) * 6 + j, 0, 0))
    return pl.pallas_call(
        _out_proj_kernel,
        grid=(t // tm,),
        in_specs=[pl.BlockSpec((tm, o_mla.shape[1]), row),
                  pl.BlockSpec((tm, o_sb.shape[1]), row),
                  pl.BlockSpec((1, o_mla.shape[1]), full2),
                  pl.BlockSpec((1, o_sb.shape[1]), full2),
                  pl.BlockSpec((None,) + w_o.shape[1:], lambda i: (layer, 0, 0)),
                  pl.BlockSpec((tm, d), row),
                  modspec(2),
                  pl.BlockSpec((1, d), full2),
                  modspec(3),
                  modspec(4),
                  pl.BlockSpec(rw_t.shape, full2)],
        out_specs=[pl.BlockSpec((tm, d), row),
                   pl.BlockSpec((tm, d // 2), row),
                   pl.BlockSpec((N_EXPERTS, tm), lambda i: (0, i))],
        out_shape=[jax.ShapeDtypeStruct((t, d), F32),
                   jax.ShapeDtypeStruct((t, d // 2), jnp.uint32),
                   jax.ShapeDtypeStruct((N_EXPERTS, t), F32)],
        compiler_params=_params("arbitrary"),
        name="out_proj",
    )(o_mla, o_sb, gm, gs, w_o, x, mod, g2, mod, mod, rw_t)


def _route_kernel(lg_ref, rb_ref, tri_ref, e_ref, w_ref, rank_ref, cnt_ref, run_ref):
    i = pl.program_id(0)

    @pl.when(i == 0)
    def _():
        run_ref[...] = jnp.zeros_like(run_ref)

    scores = jax.nn.sigmoid(lg_ref[...])
    sel = scores + rb_ref[...]
    tm = sel.shape[1]
    epg = EXPERTS_PER_GROUP
    srow = [sel[e:e + 1, :] for e in range(N_EXPERTS)]
    prow = [scores[e:e + 1, :] for e in range(N_EXPERTS)]

    gscore = []
    for g in range(N_GROUPS):
        a, b, c, d = srow[g * epg:(g + 1) * epg]
        hi1, lo1 = jnp.maximum(a, b), jnp.minimum(a, b)
        hi2, lo2 = jnp.maximum(c, d), jnp.minimum(c, d)
        gscore.append(jnp.maximum(hi1, hi2)
                      + jnp.maximum(jnp.minimum(hi1, hi2), jnp.maximum(lo1, lo2)))
    best = gscore[0]
    gidx = jnp.zeros((1, tm), jnp.int32)
    for g in range(1, N_GROUPS):
        better = gscore[g] > best
        gidx = jnp.where(better, g, gidx)
        best = jnp.where(better, gscore[g], best)

    def pick(rows, j):
        out = rows[j]
        for g in range(1, N_GROUPS):
            out = jnp.where(gidx == g, rows[g * epg + j], out)
        return out

    v = [pick(srow, j) for j in range(epg)]
    p = [pick(prow, j) for j in range(epg)]
    pos = []
    for j in range(epg):
        r = jnp.zeros((1, tm), jnp.int32)
        for o in range(epg):
            if o == j:
                continue
            ahead = (v[o] > v[j]) | ((v[o] == v[j]) & (o < j))
            r = r + ahead.astype(jnp.int32)
        pos.append(r)
    local, wsel = [], []
    for k in range(TOP_K):
        lk = jnp.zeros((1, tm), jnp.int32)
        wk = jnp.zeros((1, tm), F32)
        for j in range(epg):
            hit = pos[j] == k
            lk = jnp.where(hit, j, lk)
            wk = jnp.where(hit, p[j], wk)
        local.append(lk)
        wsel.append(wk)
    wsum = wsel[0] + wsel[1]
    idx = [gidx * epg + local[k] for k in range(TOP_K)]

    eiota = lax.broadcasted_iota(jnp.int32, (N_EXPERTS, tm), 0)
    hit = [eiota == idx[k] for k in range(TOP_K)]
    onehot = (hit[0] | hit[1]).astype(F32)
    before = jnp.dot(onehot.astype(BF16), tri_ref[...], preferred_element_type=F32) + run_ref[...]
    for k in range(TOP_K):
        e_ref[pl.ds(k, 1), :] = idx[k]
        w_ref[pl.ds(k, 1), :] = wsel[k] / wsum
        rank_ref[pl.ds(k, 1), :] = jnp.sum(jnp.where(hit[k], before, 0.0), axis=0,
                                           keepdims=True).astype(jnp.int32)
    run_ref[...] += jnp.sum(onehot, axis=1, keepdims=True)
    cnt_ref[...] = jnp.broadcast_to(run_ref[...], cnt_ref.shape)


def _route(logits_t, router_b, tm=512):
    t = logits_t.shape[1]
    tm = min(tm, t)
    tri = (lax.broadcasted_iota(jnp.int32, (tm, tm), 0)
           < lax.broadcasted_iota(jnp.int32, (tm, tm), 1)).astype(BF16)
    blk = lambda rows: pl.BlockSpec((rows, tm), lambda i: (0, i))
    return pl.pallas_call(
        _route_kernel,
        grid=(t // tm,),
        in_specs=[blk(N_EXPERTS),
                  pl.BlockSpec((N_EXPERTS, 1), lambda i: (0, 0)),
                  pl.BlockSpec((tm, tm), lambda i: (0, 0))],
        out_specs=[blk(TOP_K), blk(TOP_K), blk(TOP_K),
                   pl.BlockSpec((N_EXPERTS, LANE), lambda i: (0, 0))],
        out_shape=[jax.ShapeDtypeStruct((TOP_K, t), jnp.int32),
                   jax.ShapeDtypeStruct((TOP_K, t), F32),
                   jax.ShapeDtypeStruct((TOP_K, t), jnp.int32),
                   jax.ShapeDtypeStruct((N_EXPERTS, LANE), F32)],
        scratch_shapes=[pltpu.VMEM((N_EXPERTS, 1), F32)],
        compiler_params=_params("arbitrary"),
        name="route",
    )(logits_t, router_b[:, None], tri)


def _dispatch_kernel(dest_ref, pend_ref, h_ref, o_ref, zero_ref, ring, in_sems, row_sems, zsem,
                     *, tm, t):
    base = pl.program_id(0) * tm

    @pl.when(pl.program_id(0) == 0)
    def _():
        zero_ref[...] = jnp.zeros_like(zero_ref)

        def fill(e):
            end = pend_ref[e]
            start = pend_ref[e - 1] if e > 0 else 0
            dst = o_ref.at[pl.ds(pl.multiple_of(jnp.maximum(end - MOE_ROWS, 0), MOE_ROWS),
                                 MOE_ROWS), :]
            return end > start, pltpu.make_async_copy(zero_ref, dst, zsem)

        for e in range(N_EXPERTS):
            nonempty, cp = fill(e)
            pl.when(nonempty)(cp.start)
        for e in range(N_EXPERTS):
            nonempty, cp = fill(e)
            pl.when(nonempty)(cp.wait)

        def tail(r):
            dst = o_ref.at[pl.ds(pl.multiple_of(r * MOE_ROWS, MOE_ROWS), MOE_ROWS), :]
            return pltpu.make_async_copy(zero_ref, dst, zsem)

        first_free = pend_ref[N_EXPERTS - 1] // MOE_ROWS
        n_blocks = o_ref.shape[0] // MOE_ROWS
        lax.fori_loop(first_free, n_blocks, lambda r, c: (tail(r).start(), c)[1], 0)
        lax.fori_loop(first_free, n_blocks, lambda r, c: (tail(r).wait(), c)[1], 0)

    i = pl.program_id(0)
    n = pl.num_programs(0)

    def stage(tile):
        return pltpu.make_async_copy(h_ref.at[pl.ds(tile * tm, tm), :], ring.at[tile % 3],
                                     in_sems.at[tile % 3])

    @pl.when(i == 0)
    def _():
        stage(0).start()

    @pl.when(i + 1 < n)
    def _():
        stage(i + 1).start()

    stage(i).wait()
    src = ring.at[i % 3]

    def issue(r, carry):
        for k in range(TOP_K):
            d = dest_ref[k * t + base + r]
            pltpu.make_async_copy(src.at[pl.ds(r, 1), :], o_ref.at[pl.ds(d, 1), :],
                                  row_sems.at[i % 2]).start()
        return carry

    lax.fori_loop(0, tm, issue, 0, unroll=DMA_ISSUE_UNROLL)

    def wait_rows(step):
        for k in range(TOP_K):
            pltpu.make_async_copy(ring.at[0], o_ref.at[pl.ds(0, tm), :],
                                  row_sems.at[step % 2]).wait()

    pl.when(i > 0)(lambda: wait_rows(i - 1))
    pl.when(i == n - 1)(lambda: wait_rows(i))


def _dispatch(h2, dest_flat, pad_end, rows, tm=256):
    t, d = h2.shape
    return pl.pallas_call(
        functools.partial(_dispatch_kernel, tm=tm, t=t),
        grid_spec=pltpu.PrefetchScalarGridSpec(
            num_scalar_prefetch=2,
            grid=(t // tm,),
            in_specs=[pl.BlockSpec(memory_space=pl.ANY)],
            out_specs=pl.BlockSpec(memory_space=pl.ANY),
            scratch_shapes=[pltpu.VMEM((MOE_ROWS, d), h2.dtype),
                            pltpu.VMEM((3, tm, d), h2.dtype),
                            pltpu.SemaphoreType.DMA((3,)), pltpu.SemaphoreType.DMA((2,)),
                            pltpu.SemaphoreType.DMA(())]),
        out_shape=jax.ShapeDtypeStruct((rows, d), h2.dtype),
        compiler_params=_params("arbitrary"),
        name="moe_dispatch",
    )(dest_flat, pad_end, h2)


def _expert_kernel(be_ref, nu_ref, next_ref, slot_ref, x_ref, wg_hbm, wu_hbm, wd_hbm, y_ref,
                   wgf, wuf, wdf, wgb, wub, wdb, sems, *, layer):
    r = pl.program_id(0)
    live = r < nu_ref[0]
    e = be_ref[r]
    new_expert = (r == 0) | (be_ref[jnp.maximum(r - 1, 0)] != e)
    slot = slot_ref[e]

    def fetch(expert, s):
        pairs = ((wg_hbm, wgf), (wu_hbm, wuf), (wd_hbm, wdf))
        return [pltpu.make_async_copy(w.at[layer, expert], buf.at[s], sems.at[s, j])
                for j, (w, buf) in enumerate(pairs)]

    @pl.when(r == 0)
    def _():
        for cp in fetch(e, slot):
            cp.start()

    @pl.when(live & new_expert)
    def _():
        nxt = next_ref[e]

        @pl.when(nxt != e)
        def _():
            for cp in fetch(nxt, 1 - slot):
                cp.start()

        for cp in fetch(e, slot):
            cp.wait()
        wgb[...] = wgf[slot].astype(BF16)
        wub[...] = wuf[slot].astype(BF16)
        wdb[...] = wdf[slot].astype(BF16)

    @pl.when(live)
    def _():
        lo, hi = _unpack_rows(x_ref[...])
        xb = jnp.concatenate([lo.astype(BF16), hi.astype(BF16)], axis=1)
        g = jnp.dot(xb, wgb[...], preferred_element_type=F32)
        u = jnp.dot(xb, wub[...], preferred_element_type=F32)
        a = (g * jax.nn.sigmoid(g)) * u
        y_ref[...] = _pack_rows(jnp.dot(a.astype(BF16), wdb[...], preferred_element_type=F32))

    @pl.when(jnp.logical_not(live))
    def _():
        y_ref[...] = jnp.zeros_like(y_ref)


def _experts(x_sorted, block_e, n_used, next_expert, expert_slot, w_gate, w_up, w_down, layer):
    rows, d = x_sorted.shape
    de = w_gate.shape[-1]
    dm = w_gate.shape[-2]
    bm = MOE_ROWS
    blk = lambda r, be, nu, nx, sl: (jnp.minimum(r, nu[0] - 1), 0)
    hbm = pl.BlockSpec(memory_space=pl.ANY)
    return pl.pallas_call(
        functools.partial(_expert_kernel, layer=layer),
        grid_spec=pltpu.PrefetchScalarGridSpec(
            num_scalar_prefetch=4,
            grid=(rows // bm,),
            in_specs=[pl.BlockSpec((bm, d), blk), hbm, hbm, hbm],
            out_specs=pl.BlockSpec((bm, d), lambda r, be, nu, nx, sl: (r, 0)),
            scratch_shapes=[pltpu.VMEM((2, dm, de), F32), pltpu.VMEM((2, dm, de), F32),
                            pltpu.VMEM((2, de, dm), F32),
                            pltpu.VMEM((dm, de), BF16), pltpu.VMEM((dm, de), BF16),
                            pltpu.VMEM((de, dm), BF16),
                            pltpu.SemaphoreType.DMA((2, 3))]),
        out_shape=jax.ShapeDtypeStruct((rows, d), jnp.uint32),
        compiler_params=_params("arbitrary"),
        name="moe_experts",
    )(block_e, n_used, next_expert, expert_slot, x_sorted, w_gate, w_up, w_down)


def _combine_kernel(dest_ref, y_ref, x_ref, gate_ref, w_ref, o_ref, buf, sems, *, tm, t):
    i = pl.program_id(0)
    slot = i % 2

    def gather(tile, dst_slot):
        base = tile * tm

        def issue(r, carry):
            for k in range(TOP_K):
                d = dest_ref[k * t + base + r]
                pltpu.make_async_copy(y_ref.at[pl.ds(d, 1), :],
                                      buf.at[dst_slot, k, pl.ds(r, 1), :],
                                      sems.at[dst_slot]).start()
            return carry

        lax.fori_loop(0, tm, issue, 0, unroll=DMA_ISSUE_UNROLL)

    @pl.when(i == 0)
    def _():
        gather(0, 0)

    @pl.when(i + 1 < pl.num_programs(0))
    def _():
        gather(i + 1, 1 - slot)

    for k in range(TOP_K):
        pltpu.make_async_copy(y_ref.at[pl.ds(0, tm), :], buf.at[slot, k], sems.at[slot]).wait()
    w = w_ref[...]
    half = buf.shape[3]
    lo0, hi0 = _unpack_rows(buf[slot, 0])
    lo1, hi1 = _unpack_rows(buf[slot, 1])
    o_ref[:, :half] = x_ref[:, :half] + gate_ref[:, :half] * (w[:, 0:1] * lo0 + w[:, 1:2] * lo1)
    o_ref[:, half:] = x_ref[:, half:] + gate_ref[:, half:] * (w[:, 0:1] * hi0 + w[:, 1:2] * hi1)


def _combine(y_sorted, dest_flat, x, mod, w_tok, seq, tm=256):
    t, d = x.shape
    per_b = seq // tm
    return pl.pallas_call(
        functools.partial(_combine_kernel, tm=tm, t=t),
        grid_spec=pltpu.PrefetchScalarGridSpec(
            num_scalar_prefetch=1,
            grid=(t // tm,),
            in_specs=[pl.BlockSpec(memory_space=pl.ANY),
                      pl.BlockSpec((tm, d), lambda i, dest: (i, 0)),
                      pl.BlockSpec((None, 1, d), lambda i, dest: ((i // per_b) * 6 + 5, 0, 0)),
                      pl.BlockSpec((tm, TOP_K), lambda i, dest: (i, 0))],
            out_specs=pl.BlockSpec((tm, d), lambda i, dest: (i, 0)),
            scratch_shapes=[pltpu.VMEM((2, TOP_K, tm, y_sorted.shape[1]), y_sorted.dtype),
                            pltpu.SemaphoreType.DMA((2,))]),
        out_shape=jax.ShapeDtypeStruct((t, d), F32),
        compiler_params=_params("arbitrary"),
        name="moe_combine",
    )(dest_flat, y_sorted, x, mod, w_tok)


def _rope_tables(seq):
    half = MLA_ROPE // 2
    inv = 1.0 / (ROPE_BASE ** (jnp.arange(0, MLA_ROPE, 2, dtype=F32) / MLA_ROPE))
    ang = jnp.arange(seq, dtype=F32)[:, None] * inv[None, :]
    cos, sin = jnp.cos(ang), jnp.sin(ang)
    z = jnp.zeros((seq, half), F32)
    zz = jnp.zeros((seq, LANE - MLA_ROPE), F32)
    cos_t = jnp.concatenate([cos, cos, zz], axis=1)
    s1_t = jnp.concatenate([z, sin, zz], axis=1)
    s2_t = jnp.concatenate([-sin, z, zz], axis=1)
    return cos_t, s1_t, s2_t


def _pad_last(a, width):
    return jnp.pad(a, [(0, 0)] * (a.ndim - 1) + [(0, width - a.shape[-1])])


def kernel(x, c, norm1_g, norm2_g, w_ada, b_ada, w_in, q_lora_g, kv_lora_g, w_uq, w_ukv,
           q_head_g, k_head_g, mla_out_g, sb_out_g, w_o, router_w, router_b,
           w_gate, w_up, w_down):
    bsz, seq, d = x.shape
    t = bsz * seq
    depth = w_ada.shape[0]

    sb_col0 = COL_KR + MLA_ROPE
    w_in_mla = _pad_last(w_in[..., :sb_col0], MLA_IN_COLS).astype(BF16)
    w_in_sb = w_in[..., sb_col0:].astype(BF16)
    wq = w_uq.reshape(depth, Q_LORA, MLA_HEADS, MLA_QK)
    half = MLA_ROPE // 2
    swap = lambda a: jnp.concatenate([a[..., half:], a[..., :half]], axis=-1)
    wq_rope = wq[..., MLA_NOPE:]
    wuq_p = jnp.concatenate(
        [wq[..., :MLA_NOPE].reshape(depth, Q_LORA, -1),
         _pad_last(wq_rope, LANE).reshape(depth, Q_LORA, -1),
         _pad_last(swap(wq_rope), LANE).reshape(depth, Q_LORA, -1)], axis=-1).astype(BF16)
    wkv = w_ukv.reshape(depth, KV_LORA, MLA_HEADS, MLA_NOPE + MLA_V)
    wukv_p = jnp.concatenate(
        [wkv[..., :MLA_NOPE].reshape(depth, KV_LORA, -1),
         wkv[..., MLA_NOPE:].reshape(depth, KV_LORA, -1)], axis=-1).astype(BF16)
    w_o_b = w_o.astype(BF16)
    q_scale = MLA_QK ** -0.5 * LOG2_E
    gqn = (q_head_g[:, None, :MLA_NOPE] * q_scale)
    gq_rope = q_head_g[:, None, MLA_NOPE:] * q_scale
    gqr = jnp.concatenate([_pad_last(gq_rope, LANE), _pad_last(swap(gq_rope), LANE)], axis=-1)
    gkn = k_head_g[:, None, :MLA_NOPE]
    gkr = _pad_last(k_head_g[:, None, MLA_NOPE:], LANE)
    rope_tabs = _rope_tables(seq)
    rw_t = router_w.T.astype(BF16)

    mod_all = _ada_modulation(c[:, :, None], w_ada, b_ada[:, None, :], layers=depth)
    n_blocks = -(-(t * TOP_K) // MOE_ROWS) + N_EXPERTS
    rows = n_blocks * MOE_ROWS

    xt = x.reshape(t, d)
    for l in range(depth):
        mod = mod_all[l].reshape(bsz * 6, 1, d)
        q2d, k2d, v2d, proj_sb = _in_proj(
            xt, norm1_g[l][None, :], mod, w_in_mla, w_in_sb, rope_tabs, q_lora_g[:, None, :],
            kv_lora_g[:, None, :], wuq_p, wukv_p, gqn, gqr, gkn, gkr, l, seq)
        o_mla = _mla_attention(q2d, k2d, v2d, bsz, seq)
        o_sb = _sb_attention(proj_sb, bsz, seq)
        xt, h2, logits_t = _out_proj(o_mla, o_sb, mla_out_g[l][None, :], sb_out_g[l][None, :],
                                     w_o_b, xt, mod, norm2_g[l][None, :], rw_t, l, seq)
        e_idx, w_tok, rank, cnt = _route(logits_t, router_b)
        counts = cnt[:, 0].astype(jnp.int32)
        padded = (counts + MOE_ROWS - 1) // MOE_ROWS * MOE_ROWS
        pad_end = jnp.cumsum(padded)
        pad_start = pad_end - padded
        eids = jnp.arange(N_EXPERTS, dtype=jnp.int32)
        seg_start = jnp.sum(jnp.where(e_idx[None] == eids[:, None, None],
                                      pad_start[:, None, None], 0), axis=0)
        dest = (seg_start + rank).reshape(TOP_K * t).astype(jnp.int32)
        n_used = (pad_end[-1:] // MOE_ROWS).astype(jnp.int32)
        first_row = jnp.minimum(jnp.arange(n_blocks, dtype=jnp.int32), n_used - 1) * MOE_ROWS
        block_e = jnp.sum(pad_end[None, :] <= first_row[:, None], axis=1).astype(jnp.int32)
        nonempty = counts > 0
        later = (eids[None, :] > eids[:, None]) & nonempty[None, :]
        next_expert = jnp.min(jnp.where(later, eids[None, :], N_EXPERTS), axis=1)
        next_expert = jnp.where(next_expert == N_EXPERTS, eids, next_expert).astype(jnp.int32)
        expert_slot = ((jnp.cumsum(nonempty) - nonempty) % 2).astype(jnp.int32)
        x_sorted = _dispatch(h2, dest, pad_end.astype(jnp.int32), rows)
        y_sorted = _experts(x_sorted, block_e, n_used, next_expert, expert_slot,
                            w_gate, w_up, w_down, l)
        xt = _combine(y_sorted, dest, xt, mod, w_tok.T, seq)
    return xt.reshape(bsz, seq, d)
```

```python
import functools

import jax
import jax.numpy as jnp
from jax import lax
from jax.experimental import pallas as pl
from jax.experimental.pallas import tpu as pltpu

F32 = jnp.float32
BF16 = jnp.bfloat16

EPS = 1e-6
ROPE_BASE = 10000.0
CHUNK = 64

MLA_HEADS = 8
MLA_NOPE = 128
MLA_ROPE = 64
MLA_QK = MLA_NOPE + MLA_ROPE
MLA_V = 128
Q_LORA = 512
KV_LORA = 256
SB_HEADS = 8
SB_DIM = 128
SB_WIDTH = SB_HEADS * SB_DIM
LANE = 128
QK_PAD = 2 * LANE
COL_CQ = 0
COL_CKV = Q_LORA
COL_KR = Q_LORA + KV_LORA
MLA_IN_COLS = COL_KR + LANE
COL_SQ = 0
COL_SK = SB_WIDTH
COL_SV = 2 * SB_WIDTH
SB_IN_COLS = 3 * SB_WIDTH

N_EXPERTS = 16
N_GROUPS = 4
EXPERTS_PER_GROUP = N_EXPERTS // N_GROUPS
TOP_K = 2
D_EXPERT = 512
MOE_ROWS = 512
DMA_ISSUE_UNROLL = 8
LOG2_E = 1.4426950408889634
SB_UNDERFLOW_LOG2 = 150.0
SB_FINISHED_LOG2 = 1e30

VMEM_LIMIT = 56 * 1024 * 1024


def _params(*dims):
    return pltpu.CompilerParams(dimension_semantics=dims, vmem_limit_bytes=VMEM_LIMIT)


def _rms(x):
    return x * lax.rsqrt(jnp.mean(x * x, axis=-1, keepdims=True) + EPS)


def _pack_rows(x):
    n = x.shape[1] // 2
    lo = lax.bitcast_convert_type(x[:, :n].astype(BF16).astype(F32), jnp.uint32)
    hi = lax.bitcast_convert_type(x[:, n:].astype(BF16).astype(F32), jnp.uint32)
    return (lo >> 16) | (hi & jnp.uint32(0xFFFF0000))


def _unpack_rows(w):
    lo = lax.bitcast_convert_type(w << 16, F32)
    hi = lax.bitcast_convert_type(w & jnp.uint32(0xFFFF0000), F32)
    return lo, hi


def _ada_kernel(c_ref, w_ref, b_ref, o_ref):
    w = w_ref[...]
    for b in range(c_ref.shape[0]):
        cb = c_ref[b]
        cb = cb * jax.nn.sigmoid(cb)
        o_ref[pl.ds(b, 1), :] = jnp.sum(cb * w, axis=0, keepdims=True) + b_ref[...]


def _ada_modulation(c_col, w_ada, b_ada3, layers, tn=1024):
    _, d, n = w_ada.shape
    bsz = c_col.shape[0]
    return pl.pallas_call(
        _ada_kernel,
        grid=(layers, n // tn),
        in_specs=[pl.BlockSpec(c_col.shape, lambda l, j: (0, 0, 0)),
                  pl.BlockSpec((None, d, tn), lambda l, j: (l, 0, j)),
                  pl.BlockSpec((None, 1, tn), lambda l, j: (l, 0, j))],
        out_specs=pl.BlockSpec((None, bsz, tn), lambda l, j: (l, 0, j)),
        out_shape=jax.ShapeDtypeStruct((layers, bsz, n), F32),
        compiler_params=_params("arbitrary", "arbitrary"),
        name="ada_modulation",
    )(c_col, w_ada, b_ada3)


def _in_proj_kernel(x_ref, g_ref, sh_ref, sc_ref, wa_ref, wb_ref, cos_ref, s1_ref, s2_ref,
                    gql_ref, gkvl_ref, wuq_ref, wukv_ref, gqn_ref, gqr_ref, gkn_ref, gkr_ref,
                    q_ref, k_ref, v_ref, ob_ref):
    h = _rms(x_ref[...]) * g_ref[...]
    h = (h * (1.0 + sc_ref[...]) + sh_ref[...]).astype(BF16)
    sb = jnp.dot(h, wb_ref[...], preferred_element_type=F32)
    ob_ref[:, :SB_WIDTH] = (sb[:, :SB_WIDTH] * (SB_DIM ** -0.5 * LOG2_E)).astype(ob_ref.dtype)
    ob_ref[:, SB_WIDTH:] = sb[:, SB_WIDTH:].astype(ob_ref.dtype)

    pa = jnp.dot(h, wa_ref[...], preferred_element_type=F32)
    nope_w = MLA_HEADS * MLA_NOPE
    cqn = _rms(pa[:, COL_CQ:COL_CKV]) * gql_ref[...]
    q = jnp.dot(cqn.astype(BF16), wuq_ref[...], preferred_element_type=F32)
    ckvn = _rms(pa[:, COL_CKV:COL_KR]) * gkvl_ref[...]
    kv = jnp.dot(ckvn.astype(BF16), wukv_ref[...], preferred_element_type=F32)
    cosv, s1, s2 = cos_ref[...], s1_ref[...], s2_ref[...]

    def rope(x):
        return (x * cosv + pltpu.roll(x, MLA_ROPE // 2, 1) * s1
                + pltpu.roll(x, LANE - MLA_ROPE // 2, 1) * s2)

    kr = pa[:, COL_KR:]
    kr_ss = jnp.sum(kr * kr, axis=-1, keepdims=True)
    kr_roped = rope(kr * gkr_ref[...])
    gqn, gkn = gqn_ref[...], gkn_ref[...]
    gqr, gqs = gqr_ref[:, :LANE], gqr_ref[:, LANE:]
    s_comb = s1 + s2
    for h in range(MLA_HEADS):
        qn = q[:, h * LANE:(h + 1) * LANE]
        qr = q[:, nope_w + h * LANE:nope_w + (h + 1) * LANE]
        ss = jnp.sum(qn * qn + qr * qr, axis=-1, keepdims=True)
        r = lax.rsqrt(ss * (1.0 / MLA_QK) + EPS)
        q_ref[:, h * QK_PAD:h * QK_PAD + LANE] = (qn * r * gqn).astype(BF16)
        qs = q[:, 2 * nope_w + h * LANE:2 * nope_w + (h + 1) * LANE]
        q_rope = (qr * gqr) * cosv + (qs * gqs) * s_comb
        q_ref[:, h * QK_PAD + LANE:(h + 1) * QK_PAD] = (q_rope * r).astype(BF16)
        kn = kv[:, h * LANE:(h + 1) * LANE]
        rk = lax.rsqrt((jnp.sum(kn * kn, axis=-1, keepdims=True) + kr_ss) * (1.0 / MLA_QK) + EPS)
        k_ref[:, h * QK_PAD:h * QK_PAD + LANE] = (kn * rk * gkn).astype(BF16)
        k_ref[:, h * QK_PAD + LANE:(h + 1) * QK_PAD] = (kr_roped * rk).astype(BF16)
        v_ref[:, 2 * h * MLA_V:(2 * h + 1) * MLA_V] = (
            kv[:, nope_w + h * MLA_V:nope_w + (h + 1) * MLA_V].astype(BF16))
        v_ref[:, (2 * h + 1) * MLA_V:(2 * h + 2) * MLA_V] = jnp.ones((kn.shape[0], MLA_V), BF16)


def _in_proj(x, g, mod, w_mla, w_sb, rope_tabs, gql, gkvl, wuq, wukv, gqn, gqr, gkn, gkr,
             layer, seq, tm=256):
    t, d = x.shape
    na, nb = w_mla.shape[-1], w_sb.shape[-1]
    per_b = seq // tm
    cos_t, s1_t, s2_t = rope_tabs
    row = lambda i: (i, 0)
    lay = lambda i: (layer, 0, 0)
    tab = lambda i: (i % per_b, 0)
    return pl.pallas_call(
        _in_proj_kernel,
        grid=(t // tm,),
        in_specs=[pl.BlockSpec((tm, d), row),
                  pl.BlockSpec((1, d), lambda i: (0, 0)),
                  pl.BlockSpec((None, 1, d), lambda i: ((i // per_b) * 6 + 0, 0, 0)),
                  pl.BlockSpec((None, 1, d), lambda i: ((i // per_b) * 6 + 1, 0, 0)),
                  pl.BlockSpec((None, d, na), lay),
                  pl.BlockSpec((None, d, nb), lay),
                  pl.BlockSpec((tm, LANE), tab),
                  pl.BlockSpec((tm, LANE), tab),
                  pl.BlockSpec((tm, LANE), tab),
                  pl.BlockSpec((None, 1, Q_LORA), lay),
                  pl.BlockSpec((None, 1, KV_LORA), lay),
                  pl.BlockSpec((None,) + wuq.shape[1:], lay),
                  pl.BlockSpec((None,) + wukv.shape[1:], lay),
                  pl.BlockSpec((None, 1, LANE), lay),
                  pl.BlockSpec((None, 1, 2 * LANE), lay),
                  pl.BlockSpec((None, 1, LANE), lay),
                  pl.BlockSpec((None, 1, LANE), lay)],
        out_specs=[pl.BlockSpec((tm, MLA_HEADS * QK_PAD), row),
                   pl.BlockSpec((tm, MLA_HEADS * QK_PAD), row),
                   pl.BlockSpec((tm, MLA_HEADS * 2 * MLA_V), row),
                   pl.BlockSpec((tm, nb), row)],
        out_shape=[jax.ShapeDtypeStruct((t, MLA_HEADS * QK_PAD), BF16),
                   jax.ShapeDtypeStruct((t, MLA_HEADS * QK_PAD), BF16),
                   jax.ShapeDtypeStruct((t, MLA_HEADS * 2 * MLA_V), BF16),
                   jax.ShapeDtypeStruct((t, nb), BF16)],
        compiler_params=_params("arbitrary"),
        name="in_proj",
    )(x, g, mod, mod, w_mla, w_sb, cos_t, s1_t, s2_t, gql, gkvl, wuq, wukv, gqn, gqr, gkn, gkr)


def _mla_attn_kernel(q_ref, k_ref, v_ref, o_ref, m_ref, l_ref, acc_ref, sa_ref, sb_ref, *,
                     tq, heads):
    i = pl.program_id(2)
    vw = 2 * MLA_V

    def scores(g, kb):
        q = q_ref[:, g * QK_PAD:(g + 1) * QK_PAD]
        k = k_ref[pl.ds(pl.multiple_of(kb * tq, tq), tq), g * QK_PAD:(g + 1) * QK_PAD]
        return lax.dot_general(q, k, (((1,), (1,)), ((), ())), preferred_element_type=F32)

    def probs(s, m):
        return jnp.concatenate(
            [jnp.exp2(s[:, c * LANE:(c + 1) * LANE] - m) for c in range(tq // LANE)],
            axis=1).astype(BF16)

    def pv(g, p, kb):
        v = v_ref[pl.ds(pl.multiple_of(kb * tq, tq), tq), g * vw:(g + 1) * vw]
        return jnp.dot(p, v, preferred_element_type=F32)

    def update(g, s, kb):
        m_old = m_ref[g]
        m_new = jnp.maximum(m_old, jnp.max(s, axis=-1, keepdims=True))
        alpha = jnp.exp2(m_old - m_new)
        r = pv(g, probs(s, m_new), kb)
        acc_ref[g] = alpha * acc_ref[g] + r[:, :MLA_V]
        l_ref[g] = alpha * l_ref[g] + r[:, MLA_V:]
        m_ref[g] = m_new

    for g in range(heads):
        m_ref[g] = jnp.full((tq, LANE), -jnp.inf, F32)
        l_ref[g] = jnp.zeros((tq, MLA_V), F32)
        acc_ref[g] = jnp.zeros((tq, MLA_V), F32)
        sa_ref[g] = scores(g, 0)

    def pair(t, carry):
        j = 2 * t
        for g in range(heads):
            sb_ref[g] = scores(g, j + 1)
            update(g, sa_ref[g], j)
        for g in range(heads):
            sa_ref[g] = scores(g, j + 2)
            update(g, sb_ref[g], j + 1)
        return carry

    lax.fori_loop(0, i // 2, pair, 0)

    q_chunk = lax.broadcasted_iota(jnp.int32, (tq, 1), 0) // CHUNK
    k_chunk = lax.broadcasted_iota(jnp.int32, (1, tq), 1) // CHUNK
    visible = k_chunk <= q_chunk

    @pl.when(i % 2 == 1)
    def _():
        for g in range(heads):
            sb_ref[g] = scores(g, i)
            update(g, sa_ref[g], i - 1)
        for g in range(heads):
            update(g, jnp.where(visible, sb_ref[g], -jnp.inf), i)

    @pl.when(i % 2 == 0)
    def _():
        for g in range(heads):
            update(g, jnp.where(visible, sa_ref[g], -jnp.inf), i)

    for g in range(heads):
        o_ref[:, g * MLA_V:(g + 1) * MLA_V] = acc_ref[g] / l_ref[g]


def _mla_attention(q2d, k2d, v2d, bsz, seq, tq=512, heads=4):
    t = q2d.shape[0]
    tq = min(tq, seq)
    nq = seq // tq
    vw = 2 * MLA_V
    return pl.pallas_call(
        functools.partial(_mla_attn_kernel, tq=tq, heads=heads),
        grid=(bsz, MLA_HEADS // heads, nq),
        in_specs=[pl.BlockSpec((tq, heads * QK_PAD), lambda b, h, i: (b * nq + i, h)),
                  pl.BlockSpec((seq, heads * QK_PAD), lambda b, h, i: (b, h)),
                  pl.BlockSpec((seq, heads * vw), lambda b, h, i: (b, h))],
        out_specs=pl.BlockSpec((tq, heads * MLA_V), lambda b, h, i: (b * nq + i, h)),
        out_shape=jax.ShapeDtypeStruct((t, MLA_HEADS * MLA_V), F32),
        scratch_shapes=[pltpu.VMEM((heads, tq, LANE), F32), pltpu.VMEM((heads, tq, MLA_V), F32),
                        pltpu.VMEM((heads, tq, MLA_V), F32),
                        pltpu.VMEM((heads, tq, tq), F32), pltpu.VMEM((heads, tq, tq), F32)],
        compiler_params=_params("arbitrary", "arbitrary", "arbitrary"),
        name="mla_attention",
    )(q2d, k2d, v2d)


def _sb_attn_kernel(q_ref, k_ref, v_ref, tri_ref, o_ref, run_ref, acc_ref, *, tq, tk, heads):
    i = pl.program_id(2)
    tri = tri_ref[...]
    sub = tq // tk
    streams = [(g, r) for g in range(heads) for r in range(sub)]

    def block(s, kb, mask):
        g, r = streams[s]
        cols = slice(g * SB_DIM, (g + 1) * SB_DIM)
        keys = pl.ds(pl.multiple_of(kb * tk, tk), tk)
        q = q_ref[r * tk:(r + 1) * tk, cols]
        z = lax.dot_general(q, k_ref[keys, cols], (((1,), (1,)), ((), ())),
                            preferred_element_type=F32)
        neg_abs = lax.bitcast_convert_type(
            lax.bitcast_convert_type(z, jnp.uint32) | jnp.uint32(0x80000000), F32)
        cost = jnp.maximum(z, 0.0) + jnp.log2(1.0 + jnp.exp2(neg_abs))
        if mask is not None:
            cost = jnp.where(mask, cost, 0.0)
        after = jnp.dot(cost.astype(BF16), tri, preferred_element_type=F32) + run_ref[s]
        a = jnp.exp2(z - cost - after)
        if mask is not None:
            a = jnp.where(mask, a, 0.0)
        acc_ref[s] += jnp.dot(a.astype(BF16), v_ref[keys, cols], preferred_element_type=F32)
        run_ref[s] += jnp.sum(cost, axis=-1, keepdims=True)

    def more():
        return jnp.min(run_ref[...]) < SB_UNDERFLOW_LOG2

    run_ref[...] = jnp.zeros_like(run_ref)
    acc_ref[...] = jnp.zeros_like(acc_ref)
    strictly_before = (lax.broadcasted_iota(jnp.int32, (tk, tk), 1)
                       < lax.broadcasted_iota(jnp.int32, (tk, tk), 0))
    for s, (g, r) in enumerate(streams):
        block(s, i * sub + r, strictly_before)

    def cond(c):
        return (c[0] < i * sub + sub - 1) & c[1]

    def body(c):
        for s, (g, r) in enumerate(streams):
            kb = i * sub + r - 1 - c[0]
            run_ref[s] = jnp.where(kb >= 0, run_ref[s], SB_FINISHED_LOG2)
            block(s, jnp.maximum(kb, 0), None)
        return c[0] + 1, more()

    lax.while_loop(cond, body, (jnp.int32(0), more()))
    for s, (g, r) in enumerate(streams):
        o_ref[r * tk:(r + 1) * tk, g * SB_DIM:(g + 1) * SB_DIM] = acc_ref[s]


def _sb_attention(proj, bsz, seq, tq=512, tk=256, heads=4):
    t = proj.shape[0]
    tq = min(tq, seq)
    nq = seq // tq
    tri = (lax.broadcasted_iota(jnp.int32, (tk, tk), 0)
           > lax.broadcasted_iota(jnp.int32, (tk, tk), 1)).astype(BF16)
    w = heads * SB_DIM
    cq, ck, cv = COL_SQ // w, COL_SK // w, COL_SV // w
    n_streams = heads * (tq // tk)
    return pl.pallas_call(
        functools.partial(_sb_attn_kernel, tq=tq, tk=tk, heads=heads),
        grid=(bsz, SB_HEADS // heads, nq),
        in_specs=[pl.BlockSpec((tq, w), lambda b, h, i: (b * nq + i, cq + h)),
                  pl.BlockSpec((seq, w), lambda b, h, i: (b, ck + h)),
                  pl.BlockSpec((seq, w), lambda b, h, i: (b, cv + h)),
                  pl.BlockSpec((tk, tk), lambda b, h, i: (0, 0))],
        out_specs=pl.BlockSpec((tq, w), lambda b, h, i: (b * nq + i, h)),
        out_shape=jax.ShapeDtypeStruct((t, SB_WIDTH), F32),
        scratch_shapes=[pltpu.VMEM((n_streams, tk, 1), F32),
                        pltpu.VMEM((n_streams, tk, SB_DIM), F32)],
        compiler_params=_params("arbitrary", "arbitrary", "arbitrary"),
        name="sb_attention",
    )(proj, proj, proj, tri)


def _out_proj_kernel(om_ref, os_ref, gm_ref, gs_ref, wo_ref, x_ref, gate_ref, g2_ref,
                     sh_ref, sc_ref, rw_ref, xo_ref, h2_ref, lg_ref):
    wm = om_ref.shape[1]
    nm = (_rms(om_ref[...]) * gm_ref[...]).astype(BF16)
    ns = (_rms(os_ref[...]) * gs_ref[...]).astype(BF16)
    y = (jnp.dot(nm, wo_ref[:wm, :], preferred_element_type=F32)
         + jnp.dot(ns, wo_ref[wm:, :], preferred_element_type=F32))
    xn = x_ref[...] + gate_ref[...] * y
    xo_ref[...] = xn
    h2 = _rms(xn) * g2_ref[...]
    h2 = h2 * (1.0 + sc_ref[...]) + sh_ref[...]
    h2_ref[...] = _pack_rows(h2)
    lg_ref[...] = lax.dot_general(rw_ref[...], h2.astype(BF16), (((1,), (1,)), ((), ())),
                                  preferred_element_type=F32)


def _out_proj(o_mla, o_sb, gm, gs, w_o, x, mod, g2, rw_t, layer, seq, tm=512):
    t, d = x.shape
    tm = min(tm, seq)
    per_b = seq // tm
    row = lambda i: (i, 0)
    full2 = lambda i: (0, 0)
    modspec = lambda j: pl.BlockSpec((None, 1, d), lambda i: ((i // per_b) * 6 + j, 0, 0))
    return pl.pallas_call(
        _out_proj_kernel,
        grid=(t // tm,),
        in_specs=[pl.BlockSpec((tm, o_mla.shape[1]), row),
                  pl.BlockSpec((tm, o_sb.shape[1]), row),
                  pl.BlockSpec((1, o_mla.shape[1]), full2),
                  pl.BlockSpec((1, o_sb.shape[1]), full2),
                  pl.BlockSpec((None,) + w_o.shape[1:], lambda i: (layer, 0, 0)),
                  pl.BlockSpec((tm, d), row),
                  modspec(2),
                  pl.BlockSpec((1, d), full2),
                  modspec(3),
                  modspec(4),
                  pl.BlockSpec(rw_t.shape, full2)],
        out_specs=[pl.BlockSpec((tm, d), row),
                   pl.BlockSpec((tm, d // 2), row),
                   pl.BlockSpec((N_EXPERTS, tm), lambda i: (0, i))],
        out_shape=[jax.ShapeDtypeStruct((t, d), F32),
                   jax.ShapeDtypeStruct((t, d // 2), jnp.uint32),
                   jax.ShapeDtypeStruct((N_EXPERTS, t), F32)],
        compiler_params=_params("arbitrary"),
        name="out_proj",
    )(o_mla, o_sb, gm, gs, w_o, x, mod, g2, mod, mod, rw_t)


def _route_kernel(lg_ref, rb_ref, tri_ref, e_ref, w_ref, rank_ref, cnt_ref, run_ref):
    i = pl.program_id(0)

    @pl.when(i == 0)
    def _():
        run_ref[...] = jnp.zeros_like(run_ref)

    scores = jax.nn.sigmoid(lg_ref[...])
    sel = scores + rb_ref[...]
    tm = sel.shape[1]
    epg = EXPERTS_PER_GROUP
    srow = [sel[e:e + 1, :] for e in range(N_EXPERTS)]
    prow = [scores[e:e + 1, :] for e in range(N_EXPERTS)]

    gscore = []
    for g in range(N_GROUPS):
        a, b, c, d = srow[g * epg:(g + 1) * epg]
        hi1, lo1 = jnp.maximum(a, b), jnp.minimum(a, b)
        hi2, lo2 = jnp.maximum(c, d), jnp.minimum(c, d)
        gscore.append(jnp.maximum(hi1, hi2)
                      + jnp.maximum(jnp.minimum(hi1, hi2), jnp.maximum(lo1, lo2)))
    best = gscore[0]
    gidx = jnp.zeros((1, tm), jnp.int32)
    for g in range(1, N_GROUPS):
        better = gscore[g] > best
        gidx = jnp.where(better, g, gidx)
        best = jnp.where(better, gscore[g], best)

    def pick(rows, j):
        out = rows[j]
        for g in range(1, N_GROUPS):
            out = jnp.where(gidx == g, rows[g * epg + j], out)
        return out

    v = [pick(srow, j) for j in range(epg)]
    p = [pick(prow, j) for j in range(epg)]
    pos = []
    for j in range(epg):
        r = jnp.zeros((1, tm), jnp.int32)
        for o in range(epg):
            if o == j:
                continue
            ahead = (v[o] > v[j]) | ((v[o] == v[j]) & (o < j))
            r = r + ahead.astype(jnp.int32)
        pos.append(r)
    local, wsel = [], []
    for k in range(TOP_K):
        lk = jnp.zeros((1, tm), jnp.int32)
        wk = jnp.zeros((1, tm), F32)
        for j in range(epg):
            hit = pos[j] == k
            lk = jnp.where(hit, j, lk)
            wk = jnp.where(hit, p[j], wk)
        local.append(lk)
        wsel.append(wk)
    wsum = wsel[0] + wsel[1]
    idx = [gidx * epg + local[k] for k in range(TOP_K)]

    eiota = lax.broadcasted_iota(jnp.int32, (N_EXPERTS, tm), 0)
    hit = [eiota == idx[k] for k in range(TOP_K)]
    onehot = (hit[0] | hit[1]).astype(F32)
    before = jnp.dot(onehot.astype(BF16), tri_ref[...], preferred_element_type=F32) + run_ref[...]
    for k in range(TOP_K):
        e_ref[pl.ds(k, 1), :] = idx[k]
        w_ref[pl.ds(k, 1), :] = wsel[k] / wsum
        rank_ref[pl.ds(k, 1), :] = jnp.sum(jnp.where(hit[k], before, 0.0), axis=0,
                                           keepdims=True).astype(jnp.int32)
    run_ref[...] += jnp.sum(onehot, axis=1, keepdims=True)
    cnt_ref[...] = jnp.broadcast_to(run_ref[...], cnt_ref.shape)


def _route(logits_t, router_b, tm=512):
    t = logits_t.shape[1]
    tm = min(tm, t)
    tri = (lax.broadcasted_iota(jnp.int32, (tm, tm), 0)
           < lax.broadcasted_iota(jnp.int32, (tm, tm), 1)).astype(BF16)
    blk = lambda rows: pl.BlockSpec((rows, tm), lambda i: (0, i))
    return pl.pallas_call(
        _route_kernel,
        grid=(t // tm,),
        in_specs=[blk(N_EXPERTS),
                  pl.BlockSpec((N_EXPERTS, 1), lambda i: (0, 0)),
                  pl.BlockSpec((tm, tm), lambda i: (0, 0))],
        out_specs=[blk(TOP_K), blk(TOP_K), blk(TOP_K),
                   pl.BlockSpec((N_EXPERTS, LANE), lambda i: (0, 0))],
        out_shape=[jax.ShapeDtypeStruct((TOP_K, t), jnp.int32),
                   jax.ShapeDtypeStruct((TOP_K, t), F32),
                   jax.ShapeDtypeStruct((TOP_K, t), jnp.int32),
                   jax.ShapeDtypeStruct((N_EXPERTS, LANE), F32)],
        scratch_shapes=[pltpu.VMEM((N_EXPERTS, 1), F32)],
        compiler_params=_params("arbitrary"),
        name="route",
    )(logits_t, router_b[:, None], tri)


def _dispatch_kernel(dest_ref, pend_ref, h_ref, o_ref, zero_ref, ring, in_sems, row_sems, zsem,
                     *, tm, t):
    base = pl.program_id(0) * tm

    @pl.when(pl.program_id(0) == 0)
    def _():
        zero_ref[...] = jnp.zeros_like(zero_ref)

        def fill(e):
            end = pend_ref[e]
            start = pend_ref[e - 1] if e > 0 else 0
            dst = o_ref.at[pl.ds(pl.multiple_of(jnp.maximum(end - MOE_ROWS, 0), MOE_ROWS),
                                 MOE_ROWS), :]
            return end > start, pltpu.make_async_copy(zero_ref, dst, zsem)

        for e in range(N_EXPERTS):
            nonempty, cp = fill(e)
            pl.when(nonempty)(cp.start)
        for e in range(N_EXPERTS):
            nonempty, cp = fill(e)
            pl.when(nonempty)(cp.wait)

        def tail(r):
            dst = o_ref.at[pl.ds(pl.multiple_of(r * MOE_ROWS, MOE_ROWS), MOE_ROWS), :]
            return pltpu.make_async_copy(zero_ref, dst, zsem)

        first_free = pend_ref[N_EXPERTS - 1] // MOE_ROWS
        n_blocks = o_ref.shape[0] // MOE_ROWS
        lax.fori_loop(first_free, n_blocks, lambda r, c: (tail(r).start(), c)[1], 0)
        lax.fori_loop(first_free, n_blocks, lambda r, c: (tail(r).wait(), c)[1], 0)

    i = pl.program_id(0)
    n = pl.num_programs(0)

    def stage(tile):
        return pltpu.make_async_copy(h_ref.at[pl.ds(tile * tm, tm), :], ring.at[tile % 3],
                                     in_sems.at[tile % 3])

    @pl.when(i == 0)
    def _():
        stage(0).start()

    @pl.when(i + 1 < n)
    def _():
        stage(i + 1).start()

    stage(i).wait()
    src = ring.at[i % 3]

    def issue(r, carry):
        for k in range(TOP_K):
            d = dest_ref[k * t + base + r]
            pltpu.make_async_copy(src.at[pl.ds(r, 1), :], o_ref.at[pl.ds(d, 1), :],
                                  row_sems.at[i % 2]).start()
        return carry

    lax.fori_loop(0, tm, issue, 0, unroll=DMA_ISSUE_UNROLL)

    def wait_rows(step):
        for k in range(TOP_K):
            pltpu.make_async_copy(ring.at[0], o_ref.at[pl.ds(0, tm), :],
                                  row_sems.at[step % 2]).wait()

    pl.when(i > 0)(lambda: wait_rows(i - 1))
    pl.when(i == n - 1)(lambda: wait_rows(i))


def _dispatch(h2, dest_flat, pad_end, rows, tm=512):
    t, d = h2.shape
    tm = min(tm, t)
    return pl.pallas_call(
        functools.partial(_dispatch_kernel, tm=tm, t=t),
        grid_spec=pltpu.PrefetchScalarGridSpec(
            num_scalar_prefetch=2,
            grid=(t // tm,),
            in_specs=[pl.BlockSpec(memory_space=pl.ANY)],
            out_specs=pl.BlockSpec(memory_space=pl.ANY),
            scratch_shapes=[pltpu.VMEM((MOE_ROWS, d), h2.dtype),
                            pltpu.VMEM((3, tm, d), h2.dtype),
                            pltpu.SemaphoreType.DMA((3,)), pltpu.SemaphoreType.DMA((2,)),
                            pltpu.SemaphoreType.DMA(())]),
        out_shape=jax.ShapeDtypeStruct((rows, d), h2.dtype),
        compiler_params=_params("arbitrary"),
        name="moe_dispatch",
    )(dest_flat, pad_end, h2)


def _expert_kernel(be_ref, nu_ref, next_ref, slot_ref, x_ref, wg_hbm, wu_hbm, wd_hbm, y_ref,
                   wgf, wuf, wdf, wgb, wub, wdb, sems, *, layer):
    r = pl.program_id(0)
    live = r < nu_ref[0]
    e = be_ref[r]
    new_expert = (r == 0) | (be_ref[jnp.maximum(r - 1, 0)] != e)
    slot = slot_ref[e]

    def fetch(expert, s):
        pairs = ((wg_hbm, wgf), (wu_hbm, wuf), (wd_hbm, wdf))
        return [pltpu.make_async_copy(w.at[layer, expert], buf.at[s], sems.at[s, j])
                for j, (w, buf) in enumerate(pairs)]

    @pl.when(r == 0)
    def _():
        for cp in fetch(e, slot):
            cp.start()

    @pl.when(live & new_expert)
    def _():
        nxt = next_ref[e]

        @pl.when(nxt != e)
        def _():
            for cp in fetch(nxt, 1 - slot):
                cp.start()

        for cp in fetch(e, slot):
            cp.wait()
        wgb[...] = wgf[slot].astype(BF16)
        wub[...] = wuf[slot].astype(BF16)
        wdb[...] = wdf[slot].astype(BF16)

    @pl.when(live)
    def _():
        lo, hi = _unpack_rows(x_ref[...])
        xb = jnp.concatenate([lo.astype(BF16), hi.astype(BF16)], axis=1)
        g = jnp.dot(xb, wgb[...], preferred_element_type=F32)
        u = jnp.dot(xb, wub[...], preferred_element_type=F32)
        a = (g * jax.nn.sigmoid(g)) * u
        y_ref[...] = _pack_rows(jnp.dot(a.astype(BF16), wdb[...], preferred_element_type=F32))

    @pl.when(jnp.logical_not(live))
    def _():
        y_ref[...] = jnp.zeros_like(y_ref)


def _experts(x_sorted, block_e, n_used, next_expert, expert_slot, w_gate, w_up, w_down, layer):
    rows, d = x_sorted.shape
    de = w_gate.shape[-1]
    dm = w_gate.shape[-2]
    bm = MOE_ROWS
    blk = lambda r, be, nu, nx, sl: (jnp.minimum(r, nu[0] - 1), 0)
    hbm = pl.BlockSpec(memory_space=pl.ANY)
    return pl.pallas_call(
        functools.partial(_expert_kernel, layer=layer),
        grid_spec=pltpu.PrefetchScalarGridSpec(
            num_scalar_prefetch=4,
            grid=(rows // bm,),
            in_specs=[pl.BlockSpec((bm, d), blk), hbm, hbm, hbm],
            out_specs=pl.BlockSpec((bm, d), lambda r, be, nu, nx, sl: (r, 0)),
            scratch_shapes=[pltpu.VMEM((2, dm, de), F32), pltpu.VMEM((2, dm, de), F32),
                            pltpu.VMEM((2, de, dm), F32),
                            pltpu.VMEM((dm, de), BF16), pltpu.VMEM((dm, de), BF16),
                            pltpu.VMEM((de, dm), BF16),
                            pltpu.SemaphoreType.DMA((2, 3))]),
        out_shape=jax.ShapeDtypeStruct((rows, d), jnp.uint32),
        compiler_params=_params("arbitrary"),
        name="moe_experts",
    )(block_e, n_used, next_expert, expert_slot, x_sorted, w_gate, w_up, w_down)


def _combine_kernel(dest_ref, y_ref, x_ref, gate_ref, w_ref, o_ref, buf, sems, *, tm, t):
    i = pl.program_id(0)
    slot = i % 2

    def gather(tile, dst_slot):
        base = tile * tm

        def issue(r, carry):
            for k in range(TOP_K):
                d = dest_ref[k * t + base + r]
                pltpu.make_async_copy(y_ref.at[pl.ds(d, 1), :],
                                      buf.at[dst_slot, k, pl.ds(r, 1), :],
                                      sems.at[dst_slot]).start()
            return carry

        lax.fori_loop(0, tm, issue, 0, unroll=DMA_ISSUE_UNROLL)

    @pl.when(i == 0)
    def _():
        gather(0, 0)

    @pl.when(i + 1 < pl.num_programs(0))
    def _():
        gather(i + 1, 1 - slot)

    for k in range(TOP_K):
        pltpu.make_async_copy(y_ref.at[pl.ds(0, tm), :], buf.at[slot, k], sems.at[slot]).wait()
    w = w_ref[...]
    half = buf.shape[3]
    lo0, hi0 = _unpack_rows(buf[slot, 0])
    lo1, hi1 = _unpack_rows(buf[slot, 1])
    o_ref[:, :half] = x_ref[:, :half] + gate_ref[:, :half] * (w[:, 0:1] * lo0 + w[:, 1:2] * lo1)
    o_ref[:, half:] = x_ref[:, half:] + gate_ref[:, half:] * (w[:, 0:1] * hi0 + w[:, 1:2] * hi1)


def _combine(y_sorted, dest_flat, x, mod, w_tok, seq, tm=512):
    t, d = x.shape
    tm = min(tm, seq)
    per_b = seq // tm
    return pl.pallas_call(
        functools.partial(_combine_kernel, tm=tm, t=t),
        grid_spec=pltpu.PrefetchScalarGridSpec(
            num_scalar_prefetch=1,
            grid=(t // tm,),
            in_specs=[pl.BlockSpec(memory_space=pl.ANY),
                      pl.BlockSpec((tm, d), lambda i, dest: (i, 0)),
                      pl.BlockSpec((None, 1, d), lambda i, dest: ((i // per_b) * 6 + 5, 0, 0)),
                      pl.BlockSpec((tm, TOP_K), lambda i, dest: (i, 0))],
            out_specs=pl.BlockSpec((tm, d), lambda i, dest: (i, 0)),
            scratch_shapes=[pltpu.VMEM((2, TOP_K, tm, y_sorted.shape[1]), y_sorted.dtype),
                            pltpu.SemaphoreType.DMA((2,))]),
        out_shape=jax.ShapeDtypeStruct((t, d), F32),
        compiler_params=_params("arbitrary"),
        name="moe_combine",
    )(dest_flat, y_sorted, x, mod, w_tok)


def _rope_tables(seq):
    half = MLA_ROPE // 2
    inv = 1.0 / (ROPE_BASE ** (jnp.arange(0, MLA_ROPE, 2, dtype=F32) / MLA_ROPE))
    ang = jnp.arange(seq, dtype=F32)[:, None] * inv[None, :]
    cos, sin = jnp.cos(ang), jnp.sin(ang)
    z = jnp.zeros((seq, half), F32)
    zz = jnp.zeros((seq, LANE - MLA_ROPE), F32)
    cos_t = jnp.concatenate([cos, cos, zz], axis=1)
    s1_t = jnp.concatenate([z, sin, zz], axis=1)
    s2_t = jnp.concatenate([-sin, z, zz], axis=1)
    return cos_t, s1_t, s2_t


def _pad_last(a, width):
    return jnp.pad(a, [(0, 0)] * (a.ndim - 1) + [(0, width - a.shape[-1])])


def kernel(x, c, norm1_g, norm2_g, w_ada, b_ada, w_in, q_lora_g, kv_lora_g, w_uq, w_ukv,
           q_head_g, k_head_g, mla_out_g, sb_out_g, w_o, router_w, router_b,
           w_gate, w_up, w_down):
    bsz, seq, d = x.shape
    t = bsz * seq
    depth = w_ada.shape[0]

    sb_col0 = COL_KR + MLA_ROPE
    w_in_mla = _pad_last(w_in[..., :sb_col0], MLA_IN_COLS).astype(BF16)
    w_in_sb = w_in[..., sb_col0:].astype(BF16)
    wq = w_uq.reshape(depth, Q_LORA, MLA_HEADS, MLA_QK)
    half = MLA_ROPE // 2
    swap = lambda a: jnp.concatenate([a[..., half:], a[..., :half]], axis=-1)
    wq_rope = wq[..., MLA_NOPE:]
    wuq_p = jnp.concatenate(
        [wq[..., :MLA_NOPE].reshape(depth, Q_LORA, -1),
         _pad_last(wq_rope, LANE).reshape(depth, Q_LORA, -1),
         _pad_last(swap(wq_rope), LANE).reshape(depth, Q_LORA, -1)], axis=-1).astype(BF16)
    wkv = w_ukv.reshape(depth, KV_LORA, MLA_HEADS, MLA_NOPE + MLA_V)
    wukv_p = jnp.concatenate(
        [wkv[..., :MLA_NOPE].reshape(depth, KV_LORA, -1),
         wkv[..., MLA_NOPE:].reshape(depth, KV_LORA, -1)], axis=-1).astype(BF16)
    w_o_b = w_o.astype(BF16)
    q_scale = MLA_QK ** -0.5 * LOG2_E
    gqn = (q_head_g[:, None, :MLA_NOPE] * q_scale)
    gq_rope = q_head_g[:, None, MLA_NOPE:] * q_scale
    gqr = jnp.concatenate([_pad_last(gq_rope, LANE), _pad_last(swap(gq_rope), LANE)], axis=-1)
    gkn = k_head_g[:, None, :MLA_NOPE]
    gkr = _pad_last(k_head_g[:, None, MLA_NOPE:], LANE)
    rope_tabs = _rope_tables(seq)
    rw_t = router_w.T.astype(BF16)

    mod_all = _ada_modulation(c[:, :, None], w_ada, b_ada[:, None, :], layers=depth)
    n_blocks = -(-(t * TOP_K) // MOE_ROWS) + N_EXPERTS
    rows = n_blocks * MOE_ROWS

    xt = x.reshape(t, d)
    for l in range(depth):
        mod = mod_all[l].reshape(bsz * 6, 1, d)
        q2d, k2d, v2d, proj_sb = _in_proj(
            xt, norm1_g[l][None, :], mod, w_in_mla, w_in_sb, rope_tabs, q_lora_g[:, None, :],
            kv_lora_g[:, None, :], wuq_p, wukv_p, gqn, gqr, gkn, gkr, l, seq)
        o_mla = _mla_attention(q2d, k2d, v2d, bsz, seq)
        o_sb = _sb_attention(proj_sb, bsz, seq)
        xt, h2, logits_t = _out_proj(o_mla, o_sb, mla_out_g[l][None, :], sb_out_g[l][None, :],
                                     w_o_b, xt, mod, norm2_g[l][None, :], rw_t, l, seq)
        e_idx, w_tok, rank, cnt = _route(logits_t, router_b)
        counts = cnt[:, 0].astype(jnp.int32)
        padded = (counts + MOE_ROWS - 1) // MOE_ROWS * MOE_ROWS
        pad_end = jnp.cumsum(padded)
        pad_start = pad_end - padded
        eids = jnp.arange(N_EXPERTS, dtype=jnp.int32)
        seg_start = jnp.sum(jnp.where(e_idx[None] == eids[:, None, None],
                                      pad_start[:, None, None], 0), axis=0)
        dest = (seg_start + rank).reshape(TOP_K * t).astype(jnp.int32)
        n_used = (pad_end[-1:] // MOE_ROWS).astype(jnp.int32)
        first_row = jnp.minimum(jnp.arange(n_blocks, dtype=jnp.int32), n_used - 1) * MOE_ROWS
        block_e = jnp.sum(pad_end[None, :] <= first_row[:, None], axis=1).astype(jnp.int32)
        nonempty = counts > 0
        later = (eids[None, :] > eids[:, None]) & nonempty[None, :]
        next_expert = jnp.min(jnp.where(later, eids[None, :], N_EXPERTS), axis=1)
        next_expert = jnp.where(next_expert == N_EXPERTS, eids, next_expert).astype(jnp.int32)
        expert_slot = ((jnp.cumsum(nonempty) - nonempty) % 2).astype(jnp.int32)
        x_sorted = _dispatch(h2, dest, pad_end.astype(jnp.int32), rows)
        y_sorted = _experts(x_sorted, block_e, n_used, next_expert, expert_slot,
                            w_gate, w_up, w_down, l)
        xt = _combine(y_sorted, dest, xt, mod, w_tok.T, seq)
    return xt.reshape(bsz, seq, d)
```

```python
import functools

import jax
import jax.numpy as jnp
from jax import lax
from jax.experimental import pallas as pl
from jax.experimental.pallas import tpu as pltpu

F32 = jnp.float32
BF16 = jnp.bfloat16

EPS = 1e-6
ROPE_BASE = 10000.0
CHUNK = 64

MLA_HEADS = 8
MLA_NOPE = 128
MLA_ROPE = 64
MLA_QK = MLA_NOPE + MLA_ROPE
MLA_V = 128
Q_LORA = 512
KV_LORA = 256
SB_HEADS = 8
SB_DIM = 128
SB_WIDTH = SB_HEADS * SB_DIM
LANE = 128
QK_PAD = 2 * LANE
COL_CQ = 0
COL_CKV = Q_LORA
COL_KR = Q_LORA + KV_LORA
MLA_IN_COLS = COL_KR + LANE
COL_SQ = 0
COL_SK = SB_WIDTH
COL_SV = 2 * SB_WIDTH
SB_IN_COLS = 3 * SB_WIDTH

N_EXPERTS = 16
N_GROUPS = 4
EXPERTS_PER_GROUP = N_EXPERTS // N_GROUPS
TOP_K = 2
D_EXPERT = 512
MOE_ROWS = 512
DMA_ISSUE_UNROLL = 16
LOG2_E = 1.4426950408889634
SB_UNDERFLOW_LOG2 = 150.0
SB_FINISHED_LOG2 = 1e30

VMEM_LIMIT = 56 * 1024 * 1024


def _params(*dims):
    return pltpu.CompilerParams(dimension_semantics=dims, vmem_limit_bytes=VMEM_LIMIT)


def _rms(x):
    return x * lax.rsqrt(jnp.mean(x * x, axis=-1, keepdims=True) + EPS)


def _pack_rows(x):
    n = x.shape[1] // 2
    lo = lax.bitcast_convert_type(x[:, :n].astype(BF16).astype(F32), jnp.uint32)
    hi = lax.bitcast_convert_type(x[:, n:].astype(BF16).astype(F32), jnp.uint32)
    return (lo >> 16) | (hi & jnp.uint32(0xFFFF0000))


def _unpack_rows(w):
    lo = lax.bitcast_convert_type(w << 16, F32)
    hi = lax.bitcast_convert_type(w & jnp.uint32(0xFFFF0000), F32)
    return lo, hi


def _ada_kernel(c_ref, w_ref, b_ref, o_ref):
    w = w_ref[...]
    for b in range(c_ref.shape[0]):
        cb = c_ref[b]
        cb = cb * jax.nn.sigmoid(cb)
        o_ref[pl.ds(b, 1), :] = jnp.sum(cb * w, axis=0, keepdims=True) + b_ref[...]


def _ada_modulation(c_col, w_ada, b_ada3, layers, tn=1024):
    _, d, n = w_ada.shape
    bsz = c_col.shape[0]
    return pl.pallas_call(
        _ada_kernel,
        grid=(layers, n // tn),
        in_specs=[pl.BlockSpec(c_col.shape, lambda l, j: (0, 0, 0)),
                  pl.BlockSpec((None, d, tn), lambda l, j: (l, 0, j)),
                  pl.BlockSpec((None, 1, tn), lambda l, j: (l, 0, j))],
        out_specs=pl.BlockSpec((None, bsz, tn), lambda l, j: (l, 0, j)),
        out_shape=jax.ShapeDtypeStruct((layers, bsz, n), F32),
        compiler_params=_params("arbitrary", "arbitrary"),
        name="ada_modulation",
    )(c_col, w_ada, b_ada3)


def _in_proj_kernel(x_ref, g_ref, sh_ref, sc_ref, wa_ref, wb_ref, cos_ref, s1_ref, s2_ref,
                    gql_ref, gkvl_ref, wuq_ref, wukv_ref, gqn_ref, gqr_ref, gkn_ref, gkr_ref,
                    q_ref, k_ref, v_ref, ob_ref):
    hn = _rms(x_ref[...]) * g_ref[...]
    hn = (hn * (1.0 + sc_ref[...]) + sh_ref[...]).astype(BF16)
    sb = jnp.dot(hn, wb_ref[...], preferred_element_type=F32)
    ob_ref[:, :SB_WIDTH] = (sb[:, :SB_WIDTH] * (SB_DIM ** -0.5 * LOG2_E)).astype(ob_ref.dtype)
    ob_ref[:, SB_WIDTH:] = sb[:, SB_WIDTH:].astype(ob_ref.dtype)

    pa = jnp.dot(hn, wa_ref[...], preferred_element_type=F32)
    nope_w = MLA_HEADS * MLA_NOPE
    cqn = _rms(pa[:, COL_CQ:COL_CKV]) * gql_ref[...]
    q = jnp.dot(cqn.astype(BF16), wuq_ref[...], preferred_element_type=F32)
    ckvn = _rms(pa[:, COL_CKV:COL_KR]) * gkvl_ref[...]
    kv = jnp.dot(ckvn.astype(BF16), wukv_ref[...], preferred_element_type=F32)
    cosv, s1, s2 = cos_ref[...], s1_ref[...], s2_ref[...]

    def rope(x):
        return (x * cosv + pltpu.roll(x, MLA_ROPE // 2, 1) * s1
                + pltpu.roll(x, LANE - MLA_ROPE // 2, 1) * s2)

    kr = pa[:, COL_KR:]
    kr_ss = jnp.sum(kr * kr, axis=-1, keepdims=True)
    kr_roped = rope(kr * gkr_ref[...])
    gqn, gkn = gqn_ref[...], gkn_ref[...]
    gqr, gqs = gqr_ref[:, :LANE], gqr_ref[:, LANE:]
    s_comb = s1 + s2
    for h in range(MLA_HEADS):
        qn = q[:, h * LANE:(h + 1) * LANE]
        qr = q[:, nope_w + h * LANE:nope_w + (h + 1) * LANE]
        ss = jnp.sum(qn * qn + qr * qr, axis=-1, keepdims=True)
        r = lax.rsqrt(ss * (1.0 / MLA_QK) + EPS)
        q_ref[:, h * QK_PAD:h * QK_PAD + LANE] = (qn * r * gqn).astype(BF16)
        qs = q[:, 2 * nope_w + h * LANE:2 * nope_w + (h + 1) * LANE]
        q_rope = (qr * gqr) * cosv + (qs * gqs) * s_comb
        q_ref[:, h * QK_PAD + LANE:(h + 1) * QK_PAD] = (q_rope * r).astype(BF16)
        kn = kv[:, h * LANE:(h + 1) * LANE]
        rk = lax.rsqrt((jnp.sum(kn * kn, axis=-1, keepdims=True) + kr_ss) * (1.0 / MLA_QK) + EPS)
        k_ref[:, h * QK_PAD:h * QK_PAD + LANE] = (kn * rk * gkn).astype(BF16)
        k_ref[:, h * QK_PAD + LANE:(h + 1) * QK_PAD] = (kr_roped * rk).astype(BF16)
        v_ref[:, 2 * h * MLA_V:(2 * h + 1) * MLA_V] = (
            kv[:, nope_w + h * MLA_V:nope_w + (h + 1) * MLA_V].astype(BF16))
        v_ref[:, (2 * h + 1) * MLA_V:(2 * h + 2) * MLA_V] = jnp.ones((kn.shape[0], MLA_V), BF16)


def _in_proj(x, g, mod, w_mla, w_sb, rope_tabs, gql, gkvl, wuq, wukv, gqn, gqr, gkn, gkr,
             layer, seq, tm=256):
    t, d = x.shape
    na, nb = w_mla.shape[-1], w_sb.shape[-1]
    per_b = seq // tm
    cos_t, s1_t, s2_t = rope_tabs
    row = lambda i: (i, 0)
    lay = lambda i: (layer, 0, 0)
    tab = lambda i: (i % per_b, 0)
    return pl.pallas_call(
        _in_proj_kernel,
        grid=(t // tm,),
        in_specs=[pl.BlockSpec((tm, d), row),
                  pl.BlockSpec((1, d), lambda i: (0, 0)),
                  pl.BlockSpec((None, 1, d), lambda i: ((i // per_b) * 6 + 0, 0, 0)),
                  pl.BlockSpec((None, 1, d), lambda i: ((i // per_b) * 6 + 1, 0, 0)),
                  pl.BlockSpec((None, d, na), lay),
                  pl.BlockSpec((None, d, nb), lay),
                  pl.BlockSpec((tm, LANE), tab),
                  pl.BlockSpec((tm, LANE), tab),
                  pl.BlockSpec((tm, LANE), tab),
                  pl.BlockSpec((None, 1, Q_LORA), lay),
                  pl.BlockSpec((None, 1, KV_LORA), lay),
                  pl.BlockSpec((None,) + wuq.shape[1:], lay),
                  pl.BlockSpec((None,) + wukv.shape[1:], lay),
                  pl.BlockSpec((None, 1, LANE), lay),
                  pl.BlockSpec((None, 1, 2 * LANE), lay),
                  pl.BlockSpec((None, 1, LANE), lay),
                  pl.BlockSpec((None, 1, LANE), lay)],
        out_specs=[pl.BlockSpec((tm, MLA_HEADS * QK_PAD), row),
                   pl.BlockSpec((tm, MLA_HEADS * QK_PAD), row),
                   pl.BlockSpec((tm, MLA_HEADS * 2 * MLA_V), row),
                   pl.BlockSpec((tm, nb), row)],
        out_shape=[jax.ShapeDtypeStruct((t, MLA_HEADS * QK_PAD), BF16),
                   jax.ShapeDtypeStruct((t, MLA_HEADS * QK_PAD), BF16),
                   jax.ShapeDtypeStruct((t, MLA_HEADS * 2 * MLA_V), BF16),
                   jax.ShapeDtypeStruct((t, nb), BF16)],
        compiler_params=_params("arbitrary"),
        name="in_proj",
    )(x, g, mod, mod, w_mla, w_sb, cos_t, s1_t, s2_t, gql, gkvl, wuq, wukv, gqn, gqr, gkn, gkr)


def _mla_attn_kernel(q_ref, k_ref, v_ref, o_ref, m_ref, l_ref, acc_ref, sa_ref, sb_ref, *,
                     tq, heads):
    i = pl.program_id(2)
    vw = 2 * MLA_V

    def scores(g, kb):
        q = q_ref[:, g * QK_PAD:(g + 1) * QK_PAD]
        k = k_ref[pl.ds(pl.multiple_of(kb * tq, tq), tq), g * QK_PAD:(g + 1) * QK_PAD]
        return lax.dot_general(q, k, (((1,), (1,)), ((), ())), preferred_element_type=F32)

    def probs(s, m):
        return jnp.concatenate(
            [jnp.exp2(s[:, c * LANE:(c + 1) * LANE] - m) for c in range(tq // LANE)],
            axis=1).astype(BF16)

    def pv(g, p, kb):
        v = v_ref[pl.ds(pl.multiple_of(kb * tq, tq), tq), g * vw:(g + 1) * vw]
        return jnp.dot(p, v, preferred_element_type=F32)

    def update(g, s, kb):
        m_old = m_ref[g]
        m_new = jnp.maximum(m_old, jnp.max(s, axis=-1, keepdims=True))
        alpha = jnp.exp2(m_old - m_new)
        r = pv(g, probs(s, m_new), kb)
        acc_ref[g] = alpha * acc_ref[g] + r[:, :MLA_V]
        l_ref[g] = alpha * l_ref[g] + r[:, MLA_V:]
        m_ref[g] = m_new

    for g in range(heads):
        m_ref[g] = jnp.full((tq, LANE), -jnp.inf, F32)
        l_ref[g] = jnp.zeros((tq, MLA_V), F32)
        acc_ref[g] = jnp.zeros((tq, MLA_V), F32)
        sa_ref[g] = scores(g, 0)

    def pair(t, carry):
        j = 2 * t
        for g in range(heads):
            sb_ref[g] = scores(g, j + 1)
            update(g, sa_ref[g], j)
        for g in range(heads):
            sa_ref[g] = scores(g, j + 2)
            update(g, sb_ref[g], j + 1)
        return carry

    lax.fori_loop(0, i // 2, pair, 0)

    q_chunk = lax.broadcasted_iota(jnp.int32, (tq, 1), 0) // CHUNK
    k_chunk = lax.broadcasted_iota(jnp.int32, (1, tq), 1) // CHUNK
    visible = k_chunk <= q_chunk

    @pl.when(i % 2 == 1)
    def _():
        for g in range(heads):
            sb_ref[g] = scores(g, i)
            update(g, sa_ref[g], i - 1)
        for g in range(heads):
            update(g, jnp.where(visible, sb_ref[g], -jnp.inf), i)

    @pl.when(i % 2 == 0)
    def _():
        for g in range(heads):
            update(g, jnp.where(visible, sa_ref[g], -jnp.inf), i)

    for g in range(heads):
        o_ref[:, g * MLA_V:(g + 1) * MLA_V] = acc_ref[g] / l_ref[g]


def _mla_attention(q2d, k2d, v2d, bsz, seq, tq=512, heads=4):
    t = q2d.shape[0]
    tq = min(tq, seq)
    nq = seq // tq
    vw = 2 * MLA_V
    return pl.pallas_call(
        functools.partial(_mla_attn_kernel, tq=tq, heads=heads),
        grid=(bsz, MLA_HEADS // heads, nq),
        in_specs=[pl.BlockSpec((tq, heads * QK_PAD), lambda b, h, i: (b * nq + i, h)),
                  pl.BlockSpec((seq, heads * QK_PAD), lambda b, h, i: (b, h)),
                  pl.BlockSpec((seq, heads * vw), lambda b, h, i: (b, h))],
        out_specs=pl.BlockSpec((tq, heads * MLA_V), lambda b, h, i: (b * nq + i, h)),
        out_shape=jax.ShapeDtypeStruct((t, MLA_HEADS * MLA_V), F32),
        scratch_shapes=[pltpu.VMEM((heads, tq, LANE), F32), pltpu.VMEM((heads, tq, MLA_V), F32),
                        pltpu.VMEM((heads, tq, MLA_V), F32),
                        pltpu.VMEM((heads, tq, tq), F32), pltpu.VMEM((heads, tq, tq), F32)],
        compiler_params=_params("arbitrary", "arbitrary", "arbitrary"),
        name="mla_attention",
    )(q2d, k2d, v2d)


def _sb_attn_kernel(q_ref, k_ref, v_ref, tri_ref, o_ref, run_ref, acc_ref, *, tq, tk, heads):
    i = pl.program_id(2)
    tri = tri_ref[...]
    sub = tq // tk
    streams = [(g, r) for g in range(heads) for r in range(sub)]

    def block(s, kb, mask):
        g, r = streams[s]
        cols = slice(g * SB_DIM, (g + 1) * SB_DIM)
        keys = pl.ds(pl.multiple_of(kb * tk, tk), tk)
        q = q_ref[r * tk:(r + 1) * tk, cols]
        z = lax.dot_general(q, k_ref[keys, cols], (((1,), (1,)), ((), ())),
                            preferred_element_type=F32)
        neg_abs = lax.bitcast_convert_type(
            lax.bitcast_convert_type(z, jnp.uint32) | jnp.uint32(0x80000000), F32)
        cost = jnp.maximum(z, 0.0) + jnp.log2(1.0 + jnp.exp2(neg_abs))
        if mask is not None:
            cost = jnp.where(mask, cost, 0.0)
        after = jnp.dot(cost.astype(BF16), tri, preferred_element_type=F32) + run_ref[s]
        a = jnp.exp2(z - cost - after)
        if mask is not None:
            a = jnp.where(mask, a, 0.0)
        acc_ref[s] += jnp.dot(a.astype(BF16), v_ref[keys, cols], preferred_element_type=F32)
        run_ref[s] += jnp.sum(cost, axis=-1, keepdims=True)

    def more():
        return jnp.min(run_ref[...]) < SB_UNDERFLOW_LOG2

    run_ref[...] = jnp.zeros_like(run_ref)
    acc_ref[...] = jnp.zeros_like(acc_ref)
    strictly_before = (lax.broadcasted_iota(jnp.int32, (tk, tk), 1)
                       < lax.broadcasted_iota(jnp.int32, (tk, tk), 0))
    for s, (g, r) in enumerate(streams):
        block(s, i * sub + r, strictly_before)

    def cond(c):
        return (c[0] < i * sub + sub - 1) & c[1]

    def body(c):
        for s, (g, r) in enumerate(streams):
            kb = i * sub + r - 1 - c[0]
            run_ref[s] = jnp.where(kb >= 0, run_ref[s], SB_FINISHED_LOG2)
            block(s, jnp.maximum(kb, 0), None)
        return c[0] + 1, more()

    lax.while_loop(cond, body, (jnp.int32(0), more()))
    for s, (g, r) in enumerate(streams):
        o_ref[r * tk:(r + 1) * tk, g * SB_DIM:(g + 1) * SB_DIM] = acc_ref[s]


def _sb_attention(proj, bsz, seq, tq=512, tk=256, heads=4):
    t = proj.shape[0]
    tq = min(tq, seq)
    nq = seq // tq
    tri = (lax.broadcasted_iota(jnp.int32, (tk, tk), 0)
           > lax.broadcasted_iota(jnp.int32, (tk, tk), 1)).astype(BF16)
    w = heads * SB_DIM
    cq, ck, cv = COL_SQ // w, COL_SK // w, COL_SV // w
    n_streams = heads * (tq // tk)
    return pl.pallas_call(
        functools.partial(_sb_attn_kernel, tq=tq, tk=tk, heads=heads),
        grid=(bsz, SB_HEADS // heads, nq),
        in_specs=[pl.BlockSpec((tq, w), lambda b, h, i: (b * nq + i, cq + h)),
                  pl.BlockSpec((seq, w), lambda b, h, i: (b, ck + h)),
                  pl.BlockSpec((seq, w), lambda b, h, i: (b, cv + h)),
                  pl.BlockSpec((tk, tk), lambda b, h, i: (0, 0))],
        out_specs=pl.BlockSpec((tq, w), lambda b, h, i: (b * nq + i, h)),
        out_shape=jax.ShapeDtypeStruct((t, SB_WIDTH), F32),
        scratch_shapes=[pltpu.VMEM((n_streams, tk, 1), F32),
                        pltpu.VMEM((n_streams, tk, SB_DIM), F32)],
        compiler_params=_params("arbitrary", "arbitrary", "arbitrary"),
        name="sb_attention",
    )(proj, proj, proj, tri)


def _out_proj_kernel(om_ref, os_ref, gm_ref, gs_ref, wo_ref, x_ref, gate_ref, g2_ref,
                     sh_ref, sc_ref, rw_ref, xo_ref, h2_ref, lg_ref):
    wm = om_ref.shape[1]
    nm = (_rms(om_ref[...]) * gm_ref[...]).astype(BF16)
    ns = (_rms(os_ref[...]) * gs_ref[...]).astype(BF16)
    y = (jnp.dot(nm, wo_ref[:wm, :], preferred_element_type=F32)
         + jnp.dot(ns, wo_ref[wm:, :], preferred_element_type=F32))
    xn = x_ref[...] + gate_ref[...] * y
    xo_ref[...] = xn
    h2 = _rms(xn) * g2_ref[...]
    h2 = h2 * (1.0 + sc_ref[...]) + sh_ref[...]
    h2_ref[...] = _pack_rows(h2)
    lg_ref[...] = lax.dot_general(rw_ref[...], h2.astype(BF16), (((1,), (1,)), ((), ())),
                                  preferred_element_type=F32)


def _out_proj(o_mla, o_sb, gm, gs, w_o, x, mod, g2, rw_t, layer, seq, tm=512):
    t, d = x.shape
    tm = min(tm, seq)
    per_b = seq // tm
    row = lambda i: (i, 0)
    full2 = lambda i: (0, 0)
    modspec = lambda j: pl.BlockSpec((None, 1, d), lambda i: ((i // per_b) * 6 + j, 0, 0))
    return pl.pallas_call(
        _out_proj_kernel,
        grid=(t // tm,),
        in_specs=[pl.BlockSpec((tm, o_mla.shape[1]), row),
                  pl.BlockSpec((tm, o_sb.shape[1]), row),
                  pl.BlockSpec((1, o_mla.shape[1]), full2),
                  pl.BlockSpec((1, o_sb.shape[1]), full2),
                  pl.BlockSpec((None,) + w_o.shape[1:], lambda i: (layer, 0, 0)),
                  pl.BlockSpec((tm, d), row),
                  modspec(2),
                  pl.BlockSpec((1, d), full2),
                  modspec(3),
                  modspec(4),
                  pl.BlockSpec(rw_t.shape, full2)],
        out_specs=[pl.BlockSpec((tm, d), row),
                   pl.BlockSpec((tm, d // 2), row),
                   pl.BlockSpec((N_EXPERTS, tm), lambda i: (0, i))],
        out_shape=[jax.ShapeDtypeStruct((t, d), F32),
                   jax.ShapeDtypeStruct((t, d // 2), jnp.uint32),
                   jax.ShapeDtypeStruct((N_EXPERTS, t), F32)],
        compiler_params=_params("arbitrary"),
        name="out_proj",
    )(o_mla, o_sb, gm, gs, w_o, x, mod, g2, mod, mod, rw_t)


def _route_kernel(lg_ref, rb_ref, tri_ref, e_ref, w_ref, rank_ref, cnt_ref, run_ref):
    i = pl.program_id(0)

    @pl.when(i == 0)
    def _():
        run_ref[...] = jnp.zeros_like(run_ref)

    scores = jax.nn.sigmoid(lg_ref[...])
    sel = scores + rb_ref[...]
    tm = sel.shape[1]
    epg = EXPERTS_PER_GROUP
    srow = [sel[e:e + 1, :] for e in range(N_EXPERTS)]
    prow = [scores[e:e + 1, :] for e in range(N_EXPERTS)]

    gscore = []
    for g in range(N_GROUPS):
        a, b, c, d = srow[g * epg:(g + 1) * epg]
        hi1, lo1 = jnp.maximum(a, b), jnp.minimum(a, b)
        hi2, lo2 = jnp.maximum(c, d), jnp.minimum(c, d)
        gscore.append(jnp.maximum(hi1, hi2)
                      + jnp.maximum(jnp.minimum(hi1, hi2), jnp.maximum(lo1, lo2)))
    best = gscore[0]
    gidx = jnp.zeros((1, tm), jnp.int32)
    for g in range(1, N_GROUPS):
        better = gscore[g] > best
        gidx = jnp.where(better, g, gidx)
        best = jnp.where(better, gscore[g], best)

    def pick(rows, j):
        out = rows[j]
        for g in range(1, N_GROUPS):
            out = jnp.where(gidx == g, rows[g * epg + j], out)
        return out

    v = [pick(srow, j) for j in range(epg)]
    p = [pick(prow, j) for j in range(epg)]
    pos = []
    for j in range(epg):
        r = jnp.zeros((1, tm), jnp.int32)
        for o in range(epg):
            if o == j:
                continue
            ahead = (v[o] > v[j]) | ((v[o] == v[j]) & (o < j))
            r = r + ahead.astype(jnp.int32)
        pos.append(r)
    local, wsel = [], []
    for k in range(TOP_K):
        lk = jnp.zeros((1, tm), jnp.int32)
        wk = jnp.zeros((1, tm), F32)
        for j in range(epg):
            hit = pos[j] == k
            lk = jnp.where(hit, j, lk)
            wk = jnp.where(hit, p[j], wk)
        local.append(lk)
        wsel.append(wk)
    wsum = wsel[0] + wsel[1]
    idx = [gidx * epg + local[k] for k in range(TOP_K)]

    eiota = lax.broadcasted_iota(jnp.int32, (N_EXPERTS, tm), 0)
    hit = [eiota == idx[k] for k in range(TOP_K)]
    onehot = (hit[0] | hit[1]).astype(F32)
    before = jnp.dot(onehot.astype(BF16), tri_ref[...], preferred_element_type=F32) + run_ref[...]
    for k in range(TOP_K):
        e_ref[pl.ds(k, 1), :] = idx[k]
        w_ref[pl.ds(k, 1), :] = wsel[k] / wsum
        rank_ref[pl.ds(k, 1), :] = jnp.sum(jnp.where(hit[k], before, 0.0), axis=0,
                                           keepdims=True).astype(jnp.int32)
    run_ref[...] += jnp.sum(onehot, axis=1, keepdims=True)
    cnt_ref[...] = jnp.broadcast_to(run_ref[...], cnt_ref.shape)


def _route(logits_t, router_b, tm=512):
    t = logits_t.shape[1]
    tm = min(tm, t)
    tri = (lax.broadcasted_iota(jnp.int32, (tm, tm), 0)
           < lax.broadcasted_iota(jnp.int32, (tm, tm), 1)).astype(BF16)
    blk = lambda rows: pl.BlockSpec((rows, tm), lambda i: (0, i))
    return pl.pallas_call(
        _route_kernel,
        grid=(t // tm,),
        in_specs=[blk(N_EXPERTS),
                  pl.BlockSpec((N_EXPERTS, 1), lambda i: (0, 0)),
                  pl.BlockSpec((tm, tm), lambda i: (0, 0))],
        out_specs=[blk(TOP_K), blk(TOP_K), blk(TOP_K),
                   pl.BlockSpec((N_EXPERTS, LANE), lambda i: (0, 0))],
        out_shape=[jax.ShapeDtypeStruct((TOP_K, t), jnp.int32),
                   jax.ShapeDtypeStruct((TOP_K, t), F32),
                   jax.ShapeDtypeStruct((TOP_K, t), jnp.int32),
                   jax.ShapeDtypeStruct((N_EXPERTS, LANE), F32)],
        scratch_shapes=[pltpu.VMEM((N_EXPERTS, 1), F32)],
        compiler_params=_params("arbitrary"),
        name="route",
    )(logits_t, router_b[:, None], tri)


def _dispatch_kernel(dest_ref, pend_ref, h_ref, o_ref, zero_ref, ring, in_sems, row_sems, zsem,
                     *, tm, t):
    base = pl.program_id(0) * tm

    @pl.when(pl.program_id(0) == 0)
    def _():
        zero_ref[...] = jnp.zeros_like(zero_ref)

        def fill(e):
            end = pend_ref[e]
            start = pend_ref[e - 1] if e > 0 else 0
            dst = o_ref.at[pl.ds(pl.multiple_of(jnp.maximum(end - MOE_ROWS, 0), MOE_ROWS),
                                 MOE_ROWS), :]
            return end > start, pltpu.make_async_copy(zero_ref, dst, zsem)

        for e in range(N_EXPERTS):
            nonempty, cp = fill(e)
            pl.when(nonempty)(cp.start)
        for e in range(N_EXPERTS):
            nonempty, cp = fill(e)
            pl.when(nonempty)(cp.wait)

        def tail(r):
            dst = o_ref.at[pl.ds(pl.multiple_of(r * MOE_ROWS, MOE_ROWS), MOE_ROWS), :]
            return pltpu.make_async_copy(zero_ref, dst, zsem)

        first_free = pend_ref[N_EXPERTS - 1] // MOE_ROWS
        n_blocks = o_ref.shape[0] // MOE_ROWS
        lax.fori_loop(first_free, n_blocks, lambda r, c: (tail(r).start(), c)[1], 0)
        lax.fori_loop(first_free, n_blocks, lambda r, c: (tail(r).wait(), c)[1], 0)

    i = pl.program_id(0)
    n = pl.num_programs(0)

    def stage(tile):
        return pltpu.make_async_copy(h_ref.at[pl.ds(tile * tm, tm), :], ring.at[tile % 3],
                                     in_sems.at[tile % 3])

    @pl.when(i == 0)
    def _():
        stage(0).start()

    @pl.when(i + 1 < n)
    def _():
        stage(i + 1).start()

    stage(i).wait()
    src = ring.at[i % 3]

    def issue(r, carry):
        for k in range(TOP_K):
            d = dest_ref[k * t + base + r]
            pltpu.make_async_copy(src.at[pl.ds(r, 1), :], o_ref.at[pl.ds(d, 1), :],
                                  row_sems.at[i % 2]).start()
        return carry

    lax.fori_loop(0, tm, issue, 0, unroll=DMA_ISSUE_UNROLL)

    def wait_rows(step):
        for k in range(TOP_K):
            pltpu.make_async_copy(ring.at[0], o_ref.at[pl.ds(0, tm), :],
                                  row_sems.at[step % 2]).wait()

    pl.when(i > 0)(lambda: wait_rows(i - 1))
    pl.when(i == n - 1)(lambda: wait_rows(i))


def _dispatch(h2, dest_flat, pad_end, rows, tm=512):
    t, d = h2.shape
    tm = min(tm, t)
    return pl.pallas_call(
        functools.partial(_dispatch_kernel, tm=tm, t=t),
        grid_spec=pltpu.PrefetchScalarGridSpec(
            num_scalar_prefetch=2,
            grid=(t // tm,),
            in_specs=[pl.BlockSpec(memory_space=pl.ANY)],
            out_specs=pl.BlockSpec(memory_space=pl.ANY),
            scratch_shapes=[pltpu.VMEM((MOE_ROWS, d), h2.dtype),
                            pltpu.VMEM((3, tm, d), h2.dtype),
                            pltpu.SemaphoreType.DMA((3,)), pltpu.SemaphoreType.DMA((2,)),
                            pltpu.SemaphoreType.DMA(())]),
        out_shape=jax.ShapeDtypeStruct((rows, d), h2.dtype),
        compiler_params=_params("arbitrary"),
        name="moe_dispatch",
    )(dest_flat, pad_end, h2)


def _expert_kernel(be_ref, nu_ref, next_ref, slot_ref, x_ref, wg_hbm, wu_hbm, wd_hbm, y_ref,
                   wgf, wuf, wdf, wgb, wub, wdb, sems, *, layer):
    r = pl.program_id(0)
    live = r < nu_ref[0]
    e = be_ref[r]
    new_expert = (r == 0) | (be_ref[jnp.maximum(r - 1, 0)] != e)
    slot = slot_ref[e]

    def fetch(expert, s):
        pairs = ((wg_hbm, wgf), (wu_hbm, wuf), (wd_hbm, wdf))
        return [pltpu.make_async_copy(w.at[layer, expert], buf.at[s], sems.at[s, j])
                for j, (w, buf) in enumerate(pairs)]

    @pl.when(r == 0)
    def _():
        for cp in fetch(e, slot):
            cp.start()

    @pl.when(live & new_expert)
    def _():
        nxt = next_ref[e]

        @pl.when(nxt != e)
        def _():
            for cp in fetch(nxt, 1 - slot):
                cp.start()

        for cp in fetch(e, slot):
            cp.wait()
        wgb[...] = wgf[slot].astype(BF16)
        wub[...] = wuf[slot].astype(BF16)
        wdb[...] = wdf[slot].astype(BF16)

    @pl.when(live)
    def _():
        lo, hi = _unpack_rows(x_ref[...])
        xb = jnp.concatenate([lo.astype(BF16), hi.astype(BF16)], axis=1)
        g = jnp.dot(xb, wgb[...], preferred_element_type=F32)
        u = jnp.dot(xb, wub[...], preferred_element_type=F32)
        a = (g * jax.nn.sigmoid(g)) * u
        y_ref[...] = _pack_rows(jnp.dot(a.astype(BF16), wdb[...], preferred_element_type=F32))

    @pl.when(jnp.logical_not(live))
    def _():
        y_ref[...] = jnp.zeros_like(y_ref)


def _experts(x_sorted, block_e, n_used, next_expert, expert_slot, w_gate, w_up, w_down, layer):
    rows, d = x_sorted.shape
    de = w_gate.shape[-1]
    dm = w_gate.shape[-2]
    bm = MOE_ROWS
    blk = lambda r, be, nu, nx, sl: (jnp.minimum(r, nu[0] - 1), 0)
    hbm = pl.BlockSpec(memory_space=pl.ANY)
    return pl.pallas_call(
        functools.partial(_expert_kernel, layer=layer),
        grid_spec=pltpu.PrefetchScalarGridSpec(
            num_scalar_prefetch=4,
            grid=(rows // bm,),
            in_specs=[pl.BlockSpec((bm, d), blk), hbm, hbm, hbm],
            out_specs=pl.BlockSpec((bm, d), lambda r, be, nu, nx, sl: (r, 0)),
            scratch_shapes=[pltpu.VMEM((2, dm, de), F32), pltpu.VMEM((2, dm, de), F32),
                            pltpu.VMEM((2, de, dm), F32),
                            pltpu.VMEM((dm, de), BF16), pltpu.VMEM((dm, de), BF16),
                            pltpu.VMEM((de, dm), BF16),
                            pltpu.SemaphoreType.DMA((2, 3))]),
        out_shape=jax.ShapeDtypeStruct((rows, d), jnp.uint32),
        compiler_params=_params("arbitrary"),
        name="moe_experts",
    )(block_e, n_used, next_expert, expert_slot, x_sorted, w_gate, w_up, w_down)


def _combine_kernel(dest_ref, y_ref, x_ref, gate_ref, w_ref, o_ref, buf, sems, *, tm, t):
    i = pl.program_id(0)
    slot = i % 2

    def gather(tile, dst_slot):
        base = tile * tm

        def issue(r, carry):
            for k in range(TOP_K):
                d = dest_ref[k * t + base + r]
                pltpu.make_async_copy(y_ref.at[pl.ds(d, 1), :],
                                      buf.at[dst_slot, k, pl.ds(r, 1), :],
                                      sems.at[dst_slot]).start()
            return carry

        lax.fori_loop(0, tm, issue, 0, unroll=DMA_ISSUE_UNROLL)

    @pl.when(i == 0)
    def _():
        gather(0, 0)

    @pl.when(i + 1 < pl.num_programs(0))
    def _():
        gather(i + 1, 1 - slot)

    for k in range(TOP_K):
        pltpu.make_async_copy(y_ref.at[pl.ds(0, tm), :], buf.at[slot, k], sems.at[slot]).wait()
    w = w_ref[...]
    half = buf.shape[3]
    lo0, hi0 = _unpack_rows(buf[slot, 0])
    lo1, hi1 = _unpack_rows(buf[slot, 1])
    o_ref[:, :half] = x_ref[:, :half] + gate_ref[:, :half] * (w[:, 0:1] * lo0 + w[:, 1:2] * lo1)
    o_ref[:, half:] = x_ref[:, half:] + gate_ref[:, half:] * (w[:, 0:1] * hi0 + w[:, 1:2] * hi1)


def _combine(y_sorted, dest_flat, x, mod, w_tok, seq, tm=512):
    t, d = x.shape
    tm = min(tm, seq)
    per_b = seq // tm
    return pl.pallas_call(
        functools.partial(_combine_kernel, tm=tm, t=t),
        grid_spec=pltpu.PrefetchScalarGridSpec(
            num_scalar_prefetch=1,
            grid=(t // tm,),
            in_specs=[pl.BlockSpec(memory_space=pl.ANY),
                      pl.BlockSpec((tm, d), lambda i, dest: (i, 0)),
                      pl.BlockSpec((None, 1, d), lambda i, dest: ((i // per_b) * 6 + 5, 0, 0)),
                      pl.BlockSpec((tm, TOP_K), lambda i, dest: (i, 0))],
            out_specs=pl.BlockSpec((tm, d), lambda i, dest: (i, 0)),
            scratch_shapes=[pltpu.VMEM((2, TOP_K, tm, y_sorted.shape[1]), y_sorted.dtype),
                            pltpu.SemaphoreType.DMA((2,))]),
        out_shape=jax.ShapeDtypeStruct((t, d), F32),
        compiler_params=_params("arbitrary"),
        name="moe_combine",
    )(dest_flat, y_sorted, x, mod, w_tok)


def _rope_tables(seq):
    half = MLA_ROPE // 2
    inv = 1.0 / (ROPE_BASE ** (jnp.arange(0, MLA_ROPE, 2, dtype=F32) / MLA_ROPE))
    ang = jnp.arange(seq, dtype=F32)[:, None] * inv[None, :]
    cos, sin = jnp.cos(ang), jnp.sin(ang)
    z = jnp.zeros((seq, half), F32)
    zz = jnp.zeros((seq, LANE - MLA_ROPE), F32)
    cos_t = jnp.concatenate([cos, cos, zz], axis=1)
    s1_t = jnp.concatenate([z, sin, zz], axis=1)
    s2_t = jnp.concatenate([-sin, z, zz], axis=1)
    return cos_t, s1_t, s2_t


def _pad_last(a, width):
    return jnp.pad(a, [(0, 0)] * (a.ndim - 1) + [(0, width - a.shape[-1])])


def kernel(x, c, norm1_g, norm2_g, w_ada, b_ada, w_in, q_lora_g, kv_lora_g, w_uq, w_ukv,
           q_head_g, k_head_g, mla_out_g, sb_out_g, w_o, router_w, router_b,
           w_gate, w_up, w_down):
    bsz, seq, d = x.shape
    t = bsz * seq
    depth = w_ada.shape[0]

    sb_col0 = COL_KR + MLA_ROPE
    w_in_mla = _pad_last(w_in[..., :sb_col0], MLA_IN_COLS).astype(BF16)
    w_in_sb = w_in[..., sb_col0:].astype(BF16)
    wq = w_uq.reshape(depth, Q_LORA, MLA_HEADS, MLA_QK)
    half = MLA_ROPE // 2
    swap = lambda a: jnp.concatenate([a[..., half:], a[..., :half]], axis=-1)
    wq_rope = wq[..., MLA_NOPE:]
    wuq_p = jnp.concatenate(
        [wq[..., :MLA_NOPE].reshape(depth, Q_LORA, -1),
         _pad_last(wq_rope, LANE).reshape(depth, Q_LORA, -1),
         _pad_last(swap(wq_rope), LANE).reshape(depth, Q_LORA, -1)], axis=-1).astype(BF16)
    wkv = w_ukv.reshape(depth, KV_LORA, MLA_HEADS, MLA_NOPE + MLA_V)
    wukv_p = jnp.concatenate(
        [wkv[..., :MLA_NOPE].reshape(depth, KV_LORA, -1),
         wkv[..., MLA_NOPE:].reshape(depth, KV_LORA, -1)], axis=-1).astype(BF16)
    w_o_b = w_o.astype(BF16)
    q_scale = MLA_QK ** -0.5 * LOG2_E
    gqn = (q_head_g[:, None, :MLA_NOPE] * q_scale)
    gq_rope = q_head_g[:, None, MLA_NOPE:] * q_scale
    gqr = jnp.concatenate([_pad_last(gq_rope, LANE), _pad_last(swap(gq_rope), LANE)], axis=-1)
    gkn = k_head_g[:, None, :MLA_NOPE]
    gkr = _pad_last(k_head_g[:, None, MLA_NOPE:], LANE)
    rope_tabs = _rope_tables(seq)
    rw_t = router_w.T.astype(BF16)

    mod_all = _ada_modulation(c[:, :, None], w_ada, b_ada[:, None, :], layers=depth)
    n_blocks = -(-(t * TOP_K) // MOE_ROWS) + N_EXPERTS
    rows = n_blocks * MOE_ROWS

    xt = x.reshape(t, d)
    for l in range(depth):
        mod = mod_all[l].reshape(bsz * 6, 1, d)
        q2d, k2d, v2d, proj_sb = _in_proj(
            xt, norm1_g[l][None, :], mod, w_in_mla, w_in_sb, rope_tabs, q_lora_g[:, None, :],
            kv_lora_g[:, None, :], wuq_p, wukv_p, gqn, gqr, gkn, gkr, l, seq)
        o_mla = _mla_attention(q2d, k2d, v2d, bsz, seq)
        o_sb = _sb_attention(proj_sb, bsz, seq)
        xt, h2, logits_t = _out_proj(o_mla, o_sb, mla_out_g[l][None, :], sb_out_g[l][None, :],
                                     w_o_b, xt, mod, norm2_g[l][None, :], rw_t, l, seq)
        e_idx, w_tok, rank, cnt = _route(logits_t, router_b)
        counts = cnt[:, 0].astype(jnp.int32)
        padded = (counts + MOE_ROWS - 1) // MOE_ROWS * MOE_ROWS
        pad_end = jnp.cumsum(padded)
        pad_start = pad_end - padded
        eids = jnp.arange(N_EXPERTS, dtype=jnp.int32)
        seg_start = jnp.sum(jnp.where(e_idx[None] == eids[:, None, None],
                                      pad_start[:, None, None], 0), axis=0)
        dest = (seg_start + rank).reshape(TOP_K * t).astype(jnp.int32)
        n_used = (pad_end[-1:] // MOE_ROWS).astype(jnp.int32)
        first_row = jnp.minimum(jnp.arange(n_blocks, dtype=jnp.int32), n_used - 1) * MOE_ROWS
        block_e = jnp.sum(pad_end[None, :] <= first_row[:, None], axis=1).astype(jnp.int32)
        nonempty = counts > 0
        later = (eids[None, :] > eids[:, None]) & nonempty[None, :]
        next_expert = jnp.min(jnp.where(later, eids[None, :], N_EXPERTS), axis=1)
        next_expert = jnp.where(next_expert == N_EXPERTS, eids, next_expert).astype(jnp.int32)
        expert_slot = ((jnp.cumsum(nonempty) - nonempty) % 2).astype(jnp.int32)
        x_sorted = _dispatch(h2, dest, pad_end.astype(jnp.int32), rows)
        y_sorted = _experts(x_sorted, block_e, n_used, next_expert, expert_slot,
                            w_gate, w_up, w_down, l)
        xt = _combine(y_sorted, dest, xt, mod, w_tok.T, seq)
    return xt.reshape(bsz, seq, d)
```

```python
import functools

import jax
import jax.numpy as jnp
from jax import lax
from jax.experimental import pallas as pl
from jax.experimental.pallas import tpu as pltpu

F32 = jnp.float32
BF16 = jnp.bfloat16

EPS = 1e-6
ROPE_BASE = 10000.0
CHUNK = 64

MLA_HEADS = 8
MLA_NOPE = 128
MLA_ROPE = 64
MLA_QK = MLA_NOPE + MLA_ROPE
MLA_V = 128
Q_LORA = 512
KV_LORA = 256
SB_HEADS = 8
SB_DIM = 128
SB_WIDTH = SB_HEADS * SB_DIM
LANE = 128
QK_PAD = 2 * LANE
COL_CQ = 0
COL_CKV = Q_LORA
COL_KR = Q_LORA + KV_LORA
MLA_IN_COLS = COL_KR + LANE
COL_SQ = 0
COL_SK = SB_WIDTH
COL_SV = 2 * SB_WIDTH
SB_IN_COLS = 3 * SB_WIDTH

N_EXPERTS = 16
N_GROUPS = 4
EXPERTS_PER_GROUP = N_EXPERTS // N_GROUPS
TOP_K = 2
D_EXPERT = 512
MOE_ROWS = 512
DMA_ISSUE_UNROLL = 16
LOG2_E = 1.4426950408889634
SB_UNDERFLOW_LOG2 = 150.0
SB_FINISHED_LOG2 = 1e30

VMEM_LIMIT = 56 * 1024 * 1024


def _params(*dims):
    return pltpu.CompilerParams(dimension_semantics=dims, vmem_limit_bytes=VMEM_LIMIT)


def _rms(x):
    return x * lax.rsqrt(jnp.mean(x * x, axis=-1, keepdims=True) + EPS)


def _pack_rows(x):
    n = x.shape[1] // 2
    lo = lax.bitcast_convert_type(x[:, :n].astype(BF16).astype(F32), jnp.uint32)
    hi = lax.bitcast_convert_type(x[:, n:].astype(BF16).astype(F32), jnp.uint32)
    return (lo >> 16) | (hi & jnp.uint32(0xFFFF0000))


def _unpack_rows(w):
    lo = lax.bitcast_convert_type(w << 16, F32)
    hi = lax.bitcast_convert_type(w & jnp.uint32(0xFFFF0000), F32)
    return lo, hi


def _ada_kernel(c_ref, w_ref, b_ref, o_ref):
    w = w_ref[...]
    for b in range(c_ref.shape[0]):
        cb = c_ref[b]
        cb = cb * jax.nn.sigmoid(cb)
        o_ref[pl.ds(b, 1), :] = jnp.sum(cb * w, axis=0, keepdims=True) + b_ref[...]


def _ada_modulation(c_col, w_ada, b_ada3, layers, tn=1024):
    _, d, n = w_ada.shape
    bsz = c_col.shape[0]
    return pl.pallas_call(
        _ada_kernel,
        grid=(layers, n // tn),
        in_specs=[pl.BlockSpec(c_col.shape, lambda l, j: (0, 0, 0)),
                  pl.BlockSpec((None, d, tn), lambda l, j: (l, 0, j)),
                  pl.BlockSpec((None, 1, tn), lambda l, j: (l, 0, j))],
        out_specs=pl.BlockSpec((None, bsz, tn), lambda l, j: (l, 0, j)),
        out_shape=jax.ShapeDtypeStruct((layers, bsz, n), F32),
        compiler_params=_params("arbitrary", "arbitrary"),
        name="ada_modulation",
    )(c_col, w_ada, b_ada3)


def _in_proj_kernel(x_ref, g_ref, sh_ref, sc_ref, wa_ref, wb_ref, cos_ref, s1_ref, s2_ref,
                    gql_ref, gkvl_ref, wuq_ref, wukv_ref, gqn_ref, gqr_ref, gkn_ref, gkr_ref,
                    q_ref, k_ref, v_ref, ob_ref):
    hn = _rms(x_ref[...]) * g_ref[...]
    hn = (hn * (1.0 + sc_ref[...]) + sh_ref[...]).astype(BF16)
    sb = jnp.dot(hn, wb_ref[...], preferred_element_type=F32)
    ob_ref[:, :SB_WIDTH] = (sb[:, :SB_WIDTH] * (SB_DIM ** -0.5 * LOG2_E)).astype(ob_ref.dtype)
    ob_ref[:, SB_WIDTH:] = sb[:, SB_WIDTH:].astype(ob_ref.dtype)

    pa = jnp.dot(hn, wa_ref[...], preferred_element_type=F32)
    nope_w = MLA_HEADS * MLA_NOPE
    cqn = _rms(pa[:, COL_CQ:COL_CKV]) * gql_ref[...]
    q = jnp.dot(cqn.astype(BF16), wuq_ref[...], preferred_element_type=F32)
    ckvn = _rms(pa[:, COL_CKV:COL_KR]) * gkvl_ref[...]
    kv = jnp.dot(ckvn.astype(BF16), wukv_ref[...], preferred_element_type=F32)
    cosv, s1, s2 = cos_ref[...], s1_ref[...], s2_ref[...]

    def rope(x):
        return (x * cosv + pltpu.roll(x, MLA_ROPE // 2, 1) * s1
                + pltpu.roll(x, LANE - MLA_ROPE // 2, 1) * s2)

    kr = pa[:, COL_KR:]
    kr_ss = jnp.sum(kr * kr, axis=-1, keepdims=True)
    kr_roped = rope(kr * gkr_ref[...])
    gqn, gkn = gqn_ref[...], gkn_ref[...]
    gqr, gqs = gqr_ref[:, :LANE], gqr_ref[:, LANE:]
    s_comb = s1 + s2
    for h in range(MLA_HEADS):
        qn = q[:, h * LANE:(h + 1) * LANE]
        qr = q[:, nope_w + h * LANE:nope_w + (h + 1) * LANE]
        ss = jnp.sum(qn * qn + qr * qr, axis=-1, keepdims=True)
        r = lax.rsqrt(ss * (1.0 / MLA_QK) + EPS)
        q_ref[:, h * QK_PAD:h * QK_PAD + LANE] = (qn * r * gqn).astype(BF16)
        qs = q[:, 2 * nope_w + h * LANE:2 * nope_w + (h + 1) * LANE]
        q_rope = (qr * gqr) * cosv + (qs * gqs) * s_comb
        q_ref[:, h * QK_PAD + LANE:(h + 1) * QK_PAD] = (q_rope * r).astype(BF16)
        kn = kv[:, h * LANE:(h + 1) * LANE]
        rk = lax.rsqrt((jnp.sum(kn * kn, axis=-1, keepdims=True) + kr_ss) * (1.0 / MLA_QK) + EPS)
        k_ref[:, h * QK_PAD:h * QK_PAD + LANE] = (kn * rk * gkn).astype(BF16)
        k_ref[:, h * QK_PAD + LANE:(h + 1) * QK_PAD] = (kr_roped * rk).astype(BF16)
        v_ref[:, 2 * h * MLA_V:(2 * h + 1) * MLA_V] = (
            kv[:, nope_w + h * MLA_V:nope_w + (h + 1) * MLA_V].astype(BF16))
        v_ref[:, (2 * h + 1) * MLA_V:(2 * h + 2) * MLA_V] = jnp.ones((kn.shape[0], MLA_V), BF16)


def _in_proj(x, g, mod, w_mla, w_sb, rope_tabs, gql, gkvl, wuq, wukv, gqn, gqr, gkn, gkr,
             layer, seq, tm=256):
    t, d = x.shape
    na, nb = w_mla.shape[-1], w_sb.shape[-1]
    per_b = seq // tm
    cos_t, s1_t, s2_t = rope_tabs
    row = lambda i: (i, 0)
    lay = lambda i: (layer, 0, 0)
    tab = lambda i: (i % per_b, 0)
    return pl.pallas_call(
        _in_proj_kernel,
        grid=(t // tm,),
        in_specs=[pl.BlockSpec((tm, d), row),
                  pl.BlockSpec((1, d), lambda i: (0, 0)),
                  pl.BlockSpec((None, 1, d), lambda i: ((i // per_b) * 6 + 0, 0, 0)),
                  pl.BlockSpec((None, 1, d), lambda i: ((i // per_b) * 6 + 1, 0, 0)),
                  pl.BlockSpec((None, d, na), lay),
                  pl.BlockSpec((None, d, nb), lay),
                  pl.BlockSpec((tm, LANE), tab),
                  pl.BlockSpec((tm, LANE), tab),
                  pl.BlockSpec((tm, LANE), tab),
                  pl.BlockSpec((None, 1, Q_LORA), lay),
                  pl.BlockSpec((None, 1, KV_LORA), lay),
                  pl.BlockSpec((None,) + wuq.shape[1:], lay),
                  pl.BlockSpec((None,) + wukv.shape[1:], lay),
                  pl.BlockSpec((None, 1, LANE), lay),
                  pl.BlockSpec((None, 1, 2 * LANE), lay),
                  pl.BlockSpec((None, 1, LANE), lay),
                  pl.BlockSpec((None, 1, LANE), lay)],
        out_specs=[pl.BlockSpec((tm, MLA_HEADS * QK_PAD), row),
                   pl.BlockSpec((tm, MLA_HEADS * QK_PAD), row),
                   pl.BlockSpec((tm, MLA_HEADS * 2 * MLA_V), row),
                   pl.BlockSpec((tm, nb), row)],
        out_shape=[jax.ShapeDtypeStruct((t, MLA_HEADS * QK_PAD), BF16),
                   jax.ShapeDtypeStruct((t, MLA_HEADS * QK_PAD), BF16),
                   jax.ShapeDtypeStruct((t, MLA_HEADS * 2 * MLA_V), BF16),
                   jax.ShapeDtypeStruct((t, nb), BF16)],
        compiler_params=_params("arbitrary"),
        name="in_proj",
    )(x, g, mod, mod, w_mla, w_sb, cos_t, s1_t, s2_t, gql, gkvl, wuq, wukv, gqn, gqr, gkn, gkr)


def _mla_attn_kernel(q_ref, k_ref, v_ref, o_ref, m_ref, l_ref, acc_ref, sa_ref, sb_ref, *,
                     tq, heads):
    i = pl.program_id(2)
    vw = 2 * MLA_V

    def scores(g, kb):
        q = q_ref[:, g * QK_PAD:(g + 1) * QK_PAD]
        k = k_ref[pl.ds(pl.multiple_of(kb * tq, tq), tq), g * QK_PAD:(g + 1) * QK_PAD]
        return lax.dot_general(q, k, (((1,), (1,)), ((), ())), preferred_element_type=F32)

    def probs(s, m):
        return jnp.concatenate(
            [jnp.exp2(s[:, c * LANE:(c + 1) * LANE] - m) for c in range(tq // LANE)],
            axis=1).astype(BF16)

    def pv(g, p, kb):
        v = v_ref[pl.ds(pl.multiple_of(kb * tq, tq), tq), g * vw:(g + 1) * vw]
        return jnp.dot(p, v, preferred_element_type=F32)

    def update(g, s, kb):
        m_old = m_ref[g]
        m_new = jnp.maximum(m_old, jnp.max(s, axis=-1, keepdims=True))
        alpha = jnp.exp2(m_old - m_new)
        r = pv(g, probs(s, m_new), kb)
        acc_ref[g] = alpha * acc_ref[g] + r[:, :MLA_V]
        l_ref[g] = alpha * l_ref[g] + r[:, MLA_V:]
        m_ref[g] = m_new

    for g in range(heads):
        m_ref[g] = jnp.full((tq, LANE), -jnp.inf, F32)
        l_ref[g] = jnp.zeros((tq, MLA_V), F32)
        acc_ref[g] = jnp.zeros((tq, MLA_V), F32)
        sa_ref[g] = scores(g, 0)

    def pair(t, carry):
        j = 2 * t
        for g in range(heads):
            sb_ref[g] = scores(g, j + 1)
            update(g, sa_ref[g], j)
        for g in range(heads):
            sa_ref[g] = scores(g, j + 2)
            update(g, sb_ref[g], j + 1)
        return carry

    lax.fori_loop(0, i // 2, pair, 0)

    q_chunk = lax.broadcasted_iota(jnp.int32, (tq, 1), 0) // CHUNK
    k_chunk = lax.broadcasted_iota(jnp.int32, (1, tq), 1) // CHUNK
    visible = k_chunk <= q_chunk

    @pl.when(i % 2 == 1)
    def _():
        for g in range(heads):
            sb_ref[g] = scores(g, i)
            update(g, sa_ref[g], i - 1)
        for g in range(heads):
            update(g, jnp.where(visible, sb_ref[g], -jnp.inf), i)

    @pl.when(i % 2 == 0)
    def _():
        for g in range(heads):
            update(g, jnp.where(visible, sa_ref[g], -jnp.inf), i)

    for g in range(heads):
        o_ref[:, g * MLA_V:(g + 1) * MLA_V] = acc_ref[g] / l_ref[g]


def _mla_attention(q2d, k2d, v2d, bsz, seq, tq=512, heads=4):
    t = q2d.shape[0]
    tq = min(tq, seq)
    nq = seq // tq
    vw = 2 * MLA_V
    return pl.pallas_call(
        functools.partial(_mla_attn_kernel, tq=tq, heads=heads),
        grid=(bsz, MLA_HEADS // heads, nq),
        in_specs=[pl.BlockSpec((tq, heads * QK_PAD), lambda b, h, i: (b * nq + i, h)),
                  pl.BlockSpec((seq, heads * QK_PAD), lambda b, h, i: (b, h)),
                  pl.BlockSpec((seq, heads * vw), lambda b, h, i: (b, h))],
        out_specs=pl.BlockSpec((tq, heads * MLA_V), lambda b, h, i: (b * nq + i, h)),
        out_shape=jax.ShapeDtypeStruct((t, MLA_HEADS * MLA_V), F32),
        scratch_shapes=[pltpu.VMEM((heads, tq, LANE), F32), pltpu.VMEM((heads, tq, MLA_V), F32),
                        pltpu.VMEM((heads, tq, MLA_V), F32),
                        pltpu.VMEM((heads, tq, tq), F32), pltpu.VMEM((heads, tq, tq), F32)],
        compiler_params=_params("arbitrary", "arbitrary", "arbitrary"),
        name="mla_attention",
    )(q2d, k2d, v2d)


def _sb_attn_kernel(q_ref, k_ref, v_ref, tri_ref, o_ref, run_ref, acc_ref, *, tq, tk, heads):
    i = pl.program_id(2)
    tri = tri_ref[...]
    sub = tq // tk
    streams = [(g, r) for g in range(heads) for r in range(sub)]

    def block(s, kb, mask):
        g, r = streams[s]
        cols = slice(g * SB_DIM, (g + 1) * SB_DIM)
        keys = pl.ds(pl.multiple_of(kb * tk, tk), tk)
        q = q_ref[r * tk:(r + 1) * tk, cols]
        z = lax.dot_general(q, k_ref[keys, cols], (((1,), (1,)), ((), ())),
                            preferred_element_type=F32)
        neg_abs = lax.bitcast_convert_type(
            lax.bitcast_convert_type(z, jnp.uint32) | jnp.uint32(0x80000000), F32)
        cost = jnp.maximum(z, 0.0) + jnp.log2(1.0 + jnp.exp2(neg_abs))
        if mask is not None:
            cost = jnp.where(mask, cost, 0.0)
        after = jnp.dot(cost.astype(BF16), tri, preferred_element_type=F32) + run_ref[s]
        a = jnp.exp2(z - cost - after)
        if mask is not None:
            a = jnp.where(mask, a, 0.0)
        acc_ref[s] += jnp.dot(a.astype(BF16), v_ref[keys, cols], preferred_element_type=F32)
        run_ref[s] += jnp.sum(cost, axis=-1, keepdims=True)

    def more():
        return jnp.min(run_ref[...]) < SB_UNDERFLOW_LOG2

    run_ref[...] = jnp.zeros_like(run_ref)
    acc_ref[...] = jnp.zeros_like(acc_ref)
    strictly_before = (lax.broadcasted_iota(jnp.int32, (tk, tk), 1)
                       < lax.broadcasted_iota(jnp.int32, (tk, tk), 0))
    for s, (g, r) in enumerate(streams):
        block(s, i * sub + r, strictly_before)

    def cond(c):
        return (c[0] < i * sub + sub - 1) & c[1]

    def body(c):
        for s, (g, r) in enumerate(streams):
            kb = i * sub + r - 1 - c[0]
            run_ref[s] = jnp.where(kb >= 0, run_ref[s], SB_FINISHED_LOG2)
            block(s, jnp.maximum(kb, 0), None)
        return c[0] + 1, more()

    lax.while_loop(cond, body, (jnp.int32(0), more()))
    for s, (g, r) in enumerate(streams):
        o_ref[r * tk:(r + 1) * tk, g * SB_DIM:(g + 1) * SB_DIM] = acc_ref[s]


def _sb_attention(proj, bsz, seq, tq=512, tk=256, heads=8):
    t = proj.shape[0]
    tq = min(tq, seq)
    nq = seq // tq
    tri = (lax.broadcasted_iota(jnp.int32, (tk, tk), 0)
           > lax.broadcasted_iota(jnp.int32, (tk, tk), 1)).astype(BF16)
    w = heads * SB_DIM
    cq, ck, cv = COL_SQ // w, COL_SK // w, COL_SV // w
    n_streams = heads * (tq // tk)
    return pl.pallas_call(
        functools.partial(_sb_attn_kernel, tq=tq, tk=tk, heads=heads),
        grid=(bsz, SB_HEADS // heads, nq),
        in_specs=[pl.BlockSpec((tq, w), lambda b, h, i: (b * nq + i, cq + h)),
                  pl.BlockSpec((seq, w), lambda b, h, i: (b, ck + h)),
                  pl.BlockSpec((seq, w), lambda b, h, i: (b, cv + h)),
                  pl.BlockSpec((tk, tk), lambda b, h, i: (0, 0))],
        out_specs=pl.BlockSpec((tq, w), lambda b, h, i: (b * nq + i, h)),
        out_shape=jax.ShapeDtypeStruct((t, SB_WIDTH), F32),
        scratch_shapes=[pltpu.VMEM((n_streams, tk, 1), F32),
                        pltpu.VMEM((n_streams, tk, SB_DIM), F32)],
        compiler_params=_params("arbitrary", "arbitrary", "arbitrary"),
        name="sb_attention",
    )(proj, proj, proj, tri)


def _out_proj_kernel(om_ref, os_ref, gm_ref, gs_ref, wo_ref, x_ref, gate_ref, g2_ref,
                     sh_ref, sc_ref, rw_ref, xo_ref, h2_ref, lg_ref):
    wm = om_ref.shape[1]
    nm = (_rms(om_ref[...]) * gm_ref[...]).astype(BF16)
    ns = (_rms(os_ref[...]) * gs_ref[...]).astype(BF16)
    y = (jnp.dot(nm, wo_ref[:wm, :], preferred_element_type=F32)
         + jnp.dot(ns, wo_ref[wm:, :], preferred_element_type=F32))
    xn = x_ref[...] + gate_ref[...] * y
    xo_ref[...] = xn
    h2 = _rms(xn) * g2_ref[...]
    h2 = h2 * (1.0 + sc_ref[...]) + sh_ref[...]
    h2_ref[...] = _pack_rows(h2)
    lg_ref[...] = lax.dot_general(rw_ref[...], h2.astype(BF16), (((1,), (1,)), ((), ())),
                                  preferred_element_type=F32)


def _out_proj(o_mla, o_sb, gm, gs, w_o, x, mod, g2, rw_t, layer, seq, tm=512):
    t, d = x.shape
    tm = min(tm, seq)
    per_b = seq // tm
    row = lambda i: (i, 0)
    full2 = lambda i: (0, 0)
    modspec = lambda j: pl.BlockSpec((None, 1, d), lambda i: ((i // per_b) * 6 + j, 0, 0))
    return pl.pallas_call(
        _out_proj_kernel,
        grid=(t // tm,),
        in_specs=[pl.BlockSpec((tm, o_mla.shape[1]), row),
                  pl.BlockSpec((tm, o_sb.shape[1]), row),
                  pl.BlockSpec((1, o_mla.shape[1]), full2),
                  pl.BlockSpec((1, o_sb.shape[1]), full2),
                  pl.BlockSpec((None,) + w_o.shape[1:], lambda i: (layer, 0, 0)),
                  pl.BlockSpec((tm, d), row),
                  modspec(2),
                  pl.BlockSpec((1, d), full2),
                  modspec(3),
                  modspec(4),
                  pl.BlockSpec(rw_t.shape, full2)],
        out_specs=[pl.BlockSpec((tm, d), row),
                   pl.BlockSpec((tm, d // 2), row),
                   pl.BlockSpec((N_EXPERTS, tm), lambda i: (0, i))],
        out_shape=[jax.ShapeDtypeStruct((t, d), F32),
                   jax.ShapeDtypeStruct((t, d // 2), jnp.uint32),
                   jax.ShapeDtypeStruct((N_EXPERTS, t), F32)],
        compiler_params=_params("arbitrary"),
        name="out_proj",
    )(o_mla, o_sb, gm, gs, w_o, x, mod, g2, mod, mod, rw_t)


def _route_kernel(lg_ref, rb_ref, tri_ref, e_ref, w_ref, rank_ref, cnt_ref, run_ref):
    i = pl.program_id(0)

    @pl.when(i == 0)
    def _():
        run_ref[...] = jnp.zeros_like(run_ref)

    scores = jax.nn.sigmoid(lg_ref[...])
    sel = scores + rb_ref[...]
    tm = sel.shape[1]
    epg = EXPERTS_PER_GROUP
    srow = [sel[e:e + 1, :] for e in range(N_EXPERTS)]
    prow = [scores[e:e + 1, :] for e in range(N_EXPERTS)]

    gscore = []
    for g in range(N_GROUPS):
        a, b, c, d = srow[g * epg:(g + 1) * epg]
        hi1, lo1 = jnp.maximum(a, b), jnp.minimum(a, b)
        hi2, lo2 = jnp.maximum(c, d), jnp.minimum(c, d)
        gscore.append(jnp.maximum(hi1, hi2)
                      + jnp.maximum(jnp.minimum(hi1, hi2), jnp.maximum(lo1, lo2)))
    best = gscore[0]
    gidx = jnp.zeros((1, tm), jnp.int32)
    for g in range(1, N_GROUPS):
        better = gscore[g] > best
        gidx = jnp.where(better, g, gidx)
        best = jnp.where(better, gscore[g], best)

    def pick(rows, j):
        out = rows[j]
        for g in range(1, N_GROUPS):
            out = jnp.where(gidx == g, rows[g * epg + j], out)
        return out

    v = [pick(srow, j) for j in range(epg)]
    p = [pick(prow, j) for j in range(epg)]
    pos = []
    for j in range(epg):
        r = jnp.zeros((1, tm), jnp.int32)
        for o in range(epg):
            if o == j:
                continue
            ahead = (v[o] > v[j]) | ((v[o] == v[j]) & (o < j))
            r = r + ahead.astype(jnp.int32)
        pos.append(r)
    local, wsel = [], []
    for k in range(TOP_K):
        lk = jnp.zeros((1, tm), jnp.int32)
        wk = jnp.zeros((1, tm), F32)
        for j in range(epg):
            hit = pos[j] == k
            lk = jnp.where(hit, j, lk)
            wk = jnp.where(hit, p[j], wk)
        local.append(lk)
        wsel.append(wk)
    wsum = wsel[0] + wsel[1]
    idx = [gidx * epg + local[k] for k in range(TOP_K)]

    eiota = lax.broadcasted_iota(jnp.int32, (N_EXPERTS, tm), 0)
    hit = [eiota == idx[k] for k in range(TOP_K)]
    onehot = (hit[0] | hit[1]).astype(F32)
    before = jnp.dot(onehot.astype(BF16), tri_ref[...], preferred_element_type=F32) + run_ref[...]
    for k in range(TOP_K):
        e_ref[pl.ds(k, 1), :] = idx[k]
        w_ref[pl.ds(k, 1), :] = wsel[k] / wsum
        rank_ref[pl.ds(k, 1), :] = jnp.sum(jnp.where(hit[k], before, 0.0), axis=0,
                                           keepdims=True).astype(jnp.int32)
    run_ref[...] += jnp.sum(onehot, axis=1, keepdims=True)
    cnt_ref[...] = jnp.broadcast_to(run_ref[...], cnt_ref.shape)


def _route(logits_t, router_b, tm=512):
    t = logits_t.shape[1]
    tm = min(tm, t)
    tri = (lax.broadcasted_iota(jnp.int32, (tm, tm), 0)
           < lax.broadcasted_iota(jnp.int32, (tm, tm), 1)).astype(BF16)
    blk = lambda rows: pl.BlockSpec((rows, tm), lambda i: (0, i))
    return pl.pallas_call(
        _route_kernel,
        grid=(t // tm,),
        in_specs=[blk(N_EXPERTS),
                  pl.BlockSpec((N_EXPERTS, 1), lambda i: (0, 0)),
                  pl.BlockSpec((tm, tm), lambda i: (0, 0))],
        out_specs=[blk(TOP_K), blk(TOP_K), blk(TOP_K),
                   pl.BlockSpec((N_EXPERTS, LANE), lambda i: (0, 0))],
        out_shape=[jax.ShapeDtypeStruct((TOP_K, t), jnp.int32),
                   jax.ShapeDtypeStruct((TOP_K, t), F32),
                   jax.ShapeDtypeStruct((TOP_K, t), jnp.int32),
                   jax.ShapeDtypeStruct((N_EXPERTS, LANE), F32)],
        scratch_shapes=[pltpu.VMEM((N_EXPERTS, 1), F32)],
        compiler_params=_params("arbitrary"),
        name="route",
    )(logits_t, router_b[:, None], tri)


def _dispatch_kernel(dest_ref, pend_ref, h_ref, o_ref, zero_ref, ring, in_sems, row_sems, zsem,
                     *, tm, t):
    base = pl.program_id(0) * tm

    @pl.when(pl.program_id(0) == 0)
    def _():
        zero_ref[...] = jnp.zeros_like(zero_ref)

        def fill(e):
            end = pend_ref[e]
            start = pend_ref[e - 1] if e > 0 else 0
            dst = o_ref.at[pl.ds(pl.multiple_of(jnp.maximum(end - MOE_ROWS, 0), MOE_ROWS),
                                 MOE_ROWS), :]
            return end > start, pltpu.make_async_copy(zero_ref, dst, zsem)

        for e in range(N_EXPERTS):
            nonempty, cp = fill(e)
            pl.when(nonempty)(cp.start)
        for e in range(N_EXPERTS):
            nonempty, cp = fill(e)
            pl.when(nonempty)(cp.wait)

        def tail(r):
            dst = o_ref.at[pl.ds(pl.multiple_of(r * MOE_ROWS, MOE_ROWS), MOE_ROWS), :]
            return pltpu.make_async_copy(zero_ref, dst, zsem)

        first_free = pend_ref[N_EXPERTS - 1] // MOE_ROWS
        n_blocks = o_ref.shape[0] // MOE_ROWS
        lax.fori_loop(first_free, n_blocks, lambda r, c: (tail(r).start(), c)[1], 0)
        lax.fori_loop(first_free, n_blocks, lambda r, c: (tail(r).wait(), c)[1], 0)

    i = pl.program_id(0)
    n = pl.num_programs(0)

    def stage(tile):
        return pltpu.make_async_copy(h_ref.at[pl.ds(tile * tm, tm), :], ring.at[tile % 3],
                                     in_sems.at[tile % 3])

    @pl.when(i == 0)
    def _():
        stage(0).start()

    @pl.when(i + 1 < n)
    def _():
        stage(i + 1).start()

    stage(i).wait()
    src = ring.at[i % 3]

    def issue(r, carry):
        for k in range(TOP_K):
            d = dest_ref[k * t + base + r]
            pltpu.make_async_copy(src.at[pl.ds(r, 1), :], o_ref.at[pl.ds(d, 1), :],
                                  row_sems.at[i % 2]).start()
        return carry

    lax.fori_loop(0, tm, issue, 0, unroll=DMA_ISSUE_UNROLL)

    def wait_rows(step):
        for k in range(TOP_K):
            pltpu.make_async_copy(ring.at[0], o_ref.at[pl.ds(0, tm), :],
                                  row_sems.at[step % 2]).wait()

    pl.when(i > 0)(lambda: wait_rows(i - 1))
    pl.when(i == n - 1)(lambda: wait_rows(i))


def _dispatch(h2, dest_flat, pad_end, rows, tm=512):
    t, d = h2.shape
    tm = min(tm, t)
    return pl.pallas_call(
        functools.partial(_dispatch_kernel, tm=tm, t=t),
        grid_spec=pltpu.PrefetchScalarGridSpec(
            num_scalar_prefetch=2,
            grid=(t // tm,),
            in_specs=[pl.BlockSpec(memory_space=pl.ANY)],
            out_specs=pl.BlockSpec(memory_space=pl.ANY),
            scratch_shapes=[pltpu.VMEM((MOE_ROWS, d), h2.dtype),
                            pltpu.VMEM((3, tm, d), h2.dtype),
                            pltpu.SemaphoreType.DMA((3,)), pltpu.SemaphoreType.DMA((2,)),
                            pltpu.SemaphoreType.DMA(())]),
        out_shape=jax.ShapeDtypeStruct((rows, d), h2.dtype),
        compiler_params=_params("arbitrary"),
        name="moe_dispatch",
    )(dest_flat, pad_end, h2)


def _expert_kernel(be_ref, nu_ref, next_ref, slot_ref, x_ref, wg_hbm, wu_hbm, wd_hbm, y_ref,
                   wgf, wuf, wdf, wgb, wub, wdb, sems, *, layer):
    r = pl.program_id(0)
    live = r < nu_ref[0]
    e = be_ref[r]
    new_expert = (r == 0) | (be_ref[jnp.maximum(r - 1, 0)] != e)
    slot = slot_ref[e]

    def fetch(expert, s):
        pairs = ((wg_hbm, wgf), (wu_hbm, wuf), (wd_hbm, wdf))
        return [pltpu.make_async_copy(w.at[layer, expert], buf.at[s], sems.at[s, j])
                for j, (w, buf) in enumerate(pairs)]

    @pl.when(r == 0)
    def _():
        for cp in fetch(e, slot):
            cp.start()

    @pl.when(live & new_expert)
    def _():
        nxt = next_ref[e]

        @pl.when(nxt != e)
        def _():
            for cp in fetch(nxt, 1 - slot):
                cp.start()

        for cp in fetch(e, slot):
            cp.wait()
        wgb[...] = wgf[slot].astype(BF16)
        wub[...] = wuf[slot].astype(BF16)
        wdb[...] = wdf[slot].astype(BF16)

    @pl.when(live)
    def _():
        lo, hi = _unpack_rows(x_ref[...])
        xb = jnp.concatenate([lo.astype(BF16), hi.astype(BF16)], axis=1)
        g = jnp.dot(xb, wgb[...], preferred_element_type=F32)
        u = jnp.dot(xb, wub[...], preferred_element_type=F32)
        a = (g * jax.nn.sigmoid(g)) * u
        y_ref[...] = _pack_rows(jnp.dot(a.astype(BF16), wdb[...], preferred_element_type=F32))

    @pl.when(jnp.logical_not(live))
    def _():
        y_ref[...] = jnp.zeros_like(y_ref)


def _experts(x_sorted, block_e, n_used, next_expert, expert_slot, w_gate, w_up, w_down, layer):
    rows, d = x_sorted.shape
    de = w_gate.shape[-1]
    dm = w_gate.shape[-2]
    bm = MOE_ROWS
    blk = lambda r, be, nu, nx, sl: (jnp.minimum(r, nu[0] - 1), 0)
    hbm = pl.BlockSpec(memory_space=pl.ANY)
    return pl.pallas_call(
        functools.partial(_expert_kernel, layer=layer),
        grid_spec=pltpu.PrefetchScalarGridSpec(
            num_scalar_prefetch=4,
            grid=(rows // bm,),
            in_specs=[pl.BlockSpec((bm, d), blk), hbm, hbm, hbm],
            out_specs=pl.BlockSpec((bm, d), lambda r, be, nu, nx, sl: (r, 0)),
            scratch_shapes=[pltpu.VMEM((2, dm, de), F32), pltpu.VMEM((2, dm, de), F32),
                            pltpu.VMEM((2, de, dm), F32),
                            pltpu.VMEM((dm, de), BF16), pltpu.VMEM((dm, de), BF16),
                            pltpu.VMEM((de, dm), BF16),
                            pltpu.SemaphoreType.DMA((2, 3))]),
        out_shape=jax.ShapeDtypeStruct((rows, d), jnp.uint32),
        compiler_params=_params("arbitrary"),
        name="moe_experts",
    )(block_e, n_used, next_expert, expert_slot, x_sorted, w_gate, w_up, w_down)


def _combine_kernel(dest_ref, y_ref, x_ref, gate_ref, w_ref, o_ref, buf, sems, *, tm, t):
    i = pl.program_id(0)
    slot = i % 2

    def gather(tile, dst_slot):
        base = tile * tm

        def issue(r, carry):
            for k in range(TOP_K):
                d = dest_ref[k * t + base + r]
                pltpu.make_async_copy(y_ref.at[pl.ds(d, 1), :],
                                      buf.at[dst_slot, k, pl.ds(r, 1), :],
                                      sems.at[dst_slot]).start()
            return carry

        lax.fori_loop(0, tm, issue, 0, unroll=DMA_ISSUE_UNROLL)

    @pl.when(i == 0)
    def _():
        gather(0, 0)

    @pl.when(i + 1 < pl.num_programs(0))
    def _():
        gather(i + 1, 1 - slot)

    for k in range(TOP_K):
        pltpu.make_async_copy(y_ref.at[pl.ds(0, tm), :], buf.at[slot, k], sems.at[slot]).wait()
    w = w_ref[...]
    half = buf.shape[3]
    lo0, hi0 = _unpack_rows(buf[slot, 0])
    lo1, hi1 = _unpack_rows(buf[slot, 1])
    o_ref[:, :half] = x_ref[:, :half] + gate_ref[:, :half] * (w[:, 0:1] * lo0 + w[:, 1:2] * lo1)
    o_ref[:, half:] = x_ref[:, half:] + gate_ref[:, half:] * (w[:, 0:1] * hi0 + w[:, 1:2] * hi1)


def _combine(y_sorted, dest_flat, x, mod, w_tok, seq, tm=512):
    t, d = x.shape
    tm = min(tm, seq)
    per_b = seq // tm
    return pl.pallas_call(
        functools.partial(_combine_kernel, tm=tm, t=t),
        grid_spec=pltpu.PrefetchScalarGridSpec(
            num_scalar_prefetch=1,
            grid=(t // tm,),
            in_specs=[pl.BlockSpec(memory_space=pl.ANY),
                      pl.BlockSpec((tm, d), lambda i, dest: (i, 0)),
                      pl.BlockSpec((None, 1, d), lambda i, dest: ((i // per_b) * 6 + 5, 0, 0)),
                      pl.BlockSpec((tm, TOP_K), lambda i, dest: (i, 0))],
            out_specs=pl.BlockSpec((tm, d), lambda i, dest: (i, 0)),
            scratch_shapes=[pltpu.VMEM((2, TOP_K, tm, y_sorted.shape[1]), y_sorted.dtype),
                            pltpu.SemaphoreType.DMA((2,))]),
        out_shape=jax.ShapeDtypeStruct((t, d), F32),
        compiler_params=_params("arbitrary"),
        name="moe_combine",
    )(dest_flat, y_sorted, x, mod, w_tok)


def _rope_tables(seq):
    half = MLA_ROPE // 2
    inv = 1.0 / (ROPE_BASE ** (jnp.arange(0, MLA_ROPE, 2, dtype=F32) / MLA_ROPE))
    ang = jnp.arange(seq, dtype=F32)[:, None] * inv[None, :]
    cos, sin = jnp.cos(ang), jnp.sin(ang)
    z = jnp.zeros((seq, half), F32)
    zz = jnp.zeros((seq, LANE - MLA_ROPE), F32)
    cos_t = jnp.concatenate([cos, cos, zz], axis=1)
    s1_t = jnp.concatenate([z, sin, zz], axis=1)
    s2_t = jnp.concatenate([-sin, z, zz], axis=1)
    return cos_t, s1_t, s2_t


def _pad_last(a, width):
    return jnp.pad(a, [(0, 0)] * (a.ndim - 1) + [(0, width - a.shape[-1])])


def kernel(x, c, norm1_g, norm2_g, w_ada, b_ada, w_in, q_lora_g, kv_lora_g, w_uq, w_ukv,
           q_head_g, k_head_g, mla_out_g, sb_out_g, w_o, router_w, router_b,
           w_gate, w_up, w_down):
    bsz, seq, d = x.shape
    t = bsz * seq
    depth = w_ada.shape[0]

    sb_col0 = COL_KR + MLA_ROPE
    w_in_mla = _pad_last(w_in[..., :sb_col0], MLA_IN_COLS).astype(BF16)
    w_in_sb = w_in[..., sb_col0:].astype(BF16)
    wq = w_uq.reshape(depth, Q_LORA, MLA_HEADS, MLA_QK)
    half = MLA_ROPE // 2
    swap = lambda a: jnp.concatenate([a[..., half:], a[..., :half]], axis=-1)
    wq_rope = wq[..., MLA_NOPE:]
    wuq_p = jnp.concatenate(
        [wq[..., :MLA_NOPE].reshape(depth, Q_LORA, -1),
         _pad_last(wq_rope, LANE).reshape(depth, Q_LORA, -1),
         _pad_last(swap(wq_rope), LANE).reshape(depth, Q_LORA, -1)], axis=-1).astype(BF16)
    wkv = w_ukv.reshape(depth, KV_LORA, MLA_HEADS, MLA_NOPE + MLA_V)
    wukv_p = jnp.concatenate(
        [wkv[..., :MLA_NOPE].reshape(depth, KV_LORA, -1),
         wkv[..., MLA_NOPE:].reshape(depth, KV_LORA, -1)], axis=-1).astype(BF16)
    w_o_b = w_o.astype(BF16)
    q_scale = MLA_QK ** -0.5 * LOG2_E
    gqn = (q_head_g[:, None, :MLA_NOPE] * q_scale)
    gq_rope = q_head_g[:, None, MLA_NOPE:] * q_scale
    gqr = jnp.concatenate([_pad_last(gq_rope, LANE), _pad_last(swap(gq_rope), LANE)], axis=-1)
    gkn = k_head_g[:, None, :MLA_NOPE]
    gkr = _pad_last(k_head_g[:, None, MLA_NOPE:], LANE)
    rope_tabs = _rope_tables(seq)
    rw_t = router_w.T.astype(BF16)

    mod_all = _ada_modulation(c[:, :, None], w_ada, b_ada[:, None, :], layers=depth)
    n_blocks = -(-(t * TOP_K) // MOE_ROWS) + N_EXPERTS
    rows = n_blocks * MOE_ROWS

    xt = x.reshape(t, d)
    for l in range(depth):
        mod = mod_all[l].reshape(bsz * 6, 1, d)
        q2d, k2d, v2d, proj_sb = _in_proj(
            xt, norm1_g[l][None, :], mod, w_in_mla, w_in_sb, rope_tabs, q_lora_g[:, None, :],
            kv_lora_g[:, None, :], wuq_p, wukv_p, gqn, gqr, gkn, gkr, l, seq)
        o_mla = _mla_attention(q2d, k2d, v2d, bsz, seq)
        o_sb = _sb_attention(proj_sb, bsz, seq)
        xt, h2, logits_t = _out_proj(o_mla, o_sb, mla_out_g[l][None, :], sb_out_g[l][None, :],
                                     w_o_b, xt, mod, norm2_g[l][None, :], rw_t, l, seq)
        e_idx, w_tok, rank, cnt = _route(logits_t, router_b)
        counts = cnt[:, 0].astype(jnp.int32)
        padded = (counts + MOE_ROWS - 1) // MOE_ROWS * MOE_ROWS
        pad_end = jnp.cumsum(padded)
        pad_start = pad_end - padded
        eids = jnp.arange(N_EXPERTS, dtype=jnp.int32)
        seg_start = jnp.sum(jnp.where(e_idx[None] == eids[:, None, None],
                                      pad_start[:, None, None], 0), axis=0)
        dest = (seg_start + rank).reshape(TOP_K * t).astype(jnp.int32)
        n_used = (pad_end[-1:] // MOE_ROWS).astype(jnp.int32)
        first_row = jnp.minimum(jnp.arange(n_blocks, dtype=jnp.int32), n_used - 1) * MOE_ROWS
        block_e = jnp.sum(pad_end[None, :] <= first_row[:, None], axis=1).astype(jnp.int32)
        nonempty = counts > 0
        later = (eids[None, :] > eids[:, None]) & nonempty[None, :]
        next_expert = jnp.min(jnp.where(later, eids[None, :], N_EXPERTS), axis=1)
        next_expert = jnp.where(next_expert == N_EXPERTS, eids, next_expert).astype(jnp.int32)
        expert_slot = ((jnp.cumsum(nonempty) - nonempty) % 2).astype(jnp.int32)
        x_sorted = _dispatch(h2, dest, pad_end.astype(jnp.int32), rows)
        y_sorted = _experts(x_sorted, block_e, n_used, next_expert, expert_slot,
                            w_gate, w_up, w_down, l)
        xt = _combine(y_sorted, dest, xt, mod, w_tok.T, seq)
    return xt.reshape(bsz, seq, d)
```

```python
import functools

import jax
import jax.numpy as jnp
from jax import lax
from jax.experimental import pallas as pl
from jax.experimental.pallas import tpu as pltpu

F32 = jnp.float32
BF16 = jnp.bfloat16

EPS = 1e-6
ROPE_BASE = 10000.0
CHUNK = 64

MLA_HEADS = 8
MLA_NOPE = 128
MLA_ROPE = 64
MLA_QK = MLA_NOPE + MLA_ROPE
MLA_V = 128
Q_LORA = 512
KV_LORA = 256
SB_HEADS = 8
SB_DIM = 128
SB_WIDTH = SB_HEADS * SB_DIM
LANE = 128
QK_PAD = 2 * LANE
COL_CQ = 0
COL_CKV = Q_LORA
COL_KR = Q_LORA + KV_LORA
MLA_IN_COLS = COL_KR + LANE
COL_SQ = 0
COL_SK = SB_WIDTH
COL_SV = 2 * SB_WIDTH
SB_IN_COLS = 3 * SB_WIDTH

N_EXPERTS = 16
N_GROUPS = 4
EXPERTS_PER_GROUP = N_EXPERTS // N_GROUPS
TOP_K = 2
D_EXPERT = 512
MOE_ROWS = 256
DMA_ISSUE_UNROLL = 16
LOG2_E = 1.4426950408889634
SB_UNDERFLOW_LOG2 = 150.0
SB_FINISHED_LOG2 = 1e30

VMEM_LIMIT = 56 * 1024 * 1024


def _params(*dims):
    return pltpu.CompilerParams(dimension_semantics=dims, vmem_limit_bytes=VMEM_LIMIT)


def _rms(x):
    return x * lax.rsqrt(jnp.mean(x * x, axis=-1, keepdims=True) + EPS)


def _pack_rows(x):
    n = x.shape[1] // 2
    lo = lax.bitcast_convert_type(x[:, :n].astype(BF16).astype(F32), jnp.uint32)
    hi = lax.bitcast_convert_type(x[:, n:].astype(BF16).astype(F32), jnp.uint32)
    return (lo >> 16) | (hi & jnp.uint32(0xFFFF0000))


def _unpack_rows(w):
    lo = lax.bitcast_convert_type(w << 16, F32)
    hi = lax.bitcast_convert_type(w & jnp.uint32(0xFFFF0000), F32)
    return lo, hi


def _ada_kernel(c_ref, w_ref, b_ref, o_ref):
    w = w_ref[...]
    for b in range(c_ref.shape[0]):
        cb = c_ref[b]
        cb = cb * jax.nn.sigmoid(cb)
        o_ref[pl.ds(b, 1), :] = jnp.sum(cb * w, axis=0, keepdims=True) + b_ref[...]


def _ada_modulation(c_col, w_ada, b_ada3, layers, tn=1024):
    _, d, n = w_ada.shape
    bsz = c_col.shape[0]
    return pl.pallas_call(
        _ada_kernel,
        grid=(layers, n // tn),
        in_specs=[pl.BlockSpec(c_col.shape, lambda l, j: (0, 0, 0)),
                  pl.BlockSpec((None, d, tn), lambda l, j: (l, 0, j)),
                  pl.BlockSpec((None, 1, tn), lambda l, j: (l, 0, j))],
        out_specs=pl.BlockSpec((None, bsz, tn), lambda l, j: (l, 0, j)),
        out_shape=jax.ShapeDtypeStruct((layers, bsz, n), F32),
        compiler_params=_params("arbitrary", "arbitrary"),
        name="ada_modulation",
    )(c_col, w_ada, b_ada3)


def _in_proj_kernel(x_ref, g_ref, sh_ref, sc_ref, wa_ref, wb_ref, cos_ref, s1_ref, s2_ref,
                    gql_ref, gkvl_ref, wuq_ref, wukv_ref, gqn_ref, gqr_ref, gkn_ref, gkr_ref,
                    q_ref, k_ref, v_ref, ob_ref):
    hn = _rms(x_ref[...]) * g_ref[...]
    hn = (hn * (1.0 + sc_ref[...]) + sh_ref[...]).astype(BF16)
    sb = jnp.dot(hn, wb_ref[...], preferred_element_type=F32)
    ob_ref[:, :SB_WIDTH] = (sb[:, :SB_WIDTH] * (SB_DIM ** -0.5 * LOG2_E)).astype(ob_ref.dtype)
    ob_ref[:, SB_WIDTH:] = sb[:, SB_WIDTH:].astype(ob_ref.dtype)

    pa = jnp.dot(hn, wa_ref[...], preferred_element_type=F32)
    nope_w = MLA_HEADS * MLA_NOPE
    cqn = _rms(pa[:, COL_CQ:COL_CKV]) * gql_ref[...]
    q = jnp.dot(cqn.astype(BF16), wuq_ref[...], preferred_element_type=F32)
    ckvn = _rms(pa[:, COL_CKV:COL_KR]) * gkvl_ref[...]
    kv = jnp.dot(ckvn.astype(BF16), wukv_ref[...], preferred_element_type=F32)
    cosv, s1, s2 = cos_ref[...], s1_ref[...], s2_ref[...]

    def rope(x):
        return (x * cosv + pltpu.roll(x, MLA_ROPE // 2, 1) * s1
                + pltpu.roll(x, LANE - MLA_ROPE // 2, 1) * s2)

    kr = pa[:, COL_KR:]
    kr_ss = jnp.sum(kr * kr, axis=-1, keepdims=True)
    kr_roped = rope(kr * gkr_ref[...])
    gqn, gkn = gqn_ref[...], gkn_ref[...]
    gqr, gqs = gqr_ref[:, :LANE], gqr_ref[:, LANE:]
    s_comb = s1 + s2
    for h in range(MLA_HEADS):
        qn = q[:, h * LANE:(h + 1) * LANE]
        qr = q[:, nope_w + h * LANE:nope_w + (h + 1) * LANE]
        ss = jnp.sum(qn * qn + qr * qr, axis=-1, keepdims=True)
        r = lax.rsqrt(ss * (1.0 / MLA_QK) + EPS)
        q_ref[:, h * QK_PAD:h * QK_PAD + LANE] = (qn * r * gqn).astype(BF16)
        qs = q[:, 2 * nope_w + h * LANE:2 * nope_w + (h + 1) * LANE]
        q_rope = (qr * gqr) * cosv + (qs * gqs) * s_comb
        q_ref[:, h * QK_PAD + LANE:(h + 1) * QK_PAD] = (q_rope * r).astype(BF16)
        kn = kv[:, h * LANE:(h + 1) * LANE]
        rk = lax.rsqrt((jnp.sum(kn * kn, axis=-1, keepdims=True) + kr_ss) * (1.0 / MLA_QK) + EPS)
        k_ref[:, h * QK_PAD:h * QK_PAD + LANE] = (kn * rk * gkn).astype(BF16)
        k_ref[:, h * QK_PAD + LANE:(h + 1) * QK_PAD] = (kr_roped * rk).astype(BF16)
        v_ref[:, 2 * h * MLA_V:(2 * h + 1) * MLA_V] = (
            kv[:, nope_w + h * MLA_V:nope_w + (h + 1) * MLA_V].astype(BF16))
        v_ref[:, (2 * h + 1) * MLA_V:(2 * h + 2) * MLA_V] = jnp.ones((kn.shape[0], MLA_V), BF16)


def _in_proj(x, g, mod, w_mla, w_sb, rope_tabs, gql, gkvl, wuq, wukv, gqn, gqr, gkn, gkr,
             layer, seq, tm=256):
    t, d = x.shape
    na, nb = w_mla.shape[-1], w_sb.shape[-1]
    per_b = seq // tm
    cos_t, s1_t, s2_t = rope_tabs
    row = lambda i: (i, 0)
    lay = lambda i: (layer, 0, 0)
    tab = lambda i: (i % per_b, 0)
    return pl.pallas_call(
        _in_proj_kernel,
        grid=(t // tm,),
        in_specs=[pl.BlockSpec((tm, d), row),
                  pl.BlockSpec((1, d), lambda i: (0, 0)),
                  pl.BlockSpec((None, 1, d), lambda i: ((i // per_b) * 6 + 0, 0, 0)),
                  pl.BlockSpec((None, 1, d), lambda i: ((i // per_b) * 6 + 1, 0, 0)),
                  pl.BlockSpec((None, d, na), lay),
                  pl.BlockSpec((None, d, nb), lay),
                  pl.BlockSpec((tm, LANE), tab),
                  pl.BlockSpec((tm, LANE), tab),
                  pl.BlockSpec((tm, LANE), tab),
                  pl.BlockSpec((None, 1, Q_LORA), lay),
                  pl.BlockSpec((None, 1, KV_LORA), lay),
                  pl.BlockSpec((None,) + wuq.shape[1:], lay),
                  pl.BlockSpec((None,) + wukv.shape[1:], lay),
                  pl.BlockSpec((None, 1, LANE), lay),
                  pl.BlockSpec((None, 1, 2 * LANE), lay),
                  pl.BlockSpec((None, 1, LANE), lay),
                  pl.BlockSpec((None, 1, LANE), lay)],
        out_specs=[pl.BlockSpec((tm, MLA_HEADS * QK_PAD), row),
                   pl.BlockSpec((tm, MLA_HEADS * QK_PAD), row),
                   pl.BlockSpec((tm, MLA_HEADS * 2 * MLA_V), row),
                   pl.BlockSpec((tm, nb), row)],
        out_shape=[jax.ShapeDtypeStruct((t, MLA_HEADS * QK_PAD), BF16),
                   jax.ShapeDtypeStruct((t, MLA_HEADS * QK_PAD), BF16),
                   jax.ShapeDtypeStruct((t, MLA_HEADS * 2 * MLA_V), BF16),
                   jax.ShapeDtypeStruct((t, nb), BF16)],
        compiler_params=_params("arbitrary"),
        name="in_proj",
    )(x, g, mod, mod, w_mla, w_sb, cos_t, s1_t, s2_t, gql, gkvl, wuq, wukv, gqn, gqr, gkn, gkr)


def _mla_attn_kernel(q_ref, k_ref, v_ref, o_ref, m_ref, l_ref, acc_ref, sa_ref, sb_ref, *,
                     tq, heads):
    i = pl.program_id(2)
    vw = 2 * MLA_V

    def scores(g, kb):
        q = q_ref[:, g * QK_PAD:(g + 1) * QK_PAD]
        k = k_ref[pl.ds(pl.multiple_of(kb * tq, tq), tq), g * QK_PAD:(g + 1) * QK_PAD]
        return lax.dot_general(q, k, (((1,), (1,)), ((), ())), preferred_element_type=F32)

    def probs(s, m):
        return jnp.concatenate(
            [jnp.exp2(s[:, c * LANE:(c + 1) * LANE] - m) for c in range(tq // LANE)],
            axis=1).astype(BF16)

    def pv(g, p, kb):
        v = v_ref[pl.ds(pl.multiple_of(kb * tq, tq), tq), g * vw:(g + 1) * vw]
        return jnp.dot(p, v, preferred_element_type=F32)

    def update(g, s, kb):
        m_old = m_ref[g]
        m_new = jnp.maximum(m_old, jnp.max(s, axis=-1, keepdims=True))
        alpha = jnp.exp2(m_old - m_new)
        r = pv(g, probs(s, m_new), kb)
        acc_ref[g] = alpha * acc_ref[g] + r[:, :MLA_V]
        l_ref[g] = alpha * l_ref[g] + r[:, MLA_V:]
        m_ref[g] = m_new

    for g in range(heads):
        m_ref[g] = jnp.full((tq, LANE), -jnp.inf, F32)
        l_ref[g] = jnp.zeros((tq, MLA_V), F32)
        acc_ref[g] = jnp.zeros((tq, MLA_V), F32)
        sa_ref[g] = scores(g, 0)

    def pair(t, carry):
        j = 2 * t
        for g in range(heads):
            sb_ref[g] = scores(g, j + 1)
            update(g, sa_ref[g], j)
        for g in range(heads):
            sa_ref[g] = scores(g, j + 2)
            update(g, sb_ref[g], j + 1)
        return carry

    lax.fori_loop(0, i // 2, pair, 0)

    q_chunk = lax.broadcasted_iota(jnp.int32, (tq, 1), 0) // CHUNK
    k_chunk = lax.broadcasted_iota(jnp.int32, (1, tq), 1) // CHUNK
    visible = k_chunk <= q_chunk

    @pl.when(i % 2 == 1)
    def _():
        for g in range(heads):
            sb_ref[g] = scores(g, i)
            update(g, sa_ref[g], i - 1)
        for g in range(heads):
            update(g, jnp.where(visible, sb_ref[g], -jnp.inf), i)

    @pl.when(i % 2 == 0)
    def _():
        for g in range(heads):
            update(g, jnp.where(visible, sa_ref[g], -jnp.inf), i)

    for g in range(heads):
        o_ref[:, g * MLA_V:(g + 1) * MLA_V] = acc_ref[g] / l_ref[g]


def _mla_attention(q2d, k2d, v2d, bsz, seq, tq=512, heads=4):
    t = q2d.shape[0]
    tq = min(tq, seq)
    nq = seq // tq
    vw = 2 * MLA_V
    return pl.pallas_call(
        functools.partial(_mla_attn_kernel, tq=tq, heads=heads),
        grid=(bsz, MLA_HEADS // heads, nq),
        in_specs=[pl.BlockSpec((tq, heads * QK_PAD), lambda b, h, i: (b * nq + i, h)),
                  pl.BlockSpec((seq, heads * QK_PAD), lambda b, h, i: (b, h)),
                  pl.BlockSpec((seq, heads * vw), lambda b, h, i: (b, h))],
        out_specs=pl.BlockSpec((tq, heads * MLA_V), lambda b, h, i: (b * nq + i, h)),
        out_shape=jax.ShapeDtypeStruct((t, MLA_HEADS * MLA_V), F32),
        scratch_shapes=[pltpu.VMEM((heads, tq, LANE), F32), pltpu.VMEM((heads, tq, MLA_V), F32),
                        pltpu.VMEM((heads, tq, MLA_V), F32),
                        pltpu.VMEM((heads, tq, tq), F32), pltpu.VMEM((heads, tq, tq), F32)],
        compiler_params=_params("arbitrary", "arbitrary", "arbitrary"),
        name="mla_attention",
    )(q2d, k2d, v2d)


def _sb_attn_kernel(q_ref, k_ref, v_ref, tri_ref, o_ref, run_ref, acc_ref, *, tq, tk, heads):
    i = pl.program_id(2)
    tri = tri_ref[...]
    sub = tq // tk
    streams = [(g, r) for g in range(heads) for r in range(sub)]

    def block(s, kb, mask):
        g, r = streams[s]
        cols = slice(g * SB_DIM, (g + 1) * SB_DIM)
        keys = pl.ds(pl.multiple_of(kb * tk, tk), tk)
        q = q_ref[r * tk:(r + 1) * tk, cols]
        z = lax.dot_general(q, k_ref[keys, cols], (((1,), (1,)), ((), ())),
                            preferred_element_type=F32)
        neg_abs = lax.bitcast_convert_type(
            lax.bitcast_convert_type(z, jnp.uint32) | jnp.uint32(0x80000000), F32)
        cost = jnp.maximum(z, 0.0) + jnp.log2(1.0 + jnp.exp2(neg_abs))
        if mask is not None:
            cost = jnp.where(mask, cost, 0.0)
        after = jnp.dot(cost.astype(BF16), tri, preferred_element_type=F32) + run_ref[s]
        a = jnp.exp2(z - cost - after)
        if mask is not None:
            a = jnp.where(mask, a, 0.0)
        acc_ref[s] += jnp.dot(a.astype(BF16), v_ref[keys, cols], preferred_element_type=F32)
        run_ref[s] += jnp.sum(cost, axis=-1, keepdims=True)

    def more():
        return jnp.min(run_ref[...]) < SB_UNDERFLOW_LOG2

    run_ref[...] = jnp.zeros_like(run_ref)
    acc_ref[...] = jnp.zeros_like(acc_ref)
    strictly_before = (lax.broadcasted_iota(jnp.int32, (tk, tk), 1)
                       < lax.broadcasted_iota(jnp.int32, (tk, tk), 0))
    for s, (g, r) in enumerate(streams):
        block(s, i * sub + r, strictly_before)

    def cond(c):
        return (c[0] < i * sub + sub - 1) & c[1]

    def body(c):
        for s, (g, r) in enumerate(streams):
            kb = i * sub + r - 1 - c[0]
            run_ref[s] = jnp.where(kb >= 0, run_ref[s], SB_FINISHED_LOG2)
            block(s, jnp.maximum(kb, 0), None)
        return c[0] + 1, more()

    lax.while_loop(cond, body, (jnp.int32(0), more()))
    for s, (g, r) in enumerate(streams):
        o_ref[r * tk:(r + 1) * tk, g * SB_DIM:(g + 1) * SB_DIM] = acc_ref[s]


def _sb_attention(proj, bsz, seq, tq=512, tk=256, heads=8):
    t = proj.shape[0]
    tq = min(tq, seq)
    nq = seq // tq
    tri = (lax.broadcasted_iota(jnp.int32, (tk, tk), 0)
           > lax.broadcasted_iota(jnp.int32, (tk, tk), 1)).astype(BF16)
    w = heads * SB_DIM
    cq, ck, cv = COL_SQ // w, COL_SK // w, COL_SV // w
    n_streams = heads * (tq // tk)
    return pl.pallas_call(
        functools.partial(_sb_attn_kernel, tq=tq, tk=tk, heads=heads),
        grid=(bsz, SB_HEADS // heads, nq),
        in_specs=[pl.BlockSpec((tq, w), lambda b, h, i: (b * nq + i, cq + h)),
                  pl.BlockSpec((seq, w), lambda b, h, i: (b, ck + h)),
                  pl.BlockSpec((seq, w), lambda b, h, i: (b, cv + h)),
                  pl.BlockSpec((tk, tk), lambda b, h, i: (0, 0))],
        out_specs=pl.BlockSpec((tq, w), lambda b, h, i: (b * nq + i, h)),
        out_shape=jax.ShapeDtypeStruct((t, SB_WIDTH), F32),
        scratch_shapes=[pltpu.VMEM((n_streams, tk, 1), F32),
                        pltpu.VMEM((n_streams, tk, SB_DIM), F32)],
        compiler_params=_params("arbitrary", "arbitrary", "arbitrary"),
        name="sb_attention",
    )(proj, proj, proj, tri)


def _out_proj_kernel(om_ref, os_ref, gm_ref, gs_ref, wo_ref, x_ref, gate_ref, g2_ref,
                     sh_ref, sc_ref, rw_ref, xo_ref, h2_ref, lg_ref):
    wm = om_ref.shape[1]
    nm = (_rms(om_ref[...]) * gm_ref[...]).astype(BF16)
    ns = (_rms(os_ref[...]) * gs_ref[...]).astype(BF16)
    y = (jnp.dot(nm, wo_ref[:wm, :], preferred_element_type=F32)
         + jnp.dot(ns, wo_ref[wm:, :], preferred_element_type=F32))
    xn = x_ref[...] + gate_ref[...] * y
    xo_ref[...] = xn
    h2 = _rms(xn) * g2_ref[...]
    h2 = h2 * (1.0 + sc_ref[...]) + sh_ref[...]
    h2_ref[...] = _pack_rows(h2)
    lg_ref[...] = lax.dot_general(rw_ref[...], h2.astype(BF16), (((1,), (1,)), ((), ())),
                                  preferred_element_type=F32)


def _out_proj(o_mla, o_sb, gm, gs, w_o, x, mod, g2, rw_t, layer, seq, tm=512):
    t, d = x.shape
    tm = min(tm, seq)
    per_b = seq // tm
    row = lambda i: (i, 0)
    full2 = lambda i: (0, 0)
    modspec = lambda j: pl.BlockSpec((None, 1, d), lambda i: ((i // per_b) * 6 + j, 0, 0))
    return pl.pallas_call(
        _out_proj_kernel,
        grid=(t // tm,),
        in_specs=[pl.BlockSpec((tm, o_mla.shape[1]), row),
                  pl.BlockSpec((tm, o_sb.shape[1]), row),
                  pl.BlockSpec((1, o_mla.shape[1]), full2),
                  pl.BlockSpec((1, o_sb.shape[1]), full2),
                  pl.BlockSpec((None,) + w_o.shape[1:], lambda i: (layer, 0, 0)),
                  pl.BlockSpec((tm, d), row),
                  modspec(2),
                  pl.BlockSpec((1, d), full2),
                  modspec(3),
                  modspec(4),
                  pl.BlockSpec(rw_t.shape, full2)],
        out_specs=[pl.BlockSpec((tm, d), row),
                   pl.BlockSpec((tm, d // 2), row),
                   pl.BlockSpec((N_EXPERTS, tm), lambda i: (0, i))],
        out_shape=[jax.ShapeDtypeStruct((t, d), F32),
                   jax.ShapeDtypeStruct((t, d // 2), jnp.uint32),
                   jax.ShapeDtypeStruct((N_EXPERTS, t), F32)],
        compiler_params=_params("arbitrary"),
        name="out_proj",
    )(o_mla, o_sb, gm, gs, w_o, x, mod, g2, mod, mod, rw_t)


def _route_kernel(lg_ref, rb_ref, tri_ref, e_ref, w_ref, rank_ref, cnt_ref, run_ref):
    i = pl.program_id(0)

    @pl.when(i == 0)
    def _():
        run_ref[...] = jnp.zeros_like(run_ref)

    scores = jax.nn.sigmoid(lg_ref[...])
    sel = scores + rb_ref[...]
    tm = sel.shape[1]
    epg = EXPERTS_PER_GROUP
    srow = [sel[e:e + 1, :] for e in range(N_EXPERTS)]
    prow = [scores[e:e + 1, :] for e in range(N_EXPERTS)]

    gscore = []
    for g in range(N_GROUPS):
        a, b, c, d = srow[g * epg:(g + 1) * epg]
        hi1, lo1 = jnp.maximum(a, b), jnp.minimum(a, b)
        hi2, lo2 = jnp.maximum(c, d), jnp.minimum(c, d)
        gscore.append(jnp.maximum(hi1, hi2)
                      + jnp.maximum(jnp.minimum(hi1, hi2), jnp.maximum(lo1, lo2)))
    best = gscore[0]
    gidx = jnp.zeros((1, tm), jnp.int32)
    for g in range(1, N_GROUPS):
        better = gscore[g] > best
        gidx = jnp.where(better, g, gidx)
        best = jnp.where(better, gscore[g], best)

    def pick(rows, j):
        out = rows[j]
        for g in range(1, N_GROUPS):
            out = jnp.where(gidx == g, rows[g * epg + j], out)
        return out

    v = [pick(srow, j) for j in range(epg)]
    p = [pick(prow, j) for j in range(epg)]
    pos = []
    for j in range(epg):
        r = jnp.zeros((1, tm), jnp.int32)
        for o in range(epg):
            if o == j:
                continue
            ahead = (v[o] > v[j]) | ((v[o] == v[j]) & (o < j))
            r = r + ahead.astype(jnp.int32)
        pos.append(r)
    local, wsel = [], []
    for k in range(TOP_K):
        lk = jnp.zeros((1, tm), jnp.int32)
        wk = jnp.zeros((1, tm), F32)
        for j in range(epg):
            hit = pos[j] == k
            lk = jnp.where(hit, j, lk)
            wk = jnp.where(hit, p[j], wk)
        local.append(lk)
        wsel.append(wk)
    wsum = wsel[0] + wsel[1]
    idx = [gidx * epg + local[k] for k in range(TOP_K)]

    eiota = lax.broadcasted_iota(jnp.int32, (N_EXPERTS, tm), 0)
    hit = [eiota == idx[k] for k in range(TOP_K)]
    onehot = (hit[0] | hit[1]).astype(F32)
    before = jnp.dot(onehot.astype(BF16), tri_ref[...], preferred_element_type=F32) + run_ref[...]
    for k in range(TOP_K):
        e_ref[pl.ds(k, 1), :] = idx[k]
        w_ref[pl.ds(k, 1), :] = wsel[k] / wsum
        rank_ref[pl.ds(k, 1), :] = jnp.sum(jnp.where(hit[k], before, 0.0), axis=0,
                                           keepdims=True).astype(jnp.int32)
    run_ref[...] += jnp.sum(onehot, axis=1, keepdims=True)
    cnt_ref[...] = jnp.broadcast_to(run_ref[...], cnt_ref.shape)


def _route(logits_t, router_b, tm=512):
    t = logits_t.shape[1]
    tm = min(tm, t)
    tri = (lax.broadcasted_iota(jnp.int32, (tm, tm), 0)
           < lax.broadcasted_iota(jnp.int32, (tm, tm), 1)).astype(BF16)
    blk = lambda rows: pl.BlockSpec((rows, tm), lambda i: (0, i))
    return pl.pallas_call(
        _route_kernel,
        grid=(t // tm,),
        in_specs=[blk(N_EXPERTS),
                  pl.BlockSpec((N_EXPERTS, 1), lambda i: (0, 0)),
                  pl.BlockSpec((tm, tm), lambda i: (0, 0))],
        out_specs=[blk(TOP_K), blk(TOP_K), blk(TOP_K),
                   pl.BlockSpec((N_EXPERTS, LANE), lambda i: (0, 0))],
        out_shape=[jax.ShapeDtypeStruct((TOP_K, t), jnp.int32),
                   jax.ShapeDtypeStruct((TOP_K, t), F32),
                   jax.ShapeDtypeStruct((TOP_K, t), jnp.int32),
                   jax.ShapeDtypeStruct((N_EXPERTS, LANE), F32)],
        scratch_shapes=[pltpu.VMEM((N_EXPERTS, 1), F32)],
        compiler_params=_params("arbitrary"),
        name="route",
    )(logits_t, router_b[:, None], tri)


def _dispatch_kernel(dest_ref, pend_ref, h_ref, o_ref, zero_ref, ring, in_sems, row_sems, zsem,
                     *, tm, t):
    base = pl.program_id(0) * tm

    @pl.when(pl.program_id(0) == 0)
    def _():
        zero_ref[...] = jnp.zeros_like(zero_ref)

        def fill(e):
            end = pend_ref[e]
            start = pend_ref[e - 1] if e > 0 else 0
            dst = o_ref.at[pl.ds(pl.multiple_of(jnp.maximum(end - MOE_ROWS, 0), MOE_ROWS),
                                 MOE_ROWS), :]
            return end > start, pltpu.make_async_copy(zero_ref, dst, zsem)

        for e in range(N_EXPERTS):
            nonempty, cp = fill(e)
            pl.when(nonempty)(cp.start)
        for e in range(N_EXPERTS):
            nonempty, cp = fill(e)
            pl.when(nonempty)(cp.wait)

        def tail(r):
            dst = o_ref.at[pl.ds(pl.multiple_of(r * MOE_ROWS, MOE_ROWS), MOE_ROWS), :]
            return pltpu.make_async_copy(zero_ref, dst, zsem)

        first_free = pend_ref[N_EXPERTS - 1] // MOE_ROWS
        n_blocks = o_ref.shape[0] // MOE_ROWS
        lax.fori_loop(first_free, n_blocks, lambda r, c: (tail(r).start(), c)[1], 0)
        lax.fori_loop(first_free, n_blocks, lambda r, c: (tail(r).wait(), c)[1], 0)

    i = pl.program_id(0)
    n = pl.num_programs(0)

    def stage(tile):
        return pltpu.make_async_copy(h_ref.at[pl.ds(tile * tm, tm), :], ring.at[tile % 3],
                                     in_sems.at[tile % 3])

    @pl.when(i == 0)
    def _():
        stage(0).start()

    @pl.when(i + 1 < n)
    def _():
        stage(i + 1).start()

    stage(i).wait()
    src = ring.at[i % 3]

    def issue(r, carry):
        for k in range(TOP_K):
            d = dest_ref[k * t + base + r]
            pltpu.make_async_copy(src.at[pl.ds(r, 1), :], o_ref.at[pl.ds(d, 1), :],
                                  row_sems.at[i % 2]).start()
        return carry

    lax.fori_loop(0, tm, issue, 0, unroll=DMA_ISSUE_UNROLL)

    def wait_rows(step):
        for k in range(TOP_K):
            pltpu.make_async_copy(ring.at[0], o_ref.at[pl.ds(0, tm), :],
                                  row_sems.at[step % 2]).wait()

    pl.when(i > 0)(lambda: wait_rows(i - 1))
    pl.when(i == n - 1)(lambda: wait_rows(i))


def _dispatch(h2, dest_flat, pad_end, rows, tm=512):
    t, d = h2.shape
    tm = min(tm, t)
    return pl.pallas_call(
        functools.partial(_dispatch_kernel, tm=tm, t=t),
        grid_spec=pltpu.PrefetchScalarGridSpec(
            num_scalar_prefetch=2,
            grid=(t // tm,),
            in_specs=[pl.BlockSpec(memory_space=pl.ANY)],
            out_specs=pl.BlockSpec(memory_space=pl.ANY),
            scratch_shapes=[pltpu.VMEM((MOE_ROWS, d), h2.dtype),
                            pltpu.VMEM((3, tm, d), h2.dtype),
                            pltpu.SemaphoreType.DMA((3,)), pltpu.SemaphoreType.DMA((2,)),
                            pltpu.SemaphoreType.DMA(())]),
        out_shape=jax.ShapeDtypeStruct((rows, d), h2.dtype),
        compiler_params=_params("arbitrary"),
        name="moe_dispatch",
    )(dest_flat, pad_end, h2)


def _expert_kernel(be_ref, nu_ref, next_ref, slot_ref, x_ref, wg_hbm, wu_hbm, wd_hbm, y_ref,
                   wgf, wuf, wdf, wgb, wub, wdb, sems, *, layer):
    r = pl.program_id(0)
    live = r < nu_ref[0]
    e = be_ref[r]
    new_expert = (r == 0) | (be_ref[jnp.maximum(r - 1, 0)] != e)
    slot = slot_ref[e]

    def fetch(expert, s):
        pairs = ((wg_hbm, wgf), (wu_hbm, wuf), (wd_hbm, wdf))
        return [pltpu.make_async_copy(w.at[layer, expert], buf.at[s], sems.at[s, j])
                for j, (w, buf) in enumerate(pairs)]

    @pl.when(r == 0)
    def _():
        for cp in fetch(e, slot):
            cp.start()

    @pl.when(live & new_expert)
    def _():
        nxt = next_ref[e]

        @pl.when(nxt != e)
        def _():
            for cp in fetch(nxt, 1 - slot):
                cp.start()

        for cp in fetch(e, slot):
            cp.wait()
        wgb[...] = wgf[slot].astype(BF16)
        wub[...] = wuf[slot].astype(BF16)
        wdb[...] = wdf[slot].astype(BF16)

    @pl.when(live)
    def _():
        lo, hi = _unpack_rows(x_ref[...])
        xb = jnp.concatenate([lo.astype(BF16), hi.astype(BF16)], axis=1)
        g = jnp.dot(xb, wgb[...], preferred_element_type=F32)
        u = jnp.dot(xb, wub[...], preferred_element_type=F32)
        a = (g * jax.nn.sigmoid(g)) * u
        y_ref[...] = _pack_rows(jnp.dot(a.astype(BF16), wdb[...], preferred_element_type=F32))

    @pl.when(jnp.logical_not(live))
    def _():
        y_ref[...] = jnp.zeros_like(y_ref)


def _experts(x_sorted, block_e, n_used, next_expert, expert_slot, w_gate, w_up, w_down, layer):
    rows, d = x_sorted.shape
    de = w_gate.shape[-1]
    dm = w_gate.shape[-2]
    bm = MOE_ROWS
    blk = lambda r, be, nu, nx, sl: (jnp.minimum(r, nu[0] - 1), 0)
    hbm = pl.BlockSpec(memory_space=pl.ANY)
    return pl.pallas_call(
        functools.partial(_expert_kernel, layer=layer),
        grid_spec=pltpu.PrefetchScalarGridSpec(
            num_scalar_prefetch=4,
            grid=(rows // bm,),
            in_specs=[pl.BlockSpec((bm, d), blk), hbm, hbm, hbm],
            out_specs=pl.BlockSpec((bm, d), lambda r, be, nu, nx, sl: (r, 0)),
            scratch_shapes=[pltpu.VMEM((2, dm, de), F32), pltpu.VMEM((2, dm, de), F32),
                            pltpu.VMEM((2, de, dm), F32),
                            pltpu.VMEM((dm, de), BF16), pltpu.VMEM((dm, de), BF16),
                            pltpu.VMEM((de, dm), BF16),
                            pltpu.SemaphoreType.DMA((2, 3))]),
        out_shape=jax.ShapeDtypeStruct((rows, d), jnp.uint32),
        compiler_params=_params("arbitrary"),
        name="moe_experts",
    )(block_e, n_used, next_expert, expert_slot, x_sorted, w_gate, w_up, w_down)


def _combine_kernel(dest_ref, y_ref, x_ref, gate_ref, w_ref, o_ref, buf, sems, *, tm, t):
    i = pl.program_id(0)
    slot = i % 2

    def gather(tile, dst_slot):
        base = tile * tm

        def issue(r, carry):
            for k in range(TOP_K):
                d = dest_ref[k * t + base + r]
                pltpu.make_async_copy(y_ref.at[pl.ds(d, 1), :],
                                      buf.at[dst_slot, k, pl.ds(r, 1), :],
                                      sems.at[dst_slot]).start()
            return carry

        lax.fori_loop(0, tm, issue, 0, unroll=DMA_ISSUE_UNROLL)

    @pl.when(i == 0)
    def _():
        gather(0, 0)

    @pl.when(i + 1 < pl.num_programs(0))
    def _():
        gather(i + 1, 1 - slot)

    for k in range(TOP_K):
        pltpu.make_async_copy(y_ref.at[pl.ds(0, tm), :], buf.at[slot, k], sems.at[slot]).wait()
    w = w_ref[...]
    half = buf.shape[3]
    lo0, hi0 = _unpack_rows(buf[slot, 0])
    lo1, hi1 = _unpack_rows(buf[slot, 1])
    o_ref[:, :half] = x_ref[:, :half] + gate_ref[:, :half] * (w[:, 0:1] * lo0 + w[:, 1:2] * lo1)
    o_ref[:, half:] = x_ref[:, half:] + gate_ref[:, half:] * (w[:, 0:1] * hi0 + w[:, 1:2] * hi1)


def _combine(y_sorted, dest_flat, x, mod, w_tok, seq, tm=512):
    t, d = x.shape
    tm = min(tm, seq)
    per_b = seq // tm
    return pl.pallas_call(
        functools.partial(_combine_kernel, tm=tm, t=t),
        grid_spec=pltpu.PrefetchScalarGridSpec(
            num_scalar_prefetch=1,
            grid=(t // tm,),
            in_specs=[pl.BlockSpec(memory_space=pl.ANY),
                      pl.BlockSpec((tm, d), lambda i, dest: (i, 0)),
                      pl.BlockSpec((None, 1, d), lambda i, dest: ((i // per_b) * 6 + 5, 0, 0)),
                      pl.BlockSpec((tm, TOP_K), lambda i, dest: (i, 0))],
            out_specs=pl.BlockSpec((tm, d), lambda i, dest: (i, 0)),
            scratch_shapes=[pltpu.VMEM((2, TOP_K, tm, y_sorted.shape[1]), y_sorted.dtype),
                            pltpu.SemaphoreType.DMA((2,))]),
        out_shape=jax.ShapeDtypeStruct((t, d), F32),
        compiler_params=_params("arbitrary"),
        name="moe_combine",
    )(dest_flat, y_sorted, x, mod, w_tok)


def _rope_tables(seq):
    half = MLA_ROPE // 2
    inv = 1.0 / (ROPE_BASE ** (jnp.arange(0, MLA_ROPE, 2, dtype=F32) / MLA_ROPE))
    ang = jnp.arange(seq, dtype=F32)[:, None] * inv[None, :]
    cos, sin = jnp.cos(ang), jnp.sin(ang)
    z = jnp.zeros((seq, half), F32)
    zz = jnp.zeros((seq, LANE - MLA_ROPE), F32)
    cos_t = jnp.concatenate([cos, cos, zz], axis=1)
    s1_t = jnp.concatenate([z, sin, zz], axis=1)
    s2_t = jnp.concatenate([-sin, z, zz], axis=1)
    return cos_t, s1_t, s2_t


def _pad_last(a, width):
    return jnp.pad(a, [(0, 0)] * (a.ndim - 1) + [(0, width - a.shape[-1])])


def kernel(x, c, norm1_g, norm2_g, w_ada, b_ada, w_in, q_lora_g, kv_lora_g, w_uq, w_ukv,
           q_head_g, k_head_g, mla_out_g, sb_out_g, w_o, router_w, router_b,
           w_gate, w_up, w_down):
    bsz, seq, d = x.shape
    t = bsz * seq
    depth = w_ada.shape[0]

    sb_col0 = COL_KR + MLA_ROPE
    w_in_mla = _pad_last(w_in[..., :sb_col0], MLA_IN_COLS).astype(BF16)
    w_in_sb = w_in[..., sb_col0:].astype(BF16)
    wq = w_uq.reshape(depth, Q_LORA, MLA_HEADS, MLA_QK)
    half = MLA_ROPE // 2
    swap = lambda a: jnp.concatenate([a[..., half:], a[..., :half]], axis=-1)
    wq_rope = wq[..., MLA_NOPE:]
    wuq_p = jnp.concatenate(
        [wq[..., :MLA_NOPE].reshape(depth, Q_LORA, -1),
         _pad_last(wq_rope, LANE).reshape(depth, Q_LORA, -1),
         _pad_last(swap(wq_rope), LANE).reshape(depth, Q_LORA, -1)], axis=-1).astype(BF16)
    wkv = w_ukv.reshape(depth, KV_LORA, MLA_HEADS, MLA_NOPE + MLA_V)
    wukv_p = jnp.concatenate(
        [wkv[..., :MLA_NOPE].reshape(depth, KV_LORA, -1),
         wkv[..., MLA_NOPE:].reshape(depth, KV_LORA, -1)], axis=-1).astype(BF16)
    w_o_b = w_o.astype(BF16)
    q_scale = MLA_QK ** -0.5 * LOG2_E
    gqn = (q_head_g[:, None, :MLA_NOPE] * q_scale)
    gq_rope = q_head_g[:, None, MLA_NOPE:] * q_scale
    gqr = jnp.concatenate([_pad_last(gq_rope, LANE), _pad_last(swap(gq_rope), LANE)], axis=-1)
    gkn = k_head_g[:, None, :MLA_NOPE]
    gkr = _pad_last(k_head_g[:, None, MLA_NOPE:], LANE)
    rope_tabs = _rope_tables(seq)
    rw_t = router_w.T.astype(BF16)

    mod_all = _ada_modulation(c[:, :, None], w_ada, b_ada[:, None, :], layers=depth)
    n_blocks = -(-(t * TOP_K) // MOE_ROWS) + N_EXPERTS
    rows = n_blocks * MOE_ROWS

    xt = x.reshape(t, d)
    for l in range(depth):
        mod = mod_all[l].reshape(bsz * 6, 1, d)
        q2d, k2d, v2d, proj_sb = _in_proj(
            xt, norm1_g[l][None, :], mod, w_in_mla, w_in_sb, rope_tabs, q_lora_g[:, None, :],
            kv_lora_g[:, None, :], wuq_p, wukv_p, gqn, gqr, gkn, gkr, l, seq)
        o_mla = _mla_attention(q2d, k2d, v2d, bsz, seq)
        o_sb = _sb_attention(proj_sb, bsz, seq)
        xt, h2, logits_t = _out_proj(o_mla, o_sb, mla_out_g[l][None, :], sb_out_g[l][None, :],
                                     w_o_b, xt, mod, norm2_g[l][None, :], rw_t, l, seq)
        e_idx, w_tok, rank, cnt = _route(logits_t, router_b)
        counts = cnt[:, 0].astype(jnp.int32)
        padded = (counts + MOE_ROWS - 1) // MOE_ROWS * MOE_ROWS
        pad_end = jnp.cumsum(padded)
        pad_start = pad_end - padded
        eids = jnp.arange(N_EXPERTS, dtype=jnp.int32)
        seg_start = jnp.sum(jnp.where(e_idx[None] == eids[:, None, None],
                                      pad_start[:, None, None], 0), axis=0)
        dest = (seg_start + rank).reshape(TOP_K * t).astype(jnp.int32)
        n_used = (pad_end[-1:] // MOE_ROWS).astype(jnp.int32)
        first_row = jnp.minimum(jnp.arange(n_blocks, dtype=jnp.int32), n_used - 1) * MOE_ROWS
        block_e = jnp.sum(pad_end[None, :] <= first_row[:, None], axis=1).astype(jnp.int32)
        nonempty = counts > 0
        later = (eids[None, :] > eids[:, None]) & nonempty[None, :]
        next_expert = jnp.min(jnp.where(later, eids[None, :], N_EXPERTS), axis=1)
        next_expert = jnp.where(next_expert == N_EXPERTS, eids, next_expert).astype(jnp.int32)
        expert_slot = ((jnp.cumsum(nonempty) - nonempty) % 2).astype(jnp.int32)
        x_sorted = _dispatch(h2, dest, pad_end.astype(jnp.int32), rows)
        y_sorted = _experts(x_sorted, block_e, n_used, next_expert, expert_slot,
                            w_gate, w_up, w_down, l)
        xt = _combine(y_sorted, dest, xt, mod, w_tok.T, seq)
    return xt.reshape(bsz, seq, d)
```
